```python
import math
import jax
import jax.numpy as jnp
from jax import lax
import numpy as np


D_MODEL = 2048
BATCH = 4
SEQ = 4096
DEPTH = 2

GRID_W = 64
CTX_LEN = 256
EPS = 1e-6
N_AB = (DEPTH + 1) // 2
N_CD = DEPTH // 2
D_FF = 4 * D_MODEL

RWKV_HEADS = 16
RWKV_HD = 64
RWKV_W = RWKV_HEADS * RWKV_HD
DECAY_LORA = 96
AAA_LORA = 96
GATE_LORA = 256
RWKV_COLS = 3 * RWKV_W + DECAY_LORA + AAA_LORA + GATE_LORA
GN_EPS = 64e-5

S5_W = 1024
S5_P = 16
S5_G = S5_W // S5_P
S5_N = 64
AB_IN = RWKV_COLS + S5_W
AB_MIX = RWKV_W + S5_W

SSD_HEADS = 32
SSD_HD = 64
SSD_W = SSD_HEADS * SSD_HD
SSD_GROUPS = 4
SSD_N = 128
SSD_CHUNK = 128
CONV_W = 5
XBC_W = SSD_W + 2 * SSD_GROUPS * SSD_N

ATT_HEADS = 16
KV_HEADS = 4
ATT_HD = 64
ATT_Q_W = ATT_HEADS * ATT_HD
ATT_KV_W = KV_HEADS * ATT_HD
WINDOW = 128
ATT_BLOCK = 128
ROPE_BASE = 10000.0

CD_IN = SSD_W + XBC_W + 2 * SSD_HEADS + ATT_Q_W + 2 * ATT_KV_W
CD_MIX = SSD_W + ATT_Q_W

kernel_name = 'hybrid_rwkv7_s5_ssd_swa_dit'


def rmsnorm(x, g):
    xf = x.astype(jnp.float32)
    xf = xf * lax.rsqrt(jnp.mean(xf * xf, axis=-1, keepdims=True) + EPS)
    return xf.astype(x.dtype) * g


def modulate(h, shift, scale):
    return h * (1 + scale) + shift


def squared_relu_mlp(h, w1, w2):
    return jnp.square(jax.nn.relu(h @ w1)) @ w2


def token_shift(h, mu):
    prev = jnp.pad(h[:, :-1], ((0, 0), (1, 0), (0, 0)))
    nxt = jnp.pad(h[:, 1:], ((0, 0), (0, 1), (0, 0)))
    return h + mu[0] * (prev - h) + mu[1] * (nxt - h)


def dwconv_centred(h, w, b):
    pad = w.shape[0] // 2
    y = lax.conv_general_dilated(h, w[:, None, :], window_strides=(1,), padding=((pad, pad),),
                                 dimension_numbers=('NWC', 'WIO', 'NWC'), feature_group_count=h.shape[-1])
    return y + b


def rwkv7_scan(S0, r, w, k, v, kk, a, reverse):
    def step(S, inp):
        r_t, w_t, k_t, v_t, kk_t, a_t = inp
        s_kk = jnp.einsum('bhvk,bhk->bhv', S, kk_t)
        S = (S * w_t[:, :, None, :] - s_kk[..., None] * (kk_t * a_t)[:, :, None, :]
             + v_t[..., None] * k_t[:, :, None, :])
        return S, jnp.einsum('bhvk,bhk->bhv', S, r_t)
    xs = tuple(jnp.moveaxis(t, 1, 0) for t in (r, w, k, v, kk, a))
    S_fin, ys = lax.scan(step, S0, xs, reverse=reverse)
    return S_fin, jnp.moveaxis(ys, 0, 1)


def rwkv7_mix(p, S0, w0, w_up, a0, a_up, g_up, k_k, k_a, r_k, ln_g, ln_b):
    Bsz, L, _ = p.shape
    W = RWKV_W
    r, k, v, xw, xa, xg = jnp.split(p, [W, 2 * W, 3 * W, 3 * W + DECAY_LORA, 3 * W + DECAY_LORA + AAA_LORA], axis=-1)
    heads = lambda t: t.reshape(Bsz, L, RWKV_HEADS, RWKV_HD)
    g = jax.nn.sigmoid(xg) @ g_up
    kk = heads(k * k_k).astype(jnp.float32)
    kk = (kk / jnp.maximum(jnp.sqrt(jnp.sum(kk * kk, axis=-1, keepdims=True)), 1e-12)).astype(p.dtype)
    rh, vh = heads(r), heads(v)
    y = jnp.zeros_like(rh)
    bonus = jnp.zeros_like(rh)
    finals = []
    for d in range(2):
        w_log = -jax.nn.softplus(-(w0[d] + jnp.tanh(xw) @ w_up[d])) - 0.5
        decay = jnp.exp(-jnp.exp(w_log))
        a = jax.nn.sigmoid(a0[d] + xa @ a_up[d])
        kd = heads(k * (1 + (a - 1) * k_a))
        S_fin, yd = rwkv7_scan(S0[d], rh, heads(decay), kd, vh, kk, heads(a), reverse=(d == 1))
        y = y + yd
        bonus = bonus + jnp.sum(rh * kd * r_k, axis=-1, keepdims=True) * vh
        finals.append(S_fin)
    yf = y.astype(jnp.float32)
    mean = jnp.mean(yf, axis=-1, keepdims=True)
    var = jnp.mean(jnp.square(yf - mean), axis=-1, keepdims=True)
    yn = ((yf - mean) * lax.rsqrt(var + GN_EPS)).astype(p.dtype).reshape(Bsz, L, W)
    out = (yn * ln_g + ln_b + bonus.reshape(Bsz, L, W)) * g
    return out, (finals[0], finals[1])


def _cplx_combine(e1, e2):
    a1r, a1i, b1r, b1i = e1
    a2r, a2i, b2r, b2i = e2
    return (a2r * a1r - a2i * a1i, a2r * a1i + a2i * a1r,
            a2r * b1r - a2i * b1i + b2r, a2r * b1i + a2i * b1r + b2i)


def s5_scan(ug, A_re, A_im, log_dt, B_re, B_im, h0, reverse):
    lam_re = jnp.minimum(A_re, -1e-4)
    dt = jnp.exp(log_dt)[:, None]
    mag = jnp.exp(lam_re * dt)
    ab_re, ab_im = mag * jnp.cos(A_im * dt), mag * jnp.sin(A_im * dt)
    den = lam_re * lam_re + A_im * A_im
    f_re = ((ab_re - 1) * lam_re + ab_im * A_im) / den
    f_im = (ab_im * lam_re - (ab_re - 1) * A_im) / den
    bb_re = f_re[..., None] * B_re - f_im[..., None] * B_im
    bb_im = f_re[..., None] * B_im + f_im[..., None] * B_re
    if reverse:
        ug = jnp.flip(ug, 1)
    bu_re = jnp.einsum('blgp,gnp->blgn', ug, bb_re)
    bu_im = jnp.einsum('blgp,gnp->blgn', ug, bb_im)
    h0_re, h0_im = h0
    bu_re = bu_re.at[:, 0].add(ab_re * h0_re - ab_im * h0_im)
    bu_im = bu_im.at[:, 0].add(ab_re * h0_im + ab_im * h0_re)
    a_re = jnp.broadcast_to(ab_re, bu_re.shape)
    a_im = jnp.broadcast_to(ab_im, bu_im.shape)
    _, _, h_re, h_im = lax.associative_scan(_cplx_combine, (a_re, a_im, bu_re, bu_im), axis=1)
    final = (h_re[:, -1], h_im[:, -1])
    if reverse:
        h_re, h_im = jnp.flip(h_re, 1), jnp.flip(h_im, 1)
    return h_re, h_im, final


def s5_mix(u, h0, A_re, A_im, log_dt, B_re, B_im, C_re, C_im, D_skip, glu_w, glu_b):
    Bsz, L, _ = u.shape
    ug = u.reshape(Bsz, L, S5_G, S5_P)
    y = D_skip * u
    finals = []
    for d in range(2):
        h_re, h_im, fin = s5_scan(ug, A_re[d], A_im[d], log_dt[d], B_re[d], B_im[d], h0[d], reverse=(d == 1))
        yd = jnp.einsum('blgn,gpn->blgp', h_re, C_re[d]) - jnp.einsum('blgn,gpn->blgp', h_im, C_im[d])
        y = y + yd.reshape(Bsz, L, S5_W)
        finals.append(fin)
    y = jax.nn.gelu(y)
    y = y * jax.nn.sigmoid(y @ glu_w + glu_b)
    return y, (finals[0], finals[1])


def ssd_chunked(x, dt, A, Bm, Cm, h0):
    Bsz, L, H, P = x.shape
    G, N = Bm.shape[2], Bm.shape[3]
    R = H // G
    Q = SSD_CHUNK
    nc = L // Q
    xc = x.reshape(Bsz, nc, Q, G, R, P)
    dtc = dt.reshape(Bsz, nc, Q, G, R)
    Bc = Bm.reshape(Bsz, nc, Q, G, N)
    Cc = Cm.reshape(Bsz, nc, Q, G, N)
    A_cum = jnp.cumsum(dtc * A.reshape(G, R), axis=2)
    lower = jnp.tril(jnp.ones((Q, Q), dtype=bool))
    seg = A_cum[:, :, :, None] - A_cum[:, :, None, :]
    decay_ls = jnp.exp(jnp.where(lower[None, None, :, :, None, None], seg, -jnp.inf))
    cb = jnp.einsum('bclgn,bcsgn->bclsg', Cc, Bc)
    w_ls = cb[..., None] * decay_ls * dtc[:, :, None]
    y_diag = jnp.einsum('bclsgr,bcsgrp->bclgrp', w_ls, xc)
    to_end = jnp.exp(A_cum[:, :, -1:] - A_cum) * dtc
    states = jnp.einsum('bcsgn,bcsgr,bcsgrp->bcgrpn', Bc, to_end, xc)
    chunk_decay = jnp.exp(A_cum[:, :, -1])

    def step(h, inp):
        st, dec = inp
        return h * dec[..., None, None] + st, h

    h_fin, h_in = lax.scan(step, h0.reshape(Bsz, G, R, P, N),
                           (jnp.moveaxis(states, 1, 0), jnp.moveaxis(chunk_decay, 1, 0)))
    h_in = jnp.moveaxis(h_in, 0, 1)
    y_off = jnp.einsum('bclgn,bcgrpn,bclgr->bclgrp', Cc, h_in, jnp.exp(A_cum))
    return (y_diag + y_off).reshape(Bsz, L, H, P), h_fin.reshape(Bsz, H, P, N)


def ssd_mix(z, xbc, dt_raw, h0f, h0b, conv_w, conv_b, A_log, dt_bias, D_skip, norm_g):
    Bsz, L, _ = z.shape
    xbc = jax.nn.silu(dwconv_centred(xbc, conv_w, conv_b))
    xs, Bm, Cm = jnp.split(xbc, [SSD_W, SSD_W + SSD_GROUPS * SSD_N], axis=-1)
    xh = xs.reshape(Bsz, L, SSD_HEADS, SSD_HD)
    Bm = Bm.reshape(Bsz, L, SSD_GROUPS, SSD_N)
    Cm = Cm.reshape(Bsz, L, SSD_GROUPS, SSD_N)
    dt = jax.nn.softplus(dt_raw.reshape(Bsz, L, 2, SSD_HEADS) + dt_bias)
    A = -jnp.exp(A_log)
    flip = lambda t: jnp.flip(t, 1)
    y_f, hf = ssd_chunked(xh, dt[:, :, 0], A[0], Bm, Cm, h0f)
    y_b, hb = ssd_chunked(flip(xh), flip(dt[:, :, 1]), A[1], flip(Bm), flip(Cm), h0b)
    y = y_f + flip(y_b) + D_skip[:, None] * xh
    y = y.reshape(Bsz, L, SSD_W) * jax.nn.silu(z)
    yg = y.reshape(Bsz, L, SSD_GROUPS, SSD_W // SSD_GROUPS).astype(jnp.float32)
    yg = yg * lax.rsqrt(jnp.mean(yg * yg, axis=-1, keepdims=True) + EPS)
    return yg.reshape(Bsz, L, SSD_W).astype(z.dtype) * norm_g, hf, hb


def axial_rope_tables(rows):
    nf = ATT_HD // 4
    inv_freq = ROPE_BASE ** (-jnp.arange(nf, dtype=jnp.float32) / nf)
    row_id = jnp.repeat(jnp.arange(rows, dtype=jnp.float32), GRID_W)
    col_id = jnp.tile(jnp.arange(GRID_W, dtype=jnp.float32), rows)
    ang = jnp.stack([row_id[:, None] * inv_freq, col_id[:, None] * inv_freq], axis=1)
    return jnp.cos(ang), jnp.sin(ang)


def apply_rope(x, cos, sin):
    Bsz, L, H, D = x.shape
    xr = x.reshape(Bsz, L, H, 2, 2, D // 4)
    x1, x2 = xr[..., 0, :], xr[..., 1, :]
    c = cos[None, :, None].astype(x.dtype)
    s = sin[None, :, None].astype(x.dtype)
    return jnp.stack([x1 * c - x2 * s, x1 * s + x2 * c], axis=-2).reshape(Bsz, L, H, D)


def window_attention(q, k, v, kc, vc, sink):
    Bsz, L = q.shape[:2]
    nb = L // ATT_BLOCK
    R = ATT_HEADS // KV_HEADS
    scale = ATT_HD ** -0.5
    qb = q.reshape(Bsz, nb, ATT_BLOCK, KV_HEADS, R, ATT_HD)

    def band(t):
        tp = jnp.pad(t, ((0, 0), (ATT_BLOCK, ATT_BLOCK), (0, 0), (0, 0)))
        tp = tp.reshape(Bsz, nb + 2, ATT_BLOCK, KV_HEADS, ATT_HD)
        return jnp.concatenate([tp[:, :-2], tp[:, 1:-1], tp[:, 2:]], axis=2)

    kb, vb = band(k), band(v)
    s_band = jnp.einsum('bnqhrd,bnkhd->bnhrqk', qb, kb).astype(jnp.float32) * scale
    blk = jnp.arange(nb)[:, None, None]
    q_pos = blk * ATT_BLOCK + jnp.arange(ATT_BLOCK)[None, :, None]
    k_pos = (blk - 1) * ATT_BLOCK + jnp.arange(3 * ATT_BLOCK)[None, None, :]
    valid = (jnp.abs(q_pos - k_pos) <= WINDOW) & (k_pos >= 0) & (k_pos < L)
    s_band = jnp.where(valid[None, :, None, None], s_band, -jnp.inf)
    s_ctx = jnp.einsum('bnqhrd,bchd->bnhrqc', qb, kc).astype(jnp.float32) * scale
    s_sink = jnp.broadcast_to(sink.astype(jnp.float32).reshape(KV_HEADS, R)[None, None, :, :, None, None],
                              s_ctx.shape[:-1] + (1,))
    p = jax.nn.softmax(jnp.concatenate([s_sink, s_ctx, s_band], axis=-1), axis=-1).astype(v.dtype)
    n_ctx = kc.shape[1]
    out = (jnp.einsum('bnhrqc,bchd->bnqhrd', p[..., 1:1 + n_ctx], vc)
           + jnp.einsum('bnhrqk,bnkhd->bnqhrd', p[..., 1 + n_ctx:], vb))
    return out.reshape(Bsz, L, ATT_Q_W)


def context_attention(qc, kc, vc, sink):
    Bsz, C = qc.shape[:2]
    R = ATT_HEADS // KV_HEADS
    qg = qc.reshape(Bsz, C, KV_HEADS, R, ATT_HD)
    s = jnp.einsum('bqhrd,bkhd->bhrqk', qg, kc).astype(jnp.float32) * (ATT_HD ** -0.5)
    s_sink = jnp.broadcast_to(sink.astype(jnp.float32).reshape(KV_HEADS, R)[None, :, :, None, None],
                              s.shape[:-1] + (1,))
    p = jax.nn.softmax(jnp.concatenate([s_sink, s], axis=-1), axis=-1)[..., 1:].astype(vc.dtype)
    return jnp.einsum('bhrqk,bkhd->bqhrd', p, vc).reshape(Bsz, C, ATT_Q_W)


def mixer_ab(h, hc, need_ctx, w_in, w_out, mu, w0, w_up, a0, a_up, g_up, k_k, k_a, r_k, ln_g, ln_b,
             A_re, A_im, log_dt, B_re, B_im, C_re, C_im, D_skip, glu_w, glu_b):
    Bsz = h.shape[0]
    p, pc = h @ w_in, hc @ w_in
    rwkv_prm = (w0, w_up, a0, a_up, g_up, k_k, k_a, r_k, ln_g, ln_b)
    s5_prm = (A_re, A_im, log_dt, B_re, B_im, C_re, C_im, D_skip, glu_w, glu_b)
    S_zero = jnp.zeros((Bsz, RWKV_HEADS, RWKV_HD, RWKV_HD), h.dtype)
    h_zero = jnp.zeros((Bsz, S5_G, S5_N), h.dtype)
    ya_c, S_ctx = rwkv7_mix(token_shift(pc[..., :RWKV_COLS], mu), (S_zero, S_zero), *rwkv_prm)
    ya, _ = rwkv7_mix(token_shift(p[..., :RWKV_COLS], mu), S_ctx, *rwkv_prm)
    yb_c, h_ctx = s5_mix(pc[..., RWKV_COLS:], ((h_zero, h_zero), (h_zero, h_zero)), *s5_prm)
    yb, _ = s5_mix(p[..., RWKV_COLS:], h_ctx, *s5_prm)
    out = jnp.concatenate([ya, yb], axis=-1) @ w_out
    out_c = jnp.concatenate([ya_c, yb_c], axis=-1) @ w_out if need_ctx else None
    return out, out_c


def mixer_cd(h, hc, cos, sin, need_ctx, w_in, w_out, conv_w, conv_b, A_log, dt_bias, D_skip, norm_g, sink):
    Bsz, L, _ = h.shape
    C = hc.shape[1]
    splits = [SSD_W, SSD_W + XBC_W, SSD_W + XBC_W + 2 * SSD_HEADS,
              SSD_W + XBC_W + 2 * SSD_HEADS + ATT_Q_W, SSD_W + XBC_W + 2 * SSD_HEADS + ATT_Q_W + ATT_KV_W]
    z, xbc, dtr, q, k, v = jnp.split(h @ w_in, splits, axis=-1)
    zc, xbcc, dtrc, qc, kc, vc = jnp.split(hc @ w_in, splits, axis=-1)
    ssd_prm = (conv_w, conv_b, A_log, dt_bias, D_skip, norm_g)
    h_zero = jnp.zeros((Bsz, SSD_HEADS, SSD_HD, SSD_N), h.dtype)
    yc_c, hf, hb = ssd_mix(zc, xbcc, dtrc, h_zero, h_zero, *ssd_prm)
    y_c, _, _ = ssd_mix(z, xbc, dtr, hf, hb, *ssd_prm)
    q = apply_rope(q.reshape(Bsz, L, ATT_HEADS, ATT_HD), cos, sin)
    k = apply_rope(k.reshape(Bsz, L, KV_HEADS, ATT_HD), cos, sin)
    v = v.reshape(Bsz, L, KV_HEADS, ATT_HD)
    kc = kc.reshape(Bsz, C, KV_HEADS, ATT_HD)
    vc = vc.reshape(Bsz, C, KV_HEADS, ATT_HD)
    y_d = window_attention(q, k, v, kc, vc, sink)
    out = jnp.concatenate([y_c, y_d], axis=-1) @ w_out
    out_c = None
    if need_ctx:
        yc_d = context_attention(qc.reshape(Bsz, C, ATT_HEADS, ATT_HD), kc, vc, sink)
        out_c = jnp.concatenate([yc_c, yc_d], axis=-1) @ w_out
    return out, out_c


def setup_inputs(seed: int = 0) -> dict:
    key = jax.random.key(seed)
    keys = jax.random.split(key, 64)
    counter = [0]

    def nk():
        counter[0] += 1
        return keys[counter[0] - 1]

    def nrm(shape, scale=1.0):
        return scale * jax.random.normal(nk(), shape, jnp.float32)

    def unif(shape, lo, hi):
        return jax.random.uniform(nk(), shape, jnp.float32, lo, hi)

    D = D_MODEL
    log_dt_lo, log_dt_hi = math.log(1e-3), math.log(1e-1)
    ssd_dt = jnp.exp(unif((N_CD, 2, SSD_HEADS), log_dt_lo, log_dt_hi))
    return {
        'x': nrm((BATCH, SEQ, D)),
        'c': nrm((BATCH, D)),
        'ctx': nrm((BATCH, CTX_LEN, D)),
        'c_ctx': nrm((D,)),
        'ada_w': nrm((DEPTH, D, 6 * D), 0.5 * D ** -0.5),
        'ada_b': nrm((DEPTH, 6 * D), 0.01),
        'norm1_g': 1.0 + nrm((DEPTH, D), 0.02),
        'norm2_g': 1.0 + nrm((DEPTH, D), 0.02),
        'mlp_w1': nrm((DEPTH, D, D_FF), D ** -0.5),
        'mlp_w2': nrm((DEPTH, D_FF, D), D_FF ** -0.5),
        'final_g': 1.0 + nrm((D,), 0.02),
        'ab_w_in': nrm((N_AB, D, AB_IN), D ** -0.5),
        'ab_w_out': nrm((N_AB, AB_MIX, D), AB_MIX ** -0.5),
        'rwkv_mu': unif((N_AB, 2, RWKV_COLS), 0.0, 0.5),
        'rwkv_w0': unif((N_AB, 2, RWKV_W), -7.0, -2.0),
        'rwkv_w_up': nrm((N_AB, 2, DECAY_LORA, RWKV_W), 0.1 * DECAY_LORA ** -0.5),
        'rwkv_a0': nrm((N_AB, 2, RWKV_W), 0.1),
        'rwkv_a_up': nrm((N_AB, 2, AAA_LORA, RWKV_W), AAA_LORA ** -0.5),
        'rwkv_g_up': nrm((N_AB, GATE_LORA, RWKV_W), GATE_LORA ** -0.5),
        'rwkv_k_k': 0.85 + nrm((N_AB, RWKV_W), 0.05),
        'rwkv_k_a': 1.0 + nrm((N_AB, RWKV_W), 0.05),
        'rwkv_r_k': nrm((N_AB, RWKV_HEADS, RWKV_HD), 0.1),
        'rwkv_ln_g': 1.0 + nrm((N_AB, RWKV_W), 0.02),
        'rwkv_ln_b': nrm((N_AB, RWKV_W), 0.01),
        's5_A_re': -0.5 + nrm((N_AB, 2, S5_G, S5_N), 0.01),
        's5_A_im': np.pi * jnp.arange(S5_N, dtype=jnp.float32) + nrm((N_AB, 2, S5_G, S5_N), 0.01),
        's5_log_dt': unif((N_AB, 2, S5_G), log_dt_lo, log_dt_hi),
        's5_B_re': nrm((N_AB, 2, S5_G, S5_N, S5_P), (2 * S5_P) ** -0.5),
        's5_B_im': nrm((N_AB, 2, S5_G, S5_N, S5_P), (2 * S5_P) ** -0.5),
        's5_C_re': nrm((N_AB, 2, S5_G, S5_P, S5_N), (2 * S5_N) ** -0.5),
        's5_C_im': nrm((N_AB, 2, S5_G, S5_P, S5_N), (2 * S5_N) ** -0.5),
        's5_D': nrm((N_AB, S5_W)),
        's5_glu_w': nrm((N_AB, S5_W, S5_W), S5_W ** -0.5),
        's5_glu_b': nrm((N_AB, S5_W), 0.01),
        'cd_w_in': nrm((N_CD, D, CD_IN), D ** -0.5),
        'cd_w_out': nrm((N_CD, CD_MIX, D), CD_MIX ** -0.5),
        'ssd_conv_w': nrm((N_CD, CONV_W, XBC_W), CONV_W ** -0.5),
        'ssd_conv_b': nrm((N_CD, XBC_W), 0.01),
        'ssd_A_log': jnp.log(unif((N_CD, 2, SSD_HEADS), 1.0, 16.0)),
        'ssd_dt_bias': ssd_dt + jnp.log(-jnp.expm1(-ssd_dt)),
        'ssd_D': 1.0 + nrm((N_CD, SSD_HEADS), 0.1),
        'ssd_norm_g': 1.0 + nrm((N_CD, SSD_W), 0.02),
        'attn_sink': nrm((N_CD, ATT_HEADS), 0.5),
    }


def reference(x, c, ctx, c_ctx, ada_w, ada_b, norm1_g, norm2_g, mlp_w1, mlp_w2, final_g,
              ab_w_in, ab_w_out, rwkv_mu, rwkv_w0, rwkv_w_up, rwkv_a0, rwkv_a_up, rwkv_g_up,
              rwkv_k_k, rwkv_k_a, rwkv_r_k, rwkv_ln_g, rwkv_ln_b,
              s5_A_re, s5_A_im, s5_log_dt, s5_B_re, s5_B_im, s5_C_re, s5_C_im, s5_D, s5_glu_w, s5_glu_b,
              cd_w_in, cd_w_out, ssd_conv_w, ssd_conv_b, ssd_A_log, ssd_dt_bias, ssd_D, ssd_norm_g, attn_sink):
    L = x.shape[1]
    rows = L // GRID_W
    cos, sin = axial_rope_tables(rows)
    cond = jax.nn.silu(c)
    cond_ctx = jax.nn.silu(c_ctx)
    cx = ctx
    for i in range(DEPTH):
        last = i == DEPTH - 1
        j = i // 2
        mod = (cond @ ada_w[i] + ada_b[i])[:, None, :]
        mod_c = cond_ctx @ ada_w[i] + ada_b[i]
        sh1, sc1, g1, sh2, sc2, g2 = jnp.split(mod, 6, axis=-1)
        sh1c, sc1c, g1c, sh2c, sc2c, g2c = jnp.split(mod_c, 6, axis=-1)
        h = modulate(rmsnorm(x, norm1_g[i]), sh1, sc1)
        hc = modulate(rmsnorm(cx, norm1_g[i]), sh1c, sc1c)
        if i % 2 == 0:
            out, out_c = mixer_ab(h, hc, not last, ab_w_in[j], ab_w_out[j], rwkv_mu[j], rwkv_w0[j], rwkv_w_up[j],
                                  rwkv_a0[j], rwkv_a_up[j], rwkv_g_up[j], rwkv_k_k[j], rwkv_k_a[j], rwkv_r_k[j],
                                  rwkv_ln_g[j], rwkv_ln_b[j], s5_A_re[j], s5_A_im[j], s5_log_dt[j], s5_B_re[j],
                                  s5_B_im[j], s5_C_re[j], s5_C_im[j], s5_D[j], s5_glu_w[j], s5_glu_b[j])
        else:
            out, out_c = mixer_cd(h, hc, cos, sin, not last, cd_w_in[j], cd_w_out[j], ssd_conv_w[j], ssd_conv_b[j],
                                  ssd_A_log[j], ssd_dt_bias[j], ssd_D[j], ssd_norm_g[j], attn_sink[j])
        x = x + g1 * out
        x = x + g2 * squared_relu_mlp(modulate(rmsnorm(x, norm2_g[i]), sh2, sc2), mlp_w1[i], mlp_w2[i])
        if not last:
            cx = cx + g1c * out_c
            cx = cx + g2c * squared_relu_mlp(modulate(rmsnorm(cx, norm2_g[i]), sh2c, sc2c), mlp_w1[i], mlp_w2[i])
    return rmsnorm(x, final_g)
```

```python
import functools
import math

import jax
import jax.numpy as jnp
from jax import lax
from jax.experimental import pallas as pl
from jax.experimental.pallas import tpu as pltpu

F32 = jnp.float32
BF16 = jnp.bfloat16

EPS = 1e-6
GN_EPS = 64e-5
GRID_W = 64
ROPE_BASE = 10000.0

RWKV_HEADS = 16
RWKV_HD = 64
RWKV_W = RWKV_HEADS * RWKV_HD
DECAY_LORA = 96
AAA_LORA = 96
GATE_LORA = 256
RWKV_COLS = 3 * RWKV_W + DECAY_LORA + AAA_LORA + GATE_LORA
S5_W = 1024
S5_P = 16
S5_G = S5_W // S5_P
S5_N = 64
SSD_HEADS = 32
SSD_HD = 64
SSD_W = SSD_HEADS * SSD_HD
SSD_GROUPS = 4
SSD_N = 128
SSD_CHUNK = 128
XBC_W = SSD_W + 2 * SSD_GROUPS * SSD_N
ATT_HEADS = 16
KV_HEADS = 4
ATT_HD = 64
ATT_Q_W = ATT_HEADS * ATT_HD
ATT_KV_W = KV_HEADS * ATT_HD
WINDOW = 128
ATT_BLOCK = 128

LANES = 128
SUBLANES = 8
SEQ_BLOCK = 256
RWKV_CHUNK = 64
S5_GB = 8
VMEM_LIMIT = 48 * 1024 * 1024


def _cparams(sem):
    return pltpu.CompilerParams(dimension_semantics=sem, vmem_limit_bytes=VMEM_LIMIT)


def _pick(n, cands):
    for c in cands:
        if n % c == 0:
            return c
    return n


def _dot(a, b):
    return jnp.dot(a.astype(BF16), b.astype(BF16), preferred_element_type=F32)


def _dot_nt(a, b):
    return lax.dot_general(a.astype(BF16), b.astype(BF16), (((1,), (1,)), ((), ())), preferred_element_type=F32)


def _dot_tn(a, b):
    return lax.dot_general(a.astype(BF16), b.astype(BF16), (((0,), (0,)), ((), ())), preferred_element_type=F32)


def _split3(x):
    hi = x.astype(BF16)
    r1 = x - hi.astype(F32)
    mid = r1.astype(BF16)
    lo = (r1 - mid.astype(F32)).astype(BF16)
    return hi, mid, lo


def _dot_ones_x(tri, x):
    hi, mid, lo = _split3(x)
    f = lambda t: jnp.dot(tri, t, preferred_element_type=F32)
    return f(hi) + f(mid) + f(lo)


def _dot_x_ones_nt(x, tri):
    hi, mid, lo = _split3(x)
    f = lambda t: lax.dot_general(t, tri, (((1,), (1,)), ((), ())), preferred_element_type=F32)
    return f(hi) + f(mid) + f(lo)


def _mm_body(*refs, nk, act, residual):
    if residual:
        a_ref, b_ref, res_ref, gate_ref, o_ref, acc_ref = refs
    else:
        a_ref, b_ref, o_ref, acc_ref = refs
    k = pl.program_id(2)
    part = jnp.dot(a_ref[...].astype(BF16), b_ref[...].astype(BF16), preferred_element_type=F32)

    def finish(acc):
        if act == "relu2":
            acc = jnp.square(jnp.maximum(acc, 0.0))
        if residual:
            acc = res_ref[...] + gate_ref[0] * acc
        o_ref[...] = acc.astype(o_ref.dtype)

    if nk == 1:
        finish(part)
    else:
        @pl.when(k == 0)
        def _():
            acc_ref[...] = part

        @pl.when(k > 0)
        def _():
            acc_ref[...] += part

        @pl.when(k == nk - 1)
        def _():
            finish(acc_ref[...])


def _mm(a, b, *, out_dtype=F32, act=None, res=None, gate=None, gate_idx=None, name="mm"):
    M, K = a.shape
    N = b.shape[1]
    tm = _pick(M, (1024, 512, 256, 128, 64, 32, 16, 8))
    tn = _pick(N, (512, 384, 256, 128))
    tk = _pick(K, (2048, 1536, 1024, 512, 256, 128))
    nk = K // tk
    residual = res is not None
    in_specs = [pl.BlockSpec((tm, tk), lambda i, j, k: (i, k)),
                pl.BlockSpec((tk, tn), lambda i, j, k: (k, j))]
    args = [a, b]
    if residual:
        in_specs += [pl.BlockSpec((tm, tn), lambda i, j, k: (i, j)),
                     pl.BlockSpec((1, 1, tn), lambda i, j, k: (gate_idx(i * tm), 0, j))]
        args += [res, gate]
    return pl.pallas_call(
        functools.partial(_mm_body, nk=nk, act=act, residual=residual),
        grid=(M // tm, N // tn, nk),
        in_specs=in_specs,
        out_specs=pl.BlockSpec((tm, tn), lambda i, j, k: (i, j)),
        out_shape=jax.ShapeDtypeStruct((M, N), out_dtype),
        scratch_shapes=[pltpu.VMEM((tm, tn), F32)],
        compiler_params=_cparams(("parallel", "parallel", "arbitrary")),
        name=name,
    )(*args)


def _norm_body(x_ref, g_ref, sh_ref, sc_ref, o_ref):
    x = x_ref[...]
    ms = jnp.mean(x * x, axis=-1, keepdims=True)
    xn = x * lax.rsqrt(ms + EPS) * g_ref[...]
    o_ref[...] = (xn * (1.0 + sc_ref[0]) + sh_ref[0]).astype(o_ref.dtype)


def _norm_mod(x, g, shift, scale, group_idx, out_dtype=BF16):
    M, D = x.shape
    tm = _pick(M, (256, 128, 64, 32, 16, 8))
    return pl.pallas_call(
        _norm_body,
        grid=(M // tm,),
        in_specs=[pl.BlockSpec((tm, D), lambda i: (i, 0)),
                  pl.BlockSpec((1, D), lambda i: (0, 0)),
                  pl.BlockSpec((1, 1, D), lambda i: (group_idx(i * tm), 0, 0)),
                  pl.BlockSpec((1, 1, D), lambda i: (group_idx(i * tm), 0, 0))],
        out_specs=pl.BlockSpec((tm, D), lambda i: (i, 0)),
        out_shape=jax.ShapeDtypeStruct((M, D), out_dtype),
        compiler_params=_cparams(("parallel",)),
        name="norm_mod",
    )(x, g.reshape(1, D), shift, scale)


def _seq_block(b, s, *, n_ctx_blk, n_x_blk, n_batch, reverse):
    if reverse:
        c = b * n_ctx_blk + (n_ctx_blk - 1 - s)
        x = n_batch * n_ctx_blk + b * n_x_blk + (n_x_blk - 1 - (s - n_ctx_blk))
    else:
        c = b * n_ctx_blk + s
        x = n_batch * n_ctx_blk + b * n_x_blk + (s - n_ctx_blk)
    return jnp.where(s < n_ctx_blk, c, x)


def _rwkv_body(r_ref, v_ref, kk_ref, lw_ref, kd_ref, be_ref, y_ref, st_ref, *, reverse):
    C = RWKV_CHUNK
    nchunk = SEQ_BLOCK // C
    s = pl.program_id(2)

    @pl.when(s == 0)
    def _():
        st_ref[...] = jnp.zeros_like(st_ref)

    sgn = -1 if reverse else 1
    row = lax.broadcasted_iota(jnp.int32, (2 * C, 2 * C), 0)
    col = lax.broadcasted_iota(jnp.int32, (2 * C, 2 * C), 1)
    diff = (row - col) * sgn
    strict = diff > 0
    incl = diff >= 0
    same_head = (row // C) == (col // C)
    r64 = lax.broadcasted_iota(jnp.int32, (C, C), 0)
    c64 = lax.broadcasted_iota(jnp.int32, (C, C), 1)
    tri = jnp.where((r64 - c64) * sgn >= 0, 1.0, 0.0).astype(BF16)
    head0 = lax.broadcasted_iota(jnp.int32, (C, 2 * C), 1) < C

    def stack(z):
        return jnp.concatenate([jnp.where(head0, z, 0.0), jnp.where(head0, 0.0, z)], axis=0)

    def unstack(z):
        return z[0:C] + z[C:2 * C]

    def chunk(ci, carry):
        ce = (nchunk - 1 - ci) if reverse else ci
        sl = pl.ds(pl.multiple_of(ce * C, C), C)
        lw = lw_ref[sl, :]
        r = r_ref[sl, :]
        v = v_ref[sl, :]
        kk = kk_ref[sl, :]
        kd = kd_ref[sl, :]
        be = be_ref[sl, :]
        cum = _dot_ones_x(tri, lw)
        total = jnp.sum(lw, axis=0, keepdims=True)
        pinv = jnp.exp(-cum)
        pend = jnp.exp(total - cum)
        a_t = kk * jnp.exp(cum - lw)
        r_t = r * jnp.exp(cum)
        b_t = be * pinv
        k_t = kd * pinv
        lhs = jnp.concatenate([stack(a_t), stack(r_t)], axis=0)
        rhs = jnp.concatenate([stack(b_t), stack(k_t)], axis=0)
        g = _dot_nt(lhs, rhs)
        x_neg = jnp.where(strict, -g[0:2 * C, 0:2 * C], 0.0)
        n_ak = jnp.where(strict, g[0:2 * C, 2 * C:4 * C], 0.0)
        m_rb = jnp.where(incl, g[2 * C:4 * C, 0:2 * C], 0.0)
        m_rk = jnp.where(incl, g[2 * C:4 * C, 2 * C:4 * C], 0.0)
        st = st_ref[...]
        a_s = _dot_nt(jnp.concatenate([a_t, r_t], axis=0), st)
        v_st = stack(v)
        u = -(stack(a_s[0:C]) + _dot(n_ak, v_st))
        xp = x_neg
        n_pow = int(math.log2(C))
        for it in range(n_pow):
            u = u + _dot(xp, u)
            if it + 1 < n_pow:
                xp = _dot(xp, xp)
        y = stack(a_s[C:2 * C]) + _dot(m_rb, u) + _dot(m_rk, v_st)
        y_ref[sl, :] = unstack(y)
        uv = jnp.concatenate([unstack(u), v], axis=0)
        bk = jnp.concatenate([be * pend, kd * pend], axis=0)
        upd = _dot_tn(uv, bk)
        st_ref[...] = st * jnp.exp(total) + jnp.where(same_head, upd, 0.0)
        return carry

    lax.fori_loop(0, nchunk, chunk, 0)


def _rwkv_scan(r, v, kk, lw, kd, be, *, n_batch, n_ctx_blk, n_x_blk, reverse):
    R, W = r.shape
    npair = W // LANES
    ns = n_ctx_blk + n_x_blk
    blk = functools.partial(_seq_block, n_ctx_blk=n_ctx_blk, n_x_blk=n_x_blk, n_batch=n_batch, reverse=reverse)
    spec = pl.BlockSpec((SEQ_BLOCK, LANES), lambda b, p, s: (blk(b, s), p))
    return pl.pallas_call(
        functools.partial(_rwkv_body, reverse=reverse),
        grid=(n_batch, npair, ns),
        in_specs=[spec] * 6,
        out_specs=spec,
        out_shape=jax.ShapeDtypeStruct((R, W), F32),
        scratch_shapes=[pltpu.VMEM((LANES, LANES), F32)],
        compiler_params=_cparams(("parallel", "parallel", "arbitrary")),
        name="rwkv_scan_bwd" if reverse else "rwkv_scan_fwd",
    )(r, v, kk, lw, kd, be)


def _s5_body(u_ref, bw_ref, cw_ref, tab_ref, y_ref, hb_ref, carry_ref, *, reverse):
    W = S5_GB * S5_N
    s = pl.program_id(2)

    @pl.when(s == 0)
    def _():
        carry_ref[...] = jnp.zeros_like(carry_ref)

    hb_ref[...] = _dot(u_ref[...], bw_ref[...])
    pw_re = tab_ref[0:8, :]
    pw_im = tab_ref[8:16, :]
    steps = [(k, tab_ref[16 + 2 * i:17 + 2 * i, :], tab_ref[17 + 2 * i:18 + 2 * i, :])
             for i, k in enumerate((1, 2, 4))]
    rows8 = lax.broadcasted_iota(jnp.int32, (SUBLANES, W), 0)
    nt = SEQ_BLOCK // SUBLANES

    def shifted(h, k):
        if reverse:
            return jnp.where(rows8 < SUBLANES - k, pltpu.roll(h, SUBLANES - k, axis=0), 0.0)
        return jnp.where(rows8 >= k, pltpu.roll(h, k, axis=0), 0.0)

    def tile(j, carry):
        cr, ci = carry
        jj = (nt - 1 - j) if reverse else j
        rows = pl.ds(pl.multiple_of(jj * SUBLANES, SUBLANES), SUBLANES)
        hr = hb_ref[rows, 0:W]
        hi = hb_ref[rows, W:2 * W]
        for k, ar, ai in steps:
            sr = shifted(hr, k)
            si = shifted(hi, k)
            hr, hi = hr + ar * sr - ai * si, hi + ar * si + ai * sr
        hr, hi = hr + pw_re * cr - pw_im * ci, hi + pw_re * ci + pw_im * cr
        hb_ref[rows, 0:W] = hr
        hb_ref[rows, W:2 * W] = hi
        last = 0 if reverse else SUBLANES - 1
        cr = jnp.broadcast_to(hr[last:last + 1, :], (SUBLANES, W))
        ci = jnp.broadcast_to(hi[last:last + 1, :], (SUBLANES, W))
        return cr, ci

    cr, ci = lax.fori_loop(0, nt, tile, (carry_ref[:, 0:W], carry_ref[:, W:2 * W]))
    carry_ref[:, 0:W] = cr
    carry_ref[:, W:2 * W] = ci
    y_ref[...] = _dot(hb_ref[...], cw_ref[...])


def _s5_scan(p, col0_blk, bw, cw, tab, *, n_batch, n_ctx_blk, n_x_blk, reverse):
    R = p.shape[0]
    ngb = S5_G // S5_GB
    ns = n_ctx_blk + n_x_blk
    blk = functools.partial(_seq_block, n_ctx_blk=n_ctx_blk, n_x_blk=n_x_blk, n_batch=n_batch, reverse=reverse)
    W2 = 2 * S5_GB * S5_N
    return pl.pallas_call(
        functools.partial(_s5_body, reverse=reverse),
        grid=(n_batch, ngb, ns),
        in_specs=[pl.BlockSpec((SEQ_BLOCK, LANES), lambda b, g, s: (blk(b, s), col0_blk + g)),
                  pl.BlockSpec((None, LANES, W2), lambda b, g, s: (g, 0, 0)),
                  pl.BlockSpec((None, W2, LANES), lambda b, g, s: (g, 0, 0)),
                  pl.BlockSpec((None, 24, W2 // 2), lambda b, g, s: (g, 0, 0))],
        out_specs=pl.BlockSpec((SEQ_BLOCK, LANES), lambda b, g, s: (blk(b, s), g)),
        out_shape=jax.ShapeDtypeStruct((R, S5_W), F32),
        scratch_shapes=[pltpu.VMEM((SEQ_BLOCK, W2), F32), pltpu.VMEM((SUBLANES, W2), F32)],
        compiler_params=_cparams(("parallel", "parallel", "arbitrary")),
        name="s5_scan_bwd" if reverse else "s5_scan_fwd",
    )(p, bw, cw, tab)


def _s5_weights(A_re, A_im, log_dt, B_re, B_im, C_re, C_im, reverse):
    lam_re = jnp.minimum(A_re, -1e-4)
    dt = jnp.exp(log_dt)[:, None]
    mag = jnp.exp(lam_re * dt)
    ab_re, ab_im = mag * jnp.cos(A_im * dt), mag * jnp.sin(A_im * dt)
    den = lam_re * lam_re + A_im * A_im
    f_re = ((ab_re - 1) * lam_re + ab_im * A_im) / den
    f_im = (ab_im * lam_re - (ab_re - 1) * A_im) / den
    bb_re = f_re[..., None] * B_re - f_im[..., None] * B_im
    bb_im = f_re[..., None] * B_im + f_im[..., None] * B_re
    ngb = S5_G // S5_GB
    eye = jnp.eye(S5_GB, dtype=F32)

    def in_block(bb):
        t = bb.reshape(ngb, S5_GB, S5_N, S5_P)
        return jnp.einsum("agnp,gh->agphn", t, eye).reshape(ngb, S5_GB * S5_P, S5_GB * S5_N)

    def out_block(cc):
        t = cc.reshape(ngb, S5_GB, S5_P, S5_N)
        return jnp.einsum("agpn,gh->agnhp", t, eye).reshape(ngb, S5_GB * S5_N, S5_GB * S5_P)

    bw = jnp.concatenate([in_block(bb_re), in_block(bb_im)], axis=2).astype(BF16)
    cw = jnp.concatenate([out_block(C_re), -out_block(C_im)], axis=1).astype(BF16)

    def power(k):
        m = jnp.exp(k * lam_re * dt)
        return m * jnp.cos(k * A_im * dt), m * jnp.sin(k * A_im * dt)

    carry_pow = [power(float(SUBLANES - i if reverse else i + 1)) for i in range(SUBLANES)]
    rows = [c[0] for c in carry_pow] + [c[1] for c in carry_pow]
    for k in (1.0, 2.0, 4.0):
        rows += list(power(k))
    tab = jnp.stack(rows, axis=0)
    tab = jnp.pad(tab, ((0, 2), (0, 0), (0, 0)))
    tab = tab.reshape(24, ngb, S5_GB * S5_N).transpose(1, 0, 2)
    return bw, cw, tab


def _ssd_body(x_ref, b_ref, c_ref, dt_ref, dta_ref, dtt_ref, dtat_ref, y_ref, st_ref, *, reverse):
    Q = SSD_CHUNK
    s = pl.program_id(1)

    @pl.when(s == 0)
    def _():
        st_ref[...] = jnp.zeros_like(st_ref)

    row = lax.broadcasted_iota(jnp.int32, (Q, Q), 0)
    col = lax.broadcasted_iota(jnp.int32, (Q, Q), 1)
    mask = (col >= row) if reverse else (col <= row)
    tri = jnp.where(mask, 1.0, 0.0).astype(BF16)
    head0 = lax.broadcasted_iota(jnp.int32, (Q, LANES), 1) < SSD_HD
    head0_rows = lax.broadcasted_iota(jnp.int32, (LANES, SSD_N), 0) < SSD_HD

    dt = dt_ref[...]
    dta = dta_ref[...]
    dtt = dtt_ref[...]
    dtat = dtat_ref[...]
    acum = _dot_ones_x(tri, dta)
    acum_t = _dot_x_ones_nt(dtat, tri)
    total = jnp.sum(dta, axis=0, keepdims=True)
    total_t = jnp.sum(dtat, axis=1, keepdims=True)
    e_cum = jnp.exp(acum)
    to_end = jnp.exp(total - acum) * dt
    e_tot_t = jnp.exp(total_t)
    heads_per_group = SSD_HEADS // SSD_GROUPS
    for g in range(SSD_GROUPS):
        bg = b_ref[:, g * SSD_N:(g + 1) * SSD_N]
        cg = c_ref[:, g * SSD_N:(g + 1) * SSD_N]
        cb = _dot_nt(cg, bg)
        for pr in range(heads_per_group // 2):
            h0 = g * heads_per_group + 2 * pr
            xp = x_ref[:, h0 * SSD_HD:(h0 + 2) * SSD_HD]
            yd = []
            for h in (h0, h0 + 1):
                seg = acum[:, h:h + 1] - acum_t[h:h + 1, :]
                w = cb * jnp.exp(jnp.where(mask, seg, -jnp.inf)) * dtt[h:h + 1, :]
                yd.append(_dot(w, xp))
            y_diag = jnp.where(head0, yd[0], yd[1])
            st = st_ref[h0 * SSD_HD:(h0 + 2) * SSD_HD, :]
            scale = jnp.where(head0, e_cum[:, h0:h0 + 1], e_cum[:, h0 + 1:h0 + 2])
            y_ref[:, h0 * SSD_HD:(h0 + 2) * SSD_HD] = y_diag + _dot_nt(cg, st) * scale
            te = jnp.where(head0, to_end[:, h0:h0 + 1], to_end[:, h0 + 1:h0 + 2])
            dec = jnp.where(head0_rows, e_tot_t[h0:h0 + 1, :], e_tot_t[h0 + 1:h0 + 2, :])
            st_ref[h0 * SSD_HD:(h0 + 2) * SSD_HD, :] = st * dec + _dot_tn(xp * te, bg)


def _ssd_scan(xs, bm, cm, dt, dta, *, n_batch, n_ctx_blk, n_x_blk, reverse):
    R = xs.shape[0]
    Q = SSD_CHUNK
    blk = functools.partial(_seq_block, n_ctx_blk=n_ctx_blk, n_x_blk=n_x_blk, n_batch=n_batch, reverse=reverse)
    ns = n_ctx_blk + n_x_blk
    row_spec = lambda w: pl.BlockSpec((Q, w), lambda b, s: (blk(b, s), 0))
    col_spec = pl.BlockSpec((SSD_HEADS, Q), lambda b, s: (0, blk(b, s)))
    gw = SSD_GROUPS * SSD_N
    return pl.pallas_call(
        functools.partial(_ssd_body, reverse=reverse),
        grid=(n_batch, ns),
        in_specs=[row_spec(SSD_W), row_spec(gw), row_spec(gw), row_spec(SSD_HEADS), row_spec(SSD_HEADS),
                  col_spec, col_spec],
        out_specs=row_spec(SSD_W),
        out_shape=jax.ShapeDtypeStruct((R, SSD_W), F32),
        scratch_shapes=[pltpu.VMEM((SSD_W, SSD_N), F32)],
        compiler_params=_cparams(("parallel", "arbitrary")),
        name="ssd_scan_bwd" if reverse else "ssd_scan_fwd",
    )(xs, bm, cm, dt, dta, dt.T, dta.T)


def _attn_body(q_ref, kp_ref, kc_ref, kn_ref, vp_ref, vc_ref, vn_ref, kx_ref, vx_ref, sink_ref, o_ref, *, seq_len):
    n = pl.program_id(1)
    blk = ATT_BLOCK
    n_ctx = kx_ref.shape[0]
    k_all = jnp.concatenate([kx_ref[...], kp_ref[...], kc_ref[...], kn_ref[...]], axis=0).astype(BF16)
    v_all = jnp.concatenate([vx_ref[...], vp_ref[...], vc_ref[...], vn_ref[...]], axis=0).astype(BF16)
    nk = n_ctx + 3 * blk
    q_pos = n * blk + lax.broadcasted_iota(jnp.int32, (blk, nk), 0)
    j = lax.broadcasted_iota(jnp.int32, (blk, nk), 1)
    k_pos = (n - 1) * blk + (j - n_ctx)
    valid = (j < n_ctx) | ((jnp.abs(q_pos - k_pos) <= WINDOW) & (k_pos >= 0) & (k_pos < seq_len))
    scale = ATT_HD ** -0.5
    rep = ATT_HEADS // KV_HEADS
    outs = []
    for hq in range(ATT_HEADS):
        hk = hq // rep
        qh = q_ref[:, hq * ATT_HD:(hq + 1) * ATT_HD]
        kh = k_all[:, hk * ATT_HD:(hk + 1) * ATT_HD]
        vh = v_all[:, hk * ATT_HD:(hk + 1) * ATT_HD]
        sc = jnp.where(valid, _dot_nt(qh, kh) * scale, -jnp.inf)
        sk = sink_ref[0:1, hq:hq + 1]
        m = jnp.maximum(jnp.max(sc, axis=-1, keepdims=True), sk)
        e = jnp.exp(sc - m)
        den = jnp.sum(e, axis=-1, keepdims=True) + jnp.exp(sk - m)
        outs.append(_dot(e / den, vh))
    o_ref[...] = jnp.concatenate(outs, axis=1)


def _window_attn(q, k, v, kx, vx, sink, *, n_batch, seq_len):
    nb = seq_len // ATT_BLOCK
    n_ctx = kx.shape[0] // n_batch
    cur = lambda b, n: (b * nb + n, 0)
    prev = lambda b, n: (b * nb + jnp.maximum(n - 1, 0), 0)
    nxt = lambda b, n: (b * nb + jnp.minimum(n + 1, nb - 1), 0)
    kv_spec = lambda im: pl.BlockSpec((ATT_BLOCK, ATT_KV_W), im)
    ctx_spec = pl.BlockSpec((n_ctx, ATT_KV_W), lambda b, n: (b, 0))
    return pl.pallas_call(
        functools.partial(_attn_body, seq_len=seq_len),
        grid=(n_batch, nb),
        in_specs=[pl.BlockSpec((ATT_BLOCK, ATT_Q_W), cur),
                  kv_spec(prev), kv_spec(cur), kv_spec(nxt), kv_spec(prev), kv_spec(cur), kv_spec(nxt),
                  ctx_spec, ctx_spec,
                  pl.BlockSpec((1, ATT_HEADS), lambda b, n: (0, 0))],
        out_specs=pl.BlockSpec((ATT_BLOCK, ATT_Q_W), cur),
        out_shape=jax.ShapeDtypeStruct(q.shape, F32),
        compiler_params=_cparams(("parallel", "parallel")),
        name="window_attn",
    )(q, k, k, k, v, v, v, kx, vx, sink)


def _segment_pos(R, RC, C, L):
    row = jnp.arange(R, dtype=jnp.int32)
    in_ctx = row < RC
    pos = jnp.where(in_ctx, row % C, (row - RC) % L)
    seg = jnp.where(in_ctx, C, L)
    return pos, seg


def _shift_rows(p, k, pos, seg):
    ok = (pos + k >= 0) & (pos + k < seg)
    return jnp.where(ok[:, None], jnp.roll(p, -k, axis=0), 0.0)


def _head_sum(t):
    R = t.shape[0]
    s = jnp.sum(t.reshape(R, RWKV_HEADS, RWKV_HD), axis=-1, keepdims=True)
    return jnp.broadcast_to(s, (R, RWKV_HEADS, RWKV_HD)).reshape(R, RWKV_W)


def _rope_tables(L):
    nf = ATT_HD // 4
    inv_freq = ROPE_BASE ** (-jnp.arange(nf, dtype=F32) / nf)
    t = jnp.arange(L, dtype=jnp.int32)
    row_id = (t // GRID_W).astype(F32)
    col_id = (t % GRID_W).astype(F32)
    ar = row_id[:, None] * inv_freq
    ac = col_id[:, None] * inv_freq
    cos = jnp.concatenate([jnp.cos(ar), jnp.cos(ar), jnp.cos(ac), jnp.cos(ac)], axis=1)
    sin = jnp.concatenate([-jnp.sin(ar), jnp.sin(ar), -jnp.sin(ac), jnp.sin(ac)], axis=1)
    return cos, sin


def _rope(t, cos, sin, n_heads):
    R = t.shape[0]
    q = ATT_HD // 4
    tr = t.reshape(R, n_heads, 2, 2, q)
    swapped = jnp.stack([tr[..., 1, :], tr[..., 0, :]], axis=-2).reshape(R, n_heads, ATT_HD)
    out = t.reshape(R, n_heads, ATT_HD) * cos[:, None, :] + swapped * sin[:, None, :]
    return out.reshape(R, n_heads * ATT_HD)


def kernel(x, c, ctx, c_ctx, ada_w, ada_b, norm1_g, norm2_g, mlp_w1, mlp_w2, final_g, ab_w_in, ab_w_out, rwkv_mu, rwkv_w0, rwkv_w_up, rwkv_a0, rwkv_a_up, rwkv_g_up, rwkv_k_k, rwkv_k_a, rwkv_r_k, rwkv_ln_g, rwkv_ln_b, s5_A_re, s5_A_im, s5_log_dt, s5_B_re, s5_B_im, s5_C_re, s5_C_im, s5_D, s5_glu_w, s5_glu_b, cd_w_in, cd_w_out, ssd_conv_w, ssd_conv_b, ssd_A_log, ssd_dt_bias, ssd_D, ssd_norm_g, attn_sink):
    B, L, D = x.shape
    C = ctx.shape[1]
    RC, RX = B * C, B * L
    R = RC + RX
    assert C % SEQ_BLOCK == 0 and L % SEQ_BLOCK == 0
    geo = dict(n_batch=B, n_ctx_blk=C // SEQ_BLOCK, n_x_blk=L // SEQ_BLOCK)
    geo_ssd = dict(n_batch=B, n_ctx_blk=C // SSD_CHUNK, n_x_blk=L // SSD_CHUNK)
    pos, seg = _segment_pos(R, RC, C, L)

    group_all = lambda r0: jnp.where(r0 < RC, 0, 1 + (r0 - RC) // L)
    group_x = lambda r0: 1 + r0 // L

    xs = jnp.concatenate([ctx.reshape(RC, D), x.reshape(RX, D)], axis=0)
    cond = jax.nn.silu(jnp.concatenate([c_ctx[None, :], c], axis=0))
    cond = jnp.pad(cond, ((0, SUBLANES - (B + 1) % SUBLANES), (0, 0))) if (B + 1) % SUBLANES else cond
    n_groups = cond.shape[0]

    def modulation(i):
        mod = _mm(cond, ada_w[i], name="ada") + ada_b[i]
        return [m.reshape(n_groups, 1, D) for m in jnp.split(mod, 6, axis=-1)]

    def mlp(xin, i, sh2, sc2, g2, gidx):
        h2 = _norm_mod(xin, norm2_g[i], sh2, sc2, gidx)
        m1 = _mm(h2, mlp_w1[i].astype(BF16), out_dtype=BF16, act="relu2", name="mlp_up")
        return _mm(m1, mlp_w2[i].astype(BF16), res=xin, gate=g2, gate_idx=gidx, name="mlp_down")

    sh1, sc1, g1, sh2, sc2, g2 = modulation(0)
    h = _norm_mod(xs, norm1_g[0], sh1, sc1, group_all)
    pad_cols = (-RWKV_COLS) % LANES
    w_in = ab_w_in[0]
    w_in = jnp.concatenate([w_in[:, :RWKV_COLS], jnp.zeros((D, pad_cols), F32), w_in[:, RWKV_COLS:]], axis=1)
    s5_col0 = RWKV_COLS + pad_cols
    p = _mm(h, w_in.astype(BF16), name="ab_in")

    mu = rwkv_mu[0]
    pr = p[:, :RWKV_COLS]
    ps = pr + mu[0] * (_shift_rows(pr, -1, pos, seg) - pr) + mu[1] * (_shift_rows(pr, 1, pos, seg) - pr)
    W = RWKV_W
    r, k, v = ps[:, :W], ps[:, W:2 * W], ps[:, 2 * W:3 * W]
    xw = ps[:, 3 * W:3 * W + DECAY_LORA]
    xa = ps[:, 3 * W + DECAY_LORA:3 * W + DECAY_LORA + AAA_LORA]
    xg = ps[:, 3 * W + DECAY_LORA + AAA_LORA:]
    g_gate = _mm(jax.nn.sigmoid(xg), rwkv_g_up[0].astype(BF16), name="rwkv_gate")
    lora_in = jnp.concatenate([jnp.tanh(xw), xa], axis=1)
    lora_pad = (-lora_in.shape[1]) % LANES
    lora_in = jnp.pad(lora_in, ((0, 0), (0, lora_pad)))
    zw = jnp.zeros((DECAY_LORA, W), F32)
    lora_w = jnp.concatenate([
        jnp.concatenate([rwkv_w_up[0, 0], rwkv_w_up[0, 1], zw, zw], axis=1),
        jnp.concatenate([zw, zw, rwkv_a_up[0, 0], rwkv_a_up[0, 1]], axis=1),
        jnp.zeros((lora_pad, 4 * W), F32)], axis=0)
    lora = _mm(lora_in, lora_w.astype(BF16), name="rwkv_lora")
    kk = k * rwkv_k_k[0]
    kk = kk / jnp.maximum(jnp.sqrt(_head_sum(kk * kk)), 1e-12)
    r_k = rwkv_r_k[0].reshape(W)
    y_dirs, bonus = [], 0.0
    for d in range(2):
        w_log = -jax.nn.softplus(-(rwkv_w0[0, d] + lora[:, d * W:(d + 1) * W])) - 0.5
        lw = -jnp.exp(w_log)
        a = jax.nn.sigmoid(rwkv_a0[0, d] + lora[:, (2 + d) * W:(3 + d) * W])
        kd = k * (1 + (a - 1) * rwkv_k_a[0])
        y_dirs.append(_rwkv_scan(r, v, kk, lw, kd, a * kk, reverse=(d == 1), **geo))
        bonus = bonus + _head_sum(r * kd * r_k) * v
    y = y_dirs[0] + y_dirs[1]
    mean = _head_sum(y) / RWKV_HD
    var = _head_sum(jnp.square(y - mean)) / RWKV_HD
    yn = (y - mean) * lax.rsqrt(var + GN_EPS)
    ya = (yn * rwkv_ln_g[0] + rwkv_ln_b[0] + bonus) * g_gate

    u = p[:, s5_col0:]
    ys = s5_D[0] * u
    for d in range(2):
        bw, cw, tab = _s5_weights(s5_A_re[0, d], s5_A_im[0, d], s5_log_dt[0, d], s5_B_re[0, d], s5_B_im[0, d],
                                  s5_C_re[0, d], s5_C_im[0, d], reverse=(d == 1))
        ys = ys + _s5_scan(p, s5_col0 // LANES, bw, cw, tab, reverse=(d == 1), **geo)
    ys = jax.nn.gelu(ys)
    yb = ys * jax.nn.sigmoid(_mm(ys, s5_glu_w[0].astype(BF16), name="s5_glu") + s5_glu_b[0])

    mix = jnp.concatenate([ya, yb], axis=1).astype(BF16)
    xs = _mm(mix, ab_w_out[0].astype(BF16), res=xs, gate=g1, gate_idx=group_all, name="ab_out")
    xs = mlp(xs, 0, sh2, sc2, g2, group_all)

    sh1, sc1, g1, sh2, sc2, g2 = modulation(1)
    h = _norm_mod(xs, norm1_g[1], sh1, sc1, group_all)
    cd_in = cd_w_in[0]
    cd_cols = cd_in.shape[1]
    cd_pad = (-cd_cols) % 512
    p = _mm(h, jnp.pad(cd_in, ((0, 0), (0, cd_pad))).astype(BF16), name="cd_in")
    o = 0
    z = p[:, o:o + SSD_W]; o += SSD_W
    xbc = p[:, o:o + XBC_W]; o += XBC_W
    dtr = p[:, o:o + 2 * SSD_HEADS]; o += 2 * SSD_HEADS
    q = p[RC:, o:o + ATT_Q_W]; o += ATT_Q_W
    kkv = p[:, o:o + ATT_KV_W]; o += ATT_KV_W
    vv = p[:, o:o + ATT_KV_W]

    cw_ = ssd_conv_w[0]
    half = cw_.shape[0] // 2
    conv = ssd_conv_b[0] + sum(cw_[j] * _shift_rows(xbc, j - half, pos, seg) for j in range(cw_.shape[0]))
    xbc = jax.nn.silu(conv)
    xs_c = xbc[:, :SSD_W]
    bm = xbc[:, SSD_W:SSD_W + SSD_GROUPS * SSD_N]
    cm = xbc[:, SSD_W + SSD_GROUPS * SSD_N:]
    dt = jax.nn.softplus(dtr.reshape(R, 2, SSD_HEADS) + ssd_dt_bias[0])
    A = -jnp.exp(ssd_A_log[0])
    y = jnp.repeat(ssd_D[0], SSD_HD) * xs_c[RC:]
    for d in range(2):
        y = y + _ssd_scan(xs_c, bm, cm, dt[:, d], dt[:, d] * A[d], reverse=(d == 1), **geo_ssd)[RC:]
    y = y * jax.nn.silu(z[RC:])
    yg = y.reshape(RX, SSD_GROUPS, SSD_W // SSD_GROUPS)
    yg = yg * lax.rsqrt(jnp.mean(yg * yg, axis=-1, keepdims=True) + EPS)
    y_c = yg.reshape(RX, SSD_W) * ssd_norm_g[0]

    cos, sin = _rope_tables(L)
    cos = jnp.tile(cos, (B, 1))
    sin = jnp.tile(sin, (B, 1))
    q = _rope(q, cos, sin, ATT_HEADS)
    k_rot = _rope(kkv[RC:], cos, sin, KV_HEADS)
    y_d = _window_attn(q, k_rot, vv[RC:], kkv[:RC], vv[:RC], attn_sink[0].reshape(1, ATT_HEADS),
                       n_batch=B, seq_len=L)

    mix = jnp.concatenate([y_c, y_d], axis=1).astype(BF16)
    xo = _mm(mix, cd_w_out[0].astype(BF16), res=xs[RC:], gate=g1, gate_idx=group_x, name="cd_out")
    xo = mlp(xo, 1, sh2, sc2, g2, group_x)
    zero = jnp.zeros((1, 1, D), F32)
    out = _norm_mod(xo, final_g, zero, zero, lambda r0: 0, out_dtype=F32)
    return out.reshape(B, L, D)
```

```python
import functools
import math

import jax
import jax.numpy as jnp
from jax import lax
from jax.experimental import pallas as pl
from jax.experimental.pallas import tpu as pltpu

F32 = jnp.float32
BF16 = jnp.bfloat16

EPS = 1e-6
GN_EPS = 64e-5
GRID_W = 64
ROPE_BASE = 10000.0

RWKV_HEADS = 16
RWKV_HD = 64
RWKV_W = RWKV_HEADS * RWKV_HD
DECAY_LORA = 96
AAA_LORA = 96
GATE_LORA = 256
RWKV_COLS = 3 * RWKV_W + DECAY_LORA + AAA_LORA + GATE_LORA
S5_W = 1024
S5_P = 16
S5_G = S5_W // S5_P
S5_N = 64
SSD_HEADS = 32
SSD_HD = 64
SSD_W = SSD_HEADS * SSD_HD
SSD_GROUPS = 4
SSD_N = 128
SSD_CHUNK = 128
XBC_W = SSD_W + 2 * SSD_GROUPS * SSD_N
ATT_HEADS = 16
KV_HEADS = 4
ATT_HD = 64
ATT_Q_W = ATT_HEADS * ATT_HD
ATT_KV_W = KV_HEADS * ATT_HD
WINDOW = 128
ATT_BLOCK = 128

LANES = 128
SUBLANES = 8
SEQ_BLOCK = 256
RWKV_CHUNK = 64
S5_GB = 8
VMEM_LIMIT = 48 * 1024 * 1024


def _cparams(sem):
    return pltpu.CompilerParams(dimension_semantics=sem, vmem_limit_bytes=VMEM_LIMIT)


def _pick(n, cands):
    for c in cands:
        if n % c == 0:
            return c
    return n


def _dot(a, b):
    return jnp.dot(a.astype(BF16), b.astype(BF16), preferred_element_type=F32)


def _dot_nt(a, b):
    return lax.dot_general(a.astype(BF16), b.astype(BF16), (((1,), (1,)), ((), ())), preferred_element_type=F32)


def _dot_tn(a, b):
    return lax.dot_general(a.astype(BF16), b.astype(BF16), (((0,), (0,)), ((), ())), preferred_element_type=F32)


def _split3(x):
    hi = x.astype(BF16)
    r1 = x - hi.astype(F32)
    mid = r1.astype(BF16)
    lo = (r1 - mid.astype(F32)).astype(BF16)
    return hi, mid, lo


def _dot_ones_x(tri, x):
    hi, mid, lo = _split3(x)
    f = lambda t: jnp.dot(tri, t, preferred_element_type=F32)
    return f(hi) + f(mid) + f(lo)


def _dot_x_ones_nt(x, tri):
    hi, mid, lo = _split3(x)
    f = lambda t: lax.dot_general(t, tri, (((1,), (1,)), ((), ())), preferred_element_type=F32)
    return f(hi) + f(mid) + f(lo)


def _mm_body(*refs, nk, act, residual):
    if residual:
        a_ref, b_ref, res_ref, gate_ref, o_ref, acc_ref = refs
    else:
        a_ref, b_ref, o_ref, acc_ref = refs
    k = pl.program_id(2)
    part = jnp.dot(a_ref[...].astype(BF16), b_ref[...].astype(BF16), preferred_element_type=F32)

    def finish(acc):
        if act == "relu2":
            acc = jnp.square(jnp.maximum(acc, 0.0))
        if residual:
            acc = res_ref[...] + gate_ref[0] * acc
        o_ref[...] = acc.astype(o_ref.dtype)

    if nk == 1:
        finish(part)
    else:
        @pl.when(k == 0)
        def _():
            acc_ref[...] = part

        @pl.when(k > 0)
        def _():
            acc_ref[...] += part

        @pl.when(k == nk - 1)
        def _():
            finish(acc_ref[...])


def _mm(a, b, *, out_dtype=F32, act=None, res=None, gate=None, gate_idx=None, name="mm"):
    M, K = a.shape
    N = b.shape[1]
    tm = _pick(M, (1024, 512, 256, 128, 64, 32, 16, 8))
    tn = _pick(N, (512, 384, 256, 128))
    tk = _pick(K, (2048, 1536, 1024, 512, 256, 128))
    nk = K // tk
    residual = res is not None
    in_specs = [pl.BlockSpec((tm, tk), lambda i, j, k: (i, k)),
                pl.BlockSpec((tk, tn), lambda i, j, k: (k, j))]
    args = [a, b]
    if residual:
        in_specs += [pl.BlockSpec((tm, tn), lambda i, j, k: (i, j)),
                     pl.BlockSpec((1, 1, tn), lambda i, j, k: (gate_idx(i * tm), 0, j))]
        args += [res, gate]
    return pl.pallas_call(
        functools.partial(_mm_body, nk=nk, act=act, residual=residual),
        grid=(M // tm, N // tn, nk),
        in_specs=in_specs,
        out_specs=pl.BlockSpec((tm, tn), lambda i, j, k: (i, j)),
        out_shape=jax.ShapeDtypeStruct((M, N), out_dtype),
        scratch_shapes=[pltpu.VMEM((tm, tn), F32)],
        compiler_params=_cparams(("parallel", "parallel", "arbitrary")),
        name=name,
    )(*args)


def _norm_body(x_ref, g_ref, sh_ref, sc_ref, o_ref):
    x = x_ref[...]
    ms = jnp.mean(x * x, axis=-1, keepdims=True)
    xn = x * lax.rsqrt(ms + EPS) * g_ref[...]
    o_ref[...] = (xn * (1.0 + sc_ref[0]) + sh_ref[0]).astype(o_ref.dtype)


def _norm_mod(x, g, shift, scale, group_idx, out_dtype=BF16):
    M, D = x.shape
    tm = _pick(M, (256, 128, 64, 32, 16, 8))
    return pl.pallas_call(
        _norm_body,
        grid=(M // tm,),
        in_specs=[pl.BlockSpec((tm, D), lambda i: (i, 0)),
                  pl.BlockSpec((1, D), lambda i: (0, 0)),
                  pl.BlockSpec((1, 1, D), lambda i: (group_idx(i * tm), 0, 0)),
                  pl.BlockSpec((1, 1, D), lambda i: (group_idx(i * tm), 0, 0))],
        out_specs=pl.BlockSpec((tm, D), lambda i: (i, 0)),
        out_shape=jax.ShapeDtypeStruct((M, D), out_dtype),
        compiler_params=_cparams(("parallel",)),
        name="norm_mod",
    )(x, g.reshape(1, D), shift, scale)


def _seq_block(b, s, *, n_ctx_blk, n_x_blk, n_batch, reverse):
    if reverse:
        c = b * n_ctx_blk + (n_ctx_blk - 1 - s)
        x = n_batch * n_ctx_blk + b * n_x_blk + (n_x_blk - 1 - (s - n_ctx_blk))
    else:
        c = b * n_ctx_blk + s
        x = n_batch * n_ctx_blk + b * n_x_blk + (s - n_ctx_blk)
    return jnp.where(s < n_ctx_blk, c, x)


def _split2(x):
    hi = x.astype(BF16)
    lo = (x - hi.astype(F32)).astype(BF16)
    return hi, lo


def _rwkv_chunk(lw, r, v, kk, kd, be, st, tri, strict, incl, same_head, head0, eye):
    C = RWKV_CHUNK

    def stack(z):
        return jnp.concatenate([jnp.where(head0, z, 0.0), jnp.where(head0, 0.0, z)], axis=0)

    def unstack(z):
        return z[0:C] + z[C:2 * C]

    bdot = lambda a, b: jnp.dot(a, b, preferred_element_type=F32)
    bdot_nt = lambda a, b: lax.dot_general(a, b, (((1,), (1,)), ((), ())), preferred_element_type=F32)
    bdot_tn = lambda a, b: lax.dot_general(a, b, (((0,), (0,)), ((), ())), preferred_element_type=F32)
    cat = lambda *zs: jnp.concatenate(zs, axis=0)
    P = range(len(lw))
    cum = [_dot_ones_x(tri, lw[p]) for p in P]
    total = [jnp.sum(lw[p], axis=0, keepdims=True) for p in P]
    a_t = [kk[p] * jnp.exp(cum[p] - lw[p]) for p in P]
    r_t = [r[p] * jnp.exp(cum[p]) for p in P]
    a_sp = [_split2(a_t[p]) for p in P]
    r_hi = [r_t[p].astype(BF16) for p in P]
    sa_sp = [_split2(stack(a_t[p])) for p in P]
    sr_hi = [stack(r_t[p]).astype(BF16) for p in P]
    pinv = [jnp.exp(-cum[p]) for p in P]
    bk_sp = [_split2(cat(stack(be[p] * pinv[p]), stack(kd[p] * pinv[p]))) for p in P]
    g1 = [bdot_nt(cat(sa_sp[p][0], sa_sp[p][1], sr_hi[p]), bk_sp[p][0]) for p in P]
    g2 = [bdot_nt(sa_sp[p][0], bk_sp[p][1]) for p in P]
    n_all = [g1[p][0:2 * C] + g1[p][2 * C:4 * C] + g2[p] for p in P]
    x_neg = [jnp.where(strict, -n_all[p][:, 0:2 * C], 0.0) for p in P]
    n_ak = [jnp.where(strict, n_all[p][:, 2 * C:4 * C], 0.0) for p in P]
    m_r = [jnp.concatenate([jnp.where(incl, g1[p][4 * C:6 * C, 0:2 * C], 0.0),
                            jnp.where(incl, g1[p][4 * C:6 * C, 2 * C:4 * C], 0.0)], axis=1).astype(BF16)
           for p in P]
    xb = [x_neg[p].astype(BF16) for p in P]
    t_inv = [eye + x_neg[p] for p in P]
    n_pow = int(math.log2(C))
    for it in range(1, n_pow):
        xb = [bdot(xb[p], xb[p]).astype(BF16) for p in P]
        t_inv = [t_inv[p] + bdot(xb[p], t_inv[p].astype(BF16)) for p in P]
    t_inv = [t_inv[p].astype(BF16) for p in P]
    st_sp = [_split2(st[p]) for p in P]
    as1 = [bdot_nt(cat(a_sp[p][0], a_sp[p][1], r_hi[p]), st_sp[p][0]) for p in P]
    as2 = [bdot_nt(a_sp[p][0], st_sp[p][1]) for p in P]
    v_sp = [_split2(stack(v[p])) for p in P]
    nk_sp = [_split2(n_ak[p]) for p in P]
    nv1 = [bdot(cat(nk_sp[p][0], nk_sp[p][1]), v_sp[p][0]) for p in P]
    nv2 = [bdot(nk_sp[p][0], v_sp[p][1]) for p in P]
    rhs = [-(stack(as1[p][0:C] + as1[p][C:2 * C] + as2[p]) + nv1[p][0:2 * C] + nv1[p][2 * C:4 * C] + nv2[p])
           for p in P]
    u0 = [bdot(t_inv[p], rhs[p].astype(BF16)) for p in P]
    u0_sp = [_split2(u0[p]) for p in P]
    x_sp = [_split2(x_neg[p]) for p in P]
    xu1 = [bdot(cat(x_sp[p][0], x_sp[p][1]), u0_sp[p][0]) for p in P]
    xu2 = [bdot(x_sp[p][0], u0_sp[p][1]) for p in P]
    res = [(rhs[p] - u0[p] + xu1[p][0:2 * C] + xu1[p][2 * C:4 * C] + xu2[p]).astype(BF16) for p in P]
    u = [u0[p] + bdot(t_inv[p], res[p]) for p in P]
    y = [stack(as1[p][2 * C:3 * C]) + bdot(m_r[p], cat(u[p].astype(BF16), v_sp[p][0])) for p in P]
    uv_sp = [_split2(cat(unstack(u[p]), v[p])) for p in P]
    pend = [jnp.exp(total[p] - cum[p]) for p in P]
    e_sp = [_split2(cat(be[p] * pend[p], kd[p] * pend[p])) for p in P]
    upd = [bdot_tn(cat(uv_sp[p][0], uv_sp[p][1], uv_sp[p][0]), cat(e_sp[p][0], e_sp[p][0], e_sp[p][1]))
           for p in P]
    st_new = [st[p] * jnp.exp(total[p]) + jnp.where(same_head, upd[p], 0.0) for p in P]
    return [unstack(y[p]) for p in P], st_new


def _rwkv_body(r_ref, v_ref, kk_ref, lw_ref, kd_ref, be_ref, y_ref, st_ref, *, reverse, npair):
    C = RWKV_CHUNK
    nchunk = SEQ_BLOCK // C
    s = pl.program_id(2)

    @pl.when(s == 0)
    def _():
        st_ref[...] = jnp.zeros_like(st_ref)

    sgn = -1 if reverse else 1
    row = lax.broadcasted_iota(jnp.int32, (2 * C, 2 * C), 0)
    col = lax.broadcasted_iota(jnp.int32, (2 * C, 2 * C), 1)
    diff = (row - col) * sgn
    strict = diff > 0
    incl = diff >= 0
    same_head = (row // C) == (col // C)
    eye = jnp.where(row == col, 1.0, 0.0)
    r64 = lax.broadcasted_iota(jnp.int32, (C, C), 0)
    c64 = lax.broadcasted_iota(jnp.int32, (C, C), 1)
    tri = jnp.where((r64 - c64) * sgn >= 0, 1.0, 0.0).astype(BF16)
    head0 = lax.broadcasted_iota(jnp.int32, (C, 2 * C), 1) < C

    def chunk(ci, carry):
        ce = (nchunk - 1 - ci) if reverse else ci
        sl = pl.ds(pl.multiple_of(ce * C, C), C)
        lanes = [slice(p * LANES, (p + 1) * LANES) for p in range(npair)]
        get = lambda ref: [ref[sl, ln] for ln in lanes]
        ys, sts = _rwkv_chunk(get(lw_ref), get(r_ref), get(v_ref), get(kk_ref), get(kd_ref), get(be_ref),
                              [st_ref[p] for p in range(npair)], tri, strict, incl, same_head, head0, eye)
        for p in range(npair):
            y_ref[sl, lanes[p]] = ys[p]
            st_ref[p] = sts[p]
        return carry

    lax.fori_loop(0, nchunk, chunk, 0)


def _rwkv_scan(r, v, kk, lw, kd, be, *, n_batch, n_ctx_blk, n_x_blk, reverse, npair=8):
    R, W = r.shape
    ngrp = W // (LANES * npair)
    ns = n_ctx_blk + n_x_blk
    blk = functools.partial(_seq_block, n_ctx_blk=n_ctx_blk, n_x_blk=n_x_blk, n_batch=n_batch, reverse=reverse)
    spec = pl.BlockSpec((SEQ_BLOCK, LANES * npair), lambda b, p, s: (blk(b, s), p))
    return pl.pallas_call(
        functools.partial(_rwkv_body, reverse=reverse, npair=npair),
        grid=(n_batch, ngrp, ns),
        in_specs=[spec] * 6,
        out_specs=spec,
        out_shape=jax.ShapeDtypeStruct((R, W), F32),
        scratch_shapes=[pltpu.VMEM((npair, LANES, LANES), F32)],
        compiler_params=_cparams(("parallel", "parallel", "arbitrary")),
        name="rwkv_scan_bwd" if reverse else "rwkv_scan_fwd",
    )(r, v, kk, lw, kd, be)


def _s5_body(u_ref, bw_ref, cw_ref, tab_ref, y_ref, hb_ref, carry_ref, *, reverse):
    W = S5_GB * S5_N
    s = pl.program_id(2)

    @pl.when(s == 0)
    def _():
        carry_ref[...] = jnp.zeros_like(carry_ref)

    hb_ref[...] = _dot(u_ref[...], bw_ref[...])
    pw_re = tab_ref[0:8, :]
    pw_im = tab_ref[8:16, :]
    steps = [(k, tab_ref[16 + 2 * i:17 + 2 * i, :], tab_ref[17 + 2 * i:18 + 2 * i, :])
             for i, k in enumerate((1, 2, 4))]
    rows8 = lax.broadcasted_iota(jnp.int32, (SUBLANES, W), 0)
    nt = SEQ_BLOCK // SUBLANES

    def shifted(h, k):
        if reverse:
            return jnp.where(rows8 < SUBLANES - k, pltpu.roll(h, SUBLANES - k, axis=0), 0.0)
        return jnp.where(rows8 >= k, pltpu.roll(h, k, axis=0), 0.0)

    def tile(j, carry):
        cr, ci = carry
        jj = (nt - 1 - j) if reverse else j
        rows = pl.ds(pl.multiple_of(jj * SUBLANES, SUBLANES), SUBLANES)
        hr = hb_ref[rows, 0:W]
        hi = hb_ref[rows, W:2 * W]
        for k, ar, ai in steps:
            sr = shifted(hr, k)
            si = shifted(hi, k)
            hr, hi = hr + ar * sr - ai * si, hi + ar * si + ai * sr
        hr, hi = hr + pw_re * cr - pw_im * ci, hi + pw_re * ci + pw_im * cr
        hb_ref[rows, 0:W] = hr
        hb_ref[rows, W:2 * W] = hi
        last = 0 if reverse else SUBLANES - 1
        cr = jnp.broadcast_to(hr[last:last + 1, :], (SUBLANES, W))
        ci = jnp.broadcast_to(hi[last:last + 1, :], (SUBLANES, W))
        return cr, ci

    cr, ci = lax.fori_loop(0, nt, tile, (carry_ref[:, 0:W], carry_ref[:, W:2 * W]))
    carry_ref[:, 0:W] = cr
    carry_ref[:, W:2 * W] = ci
    y_ref[...] = _dot(hb_ref[...], cw_ref[...])


def _s5_scan(p, col0_blk, bw, cw, tab, *, n_batch, n_ctx_blk, n_x_blk, reverse):
    R = p.shape[0]
    ngb = S5_G // S5_GB
    ns = n_ctx_blk + n_x_blk
    blk = functools.partial(_seq_block, n_ctx_blk=n_ctx_blk, n_x_blk=n_x_blk, n_batch=n_batch, reverse=reverse)
    W2 = 2 * S5_GB * S5_N
    return pl.pallas_call(
        functools.partial(_s5_body, reverse=reverse),
        grid=(n_batch, ngb, ns),
        in_specs=[pl.BlockSpec((SEQ_BLOCK, LANES), lambda b, g, s: (blk(b, s), col0_blk + g)),
                  pl.BlockSpec((None, LANES, W2), lambda b, g, s: (g, 0, 0)),
                  pl.BlockSpec((None, W2, LANES), lambda b, g, s: (g, 0, 0)),
                  pl.BlockSpec((None, 24, W2 // 2), lambda b, g, s: (g, 0, 0))],
        out_specs=pl.BlockSpec((SEQ_BLOCK, LANES), lambda b, g, s: (blk(b, s), g)),
        out_shape=jax.ShapeDtypeStruct((R, S5_W), F32),
        scratch_shapes=[pltpu.VMEM((SEQ_BLOCK, W2), F32), pltpu.VMEM((SUBLANES, W2), F32)],
        compiler_params=_cparams(("parallel", "parallel", "arbitrary")),
        name="s5_scan_bwd" if reverse else "s5_scan_fwd",
    )(p, bw, cw, tab)


def _s5_weights(A_re, A_im, log_dt, B_re, B_im, C_re, C_im, reverse):
    lam_re = jnp.minimum(A_re, -1e-4)
    dt = jnp.exp(log_dt)[:, None]
    mag = jnp.exp(lam_re * dt)
    ab_re, ab_im = mag * jnp.cos(A_im * dt), mag * jnp.sin(A_im * dt)
    den = lam_re * lam_re + A_im * A_im
    f_re = ((ab_re - 1) * lam_re + ab_im * A_im) / den
    f_im = (ab_im * lam_re - (ab_re - 1) * A_im) / den
    bb_re = f_re[..., None] * B_re - f_im[..., None] * B_im
    bb_im = f_re[..., None] * B_im + f_im[..., None] * B_re
    ngb = S5_G // S5_GB
    eye = jnp.eye(S5_GB, dtype=F32)

    def in_block(bb):
        t = bb.reshape(ngb, S5_GB, S5_N, S5_P)
        return jnp.einsum("agnp,gh->agphn", t, eye).reshape(ngb, S5_GB * S5_P, S5_GB * S5_N)

    def out_block(cc):
        t = cc.reshape(ngb, S5_GB, S5_P, S5_N)
        return jnp.einsum("agpn,gh->agnhp", t, eye).reshape(ngb, S5_GB * S5_N, S5_GB * S5_P)

    bw = jnp.concatenate([in_block(bb_re), in_block(bb_im)], axis=2).astype(BF16)
    cw = jnp.concatenate([out_block(C_re), -out_block(C_im)], axis=1).astype(BF16)

    def power(k):
        m = jnp.exp(k * lam_re * dt)
        return m * jnp.cos(k * A_im * dt), m * jnp.sin(k * A_im * dt)

    carry_pow = [power(float(SUBLANES - i if reverse else i + 1)) for i in range(SUBLANES)]
    rows = [c[0] for c in carry_pow] + [c[1] for c in carry_pow]
    for k in (1.0, 2.0, 4.0):
        rows += list(power(k))
    tab = jnp.stack(rows, axis=0)
    tab = jnp.pad(tab, ((0, 2), (0, 0), (0, 0)))
    tab = tab.reshape(24, ngb, S5_GB * S5_N).transpose(1, 0, 2)
    return bw, cw, tab


def _ssd_body(x_ref, b_ref, c_ref, dt_ref, dta_ref, y_ref, st_ref, *, reverse):
    Q = SSD_CHUNK
    s = pl.program_id(1)

    @pl.when(s == 0)
    def _():
        st_ref[...] = jnp.zeros_like(st_ref)

    row = lax.broadcasted_iota(jnp.int32, (Q, Q), 0)
    col = lax.broadcasted_iota(jnp.int32, (Q, Q), 1)
    mask = (col >= row) if reverse else (col <= row)
    tri = jnp.where(mask, 1.0, 0.0).astype(BF16)
    head0 = lax.broadcasted_iota(jnp.int32, (Q, LANES), 1) < SSD_HD
    head0_rows = lax.broadcasted_iota(jnp.int32, (LANES, SSD_N), 0) < SSD_HD

    h_lo = SSD_HEADS if reverse else 0
    dt = dt_ref[:, h_lo:h_lo + SSD_HEADS]
    dta = dta_ref[:, h_lo:h_lo + SSD_HEADS]
    dtt = dt_ref[...].T[h_lo:h_lo + SSD_HEADS, :]
    dtat = dta_ref[...].T[h_lo:h_lo + SSD_HEADS, :]
    acum = _dot_ones_x(tri, dta)
    acum_t = _dot_x_ones_nt(dtat, tri)
    total = jnp.sum(dta, axis=0, keepdims=True)
    total_t = jnp.sum(dtat, axis=1, keepdims=True)
    e_cum = jnp.exp(acum)
    to_end = jnp.exp(total - acum) * dt
    e_tot_t = jnp.exp(total_t)
    heads_per_group = SSD_HEADS // SSD_GROUPS
    for g in range(SSD_GROUPS):
        bg = b_ref[:, g * SSD_N:(g + 1) * SSD_N]
        cg = c_ref[:, g * SSD_N:(g + 1) * SSD_N]
        cb = _dot_nt(cg, bg)
        for pr in range(heads_per_group // 2):
            h0 = g * heads_per_group + 2 * pr
            xp = x_ref[:, h0 * SSD_HD:(h0 + 2) * SSD_HD]
            yd = []
            for h in (h0, h0 + 1):
                seg = acum[:, h:h + 1] - acum_t[h:h + 1, :]
                w = cb * jnp.exp(jnp.where(mask, seg, -jnp.inf)) * dtt[h:h + 1, :]
                yd.append(_dot(w, xp))
            y_diag = jnp.where(head0, yd[0], yd[1])
            st = st_ref[h0 * SSD_HD:(h0 + 2) * SSD_HD, :]
            scale = jnp.where(head0, e_cum[:, h0:h0 + 1], e_cum[:, h0 + 1:h0 + 2])
            y_ref[:, h0 * SSD_HD:(h0 + 2) * SSD_HD] = y_diag + _dot_nt(cg, st) * scale
            te = jnp.where(head0, to_end[:, h0:h0 + 1], to_end[:, h0 + 1:h0 + 2])
            dec = jnp.where(head0_rows, e_tot_t[h0:h0 + 1, :], e_tot_t[h0 + 1:h0 + 2, :])
            st_ref[h0 * SSD_HD:(h0 + 2) * SSD_HD, :] = st * dec + _dot_tn(xp * te, bg)


def _ssd_scan(xbc, dt, dta, *, n_batch, n_ctx_blk, n_x_blk, reverse):
    R = xbc.shape[0]
    Q = SSD_CHUNK
    blk = functools.partial(_seq_block, n_ctx_blk=n_ctx_blk, n_x_blk=n_x_blk, n_batch=n_batch, reverse=reverse)
    ns = n_ctx_blk + n_x_blk
    gw = SSD_GROUPS * SSD_N
    row_spec = lambda w, c: pl.BlockSpec((Q, w), lambda b, s: (blk(b, s), c))
    return pl.pallas_call(
        functools.partial(_ssd_body, reverse=reverse),
        grid=(n_batch, ns),
        in_specs=[row_spec(SSD_W, 0), row_spec(gw, SSD_W // gw), row_spec(gw, SSD_W // gw + 1),
                  row_spec(LANES, 0), row_spec(LANES, 0)],
        out_specs=row_spec(SSD_W, 0),
        out_shape=jax.ShapeDtypeStruct((R, SSD_W), F32),
        scratch_shapes=[pltpu.VMEM((SSD_W, SSD_N), F32)],
        compiler_params=_cparams(("parallel", "arbitrary")),
        name="ssd_scan_bwd" if reverse else "ssd_scan_fwd",
    )(xbc, xbc, xbc, dt, dta)


def _attn_body(q_ref, kp_ref, kc_ref, kn_ref, vp_ref, vc_ref, vn_ref, kx_ref, vx_ref, sink_ref, mix_ref, o_ref,
               *, seq_len):
    del mix_ref
    n = pl.program_id(1)
    blk = ATT_BLOCK
    n_ctx = kx_ref.shape[0]
    rep = ATT_HEADS // KV_HEADS
    k_all = jnp.concatenate([kx_ref[...], kp_ref[...], kc_ref[...], kn_ref[...]], axis=0).astype(BF16)
    v_all = jnp.concatenate([vx_ref[...], vp_ref[...], vc_ref[...], vn_ref[...]], axis=0).astype(BF16)
    nk = n_ctx + 3 * blk
    r_idx = lax.broadcasted_iota(jnp.int32, (rep * blk, nk), 0)
    q_pos = n * blk + r_idx % blk
    j = lax.broadcasted_iota(jnp.int32, (rep * blk, nk), 1)
    k_pos = (n - 1) * blk + (j - n_ctx)
    valid = (j < n_ctx) | ((jnp.abs(q_pos - k_pos) <= WINDOW) & (k_pos >= 0) & (k_pos < seq_len))
    scale = ATT_HD ** -0.5
    head_of_row = lax.broadcasted_iota(jnp.int32, (rep * blk, 1), 0) // blk
    G = range(KV_HEADS)
    hd = lambda t, h: t[:, h * ATT_HD:(h + 1) * ATT_HD]
    q = q_ref[...].astype(BF16)
    qs = [jnp.concatenate([hd(q, g * rep + i) for i in range(rep)], axis=0) for g in G]
    sinks = []
    for g in G:
        sk = jnp.zeros((rep * blk, 1), F32)
        for i in range(rep):
            sk = jnp.where(head_of_row == i, sink_ref[0:1, g * rep + i:g * rep + i + 1], sk)
        sinks.append(sk)
    sc = [jnp.where(valid, lax.dot_general(qs[g], hd(k_all, g), (((1,), (1,)), ((), ())),
                                           preferred_element_type=F32) * scale, -jnp.inf) for g in G]
    m = [jnp.maximum(jnp.max(sc[g], axis=-1, keepdims=True), sinks[g]) for g in G]
    e = [jnp.exp(sc[g] - m[g]) for g in G]
    den = [jnp.sum(e[g], axis=-1, keepdims=True) + jnp.exp(sinks[g] - m[g]) for g in G]
    pv = [jnp.dot((e[g] / den[g]).astype(BF16), hd(v_all, g), preferred_element_type=F32) for g in G]
    outs = [pv[g][i * blk:(i + 1) * blk] for g in G for i in range(rep)]
    o_ref[...] = jnp.concatenate(outs, axis=1).astype(o_ref.dtype)


def _window_attn(q, k, p, v_col, sink, mix, mix_col, *, n_batch, n_ctx, seq_len):
    nb = seq_len // ATT_BLOCK
    x0 = n_batch * n_ctx // ATT_BLOCK
    vc = v_col // ATT_KV_W
    cur = lambda b, n: x0 + b * nb + n
    prev = lambda b, n: x0 + b * nb + jnp.maximum(n - 1, 0)
    nxt = lambda b, n: x0 + b * nb + jnp.minimum(n + 1, nb - 1)
    k_spec = lambda im: pl.BlockSpec((ATT_BLOCK, ATT_KV_W), lambda b, n: (im(b, n), 0))
    v_spec = lambda im: pl.BlockSpec((ATT_BLOCK, ATT_KV_W), lambda b, n: (im(b, n), vc))
    return pl.pallas_call(
        functools.partial(_attn_body, seq_len=seq_len),
        grid=(n_batch, nb),
        in_specs=[pl.BlockSpec((ATT_BLOCK, ATT_Q_W), lambda b, n: (cur(b, n), 0)),
                  k_spec(prev), k_spec(cur), k_spec(nxt), v_spec(prev), v_spec(cur), v_spec(nxt),
                  pl.BlockSpec((n_ctx, ATT_KV_W), lambda b, n: (b, 0)),
                  pl.BlockSpec((n_ctx, ATT_KV_W), lambda b, n: (b, vc)),
                  pl.BlockSpec((1, ATT_HEADS), lambda b, n: (0, 0)),
                  pl.BlockSpec(memory_space=pl.ANY)],
        out_specs=pl.BlockSpec((ATT_BLOCK, ATT_Q_W), lambda b, n: (b * nb + n, mix_col // ATT_Q_W)),
        out_shape=jax.ShapeDtypeStruct(mix.shape, mix.dtype),
        input_output_aliases={10: 0},
        compiler_params=_cparams(("parallel", "parallel")),
        name="window_attn",
    )(q, k, k, k, p, p, p, k, p, sink, mix)


def _tile_edges(row0, tm, n_ctx_rows, ctx_len, seq_len):
    in_ctx = row0 < n_ctx_rows
    pos0 = jnp.where(in_ctx, row0 % ctx_len, (row0 - n_ctx_rows) % seq_len)
    seg = jnp.where(in_ctx, ctx_len, seq_len)
    return pos0 == 0, pos0 + tm == seg


def _shifted_rows(cur, k, prev_blk, next_blk, first, last, rows):
    tm = cur.shape[0]
    if k < 0:
        edge = jnp.where(first, 0.0, pltpu.roll(prev_blk, -k, axis=0))
        body = pltpu.roll(cur, -k, axis=0)
        out = body
        for j in range(-k):
            out = jnp.where(rows == j, edge[j:j + 1, :], out)
        return out
    edge = jnp.where(last, 0.0, next_blk)
    out = pltpu.roll(cur, tm - k, axis=0)
    for j in range(k):
        out = jnp.where(rows == tm - k + j, edge[j:j + 1, :], out)
    return out


def _head_sum_mxu(t, blk):
    n = t.shape[1] // LANES
    return jnp.concatenate([_dot_x_ones_nt(t[:, j * LANES:(j + 1) * LANES], blk) for j in range(n)], axis=1)


def _same_head_matrix():
    r = lax.broadcasted_iota(jnp.int32, (LANES, LANES), 0) // RWKV_HD
    c = lax.broadcasted_iota(jnp.int32, (LANES, LANES), 1) // RWKV_HD
    return jnp.where(r == c, 1.0, 0.0).astype(BF16)


def _softplus(x):
    return jnp.maximum(x, 0.0) + jnp.log(1.0 + jnp.exp(-jnp.abs(x)))


def _rwkv_pre_body(p_ref, hp_ref, hn_ref, mu_ref, lora_ref, gw_ref, w0_ref, a0_ref, kk_ref, ka_ref, rk_ref,
                   r_out, v_out, kk_out, g_out, bonus_out, lw_out, kd_out, be_out, *, tm, geo):
    W = RWKV_W
    first, last = _tile_edges(pl.program_id(0) * tm, tm, *geo)
    rows = lax.broadcasted_iota(jnp.int32, (tm, 1), 0)

    def tshift(c0, c1):
        cur = p_ref[:, c0:c1]
        prev = _shifted_rows(cur, -1, hp_ref[:, c0:c1], hn_ref[:, c0:c1], first, last, rows)
        nxt = _shifted_rows(cur, 1, hp_ref[:, c0:c1], hn_ref[:, c0:c1], first, last, rows)
        return cur + mu_ref[0:1, c0:c1] * (prev - cur) + mu_ref[1:2, c0:c1] * (nxt - cur)

    r = tshift(0, W)
    k = tshift(W, 2 * W)
    v = tshift(2 * W, 3 * W)
    xl = tshift(3 * W, 3 * W + 4 * LANES)
    lane = lax.broadcasted_iota(jnp.int32, (tm, 2 * LANES), 1)
    lin = jnp.where(lane < DECAY_LORA, jnp.tanh(xl[:, :2 * LANES]), xl[:, :2 * LANES])
    lora = _dot(lin, lora_ref[...])
    g_out[...] = _dot(jax.nn.sigmoid(xl), gw_ref[...])
    blk = _same_head_matrix()
    kk = k * kk_ref[...]
    kk = kk / jnp.maximum(jnp.sqrt(_head_sum_mxu(kk * kk, blk)), 1e-12)
    r_out[...] = r
    v_out[...] = v
    kk_out[...] = kk
    kd_sum = jnp.zeros_like(k)
    for d in range(2):
        w_log = -_softplus(-(w0_ref[d:d + 1, :] + lora[:, d * W:(d + 1) * W])) - 0.5
        a = jax.nn.sigmoid(a0_ref[d:d + 1, :] + lora[:, (2 + d) * W:(3 + d) * W])
        kd = k * (1.0 + (a - 1.0) * ka_ref[...])
        lw_out[d] = -jnp.exp(w_log)
        kd_out[d] = kd
        be_out[d] = a * kk
        kd_sum = kd_sum + kd
    bonus_out[...] = _head_sum_mxu(r * kd_sum * rk_ref[...], blk) * v


def _rwkv_pre(p, mu, lora_w, gate_w, w0, a0, k_k, k_a, r_k, *, geo):
    R = p.shape[0]
    W = RWKV_W
    tm = SEQ_BLOCK
    pw = 3 * W + 4 * LANES
    nb8 = R // SUBLANES
    row = lambda i: (i, 0)
    vec = lambda n: pl.BlockSpec((n, W), lambda i: (0, 0))
    out2 = pl.BlockSpec((2, tm, W), lambda i: (0, i, 0))
    f = jax.ShapeDtypeStruct((R, W), F32)
    f2 = jax.ShapeDtypeStruct((2, R, W), F32)
    return pl.pallas_call(
        functools.partial(_rwkv_pre_body, tm=tm, geo=geo),
        grid=(R // tm,),
        in_specs=[pl.BlockSpec((tm, pw), row),
                  pl.BlockSpec((SUBLANES, pw), lambda i: (jnp.maximum(i * (tm // SUBLANES) - 1, 0), 0)),
                  pl.BlockSpec((SUBLANES, pw), lambda i: (jnp.minimum((i + 1) * (tm // SUBLANES), nb8 - 1), 0)),
                  pl.BlockSpec((2, pw), lambda i: (0, 0)),
                  pl.BlockSpec(lora_w.shape, lambda i: (0, 0)),
                  pl.BlockSpec(gate_w.shape, lambda i: (0, 0)),
                  vec(2), vec(2), vec(1), vec(1), vec(1)],
        out_specs=[pl.BlockSpec((tm, W), row)] * 5 + [out2] * 3,
        out_shape=[f] * 5 + [f2] * 3,
        compiler_params=_cparams(("parallel",)),
        name="rwkv_pre",
    )(p, p, p, mu, lora_w, gate_w, w0, a0, k_k, k_a, r_k)


def _rwkv_post_body(yf_ref, yb_ref, bonus_ref, g_ref, lng_ref, lnb_ref, o_ref):
    blk = _same_head_matrix()
    y = yf_ref[...] + yb_ref[...]
    mean = _head_sum_mxu(y, blk) * (1.0 / RWKV_HD)
    d = y - mean
    var = _head_sum_mxu(d * d, blk) * (1.0 / RWKV_HD)
    yn = d * lax.rsqrt(var + GN_EPS)
    o_ref[...] = ((yn * lng_ref[...] + lnb_ref[...] + bonus_ref[...]) * g_ref[...]).astype(o_ref.dtype)


def _rwkv_post(yf, yb, bonus, g, ln_g, ln_b, out_cols):
    R, W = yf.shape
    tm = SEQ_BLOCK
    row = pl.BlockSpec((tm, W), lambda i: (i, 0))
    vec = pl.BlockSpec((1, W), lambda i: (0, 0))
    return pl.pallas_call(
        _rwkv_post_body,
        grid=(R // tm,),
        in_specs=[row, row, row, row, vec, vec],
        out_specs=row,
        out_shape=jax.ShapeDtypeStruct((R, out_cols), BF16),
        compiler_params=_cparams(("parallel",)),
        name="rwkv_post",
    )(yf, yb, bonus, g, ln_g, ln_b)


def _gelu_tanh(x):
    return 0.5 * x * (1.0 + jnp.tanh(math.sqrt(2.0 / math.pi) * (x + 0.044715 * (x * x * x))))


def _s5_post_body(u0_ref, u1_ref, yf_ref, yb_ref, d_ref, w_ref, b_ref, mix_ref, o_ref):
    del mix_ref
    u = jnp.concatenate([u0_ref[...], u1_ref[...]], axis=1)
    y = _gelu_tanh(d_ref[...] * u + yf_ref[...] + yb_ref[...])
    o_ref[...] = (y * jax.nn.sigmoid(_dot(y, w_ref[...]) + b_ref[...])).astype(o_ref.dtype)


def _s5_post(p, col0, yf, yb, d_skip, glu_w, glu_b, mix):
    R, W = yf.shape
    tm = SEQ_BLOCK
    half = W // 2
    row = pl.BlockSpec((tm, W), lambda i: (i, 0))
    vec = pl.BlockSpec((1, W), lambda i: (0, 0))
    return pl.pallas_call(
        _s5_post_body,
        grid=(R // tm,),
        in_specs=[pl.BlockSpec((tm, half), lambda i: (i, col0 // half)),
                  pl.BlockSpec((tm, half), lambda i: (i, col0 // half + 1)), row, row, vec,
                  pl.BlockSpec((W, W), lambda i: (0, 0)), vec, pl.BlockSpec(memory_space=pl.ANY)],
        out_specs=pl.BlockSpec((tm, W), lambda i: (i, 1)),
        out_shape=jax.ShapeDtypeStruct(mix.shape, mix.dtype),
        input_output_aliases={7: 0},
        compiler_params=_cparams(("parallel",)),
        name="s5_post",
    )(p, p, yf, yb, d_skip, glu_w, glu_b, mix)


def _cd_conv_body(x_ref, hp_ref, hn_ref, w_ref, b_ref, o_ref, *, tm, geo, taps):
    first, last = _tile_edges(pl.program_id(0) * tm, tm, *geo)
    rows = lax.broadcasted_iota(jnp.int32, (tm, 1), 0)
    cur = x_ref[...]
    half = taps // 2
    acc = b_ref[...] + w_ref[half:half + 1, :] * cur
    for j in range(taps):
        if j != half:
            acc = acc + w_ref[j:j + 1, :] * _shifted_rows(cur, j - half, hp_ref[...], hn_ref[...], first, last, rows)
    o_ref[...] = acc * jax.nn.sigmoid(acc)


def _cd_conv(p, col0, width, conv_w, conv_b, *, geo):
    R = p.shape[0]
    tm = SEQ_BLOCK
    tw = 1024
    taps = conv_w.shape[0]
    nb8 = R // SUBLANES
    c0 = col0 // tw
    w8 = jnp.pad(conv_w, ((0, SUBLANES - taps), (0, 0)))
    return pl.pallas_call(
        functools.partial(_cd_conv_body, tm=tm, geo=geo, taps=taps),
        grid=(R // tm, width // tw),
        in_specs=[pl.BlockSpec((tm, tw), lambda i, j: (i, c0 + j)),
                  pl.BlockSpec((SUBLANES, tw), lambda i, j: (jnp.maximum(i * (tm // SUBLANES) - 1, 0), c0 + j)),
                  pl.BlockSpec((SUBLANES, tw), lambda i, j: (jnp.minimum((i + 1) * (tm // SUBLANES), nb8 - 1), c0 + j)),
                  pl.BlockSpec((SUBLANES, tw), lambda i, j: (0, j)),
                  pl.BlockSpec((1, tw), lambda i, j: (0, j))],
        out_specs=pl.BlockSpec((tm, tw), lambda i, j: (i, j)),
        out_shape=jax.ShapeDtypeStruct((R, width), F32),
        compiler_params=_cparams(("parallel", "parallel")),
        name="cd_conv",
    )(p, p, p, w8, conv_b.reshape(1, width))


def _rope_tile(x, cos, sin):
    lane = lax.broadcasted_iota(jnp.int32, x.shape, 1)
    q = ATT_HD // 4
    partner = jnp.where(lane % (2 * q) < q, pltpu.roll(x, LANES - q, axis=1), pltpu.roll(x, q, axis=1))
    return x * cos + partner * sin


def _cd_rope_dt_body(q_ref, k_ref, dtr_ref, cos_ref, sin_ref, bias_ref, a_ref, q_out, k_out, dt_out, dta_out,
                     *, tm, n_ctx_rows):
    in_ctx = pl.program_id(0) * tm < n_ctx_rows
    cos = jnp.where(in_ctx, 1.0, cos_ref[...])
    sin = jnp.where(in_ctx, 0.0, sin_ref[...])
    for j in range(ATT_Q_W // LANES):
        q_out[:, j * LANES:(j + 1) * LANES] = _rope_tile(q_ref[:, j * LANES:(j + 1) * LANES], cos, sin)
    for j in range(ATT_KV_W // LANES):
        k_out[:, j * LANES:(j + 1) * LANES] = _rope_tile(k_ref[:, j * LANES:(j + 1) * LANES], cos, sin)
    dt = _softplus(dtr_ref[...] + bias_ref[...])
    dt_out[...] = dt
    dta_out[...] = dt * a_ref[...]


def _cd_rope_dt(p, q_col, k_col, dt_col, cos, sin, dt_bias, a_neg, *, n_ctx_rows, seq_len):
    R = p.shape[0]
    tm = SEQ_BLOCK
    tab = pl.BlockSpec((tm, LANES), lambda i: (jnp.maximum(i * tm - n_ctx_rows, 0) % seq_len // tm, 0))
    vec = pl.BlockSpec((1, LANES), lambda i: (0, 0))
    return pl.pallas_call(
        functools.partial(_cd_rope_dt_body, tm=tm, n_ctx_rows=n_ctx_rows),
        grid=(R // tm,),
        in_specs=[pl.BlockSpec((tm, ATT_Q_W), lambda i: (i, q_col // ATT_Q_W)),
                  pl.BlockSpec((tm, ATT_KV_W), lambda i: (i, k_col // ATT_KV_W)),
                  pl.BlockSpec((tm, LANES), lambda i: (i, dt_col // LANES)),
                  tab, tab, vec, vec],
        out_specs=[pl.BlockSpec((tm, ATT_Q_W), lambda i: (i, 0)), pl.BlockSpec((tm, ATT_KV_W), lambda i: (i, 0)),
                   pl.BlockSpec((tm, LANES), lambda i: (i, 0)), pl.BlockSpec((tm, LANES), lambda i: (i, 0))],
        out_shape=[jax.ShapeDtypeStruct((R, ATT_Q_W), F32), jax.ShapeDtypeStruct((R, ATT_KV_W), F32),
                   jax.ShapeDtypeStruct((R, LANES), F32), jax.ShapeDtypeStruct((R, LANES), F32)],
        compiler_params=_cparams(("parallel",)),
        name="cd_rope_dt",
    )(p, p, p, cos, sin, dt_bias, a_neg)


def _ssd_post_body(yf_ref, yb_ref, x_ref, z_ref, d_ref, g_ref, o_ref):
    z = z_ref[...]
    y = (yf_ref[...] + yb_ref[...] + d_ref[...] * x_ref[...]) * (z * jax.nn.sigmoid(z))
    gw = SSD_W // SSD_GROUPS
    for g in range(SSD_GROUPS):
        yg = y[:, g * gw:(g + 1) * gw]
        ms = jnp.mean(yg * yg, axis=-1, keepdims=True)
        o_ref[:, g * gw:(g + 1) * gw] = (yg * lax.rsqrt(ms + EPS) * g_ref[:, g * gw:(g + 1) * gw]).astype(o_ref.dtype)


def _ssd_post(yf, yb, xbc, p, d_skip, norm_g, out_cols, *, n_ctx_rows):
    R = yf.shape[0]
    tm = SEQ_BLOCK
    r0 = n_ctx_rows // tm
    row = pl.BlockSpec((tm, SSD_W), lambda i: (r0 + i, 0))
    vec = pl.BlockSpec((1, SSD_W), lambda i: (0, 0))
    return pl.pallas_call(
        _ssd_post_body,
        grid=((R - n_ctx_rows) // tm,),
        in_specs=[row, row, row, row, vec, vec],
        out_specs=pl.BlockSpec((tm, SSD_W), lambda i: (i, 0)),
        out_shape=jax.ShapeDtypeStruct((R - n_ctx_rows, out_cols), BF16),
        compiler_params=_cparams(("parallel",)),
        name="ssd_post",
    )(yf, yb, xbc, p, d_skip, norm_g)


def _segment_pos(R, RC, C, L):
    row = jnp.arange(R, dtype=jnp.int32)
    in_ctx = row < RC
    pos = jnp.where(in_ctx, row % C, (row - RC) % L)
    seg = jnp.where(in_ctx, C, L)
    return pos, seg


def _shift_rows(p, k, pos, seg):
    ok = (pos + k >= 0) & (pos + k < seg)
    return jnp.where(ok[:, None], jnp.roll(p, -k, axis=0), 0.0)


def _head_sum(t):
    R = t.shape[0]
    s = jnp.sum(t.reshape(R, RWKV_HEADS, RWKV_HD), axis=-1, keepdims=True)
    return jnp.broadcast_to(s, (R, RWKV_HEADS, RWKV_HD)).reshape(R, RWKV_W)


def _rope_tables(L):
    nf = ATT_HD // 4
    inv_freq = ROPE_BASE ** (-jnp.arange(nf, dtype=F32) / nf)
    t = jnp.arange(L, dtype=jnp.int32)
    row_id = (t // GRID_W).astype(F32)
    col_id = (t % GRID_W).astype(F32)
    ar = row_id[:, None] * inv_freq
    ac = col_id[:, None] * inv_freq
    cos = jnp.concatenate([jnp.cos(ar), jnp.cos(ar), jnp.cos(ac), jnp.cos(ac)], axis=1)
    sin = jnp.concatenate([-jnp.sin(ar), jnp.sin(ar), -jnp.sin(ac), jnp.sin(ac)], axis=1)
    return cos, sin


def _rope(t, cos, sin, n_heads):
    R = t.shape[0]
    q = ATT_HD // 4
    tr = t.reshape(R, n_heads, 2, 2, q)
    swapped = jnp.stack([tr[..., 1, :], tr[..., 0, :]], axis=-2).reshape(R, n_heads, ATT_HD)
    out = t.reshape(R, n_heads, ATT_HD) * cos[:, None, :] + swapped * sin[:, None, :]
    return out.reshape(R, n_heads * ATT_HD)


def kernel(x, c, ctx, c_ctx, ada_w, ada_b, norm1_g, norm2_g, mlp_w1, mlp_w2, final_g, ab_w_in, ab_w_out, rwkv_mu, rwkv_w0, rwkv_w_up, rwkv_a0, rwkv_a_up, rwkv_g_up, rwkv_k_k, rwkv_k_a, rwkv_r_k, rwkv_ln_g, rwkv_ln_b, s5_A_re, s5_A_im, s5_log_dt, s5_B_re, s5_B_im, s5_C_re, s5_C_im, s5_D, s5_glu_w, s5_glu_b, cd_w_in, cd_w_out, ssd_conv_w, ssd_conv_b, ssd_A_log, ssd_dt_bias, ssd_D, ssd_norm_g, attn_sink):
    B, L, D = x.shape
    C = ctx.shape[1]
    RC, RX = B * C, B * L
    R = RC + RX
    assert C % SEQ_BLOCK == 0 and L % SEQ_BLOCK == 0
    geo = dict(n_batch=B, n_ctx_blk=C // SEQ_BLOCK, n_x_blk=L // SEQ_BLOCK)
    geo_ssd = dict(n_batch=B, n_ctx_blk=C // SSD_CHUNK, n_x_blk=L // SSD_CHUNK)
    group_all = lambda r0: jnp.where(r0 < RC, 0, 1 + (r0 - RC) // L)
    group_x = lambda r0: 1 + r0 // L

    xs = jnp.concatenate([ctx.reshape(RC, D), x.reshape(RX, D)], axis=0)
    cond = jax.nn.silu(jnp.concatenate([c_ctx[None, :], c], axis=0))
    cond = jnp.pad(cond, ((0, SUBLANES - (B + 1) % SUBLANES), (0, 0))) if (B + 1) % SUBLANES else cond
    n_groups = cond.shape[0]

    def modulation(i):
        mod = _mm(cond, ada_w[i], name="ada") + ada_b[i]
        return [m.reshape(n_groups, 1, D) for m in jnp.split(mod, 6, axis=-1)]

    def mlp(xin, i, sh2, sc2, g2, gidx):
        h2 = _norm_mod(xin, norm2_g[i], sh2, sc2, gidx)
        m1 = _mm(h2, mlp_w1[i].astype(BF16), out_dtype=BF16, act="relu2", name="mlp_up")
        return _mm(m1, mlp_w2[i].astype(BF16), res=xin, gate=g2, gate_idx=gidx, name="mlp_down")

    sh1, sc1, g1, sh2, sc2, g2 = modulation(0)
    h = _norm_mod(xs, norm1_g[0], sh1, sc1, group_all)
    pad_cols = (-RWKV_COLS) % LANES
    w_in = ab_w_in[0]
    w_in = jnp.concatenate([w_in[:, :RWKV_COLS], jnp.zeros((D, pad_cols), F32), w_in[:, RWKV_COLS:]], axis=1)
    s5_col0 = RWKV_COLS + pad_cols
    p = _mm(h, w_in.astype(BF16), name="ab_in")

    W = RWKV_W
    lora_rows = 2 * LANES
    zw = jnp.zeros((DECAY_LORA, W), F32)
    lora_w = jnp.concatenate([
        jnp.concatenate([rwkv_w_up[0, 0], rwkv_w_up[0, 1], zw, zw], axis=1),
        jnp.concatenate([zw, zw, rwkv_a_up[0, 0], rwkv_a_up[0, 1]], axis=1),
        jnp.zeros((lora_rows - DECAY_LORA - AAA_LORA, 4 * W), F32)], axis=0).astype(BF16)
    gate_w = jnp.pad(rwkv_g_up[0], ((DECAY_LORA + AAA_LORA, pad_cols), (0, 0))).astype(BF16)
    mu = jnp.pad(rwkv_mu[0], ((0, 0), (0, pad_cols)))
    vec = lambda t: t.reshape(1, W)
    r, v, kk, g_gate, bonus, lw, kd, be = _rwkv_pre(
        p, mu, lora_w, gate_w, rwkv_w0[0], rwkv_a0[0], vec(rwkv_k_k[0]), vec(rwkv_k_a[0]), vec(rwkv_r_k[0]),
        geo=(RC, C, L))
    y_dirs = [_rwkv_scan(r, v, kk, lw[d], kd[d], be[d], reverse=(d == 1), **geo) for d in range(2)]
    mix = _rwkv_post(y_dirs[0], y_dirs[1], bonus, g_gate, vec(rwkv_ln_g[0]), vec(rwkv_ln_b[0]), W + S5_W)

    ys_dirs = []
    for d in range(2):
        bw, cw, tab = _s5_weights(s5_A_re[0, d], s5_A_im[0, d], s5_log_dt[0, d], s5_B_re[0, d], s5_B_im[0, d],
                                  s5_C_re[0, d], s5_C_im[0, d], reverse=(d == 1))
        ys_dirs.append(_s5_scan(p, s5_col0 // LANES, bw, cw, tab, reverse=(d == 1), **geo))
    mix = _s5_post(p, s5_col0, ys_dirs[0], ys_dirs[1], vec(s5_D[0]), s5_glu_w[0].astype(BF16),
                   vec(s5_glu_b[0]), mix)
    xs = _mm(mix, ab_w_out[0].astype(BF16), res=xs, gate=g1, gate_idx=group_all, name="ab_out")
    xs = mlp(xs, 0, sh2, sc2, g2, group_all)

    sh1, sc1, g1, sh2, sc2, g2 = modulation(1)
    h = _norm_mod(xs, norm1_g[1], sh1, sc1, group_all)
    cd_in = cd_w_in[0]
    o_dt = SSD_W + XBC_W
    o_q = o_dt + 2 * SSD_HEADS
    n_qkv = ATT_Q_W + 2 * ATT_KV_W
    cd_in = jnp.concatenate([cd_in[:, :o_dt], cd_in[:, o_q:o_q + n_qkv], cd_in[:, o_dt:o_q]], axis=1)
    cd_in = jnp.pad(cd_in, ((0, 0), (0, (-cd_in.shape[1]) % 512)))
    q_col = o_dt
    k_col = q_col + ATT_Q_W
    v_col = k_col + ATT_KV_W
    dt_col = v_col + ATT_KV_W
    p = _mm(h, cd_in.astype(BF16), name="cd_in")

    xbc = _cd_conv(p, SSD_W, XBC_W, ssd_conv_w[0], ssd_conv_b[0], geo=(RC, C, L))
    cos, sin = _rope_tables(L)
    lane_pad = lambda t: jnp.pad(t.reshape(1, 2 * SSD_HEADS), ((0, 0), (0, LANES - 2 * SSD_HEADS)))
    q_rot, k_rot, dt, dta = _cd_rope_dt(
        p, q_col, k_col, dt_col, jnp.tile(cos, (1, LANES // ATT_HD)), jnp.tile(sin, (1, LANES // ATT_HD)),
        lane_pad(ssd_dt_bias[0]), lane_pad(-jnp.exp(ssd_A_log[0])), n_ctx_rows=RC, seq_len=L)
    y_dirs = [_ssd_scan(xbc, dt, dta, reverse=(d == 1), **geo_ssd) for d in range(2)]
    mix = _ssd_post(y_dirs[0], y_dirs[1], xbc, p, jnp.repeat(ssd_D[0], SSD_HD).reshape(1, SSD_W),
                    ssd_norm_g[0].reshape(1, SSD_W), SSD_W + ATT_Q_W, n_ctx_rows=RC)
    mix = _window_attn(q_rot, k_rot, p, v_col, attn_sink[0].reshape(1, ATT_HEADS), mix, SSD_W,
                       n_batch=B, n_ctx=C, seq_len=L)
    xo = _mm(mix, cd_w_out[0].astype(BF16), res=xs[RC:], gate=g1, gate_idx=group_x, name="cd_out")
    xo = mlp(xo, 1, sh2, sc2, g2, group_x)
    zero = jnp.zeros((1, 1, D), F32)
    out = _norm_mod(xo, final_g, zero, zero, lambda r0: 0, out_dtype=F32)
    return out.reshape(B, L, D)
```

```python
import functools
import math

import jax
import jax.numpy as jnp
from jax import lax
from jax.experimental import pallas as pl
from jax.experimental.pallas import tpu as pltpu

F32 = jnp.float32
BF16 = jnp.bfloat16

EPS = 1e-6
GN_EPS = 64e-5
GRID_W = 64
ROPE_BASE = 10000.0

RWKV_HEADS = 16
RWKV_HD = 64
RWKV_W = RWKV_HEADS * RWKV_HD
DECAY_LORA = 96
AAA_LORA = 96
GATE_LORA = 256
RWKV_COLS = 3 * RWKV_W + DECAY_LORA + AAA_LORA + GATE_LORA
S5_W = 1024
S5_P = 16
S5_G = S5_W // S5_P
S5_N = 64
SSD_HEADS = 32
SSD_HD = 64
SSD_W = SSD_HEADS * SSD_HD
SSD_GROUPS = 4
SSD_N = 128
SSD_CHUNK = 128
XBC_W = SSD_W + 2 * SSD_GROUPS * SSD_N
ATT_HEADS = 16
KV_HEADS = 4
ATT_HD = 64
ATT_Q_W = ATT_HEADS * ATT_HD
ATT_KV_W = KV_HEADS * ATT_HD
WINDOW = 128
ATT_BLOCK = 128

LANES = 128
SUBLANES = 8
SEQ_BLOCK = 256
RWKV_CHUNK = 64
S5_GB = 8
VMEM_LIMIT = 48 * 1024 * 1024


def _cparams(sem):
    return pltpu.CompilerParams(dimension_semantics=sem, vmem_limit_bytes=VMEM_LIMIT)


def _pick(n, cands):
    for c in cands:
        if n % c == 0:
            return c
    return n


def _dot(a, b):
    return jnp.dot(a.astype(BF16), b.astype(BF16), preferred_element_type=F32)


def _dot_nt(a, b):
    return lax.dot_general(a.astype(BF16), b.astype(BF16), (((1,), (1,)), ((), ())), preferred_element_type=F32)


def _dot_tn(a, b):
    return lax.dot_general(a.astype(BF16), b.astype(BF16), (((0,), (0,)), ((), ())), preferred_element_type=F32)


def _split3(x):
    hi = x.astype(BF16)
    r1 = x - hi.astype(F32)
    mid = r1.astype(BF16)
    lo = (r1 - mid.astype(F32)).astype(BF16)
    return hi, mid, lo


def _dot_ones_x(tri, x):
    hi, mid, lo = _split3(x)
    f = lambda t: jnp.dot(tri, t, preferred_element_type=F32)
    return f(hi) + f(mid) + f(lo)


def _dot_x_ones_nt(x, tri):
    hi, mid, lo = _split3(x)
    f = lambda t: lax.dot_general(t, tri, (((1,), (1,)), ((), ())), preferred_element_type=F32)
    return f(hi) + f(mid) + f(lo)


def _mm_body(*refs, nk, act, residual):
    if residual:
        a_ref, b_ref, res_ref, gate_ref, o_ref, acc_ref = refs
    else:
        a_ref, b_ref, o_ref, acc_ref = refs
    k = pl.program_id(2)
    part = jnp.dot(a_ref[...].astype(BF16), b_ref[...].astype(BF16), preferred_element_type=F32)

    def finish(acc):
        if act == "relu2":
            acc = jnp.square(jnp.maximum(acc, 0.0))
        if residual:
            acc = res_ref[...] + gate_ref[0] * acc
        o_ref[...] = acc.astype(o_ref.dtype)

    if nk == 1:
        finish(part)
    else:
        @pl.when(k == 0)
        def _():
            acc_ref[...] = part

        @pl.when(k > 0)
        def _():
            acc_ref[...] += part

        @pl.when(k == nk - 1)
        def _():
            finish(acc_ref[...])


def _mm(a, b, *, out_dtype=F32, act=None, res=None, res_row0=0, gate=None, gate_idx=None, name="mm"):
    M, K = a.shape
    N = b.shape[1]
    tm = _pick(M, (1024, 512, 256, 128, 64, 32, 16, 8))
    tn = _pick(N, (512, 384, 256, 128))
    tk = _pick(K, (2048, 1536, 1024, 512, 256, 128))
    nk = K // tk
    residual = res is not None
    in_specs = [pl.BlockSpec((tm, tk), lambda i, j, k: (i, k)),
                pl.BlockSpec((tk, tn), lambda i, j, k: (k, j))]
    args = [a, b]
    if residual:
        if res_row0 % tm:
            res, res_row0 = res[res_row0:res_row0 + M], 0
        r0 = res_row0 // tm
        in_specs += [pl.BlockSpec((tm, tn), lambda i, j, k: (r0 + i, j)),
                     pl.BlockSpec((1, 1, tn), lambda i, j, k: (gate_idx(i * tm), 0, j))]
        args += [res, gate]
    return pl.pallas_call(
        functools.partial(_mm_body, nk=nk, act=act, residual=residual),
        grid=(M // tm, N // tn, nk),
        in_specs=in_specs,
        out_specs=pl.BlockSpec((tm, tn), lambda i, j, k: (i, j)),
        out_shape=jax.ShapeDtypeStruct((M, N), out_dtype),
        scratch_shapes=[pltpu.VMEM((tm, tn), F32)],
        compiler_params=_cparams(("parallel", "parallel", "arbitrary")),
        name=name,
    )(*args)


def _norm_body(x_ref, g_ref, sh_ref, sc_ref, o_ref):
    x = x_ref[...]
    ms = jnp.mean(x * x, axis=-1, keepdims=True)
    xn = x * lax.rsqrt(ms + EPS) * g_ref[...]
    o_ref[...] = (xn * (1.0 + sc_ref[0]) + sh_ref[0]).astype(o_ref.dtype)


def _norm_mod(x, g, shift, scale, group_idx, out_dtype=BF16):
    M, D = x.shape
    tm = _pick(M, (256, 128, 64, 32, 16, 8))
    return pl.pallas_call(
        _norm_body,
        grid=(M // tm,),
        in_specs=[pl.BlockSpec((tm, D), lambda i: (i, 0)),
                  pl.BlockSpec((1, D), lambda i: (0, 0)),
                  pl.BlockSpec((1, 1, D), lambda i: (group_idx(i * tm), 0, 0)),
                  pl.BlockSpec((1, 1, D), lambda i: (group_idx(i * tm), 0, 0))],
        out_specs=pl.BlockSpec((tm, D), lambda i: (i, 0)),
        out_shape=jax.ShapeDtypeStruct((M, D), out_dtype),
        compiler_params=_cparams(("parallel",)),
        name="norm_mod",
    )(x, g.reshape(1, D), shift, scale)


def _seq_block(b, s, *, n_ctx_blk, n_x_blk, n_batch, reverse):
    if reverse:
        c = b * n_ctx_blk + (n_ctx_blk - 1 - s)
        x = n_batch * n_ctx_blk + b * n_x_blk + (n_x_blk - 1 - (s - n_ctx_blk))
    else:
        c = b * n_ctx_blk + s
        x = n_batch * n_ctx_blk + b * n_x_blk + (s - n_ctx_blk)
    return jnp.where(s < n_ctx_blk, c, x)


def _split2(x):
    hi = x.astype(BF16)
    lo = (x - hi.astype(F32)).astype(BF16)
    return hi, lo


def _rwkv_chunk(lw, r, v, kk, kd, be, st, tri, strict, incl, same_head, head0, eye):
    C = RWKV_CHUNK

    def stack(z):
        return jnp.concatenate([jnp.where(head0, z, 0.0), jnp.where(head0, 0.0, z)], axis=0)

    def unstack(z):
        return z[0:C] + z[C:2 * C]

    bdot = lambda a, b: jnp.dot(a, b, preferred_element_type=F32)
    bdot_nt = lambda a, b: lax.dot_general(a, b, (((1,), (1,)), ((), ())), preferred_element_type=F32)
    bdot_tn = lambda a, b: lax.dot_general(a, b, (((0,), (0,)), ((), ())), preferred_element_type=F32)
    cat = lambda *zs: jnp.concatenate(zs, axis=0)
    P = range(len(lw))
    cum = [_dot_ones_x(tri, lw[p]) for p in P]
    total = [jnp.sum(lw[p], axis=0, keepdims=True) for p in P]
    a_t = [kk[p] * jnp.exp(cum[p] - lw[p]) for p in P]
    r_t = [r[p] * jnp.exp(cum[p]) for p in P]
    a_sp = [_split2(a_t[p]) for p in P]
    r_hi = [r_t[p].astype(BF16) for p in P]
    sa_sp = [_split2(stack(a_t[p])) for p in P]
    sr_hi = [stack(r_t[p]).astype(BF16) for p in P]
    pinv = [jnp.exp(-cum[p]) for p in P]
    bk_sp = [_split2(cat(stack(be[p] * pinv[p]), stack(kd[p] * pinv[p]))) for p in P]
    g1 = [bdot_nt(cat(sa_sp[p][0], sa_sp[p][1], sr_hi[p]), bk_sp[p][0]) for p in P]
    g2 = [bdot_nt(sa_sp[p][0], bk_sp[p][1]) for p in P]
    n_all = [g1[p][0:2 * C] + g1[p][2 * C:4 * C] + g2[p] for p in P]
    x_neg = [jnp.where(strict, -n_all[p][:, 0:2 * C], 0.0) for p in P]
    n_ak = [jnp.where(strict, n_all[p][:, 2 * C:4 * C], 0.0) for p in P]
    m_r = [jnp.concatenate([jnp.where(incl, g1[p][4 * C:6 * C, 0:2 * C], 0.0),
                            jnp.where(incl, g1[p][4 * C:6 * C, 2 * C:4 * C], 0.0)], axis=1).astype(BF16)
           for p in P]
    xb = [x_neg[p].astype(BF16) for p in P]
    t_inv = [eye + x_neg[p] for p in P]
    n_pow = int(math.log2(C))
    for it in range(1, n_pow):
        xb = [bdot(xb[p], xb[p]).astype(BF16) for p in P]
        t_inv = [t_inv[p] + bdot(xb[p], t_inv[p].astype(BF16)) for p in P]
    t_inv = [t_inv[p].astype(BF16) for p in P]
    st_sp = [_split2(st[p]) for p in P]
    as1 = [bdot_nt(cat(a_sp[p][0], a_sp[p][1], r_hi[p]), st_sp[p][0]) for p in P]
    as2 = [bdot_nt(a_sp[p][0], st_sp[p][1]) for p in P]
    v_sp = [_split2(stack(v[p])) for p in P]
    nk_sp = [_split2(n_ak[p]) for p in P]
    nv1 = [bdot(cat(nk_sp[p][0], nk_sp[p][1]), v_sp[p][0]) for p in P]
    nv2 = [bdot(nk_sp[p][0], v_sp[p][1]) for p in P]
    rhs = [-(stack(as1[p][0:C] + as1[p][C:2 * C] + as2[p]) + nv1[p][0:2 * C] + nv1[p][2 * C:4 * C] + nv2[p])
           for p in P]
    u0 = [bdot(t_inv[p], rhs[p].astype(BF16)) for p in P]
    u0_sp = [_split2(u0[p]) for p in P]
    x_sp = [_split2(x_neg[p]) for p in P]
    xu1 = [bdot(cat(x_sp[p][0], x_sp[p][1]), u0_sp[p][0]) for p in P]
    xu2 = [bdot(x_sp[p][0], u0_sp[p][1]) for p in P]
    res = [(rhs[p] - u0[p] + xu1[p][0:2 * C] + xu1[p][2 * C:4 * C] + xu2[p]).astype(BF16) for p in P]
    u = [u0[p] + bdot(t_inv[p], res[p]) for p in P]
    y = [stack(as1[p][2 * C:3 * C]) + bdot(m_r[p], cat(u[p].astype(BF16), v_sp[p][0])) for p in P]
    uv_sp = [_split2(cat(unstack(u[p]), v[p])) for p in P]
    pend = [jnp.exp(total[p] - cum[p]) for p in P]
    e_sp = [_split2(cat(be[p] * pend[p], kd[p] * pend[p])) for p in P]
    upd = [bdot_tn(cat(uv_sp[p][0], uv_sp[p][1], uv_sp[p][0]), cat(e_sp[p][0], e_sp[p][0], e_sp[p][1]))
           for p in P]
    st_new = [st[p] * jnp.exp(total[p]) + jnp.where(same_head, upd[p], 0.0) for p in P]
    return [unstack(y[p]) for p in P], st_new


def _rwkv_body(r_ref, v_ref, kk_ref, lw_ref, kd_ref, be_ref, y_ref, st_ref, *, reverse, npair):
    C = RWKV_CHUNK
    nchunk = SEQ_BLOCK // C
    s = pl.program_id(2)

    @pl.when(s == 0)
    def _():
        st_ref[...] = jnp.zeros_like(st_ref)

    sgn = -1 if reverse else 1
    row = lax.broadcasted_iota(jnp.int32, (2 * C, 2 * C), 0)
    col = lax.broadcasted_iota(jnp.int32, (2 * C, 2 * C), 1)
    diff = (row - col) * sgn
    strict = diff > 0
    incl = diff >= 0
    same_head = (row // C) == (col // C)
    eye = jnp.where(row == col, 1.0, 0.0)
    r64 = lax.broadcasted_iota(jnp.int32, (C, C), 0)
    c64 = lax.broadcasted_iota(jnp.int32, (C, C), 1)
    tri = jnp.where((r64 - c64) * sgn >= 0, 1.0, 0.0).astype(BF16)
    head0 = lax.broadcasted_iota(jnp.int32, (C, 2 * C), 1) < C

    def chunk(ci, carry):
        ce = (nchunk - 1 - ci) if reverse else ci
        sl = pl.ds(pl.multiple_of(ce * C, C), C)
        lanes = [slice(p * LANES, (p + 1) * LANES) for p in range(npair)]
        get = lambda ref: [ref[sl, ln] for ln in lanes]
        ys, sts = _rwkv_chunk(get(lw_ref), get(r_ref), get(v_ref), get(kk_ref), get(kd_ref), get(be_ref),
                              [st_ref[p] for p in range(npair)], tri, strict, incl, same_head, head0, eye)
        for p in range(npair):
            y_ref[sl, lanes[p]] = ys[p]
            st_ref[p] = sts[p]
        return carry

    lax.fori_loop(0, nchunk, chunk, 0)


def _rwkv_scan(r, v, kk, lw, kd, be, *, n_batch, n_ctx_blk, n_x_blk, reverse, npair=8):
    R, W = r.shape
    ngrp = W // (LANES * npair)
    ns = n_ctx_blk + n_x_blk
    d = 1 if reverse else 0
    blk = functools.partial(_seq_block, n_ctx_blk=n_ctx_blk, n_x_blk=n_x_blk, n_batch=n_batch, reverse=reverse)
    spec = pl.BlockSpec((SEQ_BLOCK, LANES * npair), lambda b, p, s: (blk(b, s), p))
    dspec = pl.BlockSpec((None, SEQ_BLOCK, LANES * npair), lambda b, p, s: (d, blk(b, s), p))
    return pl.pallas_call(
        functools.partial(_rwkv_body, reverse=reverse, npair=npair),
        grid=(n_batch, ngrp, ns),
        in_specs=[spec] * 3 + [dspec] * 3,
        out_specs=spec,
        out_shape=jax.ShapeDtypeStruct((R, W), F32),
        scratch_shapes=[pltpu.VMEM((npair, LANES, LANES), F32)],
        compiler_params=_cparams(("parallel", "parallel", "arbitrary")),
        name="rwkv_scan_bwd" if reverse else "rwkv_scan_fwd",
    )(r, v, kk, lw, kd, be)


S5_SEG = SEQ_BLOCK // SUBLANES
S5_TAB_SLABS = 2 * S5_SEG + 2 + 8
S5_NGB = 2


def _s5_body(u_ref, bw_ref, cw_ref, tab_ref, y_ref, hb_ref, carry_ref, perm_ref, *, reverse):
    W = S5_GB * S5_N
    T = SEQ_BLOCK
    G = range(S5_NGB)
    s = pl.program_id(2)

    @pl.when(s == 0)
    def _():
        carry_ref[...] = jnp.zeros_like(carry_ref)
        prow = lax.broadcasted_iota(jnp.int32, (T, T), 0)
        pcol = lax.broadcasted_iota(jnp.int32, (T, T), 1)
        t_of_row = S5_SEG * (prow % SUBLANES) + prow // SUBLANES
        if reverse:
            t_of_row = T - 1 - t_of_row
        perm_ref[...] = jnp.where(pcol == t_of_row, 1.0, 0.0).astype(BF16)

    perm = perm_ref[...]
    u_p = jnp.dot(perm, u_ref[...].astype(BF16), preferred_element_type=F32).astype(BF16)
    for g in G:
        hb_ref[g] = jnp.dot(u_p[:, g * LANES:(g + 1) * LANES], bw_ref[g], preferred_element_type=F32)

    slab = lambda j: pl.ds(j * SUBLANES, SUBLANES)
    tab = lambda g, n: tab_ref[g, slab(n), :]
    o_seg = 2 * S5_SEG
    o_pow = o_seg + 2
    re, im = slice(0, W), slice(W, 2 * W)
    a = [(tab(g, o_pow), tab(g, o_pow + 1)) for g in G]
    h = [(hb_ref[g, slab(0), re], hb_ref[g, slab(0), im]) for g in G]
    for j in range(1, S5_SEG):
        h = [(hb_ref[g, slab(j), re] + a[g][0] * h[g][0] - a[g][1] * h[g][1],
              hb_ref[g, slab(j), im] + a[g][0] * h[g][1] + a[g][1] * h[g][0]) for g in G]
        for g in G:
            hb_ref[g, slab(j), re] = h[g][0]
            hb_ref[g, slab(j), im] = h[g][1]
    rows8 = lax.broadcasted_iota(jnp.int32, (SUBLANES, W), 0)
    ent = []
    for g in G:
        gr, gi = h[g]
        for n, k in enumerate((1, 2, 4)):
            pr, pi = tab(g, o_pow + 2 + 2 * n), tab(g, o_pow + 3 + 2 * n)
            sr = jnp.where(rows8 >= k, pltpu.roll(gr, k, axis=0), 0.0)
            si = jnp.where(rows8 >= k, pltpu.roll(gi, k, axis=0), 0.0)
            gr, gi = gr + pr * sr - pi * si, gi + pr * si + pi * sr
        cr, ci = carry_ref[g, :, re], carry_ref[g, :, im]
        qr, qi = tab(g, o_seg), tab(g, o_seg + 1)
        gr, gi = gr + qr * cr - qi * ci, gi + qr * ci + qi * cr
        ent.append((jnp.where(rows8 >= 1, pltpu.roll(gr, 1, axis=0), cr),
                    jnp.where(rows8 >= 1, pltpu.roll(gi, 1, axis=0), ci)))
        carry_ref[g, :, re] = jnp.broadcast_to(gr[SUBLANES - 1:SUBLANES, :], (SUBLANES, W))
        carry_ref[g, :, im] = jnp.broadcast_to(gi[SUBLANES - 1:SUBLANES, :], (SUBLANES, W))
    for g in G:
        in_r, in_i = ent[g]
        for j in range(S5_SEG):
            pr, pi = tab(g, j), tab(g, S5_SEG + j)
            hb_ref[g, slab(j), re] = hb_ref[g, slab(j), re] + pr * in_r - pi * in_i
            hb_ref[g, slab(j), im] = hb_ref[g, slab(j), im] + pr * in_i + pi * in_r
    y_p = jnp.concatenate([jnp.dot(hb_ref[g].astype(BF16), cw_ref[g], preferred_element_type=F32) for g in G],
                          axis=1)
    y_ref[...] = _dot_x_ones_tn(perm, y_p)


def _dot_x_ones_tn(ones, x):
    hi, mid, lo = _split3(x)
    f = lambda t: lax.dot_general(ones, t, (((0,), (0,)), ((), ())), preferred_element_type=F32)
    return f(hi) + f(mid) + f(lo)


def _s5_scan(p, col0_blk, bw, cw, tab, *, n_batch, n_ctx_blk, n_x_blk, reverse):
    R = p.shape[0]
    n = S5_NGB
    ngb = S5_G // S5_GB // n
    ns = n_ctx_blk + n_x_blk
    blk = functools.partial(_seq_block, n_ctx_blk=n_ctx_blk, n_x_blk=n_x_blk, n_batch=n_batch, reverse=reverse)
    W2 = 2 * S5_GB * S5_N
    assert col0_blk % n == 0
    return pl.pallas_call(
        functools.partial(_s5_body, reverse=reverse),
        grid=(n_batch, ngb, ns),
        in_specs=[pl.BlockSpec((SEQ_BLOCK, n * LANES), lambda b, g, s: (blk(b, s), col0_blk // n + g)),
                  pl.BlockSpec((n, LANES, W2), lambda b, g, s: (g, 0, 0)),
                  pl.BlockSpec((n, W2, LANES), lambda b, g, s: (g, 0, 0)),
                  pl.BlockSpec((n, S5_TAB_SLABS * SUBLANES, W2 // 2), lambda b, g, s: (g, 0, 0))],
        out_specs=pl.BlockSpec((SEQ_BLOCK, n * LANES), lambda b, g, s: (blk(b, s), g)),
        out_shape=jax.ShapeDtypeStruct((R, S5_W), F32),
        scratch_shapes=[pltpu.VMEM((n, SEQ_BLOCK, W2), F32), pltpu.VMEM((n, SUBLANES, W2), F32),
                        pltpu.VMEM((SEQ_BLOCK, SEQ_BLOCK), BF16)],
        compiler_params=_cparams(("parallel", "parallel", "arbitrary")),
        name="s5_scan_bwd" if reverse else "s5_scan_fwd",
    )(p, bw, cw, tab)


def _s5_weights(A_re, A_im, log_dt, B_re, B_im, C_re, C_im):
    lam_re = jnp.minimum(A_re, -1e-4)
    dt = jnp.exp(log_dt)[:, None]
    mag = jnp.exp(lam_re * dt)
    ab_re, ab_im = mag * jnp.cos(A_im * dt), mag * jnp.sin(A_im * dt)
    den = lam_re * lam_re + A_im * A_im
    f_re = ((ab_re - 1) * lam_re + ab_im * A_im) / den
    f_im = (ab_im * lam_re - (ab_re - 1) * A_im) / den
    bb_re = f_re[..., None] * B_re - f_im[..., None] * B_im
    bb_im = f_re[..., None] * B_im + f_im[..., None] * B_re
    ngb = S5_G // S5_GB
    eye = jnp.eye(S5_GB, dtype=F32)

    def in_block(bb):
        t = bb.reshape(ngb, S5_GB, S5_N, S5_P)
        return jnp.einsum("agnp,gh->agphn", t, eye).reshape(ngb, S5_GB * S5_P, S5_GB * S5_N)

    def out_block(cc):
        t = cc.reshape(ngb, S5_GB, S5_P, S5_N)
        return jnp.einsum("agpn,gh->agnhp", t, eye).reshape(ngb, S5_GB * S5_N, S5_GB * S5_P)

    bw = jnp.concatenate([in_block(bb_re), in_block(bb_im)], axis=2).astype(BF16)
    cw = jnp.concatenate([out_block(C_re), -out_block(C_im)], axis=1).astype(BF16)

    def power(k):
        m = jnp.exp(k * lam_re * dt)
        return m * jnp.cos(k * A_im * dt), m * jnp.sin(k * A_im * dt)

    rep = lambda t: jnp.broadcast_to(t[None], (SUBLANES,) + t.shape)
    slab_pow = [power(float(j + 1)) for j in range(S5_SEG)]
    seg_pow = [power(float(S5_SEG * (i + 1))) for i in range(SUBLANES)]
    slabs = ([rep(c[0]) for c in slab_pow] + [rep(c[1]) for c in slab_pow]
             + [jnp.stack([c[0] for c in seg_pow]), jnp.stack([c[1] for c in seg_pow])])
    for k in (1.0, float(S5_SEG), 2.0 * S5_SEG, 4.0 * S5_SEG):
        slabs += [rep(t) for t in power(k)]
    tab = jnp.concatenate(slabs, axis=0)
    tab = tab.reshape(S5_TAB_SLABS * SUBLANES, ngb, S5_GB * S5_N).transpose(1, 0, 2)
    return bw, cw, tab


def _ssd_body(x_ref, b_ref, c_ref, dt_ref, dta_ref, y_ref, st_ref, *, reverse):
    Q = SSD_CHUNK
    s = pl.program_id(1)

    @pl.when(s == 0)
    def _():
        st_ref[...] = jnp.zeros_like(st_ref)

    row = lax.broadcasted_iota(jnp.int32, (Q, Q), 0)
    col = lax.broadcasted_iota(jnp.int32, (Q, Q), 1)
    mask = (col >= row) if reverse else (col <= row)
    tri = jnp.where(mask, 1.0, 0.0).astype(BF16)
    head0 = lax.broadcasted_iota(jnp.int32, (Q, LANES), 1) < SSD_HD
    head0_rows = lax.broadcasted_iota(jnp.int32, (LANES, SSD_N), 0) < SSD_HD

    h_lo = SSD_HEADS if reverse else 0
    dt = dt_ref[:, h_lo:h_lo + SSD_HEADS]
    dta = dta_ref[:, h_lo:h_lo + SSD_HEADS]
    dtt = dt_ref[...].T[h_lo:h_lo + SSD_HEADS, :]
    dtat = dta_ref[...].T[h_lo:h_lo + SSD_HEADS, :]
    acum = _dot_ones_x(tri, dta)
    acum_t = _dot_x_ones_nt(dtat, tri)
    total = jnp.sum(dta, axis=0, keepdims=True)
    total_t = jnp.sum(dtat, axis=1, keepdims=True)
    e_cum = jnp.exp(acum)
    to_end = jnp.exp(total - acum) * dt
    e_tot_t = jnp.exp(total_t)
    heads_per_group = SSD_HEADS // SSD_GROUPS
    for g in range(SSD_GROUPS):
        bg = b_ref[:, g * SSD_N:(g + 1) * SSD_N]
        cg = c_ref[:, g * SSD_N:(g + 1) * SSD_N]
        cb = _dot_nt(cg, bg)
        for pr in range(heads_per_group // 2):
            h0 = g * heads_per_group + 2 * pr
            xp = x_ref[:, h0 * SSD_HD:(h0 + 2) * SSD_HD]
            yd = []
            for h in (h0, h0 + 1):
                seg = acum[:, h:h + 1] - acum_t[h:h + 1, :]
                w = cb * jnp.exp(jnp.where(mask, seg, -jnp.inf)) * dtt[h:h + 1, :]
                yd.append(_dot(w, xp))
            y_diag = jnp.where(head0, yd[0], yd[1])
            st = st_ref[h0 * SSD_HD:(h0 + 2) * SSD_HD, :]
            scale = jnp.where(head0, e_cum[:, h0:h0 + 1], e_cum[:, h0 + 1:h0 + 2])
            y_ref[:, h0 * SSD_HD:(h0 + 2) * SSD_HD] = y_diag + _dot_nt(cg, st) * scale
            te = jnp.where(head0, to_end[:, h0:h0 + 1], to_end[:, h0 + 1:h0 + 2])
            dec = jnp.where(head0_rows, e_tot_t[h0:h0 + 1, :], e_tot_t[h0 + 1:h0 + 2, :])
            st_ref[h0 * SSD_HD:(h0 + 2) * SSD_HD, :] = st * dec + _dot_tn(xp * te, bg)


def _ssd_scan(xbc, dt, dta, *, n_batch, n_ctx_blk, n_x_blk, reverse):
    R = xbc.shape[0]
    Q = SSD_CHUNK
    blk = functools.partial(_seq_block, n_ctx_blk=n_ctx_blk, n_x_blk=n_x_blk, n_batch=n_batch, reverse=reverse)
    ns = n_ctx_blk + n_x_blk
    gw = SSD_GROUPS * SSD_N
    row_spec = lambda w, c: pl.BlockSpec((Q, w), lambda b, s: (blk(b, s), c))
    return pl.pallas_call(
        functools.partial(_ssd_body, reverse=reverse),
        grid=(n_batch, ns),
        in_specs=[row_spec(SSD_W, 0), row_spec(gw, SSD_W // gw), row_spec(gw, SSD_W // gw + 1),
                  row_spec(LANES, 0), row_spec(LANES, 0)],
        out_specs=row_spec(SSD_W, 0),
        out_shape=jax.ShapeDtypeStruct((R, SSD_W), F32),
        scratch_shapes=[pltpu.VMEM((SSD_W, SSD_N), F32)],
        compiler_params=_cparams(("parallel", "arbitrary")),
        name="ssd_scan_bwd" if reverse else "ssd_scan_fwd",
    )(xbc, xbc, xbc, dt, dta)


def _attn_body(q_ref, kp_ref, kc_ref, kn_ref, vp_ref, vc_ref, vn_ref, kx_ref, vx_ref, sink_ref, mix_ref, o_ref,
               *, seq_len):
    del mix_ref
    n = pl.program_id(1)
    blk = ATT_BLOCK
    n_ctx = kx_ref.shape[0]
    rep = ATT_HEADS // KV_HEADS
    k_all = jnp.concatenate([kx_ref[...], kp_ref[...], kc_ref[...], kn_ref[...]], axis=0).astype(BF16)
    v_all = jnp.concatenate([vx_ref[...], vp_ref[...], vc_ref[...], vn_ref[...]], axis=0).astype(BF16)
    nk = n_ctx + 3 * blk
    r_idx = lax.broadcasted_iota(jnp.int32, (rep * blk, nk), 0)
    q_pos = n * blk + r_idx % blk
    j = lax.broadcasted_iota(jnp.int32, (rep * blk, nk), 1)
    k_pos = (n - 1) * blk + (j - n_ctx)
    valid = (j < n_ctx) | ((jnp.abs(q_pos - k_pos) <= WINDOW) & (k_pos >= 0) & (k_pos < seq_len))
    scale = ATT_HD ** -0.5
    head_of_row = lax.broadcasted_iota(jnp.int32, (rep * blk, 1), 0) // blk
    G = range(KV_HEADS)
    hd = lambda t, h: t[:, h * ATT_HD:(h + 1) * ATT_HD]
    q = q_ref[...].astype(BF16)
    qs = [jnp.concatenate([hd(q, g * rep + i) for i in range(rep)], axis=0) for g in G]
    sinks = []
    for g in G:
        sk = jnp.zeros((rep * blk, 1), F32)
        for i in range(rep):
            sk = jnp.where(head_of_row == i, sink_ref[0:1, g * rep + i:g * rep + i + 1], sk)
        sinks.append(sk)
    sc = [jnp.where(valid, lax.dot_general(qs[g], hd(k_all, g), (((1,), (1,)), ((), ())),
                                           preferred_element_type=F32) * scale, -jnp.inf) for g in G]
    m = [jnp.maximum(jnp.max(sc[g], axis=-1, keepdims=True), sinks[g]) for g in G]
    e = [jnp.exp(sc[g] - m[g]) for g in G]
    den = [jnp.sum(e[g], axis=-1, keepdims=True) + jnp.exp(sinks[g] - m[g]) for g in G]
    pv = [jnp.dot((e[g] / den[g]).astype(BF16), hd(v_all, g), preferred_element_type=F32) for g in G]
    outs = [pv[g][i * blk:(i + 1) * blk] for g in G for i in range(rep)]
    o_ref[...] = jnp.concatenate(outs, axis=1).astype(o_ref.dtype)


def _window_attn(q, k, p, v_col, sink, mix, mix_col, *, n_batch, n_ctx, seq_len):
    nb = seq_len // ATT_BLOCK
    x0 = n_batch * n_ctx // ATT_BLOCK
    vc = v_col // ATT_KV_W
    cur = lambda b, n: x0 + b * nb + n
    prev = lambda b, n: x0 + b * nb + jnp.maximum(n - 1, 0)
    nxt = lambda b, n: x0 + b * nb + jnp.minimum(n + 1, nb - 1)
    k_spec = lambda im: pl.BlockSpec((ATT_BLOCK, ATT_KV_W), lambda b, n: (im(b, n), 0))
    v_spec = lambda im: pl.BlockSpec((ATT_BLOCK, ATT_KV_W), lambda b, n: (im(b, n), vc))
    return pl.pallas_call(
        functools.partial(_attn_body, seq_len=seq_len),
        grid=(n_batch, nb),
        in_specs=[pl.BlockSpec((ATT_BLOCK, ATT_Q_W), lambda b, n: (cur(b, n), 0)),
                  k_spec(prev), k_spec(cur), k_spec(nxt), v_spec(prev), v_spec(cur), v_spec(nxt),
                  pl.BlockSpec((n_ctx, ATT_KV_W), lambda b, n: (b, 0)),
                  pl.BlockSpec((n_ctx, ATT_KV_W), lambda b, n: (b, vc)),
                  pl.BlockSpec((1, ATT_HEADS), lambda b, n: (0, 0)),
                  pl.BlockSpec(memory_space=pl.ANY)],
        out_specs=pl.BlockSpec((ATT_BLOCK, ATT_Q_W), lambda b, n: (b * nb + n, mix_col // ATT_Q_W)),
        out_shape=jax.ShapeDtypeStruct(mix.shape, mix.dtype),
        input_output_aliases={10: 0},
        compiler_params=_cparams(("parallel", "parallel")),
        name="window_attn",
    )(q, k, k, k, p, p, p, k, p, sink, mix)


def _tile_edges(row0, tm, n_ctx_rows, ctx_len, seq_len):
    in_ctx = row0 < n_ctx_rows
    pos0 = jnp.where(in_ctx, row0 % ctx_len, (row0 - n_ctx_rows) % seq_len)
    seg = jnp.where(in_ctx, ctx_len, seq_len)
    return pos0 == 0, pos0 + tm == seg


def _shifted_rows(cur, k, prev_blk, next_blk, first, last, rows):
    tm = cur.shape[0]
    if k < 0:
        edge = jnp.where(first, 0.0, pltpu.roll(prev_blk, -k, axis=0))
        body = pltpu.roll(cur, -k, axis=0)
        out = body
        for j in range(-k):
            out = jnp.where(rows == j, edge[j:j + 1, :], out)
        return out
    edge = jnp.where(last, 0.0, next_blk)
    out = pltpu.roll(cur, tm - k, axis=0)
    for j in range(k):
        out = jnp.where(rows == tm - k + j, edge[j:j + 1, :], out)
    return out


def _head_sum_mxu(t, blk):
    n = t.shape[1] // LANES
    return jnp.concatenate([_dot_x_ones_nt(t[:, j * LANES:(j + 1) * LANES], blk) for j in range(n)], axis=1)


def _same_head_matrix():
    r = lax.broadcasted_iota(jnp.int32, (LANES, LANES), 0) // RWKV_HD
    c = lax.broadcasted_iota(jnp.int32, (LANES, LANES), 1) // RWKV_HD
    return jnp.where(r == c, 1.0, 0.0).astype(BF16)


def _softplus(x):
    return jnp.maximum(x, 0.0) + jnp.log(1.0 + jnp.exp(-jnp.abs(x)))


def _rwkv_pre_body(p_ref, hp_ref, hn_ref, mu_ref, lora_ref, gw_ref, w0_ref, a0_ref, kk_ref, ka_ref, rk_ref,
                   r_out, v_out, kk_out, g_out, bonus_out, lw_out, kd_out, be_out, *, tm, geo):
    W = RWKV_W
    first, last = _tile_edges(pl.program_id(0) * tm, tm, *geo)
    rows = lax.broadcasted_iota(jnp.int32, (tm, 1), 0)

    def tshift(c0, c1):
        cur = p_ref[:, c0:c1]
        prev = _shifted_rows(cur, -1, hp_ref[:, c0:c1], hn_ref[:, c0:c1], first, last, rows)
        nxt = _shifted_rows(cur, 1, hp_ref[:, c0:c1], hn_ref[:, c0:c1], first, last, rows)
        return cur + mu_ref[0:1, c0:c1] * (prev - cur) + mu_ref[1:2, c0:c1] * (nxt - cur)

    r = tshift(0, W)
    k = tshift(W, 2 * W)
    v = tshift(2 * W, 3 * W)
    xl = tshift(3 * W, 3 * W + 4 * LANES)
    lane = lax.broadcasted_iota(jnp.int32, (tm, 2 * LANES), 1)
    lin = jnp.where(lane < DECAY_LORA, jnp.tanh(xl[:, :2 * LANES]), xl[:, :2 * LANES])
    lora = _dot(lin, lora_ref[...])
    g_out[...] = _dot(jax.nn.sigmoid(xl), gw_ref[...])
    blk = _same_head_matrix()
    kk = k * kk_ref[...]
    kk = kk / jnp.maximum(jnp.sqrt(_head_sum_mxu(kk * kk, blk)), 1e-12)
    r_out[...] = r
    v_out[...] = v
    kk_out[...] = kk
    kd_sum = jnp.zeros_like(k)
    for d in range(2):
        w_log = -_softplus(-(w0_ref[d:d + 1, :] + lora[:, d * W:(d + 1) * W])) - 0.5
        a = jax.nn.sigmoid(a0_ref[d:d + 1, :] + lora[:, (2 + d) * W:(3 + d) * W])
        kd = k * (1.0 + (a - 1.0) * ka_ref[...])
        lw_out[d] = -jnp.exp(w_log)
        kd_out[d] = kd
        be_out[d] = a * kk
        kd_sum = kd_sum + kd
    bonus_out[...] = _head_sum_mxu(r * kd_sum * rk_ref[...], blk) * v


def _rwkv_pre(p, mu, lora_w, gate_w, w0, a0, k_k, k_a, r_k, *, geo):
    R = p.shape[0]
    W = RWKV_W
    tm = SEQ_BLOCK
    pw = 3 * W + 4 * LANES
    nb8 = R // SUBLANES
    row = lambda i: (i, 0)
    vec = lambda n: pl.BlockSpec((n, W), lambda i: (0, 0))
    out2 = pl.BlockSpec((2, tm, W), lambda i: (0, i, 0))
    f = jax.ShapeDtypeStruct((R, W), F32)
    f2 = jax.ShapeDtypeStruct((2, R, W), F32)
    return pl.pallas_call(
        functools.partial(_rwkv_pre_body, tm=tm, geo=geo),
        grid=(R // tm,),
        in_specs=[pl.BlockSpec((tm, pw), row),
                  pl.BlockSpec((SUBLANES, pw), lambda i: (jnp.maximum(i * (tm // SUBLANES) - 1, 0), 0)),
                  pl.BlockSpec((SUBLANES, pw), lambda i: (jnp.minimum((i + 1) * (tm // SUBLANES), nb8 - 1), 0)),
                  pl.BlockSpec((2, pw), lambda i: (0, 0)),
                  pl.BlockSpec(lora_w.shape, lambda i: (0, 0)),
                  pl.BlockSpec(gate_w.shape, lambda i: (0, 0)),
                  vec(2), vec(2), vec(1), vec(1), vec(1)],
        out_specs=[pl.BlockSpec((tm, W), row)] * 5 + [out2] * 3,
        out_shape=[f] * 5 + [f2] * 3,
        compiler_params=_cparams(("parallel",)),
        name="rwkv_pre",
    )(p, p, p, mu, lora_w, gate_w, w0, a0, k_k, k_a, r_k)


def _rwkv_post_body(yf_ref, yb_ref, bonus_ref, g_ref, lng_ref, lnb_ref, o_ref):
    blk = _same_head_matrix()
    y = yf_ref[...] + yb_ref[...]
    mean = _head_sum_mxu(y, blk) * (1.0 / RWKV_HD)
    d = y - mean
    var = _head_sum_mxu(d * d, blk) * (1.0 / RWKV_HD)
    yn = d * lax.rsqrt(var + GN_EPS)
    o_ref[...] = ((yn * lng_ref[...] + lnb_ref[...] + bonus_ref[...]) * g_ref[...]).astype(o_ref.dtype)


def _rwkv_post(yf, yb, bonus, g, ln_g, ln_b, out_cols):
    R, W = yf.shape
    tm = SEQ_BLOCK
    row = pl.BlockSpec((tm, W), lambda i: (i, 0))
    vec = pl.BlockSpec((1, W), lambda i: (0, 0))
    return pl.pallas_call(
        _rwkv_post_body,
        grid=(R // tm,),
        in_specs=[row, row, row, row, vec, vec],
        out_specs=row,
        out_shape=jax.ShapeDtypeStruct((R, out_cols), BF16),
        compiler_params=_cparams(("parallel",)),
        name="rwkv_post",
    )(yf, yb, bonus, g, ln_g, ln_b)


def _gelu_tanh(x):
    return 0.5 * x * (1.0 + jnp.tanh(math.sqrt(2.0 / math.pi) * (x + 0.044715 * (x * x * x))))


def _s5_post_body(u0_ref, u1_ref, yf_ref, yb_ref, d_ref, w_ref, b_ref, mix_ref, o_ref):
    del mix_ref
    u = jnp.concatenate([u0_ref[...], u1_ref[...]], axis=1)
    y = _gelu_tanh(d_ref[...] * u + yf_ref[...] + yb_ref[...])
    o_ref[...] = (y * jax.nn.sigmoid(_dot(y, w_ref[...]) + b_ref[...])).astype(o_ref.dtype)


def _s5_post(p, col0, yf, yb, d_skip, glu_w, glu_b, mix):
    R, W = yf.shape
    tm = SEQ_BLOCK
    half = W // 2
    row = pl.BlockSpec((tm, W), lambda i: (i, 0))
    vec = pl.BlockSpec((1, W), lambda i: (0, 0))
    return pl.pallas_call(
        _s5_post_body,
        grid=(R // tm,),
        in_specs=[pl.BlockSpec((tm, half), lambda i: (i, col0 // half)),
                  pl.BlockSpec((tm, half), lambda i: (i, col0 // half + 1)), row, row, vec,
                  pl.BlockSpec((W, W), lambda i: (0, 0)), vec, pl.BlockSpec(memory_space=pl.ANY)],
        out_specs=pl.BlockSpec((tm, W), lambda i: (i, 1)),
        out_shape=jax.ShapeDtypeStruct(mix.shape, mix.dtype),
        input_output_aliases={7: 0},
        compiler_params=_cparams(("parallel",)),
        name="s5_post",
    )(p, p, yf, yb, d_skip, glu_w, glu_b, mix)


def _cd_conv_body(x_ref, hp_ref, hn_ref, w_ref, b_ref, o_ref, *, tm, geo, taps):
    first, last = _tile_edges(pl.program_id(0) * tm, tm, *geo)
    rows = lax.broadcasted_iota(jnp.int32, (tm, 1), 0)
    cur = x_ref[...]
    half = taps // 2
    acc = b_ref[...] + w_ref[half:half + 1, :] * cur
    for j in range(taps):
        if j != half:
            acc = acc + w_ref[j:j + 1, :] * _shifted_rows(cur, j - half, hp_ref[...], hn_ref[...], first, last, rows)
    o_ref[...] = acc * jax.nn.sigmoid(acc)


def _cd_conv(p, col0, width, conv_w, conv_b, *, geo):
    R = p.shape[0]
    tm = SEQ_BLOCK
    tw = 1024
    taps = conv_w.shape[0]
    nb8 = R // SUBLANES
    c0 = col0 // tw
    w8 = jnp.pad(conv_w, ((0, SUBLANES - taps), (0, 0)))
    return pl.pallas_call(
        functools.partial(_cd_conv_body, tm=tm, geo=geo, taps=taps),
        grid=(R // tm, width // tw),
        in_specs=[pl.BlockSpec((tm, tw), lambda i, j: (i, c0 + j)),
                  pl.BlockSpec((SUBLANES, tw), lambda i, j: (jnp.maximum(i * (tm // SUBLANES) - 1, 0), c0 + j)),
                  pl.BlockSpec((SUBLANES, tw), lambda i, j: (jnp.minimum((i + 1) * (tm // SUBLANES), nb8 - 1), c0 + j)),
                  pl.BlockSpec((SUBLANES, tw), lambda i, j: (0, j)),
                  pl.BlockSpec((1, tw), lambda i, j: (0, j))],
        out_specs=pl.BlockSpec((tm, tw), lambda i, j: (i, j)),
        out_shape=jax.ShapeDtypeStruct((R, width), F32),
        compiler_params=_cparams(("parallel", "parallel")),
        name="cd_conv",
    )(p, p, p, w8, conv_b.reshape(1, width))


def _rope_tile(x, cos, sin):
    lane = lax.broadcasted_iota(jnp.int32, x.shape, 1)
    q = ATT_HD // 4
    partner = jnp.where(lane % (2 * q) < q, pltpu.roll(x, LANES - q, axis=1), pltpu.roll(x, q, axis=1))
    return x * cos + partner * sin


def _cd_rope_dt_body(q_ref, k_ref, dtr_ref, cos_ref, sin_ref, bias_ref, a_ref, q_out, k_out, dt_out, dta_out,
                     *, tm, n_ctx_rows):
    in_ctx = pl.program_id(0) * tm < n_ctx_rows
    cos = jnp.where(in_ctx, 1.0, cos_ref[...])
    sin = jnp.where(in_ctx, 0.0, sin_ref[...])
    for j in range(ATT_Q_W // LANES):
        q_out[:, j * LANES:(j + 1) * LANES] = _rope_tile(q_ref[:, j * LANES:(j + 1) * LANES], cos, sin)
    for j in range(ATT_KV_W // LANES):
        k_out[:, j * LANES:(j + 1) * LANES] = _rope_tile(k_ref[:, j * LANES:(j + 1) * LANES], cos, sin)
    dt = _softplus(dtr_ref[...] + bias_ref[...])
    dt_out[...] = dt
    dta_out[...] = dt * a_ref[...]


def _cd_rope_dt(p, q_col, k_col, dt_col, cos, sin, dt_bias, a_neg, *, n_ctx_rows, seq_len):
    R = p.shape[0]
    tm = SEQ_BLOCK
    tab = pl.BlockSpec((tm, LANES), lambda i: (jnp.maximum(i * tm - n_ctx_rows, 0) % seq_len // tm, 0))
    vec = pl.BlockSpec((1, LANES), lambda i: (0, 0))
    return pl.pallas_call(
        functools.partial(_cd_rope_dt_body, tm=tm, n_ctx_rows=n_ctx_rows),
        grid=(R // tm,),
        in_specs=[pl.BlockSpec((tm, ATT_Q_W), lambda i: (i, q_col // ATT_Q_W)),
                  pl.BlockSpec((tm, ATT_KV_W), lambda i: (i, k_col // ATT_KV_W)),
                  pl.BlockSpec((tm, LANES), lambda i: (i, dt_col // LANES)),
                  tab, tab, vec, vec],
        out_specs=[pl.BlockSpec((tm, ATT_Q_W), lambda i: (i, 0)), pl.BlockSpec((tm, ATT_KV_W), lambda i: (i, 0)),
                   pl.BlockSpec((tm, LANES), lambda i: (i, 0)), pl.BlockSpec((tm, LANES), lambda i: (i, 0))],
        out_shape=[jax.ShapeDtypeStruct((R, ATT_Q_W), F32), jax.ShapeDtypeStruct((R, ATT_KV_W), F32),
                   jax.ShapeDtypeStruct((R, LANES), F32), jax.ShapeDtypeStruct((R, LANES), F32)],
        compiler_params=_cparams(("parallel",)),
        name="cd_rope_dt",
    )(p, p, p, cos, sin, dt_bias, a_neg)


def _ssd_post_body(yf_ref, yb_ref, x_ref, z_ref, d_ref, g_ref, o_ref):
    z = z_ref[...]
    y = (yf_ref[...] + yb_ref[...] + d_ref[...] * x_ref[...]) * (z * jax.nn.sigmoid(z))
    gw = SSD_W // SSD_GROUPS
    for g in range(SSD_GROUPS):
        yg = y[:, g * gw:(g + 1) * gw]
        ms = jnp.mean(yg * yg, axis=-1, keepdims=True)
        o_ref[:, g * gw:(g + 1) * gw] = (yg * lax.rsqrt(ms + EPS) * g_ref[:, g * gw:(g + 1) * gw]).astype(o_ref.dtype)


def _ssd_post(yf, yb, xbc, p, d_skip, norm_g, out_cols, *, n_ctx_rows):
    R = yf.shape[0]
    tm = SEQ_BLOCK
    r0 = n_ctx_rows // tm
    row = pl.BlockSpec((tm, SSD_W), lambda i: (r0 + i, 0))
    vec = pl.BlockSpec((1, SSD_W), lambda i: (0, 0))
    return pl.pallas_call(
        _ssd_post_body,
        grid=((R - n_ctx_rows) // tm,),
        in_specs=[row, row, row, row, vec, vec],
        out_specs=pl.BlockSpec((tm, SSD_W), lambda i: (i, 0)),
        out_shape=jax.ShapeDtypeStruct((R - n_ctx_rows, out_cols), BF16),
        compiler_params=_cparams(("parallel",)),
        name="ssd_post",
    )(yf, yb, xbc, p, d_skip, norm_g)


def _segment_pos(R, RC, C, L):
    row = jnp.arange(R, dtype=jnp.int32)
    in_ctx = row < RC
    pos = jnp.where(in_ctx, row % C, (row - RC) % L)
    seg = jnp.where(in_ctx, C, L)
    return pos, seg


def _shift_rows(p, k, pos, seg):
    ok = (pos + k >= 0) & (pos + k < seg)
    return jnp.where(ok[:, None], jnp.roll(p, -k, axis=0), 0.0)


def _head_sum(t):
    R = t.shape[0]
    s = jnp.sum(t.reshape(R, RWKV_HEADS, RWKV_HD), axis=-1, keepdims=True)
    return jnp.broadcast_to(s, (R, RWKV_HEADS, RWKV_HD)).reshape(R, RWKV_W)


def _rope_tables(L):
    nf = ATT_HD // 4
    inv_freq = ROPE_BASE ** (-jnp.arange(nf, dtype=F32) / nf)
    t = jnp.arange(L, dtype=jnp.int32)
    row_id = (t // GRID_W).astype(F32)
    col_id = (t % GRID_W).astype(F32)
    ar = row_id[:, None] * inv_freq
    ac = col_id[:, None] * inv_freq
    cos = jnp.concatenate([jnp.cos(ar), jnp.cos(ar), jnp.cos(ac), jnp.cos(ac)], axis=1)
    sin = jnp.concatenate([-jnp.sin(ar), jnp.sin(ar), -jnp.sin(ac), jnp.sin(ac)], axis=1)
    return cos, sin


def _rope(t, cos, sin, n_heads):
    R = t.shape[0]
    q = ATT_HD // 4
    tr = t.reshape(R, n_heads, 2, 2, q)
    swapped = jnp.stack([tr[..., 1, :], tr[..., 0, :]], axis=-2).reshape(R, n_heads, ATT_HD)
    out = t.reshape(R, n_heads, ATT_HD) * cos[:, None, :] + swapped * sin[:, None, :]
    return out.reshape(R, n_heads * ATT_HD)


def kernel(x, c, ctx, c_ctx, ada_w, ada_b, norm1_g, norm2_g, mlp_w1, mlp_w2, final_g, ab_w_in, ab_w_out, rwkv_mu, rwkv_w0, rwkv_w_up, rwkv_a0, rwkv_a_up, rwkv_g_up, rwkv_k_k, rwkv_k_a, rwkv_r_k, rwkv_ln_g, rwkv_ln_b, s5_A_re, s5_A_im, s5_log_dt, s5_B_re, s5_B_im, s5_C_re, s5_C_im, s5_D, s5_glu_w, s5_glu_b, cd_w_in, cd_w_out, ssd_conv_w, ssd_conv_b, ssd_A_log, ssd_dt_bias, ssd_D, ssd_norm_g, attn_sink):
    B, L, D = x.shape
    C = ctx.shape[1]
    RC, RX = B * C, B * L
    R = RC + RX
    assert C % SEQ_BLOCK == 0 and L % SEQ_BLOCK == 0
    geo = dict(n_batch=B, n_ctx_blk=C // SEQ_BLOCK, n_x_blk=L // SEQ_BLOCK)
    geo_ssd = dict(n_batch=B, n_ctx_blk=C // SSD_CHUNK, n_x_blk=L // SSD_CHUNK)
    group_all = lambda r0: jnp.where(r0 < RC, 0, 1 + (r0 - RC) // L)
    group_x = lambda r0: 1 + r0 // L

    xs = jnp.concatenate([ctx.reshape(RC, D), x.reshape(RX, D)], axis=0)
    cond = jax.nn.silu(jnp.concatenate([c_ctx[None, :], c], axis=0))
    cond = jnp.pad(cond, ((0, SUBLANES - (B + 1) % SUBLANES), (0, 0))) if (B + 1) % SUBLANES else cond
    n_groups = cond.shape[0]

    def modulation(i):
        mod = _mm(cond, ada_w[i], name="ada") + ada_b[i]
        return [m.reshape(n_groups, 1, D) for m in jnp.split(mod, 6, axis=-1)]

    def mlp(xin, i, sh2, sc2, g2, gidx):
        h2 = _norm_mod(xin, norm2_g[i], sh2, sc2, gidx)
        m1 = _mm(h2, mlp_w1[i], out_dtype=BF16, act="relu2", name="mlp_up")
        return _mm(m1, mlp_w2[i], res=xin, gate=g2, gate_idx=gidx, name="mlp_down")

    sh1, sc1, g1, sh2, sc2, g2 = modulation(0)
    h = _norm_mod(xs, norm1_g[0], sh1, sc1, group_all)
    pad_cols = (-RWKV_COLS) % LANES
    w_in = ab_w_in[0]
    w_in = jnp.concatenate([w_in[:, :RWKV_COLS], jnp.zeros((D, pad_cols), F32), w_in[:, RWKV_COLS:]], axis=1)
    s5_col0 = RWKV_COLS + pad_cols
    p = _mm(h, w_in.astype(BF16), name="ab_in")

    W = RWKV_W
    lora_rows = 2 * LANES
    zw = jnp.zeros((DECAY_LORA, W), F32)
    lora_w = jnp.concatenate([
        jnp.concatenate([rwkv_w_up[0, 0], rwkv_w_up[0, 1], zw, zw], axis=1),
        jnp.concatenate([zw, zw, rwkv_a_up[0, 0], rwkv_a_up[0, 1]], axis=1),
        jnp.zeros((lora_rows - DECAY_LORA - AAA_LORA, 4 * W), F32)], axis=0).astype(BF16)
    gate_w = jnp.pad(rwkv_g_up[0], ((DECAY_LORA + AAA_LORA, pad_cols), (0, 0))).astype(BF16)
    mu = jnp.pad(rwkv_mu[0], ((0, 0), (0, pad_cols)))
    vec = lambda t: t.reshape(1, W)
    r, v, kk, g_gate, bonus, lw, kd, be = _rwkv_pre(
        p, mu, lora_w, gate_w, rwkv_w0[0], rwkv_a0[0], vec(rwkv_k_k[0]), vec(rwkv_k_a[0]), vec(rwkv_r_k[0]),
        geo=(RC, C, L))
    y_dirs = [_rwkv_scan(r, v, kk, lw, kd, be, reverse=(d == 1), **geo) for d in range(2)]
    mix = _rwkv_post(y_dirs[0], y_dirs[1], bonus, g_gate, vec(rwkv_ln_g[0]), vec(rwkv_ln_b[0]), W + S5_W)

    ys_dirs = []
    for d in range(2):
        bw, cw, tab = _s5_weights(s5_A_re[0, d], s5_A_im[0, d], s5_log_dt[0, d], s5_B_re[0, d], s5_B_im[0, d],
                                  s5_C_re[0, d], s5_C_im[0, d])
        ys_dirs.append(_s5_scan(p, s5_col0 // LANES, bw, cw, tab, reverse=(d == 1), **geo))
    mix = _s5_post(p, s5_col0, ys_dirs[0], ys_dirs[1], vec(s5_D[0]), s5_glu_w[0].astype(BF16),
                   vec(s5_glu_b[0]), mix)
    xs = _mm(mix, ab_w_out[0].astype(BF16), res=xs, gate=g1, gate_idx=group_all, name="ab_out")
    xs = mlp(xs, 0, sh2, sc2, g2, group_all)

    sh1, sc1, g1, sh2, sc2, g2 = modulation(1)
    h = _norm_mod(xs, norm1_g[1], sh1, sc1, group_all)
    cd_in = cd_w_in[0]
    o_dt = SSD_W + XBC_W
    o_q = o_dt + 2 * SSD_HEADS
    n_qkv = ATT_Q_W + 2 * ATT_KV_W
    cd_in = jnp.concatenate([cd_in[:, :o_dt], cd_in[:, o_q:o_q + n_qkv], cd_in[:, o_dt:o_q]], axis=1)
    cd_in = jnp.pad(cd_in, ((0, 0), (0, (-cd_in.shape[1]) % 512)))
    q_col = o_dt
    k_col = q_col + ATT_Q_W
    v_col = k_col + ATT_KV_W
    dt_col = v_col + ATT_KV_W
    p = _mm(h, cd_in.astype(BF16), name="cd_in")

    xbc = _cd_conv(p, SSD_W, XBC_W, ssd_conv_w[0], ssd_conv_b[0], geo=(RC, C, L))
    cos, sin = _rope_tables(L)
    lane_pad = lambda t: jnp.pad(t.reshape(1, 2 * SSD_HEADS), ((0, 0), (0, LANES - 2 * SSD_HEADS)))
    q_rot, k_rot, dt, dta = _cd_rope_dt(
        p, q_col, k_col, dt_col, jnp.tile(cos, (1, LANES // ATT_HD)), jnp.tile(sin, (1, LANES // ATT_HD)),
        lane_pad(ssd_dt_bias[0]), lane_pad(-jnp.exp(ssd_A_log[0])), n_ctx_rows=RC, seq_len=L)
    y_dirs = [_ssd_scan(xbc, dt, dta, reverse=(d == 1), **geo_ssd) for d in range(2)]
    mix = _ssd_post(y_dirs[0], y_dirs[1], xbc, p, jnp.repeat(ssd_D[0], SSD_HD).reshape(1, SSD_W),
                    ssd_norm_g[0].reshape(1, SSD_W), SSD_W + ATT_Q_W, n_ctx_rows=RC)
    mix = _window_attn(q_rot, k_rot, p, v_col, attn_sink[0].reshape(1, ATT_HEADS), mix, SSD_W,
                       n_batch=B, n_ctx=C, seq_len=L)
    xo = _mm(mix, cd_w_out[0].astype(BF16), res=xs, res_row0=RC, gate=g1, gate_idx=group_x, name="cd_out")
    xo = mlp(xo, 1, sh2, sc2, g2, group_x)
    zero = jnp.zeros((1, 1, D), F32)
    out = _norm_mod(xo, final_g, zero, zero, lambda r0: 0, out_dtype=F32)
    return out.reshape(B, L, D)
```

```python
import functools
import math

import jax
import jax.numpy as jnp
from jax import lax
from jax.experimental import pallas as pl
from jax.experimental.pallas import tpu as pltpu

F32 = jnp.float32
BF16 = jnp.bfloat16

EPS = 1e-6
GN_EPS = 64e-5
GRID_W = 64
ROPE_BASE = 10000.0

RWKV_HEADS = 16
RWKV_HD = 64
RWKV_W = RWKV_HEADS * RWKV_HD
DECAY_LORA = 96
AAA_LORA = 96
GATE_LORA = 256
RWKV_COLS = 3 * RWKV_W + DECAY_LORA + AAA_LORA + GATE_LORA
S5_W = 1024
S5_P = 16
S5_G = S5_W // S5_P
S5_N = 64
SSD_HEADS = 32
SSD_HD = 64
SSD_W = SSD_HEADS * SSD_HD
SSD_GROUPS = 4
SSD_N = 128
SSD_CHUNK = 128
XBC_W = SSD_W + 2 * SSD_GROUPS * SSD_N
ATT_HEADS = 16
KV_HEADS = 4
ATT_HD = 64
ATT_Q_W = ATT_HEADS * ATT_HD
ATT_KV_W = KV_HEADS * ATT_HD
WINDOW = 128
ATT_BLOCK = 128

LANES = 128
SUBLANES = 8
SEQ_BLOCK = 256
RWKV_CHUNK = 64
S5_GB = 8
VMEM_LIMIT = 48 * 1024 * 1024


def _cparams(sem):
    return pltpu.CompilerParams(dimension_semantics=sem, vmem_limit_bytes=VMEM_LIMIT)


def _pick(n, cands):
    for c in cands:
        if n % c == 0:
            return c
    return n


def _dot(a, b):
    return jnp.dot(a.astype(BF16), b.astype(BF16), preferred_element_type=F32)


def _dot_nt(a, b):
    return lax.dot_general(a.astype(BF16), b.astype(BF16), (((1,), (1,)), ((), ())), preferred_element_type=F32)


def _dot_tn(a, b):
    return lax.dot_general(a.astype(BF16), b.astype(BF16), (((0,), (0,)), ((), ())), preferred_element_type=F32)


def _split3(x):
    hi = x.astype(BF16)
    r1 = x - hi.astype(F32)
    mid = r1.astype(BF16)
    lo = (r1 - mid.astype(F32)).astype(BF16)
    return hi, mid, lo


def _dot_ones_x(tri, x):
    hi, mid, lo = _split3(x)
    f = lambda t: jnp.dot(tri, t, preferred_element_type=F32)
    return f(hi) + f(mid) + f(lo)


def _dot_x_ones_nt(x, tri):
    hi, mid, lo = _split3(x)
    f = lambda t: lax.dot_general(t, tri, (((1,), (1,)), ((), ())), preferred_element_type=F32)
    return f(hi) + f(mid) + f(lo)


def _mm_body(*refs, nk, act, residual):
    if residual:
        a_ref, b_ref, res_ref, gate_ref, o_ref, acc_ref = refs
    else:
        a_ref, b_ref, o_ref, acc_ref = refs
    k = pl.program_id(2)
    part = jnp.dot(a_ref[...].astype(BF16), b_ref[...].astype(BF16), preferred_element_type=F32)

    def finish(acc):
        if act == "relu2":
            acc = jnp.square(jnp.maximum(acc, 0.0))
        if residual:
            acc = res_ref[...] + gate_ref[0] * acc
        o_ref[...] = acc.astype(o_ref.dtype)

    if nk == 1:
        finish(part)
    else:
        @pl.when(k == 0)
        def _():
            acc_ref[...] = part

        @pl.when(k > 0)
        def _():
            acc_ref[...] += part

        @pl.when(k == nk - 1)
        def _():
            finish(acc_ref[...])


def _mm(a, b, *, b_layer=0, out_dtype=F32, act=None, res=None, res_row0=0, gate=None, gate_idx=None, name="mm"):
    M, K = a.shape
    N = b.shape[-1]
    tm = _pick(M, (1024, 512, 256, 128, 64, 32, 16, 8))
    tn = _pick(N, (512, 384, 256, 128))
    tk = _pick(K, (4096, 3072, 2048, 1536, 1024, 512, 256, 128))
    nk = K // tk
    residual = res is not None
    if b.ndim == 3:
        b_spec = pl.BlockSpec((None, tk, tn), lambda i, j, k: (b_layer, k, j))
    else:
        b_spec = pl.BlockSpec((tk, tn), lambda i, j, k: (k, j))
    in_specs = [pl.BlockSpec((tm, tk), lambda i, j, k: (i, k)), b_spec]
    args = [a, b]
    if residual:
        if res_row0 % tm:
            res, res_row0 = res[res_row0:res_row0 + M], 0
        r0 = res_row0 // tm
        in_specs += [pl.BlockSpec((tm, tn), lambda i, j, k: (r0 + i, j)),
                     pl.BlockSpec((1, 1, tn), lambda i, j, k: (gate_idx(i * tm), 0, j))]
        args += [res, gate]
    return pl.pallas_call(
        functools.partial(_mm_body, nk=nk, act=act, residual=residual),
        grid=(M // tm, N // tn, nk),
        in_specs=in_specs,
        out_specs=pl.BlockSpec((tm, tn), lambda i, j, k: (i, j)),
        out_shape=jax.ShapeDtypeStruct((M, N), out_dtype),
        scratch_shapes=[pltpu.VMEM((tm, tn), F32)],
        compiler_params=_cparams(("parallel", "parallel", "arbitrary")),
        name=name,
    )(*args)


def _mm_res_norm_body(a_ref, b_ref, res_ref, gate_ref, g_ref, sh_ref, sc_ref, *out_refs, nk, keep_x):
    acc_ref = out_refs[-1]
    h_ref = out_refs[-2]
    k = pl.program_id(1)
    part = jnp.dot(a_ref[...], b_ref[...], preferred_element_type=F32)

    @pl.when(k == 0)
    def _():
        acc_ref[...] = part

    @pl.when(k > 0)
    def _():
        acc_ref[...] += part

    @pl.when(k == nk - 1)
    def _():
        x = res_ref[...] + gate_ref[0] * acc_ref[...]
        if keep_x:
            out_refs[0][...] = x
        ms = jnp.mean(x * x, axis=-1, keepdims=True)
        xn = x * lax.rsqrt(ms + EPS) * g_ref[...]
        h_ref[...] = (xn * (1.0 + sc_ref[0]) + sh_ref[0]).astype(h_ref.dtype)


def _mm_res_norm(a, b, res, gate, norm_g, shift, scale, group_idx, *, res_row0=0, keep_x=True, h_dtype=BF16,
                 name="mm_res_norm"):
    M, K = a.shape
    N = b.shape[1]
    tm = _pick(M, (512, 256, 128, 64, 32, 16, 8))
    tk = _pick(K, (1024, 512, 256, 128))
    nk = K // tk
    if res_row0 % tm:
        res, res_row0 = res[res_row0:res_row0 + M], 0
    r0 = res_row0 // tm
    grp = lambda i, k: (group_idx(i * tm), 0, 0)
    row = pl.BlockSpec((tm, N), lambda i, k: (i, 0))
    out_specs = ([row] if keep_x else []) + [row]
    out_shape = ([jax.ShapeDtypeStruct((M, N), F32)] if keep_x else []) + [jax.ShapeDtypeStruct((M, N), h_dtype)]
    out = pl.pallas_call(
        functools.partial(_mm_res_norm_body, nk=nk, keep_x=keep_x),
        grid=(M // tm, nk),
        in_specs=[pl.BlockSpec((tm, tk), lambda i, k: (i, k)),
                  pl.BlockSpec((tk, N), lambda i, k: (k, 0)),
                  pl.BlockSpec((tm, N), lambda i, k: (r0 + i, 0)),
                  pl.BlockSpec((1, 1, N), grp),
                  pl.BlockSpec((1, N), lambda i, k: (0, 0)),
                  pl.BlockSpec((1, 1, N), grp), pl.BlockSpec((1, 1, N), grp)],
        out_specs=out_specs,
        out_shape=out_shape,
        scratch_shapes=[pltpu.VMEM((tm, N), F32)],
        compiler_params=_cparams(("parallel", "arbitrary")),
        name=name,
    )(a, b, res, gate, norm_g.reshape(1, N), shift, scale)
    return out if keep_x else out[0]


def _norm_body(x_ref, g_ref, sh_ref, sc_ref, o_ref):
    x = x_ref[...]
    ms = jnp.mean(x * x, axis=-1, keepdims=True)
    xn = x * lax.rsqrt(ms + EPS) * g_ref[...]
    o_ref[...] = (xn * (1.0 + sc_ref[0]) + sh_ref[0]).astype(o_ref.dtype)


def _norm_mod(x, g, shift, scale, group_idx, out_dtype=BF16):
    M, D = x.shape
    tm = _pick(M, (256, 128, 64, 32, 16, 8))
    return pl.pallas_call(
        _norm_body,
        grid=(M // tm,),
        in_specs=[pl.BlockSpec((tm, D), lambda i: (i, 0)),
                  pl.BlockSpec((1, D), lambda i: (0, 0)),
                  pl.BlockSpec((1, 1, D), lambda i: (group_idx(i * tm), 0, 0)),
                  pl.BlockSpec((1, 1, D), lambda i: (group_idx(i * tm), 0, 0))],
        out_specs=pl.BlockSpec((tm, D), lambda i: (i, 0)),
        out_shape=jax.ShapeDtypeStruct((M, D), out_dtype),
        compiler_params=_cparams(("parallel",)),
        name="norm_mod",
    )(x, g.reshape(1, D), shift, scale)


def _seq_block(b, s, *, n_ctx_blk, n_x_blk, n_batch, reverse):
    if reverse:
        c = b * n_ctx_blk + (n_ctx_blk - 1 - s)
        x = n_batch * n_ctx_blk + b * n_x_blk + (n_x_blk - 1 - (s - n_ctx_blk))
    else:
        c = b * n_ctx_blk + s
        x = n_batch * n_ctx_blk + b * n_x_blk + (s - n_ctx_blk)
    return jnp.where(s < n_ctx_blk, c, x)


def _split2(x):
    hi = x.astype(BF16)
    lo = (x - hi.astype(F32)).astype(BF16)
    return hi, lo


def _rwkv_chunk(lw, r, v, kk, kd, be, st, tri, strict, incl, same_head, head0, eye):
    C = RWKV_CHUNK

    def stack(z):
        return jnp.concatenate([jnp.where(head0, z, 0.0), jnp.where(head0, 0.0, z)], axis=0)

    def unstack(z):
        return z[0:C] + z[C:2 * C]

    bdot = lambda a, b: jnp.dot(a, b, preferred_element_type=F32)
    bdot_nt = lambda a, b: lax.dot_general(a, b, (((1,), (1,)), ((), ())), preferred_element_type=F32)
    bdot_tn = lambda a, b: lax.dot_general(a, b, (((0,), (0,)), ((), ())), preferred_element_type=F32)
    cat = lambda *zs: jnp.concatenate(zs, axis=0)
    P = range(len(lw))
    cum = [_dot_ones_x(tri, lw[p]) for p in P]
    total = [jnp.sum(lw[p], axis=0, keepdims=True) for p in P]
    a_t = [kk[p] * jnp.exp(cum[p] - lw[p]) for p in P]
    r_t = [r[p] * jnp.exp(cum[p]) for p in P]
    a_sp = [_split2(a_t[p]) for p in P]
    r_hi = [r_t[p].astype(BF16) for p in P]
    sa_sp = [_split2(stack(a_t[p])) for p in P]
    sr_hi = [stack(r_t[p]).astype(BF16) for p in P]
    pinv = [jnp.exp(-cum[p]) for p in P]
    bk_sp = [_split2(cat(stack(be[p] * pinv[p]), stack(kd[p] * pinv[p]))) for p in P]
    g1 = [bdot_nt(cat(sa_sp[p][0], sa_sp[p][1], sr_hi[p]), bk_sp[p][0]) for p in P]
    g2 = [bdot_nt(sa_sp[p][0], bk_sp[p][1]) for p in P]
    n_all = [g1[p][0:2 * C] + g1[p][2 * C:4 * C] + g2[p] for p in P]
    x_neg = [jnp.where(strict, -n_all[p][:, 0:2 * C], 0.0) for p in P]
    n_ak = [jnp.where(strict, n_all[p][:, 2 * C:4 * C], 0.0) for p in P]
    m_r = [jnp.concatenate([jnp.where(incl, g1[p][4 * C:6 * C, 0:2 * C], 0.0),
                            jnp.where(incl, g1[p][4 * C:6 * C, 2 * C:4 * C], 0.0)], axis=1).astype(BF16)
           for p in P]
    xb = [x_neg[p].astype(BF16) for p in P]
    t_inv = [eye + x_neg[p] for p in P]
    n_pow = int(math.log2(C))
    for it in range(1, n_pow):
        xb = [bdot(xb[p], xb[p]).astype(BF16) for p in P]
        t_inv = [t_inv[p] + bdot(xb[p], t_inv[p].astype(BF16)) for p in P]
    t_inv = [t_inv[p].astype(BF16) for p in P]
    st_sp = [_split2(st[p]) for p in P]
    as1 = [bdot_nt(cat(a_sp[p][0], a_sp[p][1], r_hi[p]), st_sp[p][0]) for p in P]
    as2 = [bdot_nt(a_sp[p][0], st_sp[p][1]) for p in P]
    v_sp = [_split2(stack(v[p])) for p in P]
    nk_sp = [_split2(n_ak[p]) for p in P]
    nv1 = [bdot(cat(nk_sp[p][0], nk_sp[p][1]), v_sp[p][0]) for p in P]
    nv2 = [bdot(nk_sp[p][0], v_sp[p][1]) for p in P]
    rhs = [-(stack(as1[p][0:C] + as1[p][C:2 * C] + as2[p]) + nv1[p][0:2 * C] + nv1[p][2 * C:4 * C] + nv2[p])
           for p in P]
    u0 = [bdot(t_inv[p], rhs[p].astype(BF16)) for p in P]
    u0_sp = [_split2(u0[p]) for p in P]
    x_sp = [_split2(x_neg[p]) for p in P]
    xu1 = [bdot(cat(x_sp[p][0], x_sp[p][1]), u0_sp[p][0]) for p in P]
    xu2 = [bdot(x_sp[p][0], u0_sp[p][1]) for p in P]
    res = [(rhs[p] - u0[p] + xu1[p][0:2 * C] + xu1[p][2 * C:4 * C] + xu2[p]).astype(BF16) for p in P]
    u = [u0[p] + bdot(t_inv[p], res[p]) for p in P]
    y = [stack(as1[p][2 * C:3 * C]) + bdot(m_r[p], cat(u[p].astype(BF16), v_sp[p][0])) for p in P]
    uv_sp = [_split2(cat(unstack(u[p]), v[p])) for p in P]
    pend = [jnp.exp(total[p] - cum[p]) for p in P]
    e_sp = [_split2(cat(be[p] * pend[p], kd[p] * pend[p])) for p in P]
    upd = [bdot_tn(cat(uv_sp[p][0], uv_sp[p][1], uv_sp[p][0]), cat(e_sp[p][0], e_sp[p][0], e_sp[p][1]))
           for p in P]
    st_new = [st[p] * jnp.exp(total[p]) + jnp.where(same_head, upd[p], 0.0) for p in P]
    return [unstack(y[p]) for p in P], st_new


def _rwkv_body(r_ref, v_ref, kk_ref, lw_ref, kd_ref, be_ref, y_ref, st_ref, *, reverse, npair):
    C = RWKV_CHUNK
    nchunk = SEQ_BLOCK // C
    s = pl.program_id(2)

    @pl.when(s == 0)
    def _():
        st_ref[...] = jnp.zeros_like(st_ref)

    sgn = -1 if reverse else 1
    row = lax.broadcasted_iota(jnp.int32, (2 * C, 2 * C), 0)
    col = lax.broadcasted_iota(jnp.int32, (2 * C, 2 * C), 1)
    diff = (row - col) * sgn
    strict = diff > 0
    incl = diff >= 0
    same_head = (row // C) == (col // C)
    eye = jnp.where(row == col, 1.0, 0.0)
    r64 = lax.broadcasted_iota(jnp.int32, (C, C), 0)
    c64 = lax.broadcasted_iota(jnp.int32, (C, C), 1)
    tri = jnp.where((r64 - c64) * sgn >= 0, 1.0, 0.0).astype(BF16)
    head0 = lax.broadcasted_iota(jnp.int32, (C, 2 * C), 1) < C

    def chunk(ci, carry):
        ce = (nchunk - 1 - ci) if reverse else ci
        sl = pl.ds(pl.multiple_of(ce * C, C), C)
        lanes = [slice(p * LANES, (p + 1) * LANES) for p in range(npair)]
        get = lambda ref: [ref[sl, ln] for ln in lanes]
        ys, sts = _rwkv_chunk(get(lw_ref), get(r_ref), get(v_ref), get(kk_ref), get(kd_ref), get(be_ref),
                              [st_ref[p] for p in range(npair)], tri, strict, incl, same_head, head0, eye)
        for p in range(npair):
            y_ref[sl, lanes[p]] = ys[p]
            st_ref[p] = sts[p]
        return carry

    lax.fori_loop(0, nchunk, chunk, 0)


def _rwkv_scan(r, v, kk, lw, kd, be, *, n_batch, n_ctx_blk, n_x_blk, reverse, npair=8):
    R, W = r.shape
    ngrp = W // (LANES * npair)
    ns = n_ctx_blk + n_x_blk
    d = 1 if reverse else 0
    blk = functools.partial(_seq_block, n_ctx_blk=n_ctx_blk, n_x_blk=n_x_blk, n_batch=n_batch, reverse=reverse)
    spec = pl.BlockSpec((SEQ_BLOCK, LANES * npair), lambda b, p, s: (blk(b, s), p))
    dspec = pl.BlockSpec((None, SEQ_BLOCK, LANES * npair), lambda b, p, s: (d, blk(b, s), p))
    return pl.pallas_call(
        functools.partial(_rwkv_body, reverse=reverse, npair=npair),
        grid=(n_batch, ngrp, ns),
        in_specs=[spec] * 3 + [dspec] * 3,
        out_specs=spec,
        out_shape=jax.ShapeDtypeStruct((R, W), F32),
        scratch_shapes=[pltpu.VMEM((npair, LANES, LANES), F32)],
        compiler_params=_cparams(("parallel", "parallel", "arbitrary")),
        name="rwkv_scan_bwd" if reverse else "rwkv_scan_fwd",
    )(r, v, kk, lw, kd, be)


S5_SEG = SEQ_BLOCK // SUBLANES
S5_TAB_SLABS = 2 * S5_SEG + 2 + 8
S5_NGB = 2


def _s5_body(u_ref, bw_ref, cw_ref, tab_ref, y_ref, hb_ref, carry_ref, perm_ref, *, reverse):
    W = S5_GB * S5_N
    T = SEQ_BLOCK
    G = range(S5_NGB)
    s = pl.program_id(2)

    @pl.when(s == 0)
    def _():
        carry_ref[...] = jnp.zeros_like(carry_ref)
        prow = lax.broadcasted_iota(jnp.int32, (T, T), 0)
        pcol = lax.broadcasted_iota(jnp.int32, (T, T), 1)
        t_of_row = S5_SEG * (prow % SUBLANES) + prow // SUBLANES
        if reverse:
            t_of_row = T - 1 - t_of_row
        perm_ref[...] = jnp.where(pcol == t_of_row, 1.0, 0.0).astype(BF16)

    perm = perm_ref[...]
    u_p = jnp.dot(perm, u_ref[...].astype(BF16), preferred_element_type=F32).astype(BF16)
    for g in G:
        hb_ref[g] = jnp.dot(u_p[:, g * LANES:(g + 1) * LANES], bw_ref[g], preferred_element_type=F32)

    slab = lambda j: pl.ds(j * SUBLANES, SUBLANES)
    tab = lambda g, n: tab_ref[g, slab(n), :]
    o_seg = 2 * S5_SEG
    o_pow = o_seg + 2
    re, im = slice(0, W), slice(W, 2 * W)
    a = [(tab(g, o_pow), tab(g, o_pow + 1)) for g in G]
    h = [(hb_ref[g, slab(0), re], hb_ref[g, slab(0), im]) for g in G]
    for j in range(1, S5_SEG):
        h = [(hb_ref[g, slab(j), re] + a[g][0] * h[g][0] - a[g][1] * h[g][1],
              hb_ref[g, slab(j), im] + a[g][0] * h[g][1] + a[g][1] * h[g][0]) for g in G]
        for g in G:
            hb_ref[g, slab(j), re] = h[g][0]
            hb_ref[g, slab(j), im] = h[g][1]
    rows8 = lax.broadcasted_iota(jnp.int32, (SUBLANES, W), 0)
    ent = []
    for g in G:
        gr, gi = h[g]
        for n, k in enumerate((1, 2, 4)):
            pr, pi = tab(g, o_pow + 2 + 2 * n), tab(g, o_pow + 3 + 2 * n)
            sr = jnp.where(rows8 >= k, pltpu.roll(gr, k, axis=0), 0.0)
            si = jnp.where(rows8 >= k, pltpu.roll(gi, k, axis=0), 0.0)
            gr, gi = gr + pr * sr - pi * si, gi + pr * si + pi * sr
        cr, ci = carry_ref[g, :, re], carry_ref[g, :, im]
        qr, qi = tab(g, o_seg), tab(g, o_seg + 1)
        gr, gi = gr + qr * cr - qi * ci, gi + qr * ci + qi * cr
        ent.append((jnp.where(rows8 >= 1, pltpu.roll(gr, 1, axis=0), cr),
                    jnp.where(rows8 >= 1, pltpu.roll(gi, 1, axis=0), ci)))
        carry_ref[g, :, re] = jnp.broadcast_to(gr[SUBLANES - 1:SUBLANES, :], (SUBLANES, W))
        carry_ref[g, :, im] = jnp.broadcast_to(gi[SUBLANES - 1:SUBLANES, :], (SUBLANES, W))
    for g in G:
        in_r, in_i = ent[g]
        for j in range(S5_SEG):
            pr, pi = tab(g, j), tab(g, S5_SEG + j)
            hb_ref[g, slab(j), re] = hb_ref[g, slab(j), re] + pr * in_r - pi * in_i
            hb_ref[g, slab(j), im] = hb_ref[g, slab(j), im] + pr * in_i + pi * in_r
    y_p = jnp.concatenate([jnp.dot(hb_ref[g].astype(BF16), cw_ref[g], preferred_element_type=F32) for g in G],
                          axis=1)
    y_ref[...] = _dot_x_ones_tn(perm, y_p)


def _dot_x_ones_tn(ones, x):
    hi, mid, lo = _split3(x)
    f = lambda t: lax.dot_general(ones, t, (((0,), (0,)), ((), ())), preferred_element_type=F32)
    return f(hi) + f(mid) + f(lo)


def _s5_scan(p, col0_blk, bw, cw, tab, *, n_batch, n_ctx_blk, n_x_blk, reverse):
    R = p.shape[0]
    n = S5_NGB
    ngb = S5_G // S5_GB // n
    ns = n_ctx_blk + n_x_blk
    blk = functools.partial(_seq_block, n_ctx_blk=n_ctx_blk, n_x_blk=n_x_blk, n_batch=n_batch, reverse=reverse)
    W2 = 2 * S5_GB * S5_N
    assert col0_blk % n == 0
    return pl.pallas_call(
        functools.partial(_s5_body, reverse=reverse),
        grid=(n_batch, ngb, ns),
        in_specs=[pl.BlockSpec((SEQ_BLOCK, n * LANES), lambda b, g, s: (blk(b, s), col0_blk // n + g)),
                  pl.BlockSpec((n, LANES, W2), lambda b, g, s: (g, 0, 0)),
                  pl.BlockSpec((n, W2, LANES), lambda b, g, s: (g, 0, 0)),
                  pl.BlockSpec((n, S5_TAB_SLABS * SUBLANES, W2 // 2), lambda b, g, s: (g, 0, 0))],
        out_specs=pl.BlockSpec((SEQ_BLOCK, n * LANES), lambda b, g, s: (blk(b, s), g)),
        out_shape=jax.ShapeDtypeStruct((R, S5_W), F32),
        scratch_shapes=[pltpu.VMEM((n, SEQ_BLOCK, W2), F32), pltpu.VMEM((n, SUBLANES, W2), F32),
                        pltpu.VMEM((SEQ_BLOCK, SEQ_BLOCK), BF16)],
        compiler_params=_cparams(("parallel", "parallel", "arbitrary")),
        name="s5_scan_bwd" if reverse else "s5_scan_fwd",
    )(p, bw, cw, tab)


def _s5_weights(A_re, A_im, log_dt, B_re, B_im, C_re, C_im):
    lam_re = jnp.minimum(A_re, -1e-4)
    dt = jnp.exp(log_dt)[:, None]
    mag = jnp.exp(lam_re * dt)
    ab_re, ab_im = mag * jnp.cos(A_im * dt), mag * jnp.sin(A_im * dt)
    den = lam_re * lam_re + A_im * A_im
    f_re = ((ab_re - 1) * lam_re + ab_im * A_im) / den
    f_im = (ab_im * lam_re - (ab_re - 1) * A_im) / den
    bb_re = f_re[..., None] * B_re - f_im[..., None] * B_im
    bb_im = f_re[..., None] * B_im + f_im[..., None] * B_re
    ngb = S5_G // S5_GB
    eye = jnp.eye(S5_GB, dtype=F32)

    def in_block(bb):
        t = bb.reshape(ngb, S5_GB, S5_N, S5_P)
        return jnp.einsum("agnp,gh->agphn", t, eye).reshape(ngb, S5_GB * S5_P, S5_GB * S5_N)

    def out_block(cc):
        t = cc.reshape(ngb, S5_GB, S5_P, S5_N)
        return jnp.einsum("agpn,gh->agnhp", t, eye).reshape(ngb, S5_GB * S5_N, S5_GB * S5_P)

    bw = jnp.concatenate([in_block(bb_re), in_block(bb_im)], axis=2).astype(BF16)
    cw = jnp.concatenate([out_block(C_re), -out_block(C_im)], axis=1).astype(BF16)

    def power(k):
        m = jnp.exp(k * lam_re * dt)
        return m * jnp.cos(k * A_im * dt), m * jnp.sin(k * A_im * dt)

    rep = lambda t: jnp.broadcast_to(t[None], (SUBLANES,) + t.shape)
    slab_pow = [power(float(j + 1)) for j in range(S5_SEG)]
    seg_pow = [power(float(S5_SEG * (i + 1))) for i in range(SUBLANES)]
    slabs = ([rep(c[0]) for c in slab_pow] + [rep(c[1]) for c in slab_pow]
             + [jnp.stack([c[0] for c in seg_pow]), jnp.stack([c[1] for c in seg_pow])])
    for k in (1.0, float(S5_SEG), 2.0 * S5_SEG, 4.0 * S5_SEG):
        slabs += [rep(t) for t in power(k)]
    tab = jnp.concatenate(slabs, axis=0)
    tab = tab.reshape(S5_TAB_SLABS * SUBLANES, ngb, S5_GB * S5_N).transpose(1, 0, 2)
    return bw, cw, tab


def _ssd_body(x_ref, b_ref, c_ref, dt_ref, dta_ref, sel_ref, y_ref, st_ref, *, reverse):
    Q = SSD_CHUNK
    s = pl.program_id(1)

    @pl.when(s == 0)
    def _():
        st_ref[...] = jnp.zeros_like(st_ref)

    row = lax.broadcasted_iota(jnp.int32, (Q, Q), 0)
    col = lax.broadcasted_iota(jnp.int32, (Q, Q), 1)
    mask = (col >= row) if reverse else (col <= row)
    tri = jnp.where(mask, 1.0, 0.0).astype(BF16)
    head0 = lax.broadcasted_iota(jnp.int32, (Q, LANES), 1) < SSD_HD
    head0_rows = lax.broadcasted_iota(jnp.int32, (LANES, SSD_N), 0) < SSD_HD

    h_lo = SSD_HEADS if reverse else 0
    dt = dt_ref[:, h_lo:h_lo + SSD_HEADS]
    dta = dta_ref[:, h_lo:h_lo + SSD_HEADS]
    dtt = dt_ref[...].T[h_lo:h_lo + SSD_HEADS, :]
    dtat = dta_ref[...].T[h_lo:h_lo + SSD_HEADS, :]
    acum = _dot_ones_x(tri, dta)
    acum_t = _dot_x_ones_nt(dtat, tri)
    total = jnp.sum(dta, axis=0, keepdims=True)
    total_t = jnp.sum(dtat, axis=1, keepdims=True)
    e_tot_t = jnp.exp(total_t)
    dt_wide = jnp.dot(dt.astype(BF16), sel_ref[...], preferred_element_type=F32)
    heads_per_group = SSD_HEADS // SSD_GROUPS
    G = range(SSD_GROUPS)
    pairs = [(g, g * heads_per_group + 2 * pr) for g in G for pr in range(heads_per_group // 2)]
    lanes2 = lambda h0: slice(h0 * SSD_HD, (h0 + 2) * SSD_HD)
    bg = [b_ref[:, g * SSD_N:(g + 1) * SSD_N].astype(BF16) for g in G]
    cg = [c_ref[:, g * SSD_N:(g + 1) * SSD_N].astype(BF16) for g in G]
    cb = [lax.dot_general(cg[g], bg[g], (((1,), (1,)), ((), ())), preferred_element_type=F32) for g in G]
    xp = [x_ref[:, lanes2(h0)] for _, h0 in pairs]
    xb = [t.astype(BF16) for t in xp]
    colb = [jnp.broadcast_to(acum[:, h:h + 1], (Q, LANES)) for h in range(SSD_HEADS)]

    def weights(g, h):
        seg = colb[h] - acum_t[h:h + 1, :]
        return (cb[g] * jnp.exp(jnp.where(mask, seg, -jnp.inf)) * dtt[h:h + 1, :]).astype(BF16)

    yd = [(jnp.dot(weights(g, h0), xb[i], preferred_element_type=F32),
           jnp.dot(weights(g, h0 + 1), xb[i], preferred_element_type=F32)) for i, (g, h0) in enumerate(pairs)]
    st = [st_ref[lanes2(h0), :] for _, h0 in pairs]
    yo = [lax.dot_general(cg[g], st[i].astype(BF16), (((1,), (1,)), ((), ())), preferred_element_type=F32)
          for i, (g, _) in enumerate(pairs)]
    for i, (g, h0) in enumerate(pairs):
        scale = jnp.exp(jnp.where(head0, colb[h0], colb[h0 + 1]))
        y_ref[:, lanes2(h0)] = jnp.where(head0, yd[i][0], yd[i][1]) + yo[i] * scale
    xt = [(xp[i] * dt_wide[:, lanes2(h0)]
           * jnp.exp(jnp.where(head0, total[:, h0:h0 + 1] - colb[h0], total[:, h0 + 1:h0 + 2] - colb[h0 + 1]))
           ).astype(BF16) for i, (_, h0) in enumerate(pairs)]
    upd = [lax.dot_general(xt[i], bg[g], (((0,), (0,)), ((), ())), preferred_element_type=F32)
           for i, (g, _) in enumerate(pairs)]
    for i, (_, h0) in enumerate(pairs):
        dec = jnp.where(head0_rows, e_tot_t[h0:h0 + 1, :], e_tot_t[h0 + 1:h0 + 2, :])
        st_ref[lanes2(h0), :] = st[i] * dec + upd[i]


def _ssd_scan(xbc, dt, dta, *, n_batch, n_ctx_blk, n_x_blk, reverse):
    R = xbc.shape[0]
    Q = SSD_CHUNK
    blk = functools.partial(_seq_block, n_ctx_blk=n_ctx_blk, n_x_blk=n_x_blk, n_batch=n_batch, reverse=reverse)
    ns = n_ctx_blk + n_x_blk
    gw = SSD_GROUPS * SSD_N
    row_spec = lambda w, c: pl.BlockSpec((Q, w), lambda b, s: (blk(b, s), c))
    head_of_lane = jnp.arange(SSD_W, dtype=jnp.int32) // SSD_HD
    sel = (jnp.arange(SSD_HEADS, dtype=jnp.int32)[:, None] == head_of_lane[None, :]).astype(BF16)
    return pl.pallas_call(
        functools.partial(_ssd_body, reverse=reverse),
        grid=(n_batch, ns),
        in_specs=[row_spec(SSD_W, 0), row_spec(gw, SSD_W // gw), row_spec(gw, SSD_W // gw + 1),
                  row_spec(LANES, 0), row_spec(LANES, 0), pl.BlockSpec((SSD_HEADS, SSD_W), lambda b, s: (0, 0))],
        out_specs=row_spec(SSD_W, 0),
        out_shape=jax.ShapeDtypeStruct((R, SSD_W), F32),
        scratch_shapes=[pltpu.VMEM((SSD_W, SSD_N), F32)],
        compiler_params=_cparams(("parallel", "arbitrary")),
        name="ssd_scan_bwd" if reverse else "ssd_scan_fwd",
    )(xbc, xbc, xbc, dt, dta, sel)


def _attn_body(q_ref, kp_ref, kc_ref, kn_ref, vp_ref, vc_ref, vn_ref, kx_ref, vx_ref, sink_ref, mix_ref, o_ref,
               *, seq_len):
    del mix_ref
    n = pl.program_id(1)
    blk = ATT_BLOCK
    n_ctx = kx_ref.shape[0]
    rep = ATT_HEADS // KV_HEADS
    k_all = jnp.concatenate([kx_ref[...], kp_ref[...], kc_ref[...], kn_ref[...]], axis=0).astype(BF16)
    v_all = jnp.concatenate([vx_ref[...], vp_ref[...], vc_ref[...], vn_ref[...]], axis=0).astype(BF16)
    nk = n_ctx + 3 * blk
    r_idx = lax.broadcasted_iota(jnp.int32, (rep * blk, nk), 0)
    q_pos = n * blk + r_idx % blk
    j = lax.broadcasted_iota(jnp.int32, (rep * blk, nk), 1)
    k_pos = (n - 1) * blk + (j - n_ctx)
    valid = (j < n_ctx) | ((jnp.abs(q_pos - k_pos) <= WINDOW) & (k_pos >= 0) & (k_pos < seq_len))
    scale = ATT_HD ** -0.5
    head_of_row = lax.broadcasted_iota(jnp.int32, (rep * blk, 1), 0) // blk
    G = range(KV_HEADS)
    hd = lambda t, h: t[:, h * ATT_HD:(h + 1) * ATT_HD]
    q = q_ref[...].astype(BF16)
    qs = [jnp.concatenate([hd(q, g * rep + i) for i in range(rep)], axis=0) for g in G]
    sinks = []
    for g in G:
        sk = jnp.zeros((rep * blk, 1), F32)
        for i in range(rep):
            sk = jnp.where(head_of_row == i, sink_ref[0:1, g * rep + i:g * rep + i + 1], sk)
        sinks.append(sk)
    sc = [jnp.where(valid, lax.dot_general(qs[g], hd(k_all, g), (((1,), (1,)), ((), ())),
                                           preferred_element_type=F32) * scale, -jnp.inf) for g in G]
    m = [jnp.maximum(jnp.max(sc[g], axis=-1, keepdims=True), sinks[g]) for g in G]
    e = [jnp.exp(sc[g] - m[g]) for g in G]
    den = [jnp.sum(e[g], axis=-1, keepdims=True) + jnp.exp(sinks[g] - m[g]) for g in G]
    pv = [jnp.dot((e[g] / den[g]).astype(BF16), hd(v_all, g), preferred_element_type=F32) for g in G]
    outs = [pv[g][i * blk:(i + 1) * blk] for g in G for i in range(rep)]
    o_ref[...] = jnp.concatenate(outs, axis=1).astype(o_ref.dtype)


def _window_attn(q, k, p, v_col, sink, mix, mix_col, *, n_batch, n_ctx, seq_len):
    nb = seq_len // ATT_BLOCK
    x0 = n_batch * n_ctx // ATT_BLOCK
    vc = v_col // ATT_KV_W
    cur = lambda b, n: x0 + b * nb + n
    prev = lambda b, n: x0 + b * nb + jnp.maximum(n - 1, 0)
    nxt = lambda b, n: x0 + b * nb + jnp.minimum(n + 1, nb - 1)
    k_spec = lambda im: pl.BlockSpec((ATT_BLOCK, ATT_KV_W), lambda b, n: (im(b, n), 0))
    v_spec = lambda im: pl.BlockSpec((ATT_BLOCK, ATT_KV_W), lambda b, n: (im(b, n), vc))
    return pl.pallas_call(
        functools.partial(_attn_body, seq_len=seq_len),
        grid=(n_batch, nb),
        in_specs=[pl.BlockSpec((ATT_BLOCK, ATT_Q_W), lambda b, n: (cur(b, n), 0)),
                  k_spec(prev), k_spec(cur), k_spec(nxt), v_spec(prev), v_spec(cur), v_spec(nxt),
                  pl.BlockSpec((n_ctx, ATT_KV_W), lambda b, n: (b, 0)),
                  pl.BlockSpec((n_ctx, ATT_KV_W), lambda b, n: (b, vc)),
                  pl.BlockSpec((1, ATT_HEADS), lambda b, n: (0, 0)),
                  pl.BlockSpec(memory_space=pl.ANY)],
        out_specs=pl.BlockSpec((ATT_BLOCK, ATT_Q_W), lambda b, n: (b * nb + n, mix_col // ATT_Q_W)),
        out_shape=jax.ShapeDtypeStruct(mix.shape, mix.dtype),
        input_output_aliases={10: 0},
        compiler_params=_cparams(("parallel", "parallel")),
        name="window_attn",
    )(q, k, k, k, p, p, p, k, p, sink, mix)


def _tile_edges(row0, tm, n_ctx_rows, ctx_len, seq_len):
    in_ctx = row0 < n_ctx_rows
    pos0 = jnp.where(in_ctx, row0 % ctx_len, (row0 - n_ctx_rows) % seq_len)
    seg = jnp.where(in_ctx, ctx_len, seq_len)
    return pos0 == 0, pos0 + tm == seg


def _shifted_rows(cur, k, prev_blk, next_blk, first, last, rows):
    tm = cur.shape[0]
    if k < 0:
        edge = jnp.where(first, 0.0, pltpu.roll(prev_blk, -k, axis=0))
        body = pltpu.roll(cur, -k, axis=0)
        out = body
        for j in range(-k):
            out = jnp.where(rows == j, edge[j:j + 1, :], out)
        return out
    edge = jnp.where(last, 0.0, next_blk)
    out = pltpu.roll(cur, tm - k, axis=0)
    for j in range(k):
        out = jnp.where(rows == tm - k + j, edge[j:j + 1, :], out)
    return out


def _head_sum_mxu(t, blk):
    n = t.shape[1] // LANES
    return jnp.concatenate([_dot_x_ones_nt(t[:, j * LANES:(j + 1) * LANES], blk) for j in range(n)], axis=1)


def _same_head_matrix():
    r = lax.broadcasted_iota(jnp.int32, (LANES, LANES), 0) // RWKV_HD
    c = lax.broadcasted_iota(jnp.int32, (LANES, LANES), 1) // RWKV_HD
    return jnp.where(r == c, 1.0, 0.0).astype(BF16)


def _softplus(x):
    return jnp.maximum(x, 0.0) + jnp.log(1.0 + jnp.exp(-jnp.abs(x)))


def _rwkv_pre_body(p_ref, hp_ref, hn_ref, mu_ref, lora_ref, gw_ref, w0_ref, a0_ref, kk_ref, ka_ref, rk_ref,
                   r_out, v_out, kk_out, g_out, bonus_out, lw_out, kd_out, be_out, *, tm, geo):
    W = RWKV_W
    first, last = _tile_edges(pl.program_id(0) * tm, tm, *geo)
    rows = lax.broadcasted_iota(jnp.int32, (tm, 1), 0)

    def tshift(c0, c1):
        cur = p_ref[:, c0:c1]
        prev = _shifted_rows(cur, -1, hp_ref[:, c0:c1], hn_ref[:, c0:c1], first, last, rows)
        nxt = _shifted_rows(cur, 1, hp_ref[:, c0:c1], hn_ref[:, c0:c1], first, last, rows)
        return cur + mu_ref[0:1, c0:c1] * (prev - cur) + mu_ref[1:2, c0:c1] * (nxt - cur)

    r = tshift(0, W)
    k = tshift(W, 2 * W)
    v = tshift(2 * W, 3 * W)
    xl = tshift(3 * W, 3 * W + 4 * LANES)
    lane = lax.broadcasted_iota(jnp.int32, (tm, 2 * LANES), 1)
    lin = jnp.where(lane < DECAY_LORA, jnp.tanh(xl[:, :2 * LANES]), xl[:, :2 * LANES])
    lora = _dot(lin, lora_ref[...])
    g_out[...] = _dot(jax.nn.sigmoid(xl), gw_ref[...])
    blk = _same_head_matrix()
    kk = k * kk_ref[...]
    kk = kk / jnp.maximum(jnp.sqrt(_head_sum_mxu(kk * kk, blk)), 1e-12)
    r_out[...] = r
    v_out[...] = v
    kk_out[...] = kk
    kd_sum = jnp.zeros_like(k)
    for d in range(2):
        w_log = -_softplus(-(w0_ref[d:d + 1, :] + lora[:, d * W:(d + 1) * W])) - 0.5
        a = jax.nn.sigmoid(a0_ref[d:d + 1, :] + lora[:, (2 + d) * W:(3 + d) * W])
        kd = k * (1.0 + (a - 1.0) * ka_ref[...])
        lw_out[d] = -jnp.exp(w_log)
        kd_out[d] = kd
        be_out[d] = a * kk
        kd_sum = kd_sum + kd
    bonus_out[...] = _head_sum_mxu(r * kd_sum * rk_ref[...], blk) * v


def _rwkv_pre(p, mu, lora_w, gate_w, w0, a0, k_k, k_a, r_k, *, geo):
    R = p.shape[0]
    W = RWKV_W
    tm = SEQ_BLOCK
    pw = 3 * W + 4 * LANES
    nb8 = R // SUBLANES
    row = lambda i: (i, 0)
    vec = lambda n: pl.BlockSpec((n, W), lambda i: (0, 0))
    out2 = pl.BlockSpec((2, tm, W), lambda i: (0, i, 0))
    f = jax.ShapeDtypeStruct((R, W), F32)
    f2 = jax.ShapeDtypeStruct((2, R, W), F32)
    return pl.pallas_call(
        functools.partial(_rwkv_pre_body, tm=tm, geo=geo),
        grid=(R // tm,),
        in_specs=[pl.BlockSpec((tm, pw), row),
                  pl.BlockSpec((SUBLANES, pw), lambda i: (jnp.maximum(i * (tm // SUBLANES) - 1, 0), 0)),
                  pl.BlockSpec((SUBLANES, pw), lambda i: (jnp.minimum((i + 1) * (tm // SUBLANES), nb8 - 1), 0)),
                  pl.BlockSpec((2, pw), lambda i: (0, 0)),
                  pl.BlockSpec(lora_w.shape, lambda i: (0, 0)),
                  pl.BlockSpec(gate_w.shape, lambda i: (0, 0)),
                  vec(2), vec(2), vec(1), vec(1), vec(1)],
        out_specs=[pl.BlockSpec((tm, W), row)] * 5 + [out2] * 3,
        out_shape=[f] * 5 + [f2] * 3,
        compiler_params=_cparams(("parallel",)),
        name="rwkv_pre",
    )(p, p, p, mu, lora_w, gate_w, w0, a0, k_k, k_a, r_k)


def _rwkv_post_body(yf_ref, yb_ref, bonus_ref, g_ref, lng_ref, lnb_ref, o_ref):
    blk = _same_head_matrix()
    y = yf_ref[...] + yb_ref[...]
    mean = _head_sum_mxu(y, blk) * (1.0 / RWKV_HD)
    d = y - mean
    var = _head_sum_mxu(d * d, blk) * (1.0 / RWKV_HD)
    yn = d * lax.rsqrt(var + GN_EPS)
    o_ref[...] = ((yn * lng_ref[...] + lnb_ref[...] + bonus_ref[...]) * g_ref[...]).astype(o_ref.dtype)


def _rwkv_post(yf, yb, bonus, g, ln_g, ln_b, out_cols):
    R, W = yf.shape
    tm = SEQ_BLOCK
    row = pl.BlockSpec((tm, W), lambda i: (i, 0))
    vec = pl.BlockSpec((1, W), lambda i: (0, 0))
    return pl.pallas_call(
        _rwkv_post_body,
        grid=(R // tm,),
        in_specs=[row, row, row, row, vec, vec],
        out_specs=row,
        out_shape=jax.ShapeDtypeStruct((R, out_cols), BF16),
        compiler_params=_cparams(("parallel",)),
        name="rwkv_post",
    )(yf, yb, bonus, g, ln_g, ln_b)


def _gelu_tanh(x):
    return 0.5 * x * (1.0 + jnp.tanh(math.sqrt(2.0 / math.pi) * (x + 0.044715 * (x * x * x))))


def _s5_post_body(u0_ref, u1_ref, yf_ref, yb_ref, d_ref, w_ref, b_ref, mix_ref, o_ref):
    del mix_ref
    u = jnp.concatenate([u0_ref[...], u1_ref[...]], axis=1)
    y = _gelu_tanh(d_ref[...] * u + yf_ref[...] + yb_ref[...])
    o_ref[...] = (y * jax.nn.sigmoid(_dot(y, w_ref[...]) + b_ref[...])).astype(o_ref.dtype)


def _s5_post(p, col0, yf, yb, d_skip, glu_w, glu_b, mix):
    R, W = yf.shape
    tm = SEQ_BLOCK
    half = W // 2
    row = pl.BlockSpec((tm, W), lambda i: (i, 0))
    vec = pl.BlockSpec((1, W), lambda i: (0, 0))
    return pl.pallas_call(
        _s5_post_body,
        grid=(R // tm,),
        in_specs=[pl.BlockSpec((tm, half), lambda i: (i, col0 // half)),
                  pl.BlockSpec((tm, half), lambda i: (i, col0 // half + 1)), row, row, vec,
                  pl.BlockSpec((W, W), lambda i: (0, 0)), vec, pl.BlockSpec(memory_space=pl.ANY)],
        out_specs=pl.BlockSpec((tm, W), lambda i: (i, 1)),
        out_shape=jax.ShapeDtypeStruct(mix.shape, mix.dtype),
        input_output_aliases={7: 0},
        compiler_params=_cparams(("parallel",)),
        name="s5_post",
    )(p, p, yf, yb, d_skip, glu_w, glu_b, mix)


def _cd_conv_body(x_ref, hp_ref, hn_ref, w_ref, b_ref, o_ref, *, tm, geo, taps):
    first, last = _tile_edges(pl.program_id(0) * tm, tm, *geo)
    rows = lax.broadcasted_iota(jnp.int32, (tm, 1), 0)
    cur = x_ref[...]
    half = taps // 2
    acc = b_ref[...] + w_ref[half:half + 1, :] * cur
    for j in range(taps):
        if j != half:
            acc = acc + w_ref[j:j + 1, :] * _shifted_rows(cur, j - half, hp_ref[...], hn_ref[...], first, last, rows)
    o_ref[...] = acc * jax.nn.sigmoid(acc)


def _cd_conv(p, col0, width, conv_w, conv_b, *, geo):
    R = p.shape[0]
    tm = SEQ_BLOCK
    tw = 1024
    taps = conv_w.shape[0]
    nb8 = R // SUBLANES
    c0 = col0 // tw
    w8 = jnp.pad(conv_w, ((0, SUBLANES - taps), (0, 0)))
    return pl.pallas_call(
        functools.partial(_cd_conv_body, tm=tm, geo=geo, taps=taps),
        grid=(R // tm, width // tw),
        in_specs=[pl.BlockSpec((tm, tw), lambda i, j: (i, c0 + j)),
                  pl.BlockSpec((SUBLANES, tw), lambda i, j: (jnp.maximum(i * (tm // SUBLANES) - 1, 0), c0 + j)),
                  pl.BlockSpec((SUBLANES, tw), lambda i, j: (jnp.minimum((i + 1) * (tm // SUBLANES), nb8 - 1), c0 + j)),
                  pl.BlockSpec((SUBLANES, tw), lambda i, j: (0, j)),
                  pl.BlockSpec((1, tw), lambda i, j: (0, j))],
        out_specs=pl.BlockSpec((tm, tw), lambda i, j: (i, j)),
        out_shape=jax.ShapeDtypeStruct((R, width), F32),
        compiler_params=_cparams(("parallel", "parallel")),
        name="cd_conv",
    )(p, p, p, w8, conv_b.reshape(1, width))


def _rope_tile(x, cos, sin):
    lane = lax.broadcasted_iota(jnp.int32, x.shape, 1)
    q = ATT_HD // 4
    partner = jnp.where(lane % (2 * q) < q, pltpu.roll(x, LANES - q, axis=1), pltpu.roll(x, q, axis=1))
    return x * cos + partner * sin


def _cd_rope_dt_body(q_ref, k_ref, dtr_ref, cos_ref, sin_ref, bias_ref, a_ref, q_out, k_out, dt_out, dta_out,
                     *, tm, n_ctx_rows):
    in_ctx = pl.program_id(0) * tm < n_ctx_rows
    cos = jnp.where(in_ctx, 1.0, cos_ref[...])
    sin = jnp.where(in_ctx, 0.0, sin_ref[...])
    for j in range(ATT_Q_W // LANES):
        q_out[:, j * LANES:(j + 1) * LANES] = _rope_tile(q_ref[:, j * LANES:(j + 1) * LANES], cos, sin)
    for j in range(ATT_KV_W // LANES):
        k_out[:, j * LANES:(j + 1) * LANES] = _rope_tile(k_ref[:, j * LANES:(j + 1) * LANES], cos, sin)
    dt = _softplus(dtr_ref[...] + bias_ref[...])
    dt_out[...] = dt
    dta_out[...] = dt * a_ref[...]


def _cd_rope_dt(p, q_col, k_col, dt_col, cos, sin, dt_bias, a_neg, *, n_ctx_rows, seq_len):
    R = p.shape[0]
    tm = SEQ_BLOCK
    tab = pl.BlockSpec((tm, LANES), lambda i: (jnp.maximum(i * tm - n_ctx_rows, 0) % seq_len // tm, 0))
    vec = pl.BlockSpec((1, LANES), lambda i: (0, 0))
    return pl.pallas_call(
        functools.partial(_cd_rope_dt_body, tm=tm, n_ctx_rows=n_ctx_rows),
        grid=(R // tm,),
        in_specs=[pl.BlockSpec((tm, ATT_Q_W), lambda i: (i, q_col // ATT_Q_W)),
                  pl.BlockSpec((tm, ATT_KV_W), lambda i: (i, k_col // ATT_KV_W)),
                  pl.BlockSpec((tm, LANES), lambda i: (i, dt_col // LANES)),
                  tab, tab, vec, vec],
        out_specs=[pl.BlockSpec((tm, ATT_Q_W), lambda i: (i, 0)), pl.BlockSpec((tm, ATT_KV_W), lambda i: (i, 0)),
                   pl.BlockSpec((tm, LANES), lambda i: (i, 0)), pl.BlockSpec((tm, LANES), lambda i: (i, 0))],
        out_shape=[jax.ShapeDtypeStruct((R, ATT_Q_W), F32), jax.ShapeDtypeStruct((R, ATT_KV_W), F32),
                   jax.ShapeDtypeStruct((R, LANES), F32), jax.ShapeDtypeStruct((R, LANES), F32)],
        compiler_params=_cparams(("parallel",)),
        name="cd_rope_dt",
    )(p, p, p, cos, sin, dt_bias, a_neg)


def _ssd_post_body(yf_ref, yb_ref, x_ref, z_ref, d_ref, g_ref, o_ref):
    z = z_ref[...]
    y = (yf_ref[...] + yb_ref[...] + d_ref[...] * x_ref[...]) * (z * jax.nn.sigmoid(z))
    gw = SSD_W // SSD_GROUPS
    for g in range(SSD_GROUPS):
        yg = y[:, g * gw:(g + 1) * gw]
        ms = jnp.mean(yg * yg, axis=-1, keepdims=True)
        o_ref[:, g * gw:(g + 1) * gw] = (yg * lax.rsqrt(ms + EPS) * g_ref[:, g * gw:(g + 1) * gw]).astype(o_ref.dtype)


def _ssd_post(yf, yb, xbc, p, d_skip, norm_g, out_cols, *, n_ctx_rows):
    R = yf.shape[0]
    tm = SEQ_BLOCK
    r0 = n_ctx_rows // tm
    row = pl.BlockSpec((tm, SSD_W), lambda i: (r0 + i, 0))
    vec = pl.BlockSpec((1, SSD_W), lambda i: (0, 0))
    return pl.pallas_call(
        _ssd_post_body,
        grid=((R - n_ctx_rows) // tm,),
        in_specs=[row, row, row, row, vec, vec],
        out_specs=pl.BlockSpec((tm, SSD_W), lambda i: (i, 0)),
        out_shape=jax.ShapeDtypeStruct((R - n_ctx_rows, out_cols), BF16),
        compiler_params=_cparams(("parallel",)),
        name="ssd_post",
    )(yf, yb, xbc, p, d_skip, norm_g)


def _segment_pos(R, RC, C, L):
    row = jnp.arange(R, dtype=jnp.int32)
    in_ctx = row < RC
    pos = jnp.where(in_ctx, row % C, (row - RC) % L)
    seg = jnp.where(in_ctx, C, L)
    return pos, seg


def _shift_rows(p, k, pos, seg):
    ok = (pos + k >= 0) & (pos + k < seg)
    return jnp.where(ok[:, None], jnp.roll(p, -k, axis=0), 0.0)


def _head_sum(t):
    R = t.shape[0]
    s = jnp.sum(t.reshape(R, RWKV_HEADS, RWKV_HD), axis=-1, keepdims=True)
    return jnp.broadcast_to(s, (R, RWKV_HEADS, RWKV_HD)).reshape(R, RWKV_W)


def _rope_tables(L):
    nf = ATT_HD // 4
    inv_freq = ROPE_BASE ** (-jnp.arange(nf, dtype=F32) / nf)
    t = jnp.arange(L, dtype=jnp.int32)
    row_id = (t // GRID_W).astype(F32)
    col_id = (t % GRID_W).astype(F32)
    ar = row_id[:, None] * inv_freq
    ac = col_id[:, None] * inv_freq
    cos = jnp.concatenate([jnp.cos(ar), jnp.cos(ar), jnp.cos(ac), jnp.cos(ac)], axis=1)
    sin = jnp.concatenate([-jnp.sin(ar), jnp.sin(ar), -jnp.sin(ac), jnp.sin(ac)], axis=1)
    return cos, sin


def _rope(t, cos, sin, n_heads):
    R = t.shape[0]
    q = ATT_HD // 4
    tr = t.reshape(R, n_heads, 2, 2, q)
    swapped = jnp.stack([tr[..., 1, :], tr[..., 0, :]], axis=-2).reshape(R, n_heads, ATT_HD)
    out = t.reshape(R, n_heads, ATT_HD) * cos[:, None, :] + swapped * sin[:, None, :]
    return out.reshape(R, n_heads * ATT_HD)


def kernel(x, c, ctx, c_ctx, ada_w, ada_b, norm1_g, norm2_g, mlp_w1, mlp_w2, final_g, ab_w_in, ab_w_out, rwkv_mu, rwkv_w0, rwkv_w_up, rwkv_a0, rwkv_a_up, rwkv_g_up, rwkv_k_k, rwkv_k_a, rwkv_r_k, rwkv_ln_g, rwkv_ln_b, s5_A_re, s5_A_im, s5_log_dt, s5_B_re, s5_B_im, s5_C_re, s5_C_im, s5_D, s5_glu_w, s5_glu_b, cd_w_in, cd_w_out, ssd_conv_w, ssd_conv_b, ssd_A_log, ssd_dt_bias, ssd_D, ssd_norm_g, attn_sink):
    B, L, D = x.shape
    C = ctx.shape[1]
    RC, RX = B * C, B * L
    R = RC + RX
    assert C % SEQ_BLOCK == 0 and L % SEQ_BLOCK == 0
    geo = dict(n_batch=B, n_ctx_blk=C // SEQ_BLOCK, n_x_blk=L // SEQ_BLOCK)
    geo_ssd = dict(n_batch=B, n_ctx_blk=C // SSD_CHUNK, n_x_blk=L // SSD_CHUNK)
    group_all = lambda r0: jnp.where(r0 < RC, 0, 1 + (r0 - RC) // L)
    group_x = lambda r0: 1 + r0 // L

    xs = jnp.concatenate([ctx.reshape(RC, D), x.reshape(RX, D)], axis=0)
    cond = jax.nn.silu(jnp.concatenate([c_ctx[None, :], c], axis=0))
    cond = jnp.pad(cond, ((0, SUBLANES - (B + 1) % SUBLANES), (0, 0))) if (B + 1) % SUBLANES else cond
    n_groups = cond.shape[0]

    def modulation(i):
        mod = _mm(cond, ada_w, b_layer=i, name="ada") + ada_b[i]
        return [m.reshape(n_groups, 1, D) for m in jnp.split(mod, 6, axis=-1)]

    sh1, sc1, g1, sh2, sc2, g2 = modulation(0)
    sh1n, sc1n, g1n, sh2n, sc2n, g2n = modulation(1)
    h = _norm_mod(xs, norm1_g[0], sh1, sc1, group_all)
    pad_cols = (-RWKV_COLS) % LANES
    w_in = ab_w_in[0]
    w_in = jnp.concatenate([w_in[:, :RWKV_COLS], jnp.zeros((D, pad_cols), F32), w_in[:, RWKV_COLS:]], axis=1)
    s5_col0 = RWKV_COLS + pad_cols
    p = _mm(h, w_in.astype(BF16), name="ab_in")

    W = RWKV_W
    lora_rows = 2 * LANES
    zw = jnp.zeros((DECAY_LORA, W), F32)
    lora_w = jnp.concatenate([
        jnp.concatenate([rwkv_w_up[0, 0], rwkv_w_up[0, 1], zw, zw], axis=1),
        jnp.concatenate([zw, zw, rwkv_a_up[0, 0], rwkv_a_up[0, 1]], axis=1),
        jnp.zeros((lora_rows - DECAY_LORA - AAA_LORA, 4 * W), F32)], axis=0).astype(BF16)
    gate_w = jnp.pad(rwkv_g_up[0], ((DECAY_LORA + AAA_LORA, pad_cols), (0, 0))).astype(BF16)
    mu = jnp.pad(rwkv_mu[0], ((0, 0), (0, pad_cols)))
    vec = lambda t: t.reshape(1, W)
    r, v, kk, g_gate, bonus, lw, kd, be = _rwkv_pre(
        p, mu, lora_w, gate_w, rwkv_w0[0], rwkv_a0[0], vec(rwkv_k_k[0]), vec(rwkv_k_a[0]), vec(rwkv_r_k[0]),
        geo=(RC, C, L))
    y_dirs = [_rwkv_scan(r, v, kk, lw, kd, be, reverse=(d == 1), **geo) for d in range(2)]
    mix = _rwkv_post(y_dirs[0], y_dirs[1], bonus, g_gate, vec(rwkv_ln_g[0]), vec(rwkv_ln_b[0]), W + S5_W)

    ys_dirs = []
    for d in range(2):
        bw, cw, tab = _s5_weights(s5_A_re[0, d], s5_A_im[0, d], s5_log_dt[0, d], s5_B_re[0, d], s5_B_im[0, d],
                                  s5_C_re[0, d], s5_C_im[0, d])
        ys_dirs.append(_s5_scan(p, s5_col0 // LANES, bw, cw, tab, reverse=(d == 1), **geo))
    mix = _s5_post(p, s5_col0, ys_dirs[0], ys_dirs[1], vec(s5_D[0]), s5_glu_w[0].astype(BF16),
                   vec(s5_glu_b[0]), mix)
    xs, h2 = _mm_res_norm(mix, ab_w_out[0].astype(BF16), xs, g1, norm2_g[0], sh2, sc2, group_all, name="ab_out")
    m1 = _mm(h2, mlp_w1[0].astype(BF16), out_dtype=BF16, act="relu2", name="mlp_up")
    xs, h = _mm_res_norm(m1, mlp_w2[0].astype(BF16), xs, g2, norm1_g[1], sh1n, sc1n, group_all, name="mlp_down")

    sh1, sc1, g1, sh2, sc2, g2 = sh1n, sc1n, g1n, sh2n, sc2n, g2n
    cd_in = cd_w_in[0]
    o_dt = SSD_W + XBC_W
    o_q = o_dt + 2 * SSD_HEADS
    n_qkv = ATT_Q_W + 2 * ATT_KV_W
    cd_in = jnp.concatenate([cd_in[:, :o_dt], cd_in[:, o_q:o_q + n_qkv], cd_in[:, o_dt:o_q]], axis=1)
    cd_in = jnp.pad(cd_in, ((0, 0), (0, (-cd_in.shape[1]) % 512)))
    q_col = o_dt
    k_col = q_col + ATT_Q_W
    v_col = k_col + ATT_KV_W
    dt_col = v_col + ATT_KV_W
    p = _mm(h, cd_in.astype(BF16), name="cd_in")

    xbc = _cd_conv(p, SSD_W, XBC_W, ssd_conv_w[0], ssd_conv_b[0], geo=(RC, C, L))
    cos, sin = _rope_tables(L)
    lane_pad = lambda t: jnp.pad(t.reshape(1, 2 * SSD_HEADS), ((0, 0), (0, LANES - 2 * SSD_HEADS)))
    q_rot, k_rot, dt, dta = _cd_rope_dt(
        p, q_col, k_col, dt_col, jnp.tile(cos, (1, LANES // ATT_HD)), jnp.tile(sin, (1, LANES // ATT_HD)),
        lane_pad(ssd_dt_bias[0]), lane_pad(-jnp.exp(ssd_A_log[0])), n_ctx_rows=RC, seq_len=L)
    y_dirs = [_ssd_scan(xbc, dt, dta, reverse=(d == 1), **geo_ssd) for d in range(2)]
    mix = _ssd_post(y_dirs[0], y_dirs[1], xbc, p, jnp.repeat(ssd_D[0], SSD_HD).reshape(1, SSD_W),
                    ssd_norm_g[0].reshape(1, SSD_W), SSD_W + ATT_Q_W, n_ctx_rows=RC)
    mix = _window_attn(q_rot, k_rot, p, v_col, attn_sink[0].reshape(1, ATT_HEADS), mix, SSD_W,
                       n_batch=B, n_ctx=C, seq_len=L)
    xo, h2 = _mm_res_norm(mix, cd_w_out[0].astype(BF16), xs, g1, norm2_g[1], sh2, sc2, group_x, res_row0=RC,
                          name="cd_out")
    m1 = _mm(h2, mlp_w1[1].astype(BF16), out_dtype=BF16, act="relu2", name="mlp_up")
    zero = jnp.zeros((n_groups, 1, D), F32)
    out = _mm_res_norm(m1, mlp_w2[1].astype(BF16), xo, g2, final_g, zero, zero, group_x, keep_x=False,
                       h_dtype=F32, name="mlp_down")
    return out.reshape(B, L, D)
```

```python
import functools
import math

import jax
import jax.numpy as jnp
from jax import lax
from jax.experimental import pallas as pl
from jax.experimental.pallas import tpu as pltpu

F32 = jnp.float32
BF16 = jnp.bfloat16

EPS = 1e-6
GN_EPS = 64e-5
GRID_W = 64
ROPE_BASE = 10000.0

RWKV_HEADS = 16
RWKV_HD = 64
RWKV_W = RWKV_HEADS * RWKV_HD
DECAY_LORA = 96
AAA_LORA = 96
GATE_LORA = 256
RWKV_COLS = 3 * RWKV_W + DECAY_LORA + AAA_LORA + GATE_LORA
S5_W = 1024
S5_P = 16
S5_G = S5_W // S5_P
S5_N = 64
SSD_HEADS = 32
SSD_HD = 64
SSD_W = SSD_HEADS * SSD_HD
SSD_GROUPS = 4
SSD_N = 128
SSD_CHUNK = 128
XBC_W = SSD_W + 2 * SSD_GROUPS * SSD_N
ATT_HEADS = 16
KV_HEADS = 4
ATT_HD = 64
ATT_Q_W = ATT_HEADS * ATT_HD
ATT_KV_W = KV_HEADS * ATT_HD
WINDOW = 128
ATT_BLOCK = 128

LANES = 128
SUBLANES = 8
SEQ_BLOCK = 256
RWKV_CHUNK = 64
S5_GB = 8
VMEM_LIMIT = 48 * 1024 * 1024
VMEM_INTERNAL = 6 * 1024 * 1024


def _cparams(sem):
    return pltpu.CompilerParams(dimension_semantics=sem, vmem_limit_bytes=VMEM_LIMIT)


def _pick(n, cands):
    for c in cands:
        if n % c == 0:
            return c
    return n


def _dot(a, b):
    return jnp.dot(a.astype(BF16), b.astype(BF16), preferred_element_type=F32)


def _dot_nt(a, b):
    return lax.dot_general(a.astype(BF16), b.astype(BF16), (((1,), (1,)), ((), ())), preferred_element_type=F32)


def _dot_tn(a, b):
    return lax.dot_general(a.astype(BF16), b.astype(BF16), (((0,), (0,)), ((), ())), preferred_element_type=F32)


def _split3(x):
    hi = x.astype(BF16)
    r1 = x - hi.astype(F32)
    mid = r1.astype(BF16)
    lo = (r1 - mid.astype(F32)).astype(BF16)
    return hi, mid, lo


def _dot_ones_x(tri, x):
    hi, mid, lo = _split3(x)
    f = lambda t: jnp.dot(tri, t, preferred_element_type=F32)
    return f(hi) + f(mid) + f(lo)


def _dot_x_ones_nt(x, tri):
    hi, mid, lo = _split3(x)
    f = lambda t: lax.dot_general(t, tri, (((1,), (1,)), ((), ())), preferred_element_type=F32)
    return f(hi) + f(mid) + f(lo)


def _mm_body(*refs, nk, act, residual):
    if residual:
        a_ref, b_ref, res_ref, gate_ref, o_ref, acc_ref = refs
    else:
        a_ref, b_ref, o_ref, acc_ref = refs
    k = pl.program_id(2)
    part = jnp.dot(a_ref[...].astype(BF16), b_ref[...].astype(BF16), preferred_element_type=F32)

    def finish(acc):
        if act == "relu2":
            acc = jnp.square(jnp.maximum(acc, 0.0))
        if residual:
            acc = res_ref[...] + gate_ref[0] * acc
        o_ref[...] = acc.astype(o_ref.dtype)

    if nk == 1:
        finish(part)
    else:
        @pl.when(k == 0)
        def _():
            acc_ref[...] = part

        @pl.when(k > 0)
        def _():
            acc_ref[...] += part

        @pl.when(k == nk - 1)
        def _():
            finish(acc_ref[...])


def _mm(a, b, *, b_layer=0, out_dtype=F32, act=None, res=None, res_row0=0, gate=None, gate_idx=None, name="mm"):
    M, K = a.shape
    N = b.shape[-1]
    tn = _pick(N, (512, 384, 256, 128))
    residual = res is not None
    a_bytes, b_bytes, o_bytes = a.dtype.itemsize, b.dtype.itemsize, jnp.dtype(out_dtype).itemsize

    def vmem_bytes(tm, tk):
        blocks = tm * tk * a_bytes + tk * tn * b_bytes + tm * tn * o_bytes + (tm * tn * 4 if residual else 0)
        return 2 * blocks + (tm * tn * 4 if tk < K else 0)

    tm, tk = next(((tm, tk) for tk in (K, 4096, 2048, 1024, 512, 256, 128) if K % tk == 0
                   for tm in (1024, 512, 256, 128, 64, 32, 16, 8) if M % tm == 0
                   and vmem_bytes(tm, tk) <= VMEM_LIMIT - VMEM_INTERNAL), (8, 128))
    nk = K // tk
    if b.ndim == 3:
        b_spec = pl.BlockSpec((None, tk, tn), lambda i, j, k: (b_layer, k, j))
    else:
        b_spec = pl.BlockSpec((tk, tn), lambda i, j, k: (k, j))
    in_specs = [pl.BlockSpec((tm, tk), lambda i, j, k: (i, k)), b_spec]
    args = [a, b]
    if residual:
        if res_row0 % tm:
            res, res_row0 = res[res_row0:res_row0 + M], 0
        r0 = res_row0 // tm
        in_specs += [pl.BlockSpec((tm, tn), lambda i, j, k: (r0 + i, j)),
                     pl.BlockSpec((1, 1, tn), lambda i, j, k: (gate_idx(i * tm), 0, j))]
        args += [res, gate]
    return pl.pallas_call(
        functools.partial(_mm_body, nk=nk, act=act, residual=residual),
        grid=(M // tm, N // tn, nk),
        in_specs=in_specs,
        out_specs=pl.BlockSpec((tm, tn), lambda i, j, k: (i, j)),
        out_shape=jax.ShapeDtypeStruct((M, N), out_dtype),
        scratch_shapes=[pltpu.VMEM((tm, tn), F32)],
        compiler_params=_cparams(("parallel", "parallel", "arbitrary")),
        name=name,
    )(*args)


def _mm_res_norm_body(a_ref, b_ref, res_ref, gate_ref, g_ref, sh_ref, sc_ref, *out_refs, nk, keep_x):
    if nk > 1:
        *out_refs, acc_ref = out_refs
    h_ref = out_refs[-1]
    k = pl.program_id(1)
    part = jnp.dot(a_ref[...], b_ref[...], preferred_element_type=F32)

    def finish(acc):
        x = res_ref[...] + gate_ref[0] * acc
        if keep_x:
            out_refs[0][...] = x
        ms = jnp.mean(x * x, axis=-1, keepdims=True)
        xn = x * lax.rsqrt(ms + EPS) * g_ref[...]
        h_ref[...] = (xn * (1.0 + sc_ref[0]) + sh_ref[0]).astype(h_ref.dtype)

    if nk == 1:
        finish(part)
        return

    @pl.when(k == 0)
    def _():
        acc_ref[...] = part

    @pl.when(k > 0)
    def _():
        acc_ref[...] += part

    @pl.when(k == nk - 1)
    def _():
        finish(acc_ref[...])


def _mm_res_norm(a, b, res, gate, norm_g, shift, scale, group_idx, *, res_row0=0, keep_x=True, h_dtype=BF16,
                 name="mm_res_norm"):
    M, K = a.shape
    N = b.shape[1]
    h_bytes = jnp.dtype(h_dtype).itemsize

    def vmem_bytes(tm, tk):
        blocks = tm * tk * 2 + tk * N * 2 + tm * N * 4 + (tm * N * 4 if keep_x else 0) + tm * N * h_bytes
        return 2 * blocks + (tm * N * 4 if tk < K else 0)

    tm, tk = next(((tm, tk) for tk in (K, 1024, 512, 256, 128) if K % tk == 0
                   for tm in (512, 256, 128, 64, 32, 16, 8) if M % tm == 0
                   and vmem_bytes(tm, tk) <= VMEM_LIMIT - VMEM_INTERNAL), (8, 128))
    nk = K // tk
    if res_row0 % tm:
        res, res_row0 = res[res_row0:res_row0 + M], 0
    r0 = res_row0 // tm
    grp = lambda i, k: (group_idx(i * tm), 0, 0)
    row = pl.BlockSpec((tm, N), lambda i, k: (i, 0))
    out_specs = ([row] if keep_x else []) + [row]
    out_shape = ([jax.ShapeDtypeStruct((M, N), F32)] if keep_x else []) + [jax.ShapeDtypeStruct((M, N), h_dtype)]
    out = pl.pallas_call(
        functools.partial(_mm_res_norm_body, nk=nk, keep_x=keep_x),
        grid=(M // tm, nk),
        in_specs=[pl.BlockSpec((tm, tk), lambda i, k: (i, k)),
                  pl.BlockSpec((tk, N), lambda i, k: (k, 0)),
                  pl.BlockSpec((tm, N), lambda i, k: (r0 + i, 0)),
                  pl.BlockSpec((1, 1, N), grp),
                  pl.BlockSpec((1, N), lambda i, k: (0, 0)),
                  pl.BlockSpec((1, 1, N), grp), pl.BlockSpec((1, 1, N), grp)],
        out_specs=out_specs,
        out_shape=out_shape,
        scratch_shapes=[pltpu.VMEM((tm, N), F32)] if nk > 1 else [],
        compiler_params=_cparams(("parallel", "arbitrary")),
        name=name,
    )(a, b, res, gate, norm_g.reshape(1, N), shift, scale)
    return out if keep_x else out[0]


def _norm_body(x_ref, g_ref, sh_ref, sc_ref, o_ref):
    x = x_ref[...]
    ms = jnp.mean(x * x, axis=-1, keepdims=True)
    xn = x * lax.rsqrt(ms + EPS) * g_ref[...]
    o_ref[...] = (xn * (1.0 + sc_ref[0]) + sh_ref[0]).astype(o_ref.dtype)


def _norm_mod(x, g, shift, scale, group_idx, out_dtype=BF16):
    M, D = x.shape
    tm = _pick(M, (256, 128, 64, 32, 16, 8))
    return pl.pallas_call(
        _norm_body,
        grid=(M // tm,),
        in_specs=[pl.BlockSpec((tm, D), lambda i: (i, 0)),
                  pl.BlockSpec((1, D), lambda i: (0, 0)),
                  pl.BlockSpec((1, 1, D), lambda i: (group_idx(i * tm), 0, 0)),
                  pl.BlockSpec((1, 1, D), lambda i: (group_idx(i * tm), 0, 0))],
        out_specs=pl.BlockSpec((tm, D), lambda i: (i, 0)),
        out_shape=jax.ShapeDtypeStruct((M, D), out_dtype),
        compiler_params=_cparams(("parallel",)),
        name="norm_mod",
    )(x, g.reshape(1, D), shift, scale)


def _norm_first_body(c_ref, x_ref, g_ref, sh_ref, sc_ref, h_ref, xs_ref, *, n_ctx_tiles):
    x = jnp.where(pl.program_id(0) < n_ctx_tiles, c_ref[...], x_ref[...])
    xs_ref[...] = x
    ms = jnp.mean(x * x, axis=-1, keepdims=True)
    xn = x * lax.rsqrt(ms + EPS) * g_ref[...]
    h_ref[...] = (xn * (1.0 + sc_ref[0]) + sh_ref[0]).astype(h_ref.dtype)


def _norm_mod_first(ctx_rows, x_rows, g, shift, scale, group_idx):
    RC, D = ctx_rows.shape
    RX = x_rows.shape[0]
    tm = SEQ_BLOCK
    nc = RC // tm
    grp = lambda i: (group_idx(i * tm), 0, 0)
    row = pl.BlockSpec((tm, D), lambda i: (i, 0))
    return pl.pallas_call(
        functools.partial(_norm_first_body, n_ctx_tiles=nc),
        grid=((RC + RX) // tm,),
        in_specs=[pl.BlockSpec((tm, D), lambda i: (jnp.minimum(i, nc - 1), 0)),
                  pl.BlockSpec((tm, D), lambda i: (jnp.maximum(i - nc, 0), 0)),
                  pl.BlockSpec((1, D), lambda i: (0, 0)),
                  pl.BlockSpec((1, 1, D), grp), pl.BlockSpec((1, 1, D), grp)],
        out_specs=[row, row],
        out_shape=[jax.ShapeDtypeStruct((RC + RX, D), BF16), jax.ShapeDtypeStruct((RC + RX, D), F32)],
        compiler_params=_cparams(("parallel",)),
        name="norm_first",
    )(ctx_rows, x_rows, g.reshape(1, D), shift, scale)


def _seq_block(b, s, *, n_ctx_blk, n_x_blk, n_batch, reverse):
    if reverse:
        c = b * n_ctx_blk + (n_ctx_blk - 1 - s)
        x = n_batch * n_ctx_blk + b * n_x_blk + (n_x_blk - 1 - (s - n_ctx_blk))
    else:
        c = b * n_ctx_blk + s
        x = n_batch * n_ctx_blk + b * n_x_blk + (s - n_ctx_blk)
    return jnp.where(s < n_ctx_blk, c, x)


def _split2(x):
    hi = x.astype(BF16)
    lo = (x - hi.astype(F32)).astype(BF16)
    return hi, lo


def _rwkv_chunk(lw, r, v, kk, kd, be, st, tri, strict, incl, same_head, head0, eye):
    C = RWKV_CHUNK

    def stack(z):
        return jnp.concatenate([jnp.where(head0, z, 0.0), jnp.where(head0, 0.0, z)], axis=0)

    def unstack(z):
        return z[0:C] + z[C:2 * C]

    bdot = lambda a, b: jnp.dot(a, b, preferred_element_type=F32)
    bdot_nt = lambda a, b: lax.dot_general(a, b, (((1,), (1,)), ((), ())), preferred_element_type=F32)
    bdot_tn = lambda a, b: lax.dot_general(a, b, (((0,), (0,)), ((), ())), preferred_element_type=F32)
    cat = lambda *zs: jnp.concatenate(zs, axis=0)
    P = range(len(lw))
    cum = [_dot_ones_x(tri, lw[p]) for p in P]
    total = [jnp.sum(lw[p], axis=0, keepdims=True) for p in P]
    a_t = [kk[p] * jnp.exp(cum[p] - lw[p]) for p in P]
    r_t = [r[p] * jnp.exp(cum[p]) for p in P]
    a_sp = [_split2(a_t[p]) for p in P]
    r_hi = [r_t[p].astype(BF16) for p in P]
    sa_sp = [_split2(stack(a_t[p])) for p in P]
    sr_hi = [stack(r_t[p]).astype(BF16) for p in P]
    pinv = [jnp.exp(-cum[p]) for p in P]
    bk_sp = [_split2(cat(stack(be[p] * pinv[p]), stack(kd[p] * pinv[p]))) for p in P]
    g1 = [bdot_nt(cat(sa_sp[p][0], sa_sp[p][1], sr_hi[p]), bk_sp[p][0]) for p in P]
    g2 = [bdot_nt(sa_sp[p][0], bk_sp[p][1]) for p in P]
    n_all = [g1[p][0:2 * C] + g1[p][2 * C:4 * C] + g2[p] for p in P]
    x_neg = [jnp.where(strict, -n_all[p][:, 0:2 * C], 0.0) for p in P]
    n_ak = [jnp.where(strict, n_all[p][:, 2 * C:4 * C], 0.0) for p in P]
    m_r = [jnp.concatenate([jnp.where(incl, g1[p][4 * C:6 * C, 0:2 * C], 0.0),
                            jnp.where(incl, g1[p][4 * C:6 * C, 2 * C:4 * C], 0.0)], axis=1).astype(BF16)
           for p in P]
    xb = [x_neg[p].astype(BF16) for p in P]
    t_inv = [eye + x_neg[p] for p in P]
    n_pow = int(math.log2(C))
    for it in range(1, n_pow):
        xb = [bdot(xb[p], xb[p]).astype(BF16) for p in P]
        t_inv = [t_inv[p] + bdot(xb[p], t_inv[p].astype(BF16)) for p in P]
    t_inv = [t_inv[p].astype(BF16) for p in P]
    st_sp = [_split2(st[p]) for p in P]
    as1 = [bdot_nt(cat(a_sp[p][0], a_sp[p][1], r_hi[p]), st_sp[p][0]) for p in P]
    as2 = [bdot_nt(a_sp[p][0], st_sp[p][1]) for p in P]
    v_sp = [_split2(stack(v[p])) for p in P]
    nk_sp = [_split2(n_ak[p]) for p in P]
    nv1 = [bdot(cat(nk_sp[p][0], nk_sp[p][1]), v_sp[p][0]) for p in P]
    nv2 = [bdot(nk_sp[p][0], v_sp[p][1]) for p in P]
    rhs = [-(stack(as1[p][0:C] + as1[p][C:2 * C] + as2[p]) + nv1[p][0:2 * C] + nv1[p][2 * C:4 * C] + nv2[p])
           for p in P]
    u0 = [bdot(t_inv[p], rhs[p].astype(BF16)) for p in P]
    u0_sp = [_split2(u0[p]) for p in P]
    x_sp = [_split2(x_neg[p]) for p in P]
    xu1 = [bdot(cat(x_sp[p][0], x_sp[p][1]), u0_sp[p][0]) for p in P]
    xu2 = [bdot(x_sp[p][0], u0_sp[p][1]) for p in P]
    res = [(rhs[p] - u0[p] + xu1[p][0:2 * C] + xu1[p][2 * C:4 * C] + xu2[p]).astype(BF16) for p in P]
    u = [u0[p] + bdot(t_inv[p], res[p]) for p in P]
    y = [stack(as1[p][2 * C:3 * C]) + bdot(m_r[p], cat(u[p].astype(BF16), v_sp[p][0])) for p in P]
    uv_sp = [_split2(cat(unstack(u[p]), v[p])) for p in P]
    pend = [jnp.exp(total[p] - cum[p]) for p in P]
    e_sp = [_split2(cat(be[p] * pend[p], kd[p] * pend[p])) for p in P]
    upd = [bdot_tn(cat(uv_sp[p][0], uv_sp[p][1], uv_sp[p][0]), cat(e_sp[p][0], e_sp[p][0], e_sp[p][1]))
           for p in P]
    st_new = [st[p] * jnp.exp(total[p]) + jnp.where(same_head, upd[p], 0.0) for p in P]
    return [unstack(y[p]) for p in P], st_new


def _rwkv_body(r_ref, v_ref, kk_ref, lw_ref, kd_ref, be_ref, y_ref, st_ref, *, reverse, npair):
    C = RWKV_CHUNK
    nchunk = SEQ_BLOCK // C
    s = pl.program_id(2)

    @pl.when(s == 0)
    def _():
        st_ref[...] = jnp.zeros_like(st_ref)

    sgn = -1 if reverse else 1
    row = lax.broadcasted_iota(jnp.int32, (2 * C, 2 * C), 0)
    col = lax.broadcasted_iota(jnp.int32, (2 * C, 2 * C), 1)
    diff = (row - col) * sgn
    strict = diff > 0
    incl = diff >= 0
    same_head = (row // C) == (col // C)
    eye = jnp.where(row == col, 1.0, 0.0)
    r64 = lax.broadcasted_iota(jnp.int32, (C, C), 0)
    c64 = lax.broadcasted_iota(jnp.int32, (C, C), 1)
    tri = jnp.where((r64 - c64) * sgn >= 0, 1.0, 0.0).astype(BF16)
    head0 = lax.broadcasted_iota(jnp.int32, (C, 2 * C), 1) < C

    def chunk(ci, carry):
        ce = (nchunk - 1 - ci) if reverse else ci
        sl = pl.ds(pl.multiple_of(ce * C, C), C)
        lanes = [slice(p * LANES, (p + 1) * LANES) for p in range(npair)]
        get = lambda ref: [ref[sl, ln] for ln in lanes]
        ys, sts = _rwkv_chunk(get(lw_ref), get(r_ref), get(v_ref), get(kk_ref), get(kd_ref), get(be_ref),
                              [st_ref[p] for p in range(npair)], tri, strict, incl, same_head, head0, eye)
        for p in range(npair):
            y_ref[sl, lanes[p]] = ys[p]
            st_ref[p] = sts[p]
        return carry

    lax.fori_loop(0, nchunk, chunk, 0)


def _rwkv_scan(r, v, kk, lw, kd, be, *, n_batch, n_ctx_blk, n_x_blk, reverse, npair=8):
    R, W = r.shape
    ngrp = W // (LANES * npair)
    ns = n_ctx_blk + n_x_blk
    d = 1 if reverse else 0
    blk = functools.partial(_seq_block, n_ctx_blk=n_ctx_blk, n_x_blk=n_x_blk, n_batch=n_batch, reverse=reverse)
    spec = pl.BlockSpec((SEQ_BLOCK, LANES * npair), lambda b, p, s: (blk(b, s), p))
    dspec = pl.BlockSpec((None, SEQ_BLOCK, LANES * npair), lambda b, p, s: (d, blk(b, s), p))
    return pl.pallas_call(
        functools.partial(_rwkv_body, reverse=reverse, npair=npair),
        grid=(n_batch, ngrp, ns),
        in_specs=[spec] * 3 + [dspec] * 3,
        out_specs=spec,
        out_shape=jax.ShapeDtypeStruct((R, W), F32),
        scratch_shapes=[pltpu.VMEM((npair, LANES, LANES), F32)],
        compiler_params=_cparams(("parallel", "parallel", "arbitrary")),
        name="rwkv_scan_bwd" if reverse else "rwkv_scan_fwd",
    )(r, v, kk, lw, kd, be)


S5_SEG = SEQ_BLOCK // SUBLANES
S5_TAB_SLABS = 2 * S5_SEG + 2 + 8
S5_NGB = 4


def _s5_body(u_ref, bw_ref, cw_ref, tab_ref, y_ref, hb_ref, carry_ref, perm_ref, *, reverse):
    W = S5_GB * S5_N
    T = SEQ_BLOCK
    G = range(S5_NGB)
    s = pl.program_id(2)

    @pl.when(s == 0)
    def _():
        carry_ref[...] = jnp.zeros_like(carry_ref)
        prow = lax.broadcasted_iota(jnp.int32, (T, T), 0)
        pcol = lax.broadcasted_iota(jnp.int32, (T, T), 1)
        t_of_row = S5_SEG * (prow % SUBLANES) + prow // SUBLANES
        if reverse:
            t_of_row = T - 1 - t_of_row
        perm_ref[...] = jnp.where(pcol == t_of_row, 1.0, 0.0).astype(BF16)

    perm = perm_ref[...]
    u_p = jnp.dot(perm, u_ref[...].astype(BF16), preferred_element_type=F32).astype(BF16)
    for g in G:
        hb_ref[g] = jnp.dot(u_p[:, g * LANES:(g + 1) * LANES], bw_ref[g], preferred_element_type=F32)

    slab = lambda j: pl.ds(j * SUBLANES, SUBLANES)
    tab = lambda g, n: tab_ref[g, slab(n), :]
    o_seg = 2 * S5_SEG
    o_pow = o_seg + 2
    re, im = slice(0, W), slice(W, 2 * W)
    a = [(tab(g, o_pow), tab(g, o_pow + 1)) for g in G]
    h = [(hb_ref[g, slab(0), re], hb_ref[g, slab(0), im]) for g in G]
    for j in range(1, S5_SEG):
        h = [(hb_ref[g, slab(j), re] + a[g][0] * h[g][0] - a[g][1] * h[g][1],
              hb_ref[g, slab(j), im] + a[g][0] * h[g][1] + a[g][1] * h[g][0]) for g in G]
        for g in G:
            hb_ref[g, slab(j), re] = h[g][0]
            hb_ref[g, slab(j), im] = h[g][1]
    rows8 = lax.broadcasted_iota(jnp.int32, (SUBLANES, W), 0)
    ent = []
    for g in G:
        gr, gi = h[g]
        for n, k in enumerate((1, 2, 4)):
            pr, pi = tab(g, o_pow + 2 + 2 * n), tab(g, o_pow + 3 + 2 * n)
            sr = jnp.where(rows8 >= k, pltpu.roll(gr, k, axis=0), 0.0)
            si = jnp.where(rows8 >= k, pltpu.roll(gi, k, axis=0), 0.0)
            gr, gi = gr + pr * sr - pi * si, gi + pr * si + pi * sr
        cr, ci = carry_ref[g, :, re], carry_ref[g, :, im]
        qr, qi = tab(g, o_seg), tab(g, o_seg + 1)
        gr, gi = gr + qr * cr - qi * ci, gi + qr * ci + qi * cr
        ent.append((jnp.where(rows8 >= 1, pltpu.roll(gr, 1, axis=0), cr),
                    jnp.where(rows8 >= 1, pltpu.roll(gi, 1, axis=0), ci)))
        carry_ref[g, :, re] = jnp.broadcast_to(gr[SUBLANES - 1:SUBLANES, :], (SUBLANES, W))
        carry_ref[g, :, im] = jnp.broadcast_to(gi[SUBLANES - 1:SUBLANES, :], (SUBLANES, W))
    for g in G:
        in_r, in_i = ent[g]
        for j in range(S5_SEG):
            pr, pi = tab(g, j), tab(g, S5_SEG + j)
            hb_ref[g, slab(j), re] = hb_ref[g, slab(j), re] + pr * in_r - pi * in_i
            hb_ref[g, slab(j), im] = hb_ref[g, slab(j), im] + pr * in_i + pi * in_r
    y_p = jnp.concatenate([jnp.dot(hb_ref[g].astype(BF16), cw_ref[g], preferred_element_type=F32) for g in G],
                          axis=1)
    y_ref[...] = _dot_x_ones_tn(perm, y_p)


def _dot_x_ones_tn(ones, x):
    hi, mid, lo = _split3(x)
    f = lambda t: lax.dot_general(ones, t, (((0,), (0,)), ((), ())), preferred_element_type=F32)
    return f(hi) + f(mid) + f(lo)


def _s5_scan(p, col0_blk, bw, cw, tab, *, n_batch, n_ctx_blk, n_x_blk, reverse):
    R = p.shape[0]
    n = S5_NGB
    ngb = S5_G // S5_GB // n
    ns = n_ctx_blk + n_x_blk
    blk = functools.partial(_seq_block, n_ctx_blk=n_ctx_blk, n_x_blk=n_x_blk, n_batch=n_batch, reverse=reverse)
    W2 = 2 * S5_GB * S5_N
    assert col0_blk % n == 0
    return pl.pallas_call(
        functools.partial(_s5_body, reverse=reverse),
        grid=(n_batch, ngb, ns),
        in_specs=[pl.BlockSpec((SEQ_BLOCK, n * LANES), lambda b, g, s: (blk(b, s), col0_blk // n + g)),
                  pl.BlockSpec((n, LANES, W2), lambda b, g, s: (g, 0, 0)),
                  pl.BlockSpec((n, W2, LANES), lambda b, g, s: (g, 0, 0)),
                  pl.BlockSpec((n, S5_TAB_SLABS * SUBLANES, W2 // 2), lambda b, g, s: (g, 0, 0))],
        out_specs=pl.BlockSpec((SEQ_BLOCK, n * LANES), lambda b, g, s: (blk(b, s), g)),
        out_shape=jax.ShapeDtypeStruct((R, S5_W), F32),
        scratch_shapes=[pltpu.VMEM((n, SEQ_BLOCK, W2), F32), pltpu.VMEM((n, SUBLANES, W2), F32),
                        pltpu.VMEM((SEQ_BLOCK, SEQ_BLOCK), BF16)],
        compiler_params=_cparams(("parallel", "parallel", "arbitrary")),
        name="s5_scan_bwd" if reverse else "s5_scan_fwd",
    )(p, bw, cw, tab)


def _s5_weights(A_re, A_im, log_dt, B_re, B_im, C_re, C_im):
    lam_re = jnp.minimum(A_re, -1e-4)
    dt = jnp.exp(log_dt)[:, None]
    mag = jnp.exp(lam_re * dt)
    ab_re, ab_im = mag * jnp.cos(A_im * dt), mag * jnp.sin(A_im * dt)
    den = lam_re * lam_re + A_im * A_im
    f_re = ((ab_re - 1) * lam_re + ab_im * A_im) / den
    f_im = (ab_im * lam_re - (ab_re - 1) * A_im) / den
    bb_re = f_re[..., None] * B_re - f_im[..., None] * B_im
    bb_im = f_re[..., None] * B_im + f_im[..., None] * B_re
    ngb = S5_G // S5_GB
    eye = jnp.eye(S5_GB, dtype=F32)

    def in_block(bb):
        t = bb.reshape(ngb, S5_GB, S5_N, S5_P)
        return jnp.einsum("agnp,gh->agphn", t, eye).reshape(ngb, S5_GB * S5_P, S5_GB * S5_N)

    def out_block(cc):
        t = cc.reshape(ngb, S5_GB, S5_P, S5_N)
        return jnp.einsum("agpn,gh->agnhp", t, eye).reshape(ngb, S5_GB * S5_N, S5_GB * S5_P)

    bw = jnp.concatenate([in_block(bb_re), in_block(bb_im)], axis=2).astype(BF16)
    cw = jnp.concatenate([out_block(C_re), -out_block(C_im)], axis=1).astype(BF16)

    def power(k):
        m = jnp.exp(k * lam_re * dt)
        return m * jnp.cos(k * A_im * dt), m * jnp.sin(k * A_im * dt)

    rep = lambda t: jnp.broadcast_to(t[None], (SUBLANES,) + t.shape)
    slab_pow = [power(float(j + 1)) for j in range(S5_SEG)]
    seg_pow = [power(float(S5_SEG * (i + 1))) for i in range(SUBLANES)]
    slabs = ([rep(c[0]) for c in slab_pow] + [rep(c[1]) for c in slab_pow]
             + [jnp.stack([c[0] for c in seg_pow]), jnp.stack([c[1] for c in seg_pow])])
    for k in (1.0, float(S5_SEG), 2.0 * S5_SEG, 4.0 * S5_SEG):
        slabs += [rep(t) for t in power(k)]
    tab = jnp.concatenate(slabs, axis=0)
    tab = tab.reshape(S5_TAB_SLABS * SUBLANES, ngb, S5_GB * S5_N).transpose(1, 0, 2)
    return bw, cw, tab


def _ssd_body(x_ref, b_ref, c_ref, dt_ref, dta_ref, sel_ref, y_ref, st_ref, *, reverse):
    Q = SSD_CHUNK
    s = pl.program_id(1)

    @pl.when(s == 0)
    def _():
        st_ref[...] = jnp.zeros_like(st_ref)

    row = lax.broadcasted_iota(jnp.int32, (Q, Q), 0)
    col = lax.broadcasted_iota(jnp.int32, (Q, Q), 1)
    mask = (col >= row) if reverse else (col <= row)
    tri = jnp.where(mask, 1.0, 0.0).astype(BF16)
    head0 = lax.broadcasted_iota(jnp.int32, (Q, LANES), 1) < SSD_HD
    head0_rows = lax.broadcasted_iota(jnp.int32, (LANES, SSD_N), 0) < SSD_HD

    h_lo = SSD_HEADS if reverse else 0
    dt = dt_ref[:, h_lo:h_lo + SSD_HEADS]
    dta = dta_ref[:, h_lo:h_lo + SSD_HEADS]
    dtt = dt_ref[...].T[h_lo:h_lo + SSD_HEADS, :]
    dtat = dta_ref[...].T[h_lo:h_lo + SSD_HEADS, :]
    acum = _dot_ones_x(tri, dta)
    acum_t = _dot_x_ones_nt(dtat, tri)
    total = jnp.sum(dta, axis=0, keepdims=True)
    total_t = jnp.sum(dtat, axis=1, keepdims=True)
    e_tot_t = jnp.exp(total_t)
    dt_wide = jnp.dot(dt.astype(BF16), sel_ref[...], preferred_element_type=F32)
    heads_per_group = SSD_HEADS // SSD_GROUPS
    G = range(SSD_GROUPS)
    pairs = [(g, g * heads_per_group + 2 * pr) for g in G for pr in range(heads_per_group // 2)]
    lanes2 = lambda h0: slice(h0 * SSD_HD, (h0 + 2) * SSD_HD)
    bg = [b_ref[:, g * SSD_N:(g + 1) * SSD_N].astype(BF16) for g in G]
    cg = [c_ref[:, g * SSD_N:(g + 1) * SSD_N].astype(BF16) for g in G]
    cb = [lax.dot_general(cg[g], bg[g], (((1,), (1,)), ((), ())), preferred_element_type=F32) for g in G]
    xp = [x_ref[:, lanes2(h0)] for _, h0 in pairs]
    xb = [t.astype(BF16) for t in xp]
    colb = [jnp.broadcast_to(acum[:, h:h + 1], (Q, LANES)) for h in range(SSD_HEADS)]

    def weights(g, h):
        seg = colb[h] - acum_t[h:h + 1, :]
        return (cb[g] * jnp.exp(jnp.where(mask, seg, -jnp.inf)) * dtt[h:h + 1, :]).astype(BF16)

    yd = [(jnp.dot(weights(g, h0), xb[i], preferred_element_type=F32),
           jnp.dot(weights(g, h0 + 1), xb[i], preferred_element_type=F32)) for i, (g, h0) in enumerate(pairs)]
    st = [st_ref[lanes2(h0), :] for _, h0 in pairs]
    yo = [lax.dot_general(cg[g], st[i].astype(BF16), (((1,), (1,)), ((), ())), preferred_element_type=F32)
          for i, (g, _) in enumerate(pairs)]
    for i, (g, h0) in enumerate(pairs):
        scale = jnp.exp(jnp.where(head0, colb[h0], colb[h0 + 1]))
        y_ref[:, lanes2(h0)] = jnp.where(head0, yd[i][0], yd[i][1]) + yo[i] * scale
    xt = [(xp[i] * dt_wide[:, lanes2(h0)]
           * jnp.exp(jnp.where(head0, total[:, h0:h0 + 1] - colb[h0], total[:, h0 + 1:h0 + 2] - colb[h0 + 1]))
           ).astype(BF16) for i, (_, h0) in enumerate(pairs)]
    upd = [lax.dot_general(xt[i], bg[g], (((0,), (0,)), ((), ())), preferred_element_type=F32)
           for i, (g, _) in enumerate(pairs)]
    for i, (_, h0) in enumerate(pairs):
        dec = jnp.where(head0_rows, e_tot_t[h0:h0 + 1, :], e_tot_t[h0 + 1:h0 + 2, :])
        st_ref[lanes2(h0), :] = st[i] * dec + upd[i]


def _ssd_scan(xbc, dt, dta, *, n_batch, n_ctx_blk, n_x_blk, reverse):
    R = xbc.shape[0]
    Q = SSD_CHUNK
    blk = functools.partial(_seq_block, n_ctx_blk=n_ctx_blk, n_x_blk=n_x_blk, n_batch=n_batch, reverse=reverse)
    ns = n_ctx_blk + n_x_blk
    gw = SSD_GROUPS * SSD_N
    row_spec = lambda w, c: pl.BlockSpec((Q, w), lambda b, s: (blk(b, s), c))
    head_of_lane = jnp.arange(SSD_W, dtype=jnp.int32) // SSD_HD
    sel = (jnp.arange(SSD_HEADS, dtype=jnp.int32)[:, None] == head_of_lane[None, :]).astype(BF16)
    return pl.pallas_call(
        functools.partial(_ssd_body, reverse=reverse),
        grid=(n_batch, ns),
        in_specs=[row_spec(SSD_W, 0), row_spec(gw, SSD_W // gw), row_spec(gw, SSD_W // gw + 1),
                  row_spec(LANES, 0), row_spec(LANES, 0), pl.BlockSpec((SSD_HEADS, SSD_W), lambda b, s: (0, 0))],
        out_specs=row_spec(SSD_W, 0),
        out_shape=jax.ShapeDtypeStruct((R, SSD_W), F32),
        scratch_shapes=[pltpu.VMEM((SSD_W, SSD_N), F32)],
        compiler_params=_cparams(("parallel", "arbitrary")),
        name="ssd_scan_bwd" if reverse else "ssd_scan_fwd",
    )(xbc, xbc, xbc, dt, dta, sel)


def _attn_body(q_ref, kp_ref, kc_ref, kn_ref, vp_ref, vc_ref, vn_ref, kx_ref, vx_ref, sink_ref, mix_ref, o_ref,
               *, seq_len):
    del mix_ref
    n = pl.program_id(1)
    blk = ATT_BLOCK
    n_ctx = kx_ref.shape[0]
    rep = ATT_HEADS // KV_HEADS
    k_all = jnp.concatenate([kx_ref[...], kp_ref[...], kc_ref[...], kn_ref[...]], axis=0).astype(BF16)
    v_all = jnp.concatenate([vx_ref[...], vp_ref[...], vc_ref[...], vn_ref[...]], axis=0).astype(BF16)
    nk = n_ctx + 3 * blk
    r_idx = lax.broadcasted_iota(jnp.int32, (rep * blk, nk), 0)
    q_pos = n * blk + r_idx % blk
    j = lax.broadcasted_iota(jnp.int32, (rep * blk, nk), 1)
    k_pos = (n - 1) * blk + (j - n_ctx)
    valid = (j < n_ctx) | ((jnp.abs(q_pos - k_pos) <= WINDOW) & (k_pos >= 0) & (k_pos < seq_len))
    scale = ATT_HD ** -0.5
    head_of_row = lax.broadcasted_iota(jnp.int32, (rep * blk, 1), 0) // blk
    G = range(KV_HEADS)
    hd = lambda t, h: t[:, h * ATT_HD:(h + 1) * ATT_HD]
    q = q_ref[...].astype(BF16)
    qs = [jnp.concatenate([hd(q, g * rep + i) for i in range(rep)], axis=0) for g in G]
    sinks = []
    for g in G:
        sk = jnp.zeros((rep * blk, 1), F32)
        for i in range(rep):
            sk = jnp.where(head_of_row == i, sink_ref[0:1, g * rep + i:g * rep + i + 1], sk)
        sinks.append(sk)
    sc = [jnp.where(valid, lax.dot_general(qs[g], hd(k_all, g), (((1,), (1,)), ((), ())),
                                           preferred_element_type=F32) * scale, -jnp.inf) for g in G]
    m = [jnp.maximum(jnp.max(sc[g], axis=-1, keepdims=True), sinks[g]) for g in G]
    e = [jnp.exp(sc[g] - m[g]) for g in G]
    den = [jnp.sum(e[g], axis=-1, keepdims=True) + jnp.exp(sinks[g] - m[g]) for g in G]
    pv = [jnp.dot((e[g] / den[g]).astype(BF16), hd(v_all, g), preferred_element_type=F32) for g in G]
    outs = [pv[g][i * blk:(i + 1) * blk] for g in G for i in range(rep)]
    o_ref[...] = jnp.concatenate(outs, axis=1).astype(o_ref.dtype)


def _window_attn(q, k, p, v_col, sink, mix, mix_col, *, n_batch, n_ctx, seq_len):
    nb = seq_len // ATT_BLOCK
    x0 = n_batch * n_ctx // ATT_BLOCK
    vc = v_col // ATT_KV_W
    cur = lambda b, n: x0 + b * nb + n
    prev = lambda b, n: x0 + b * nb + jnp.maximum(n - 1, 0)
    nxt = lambda b, n: x0 + b * nb + jnp.minimum(n + 1, nb - 1)
    k_spec = lambda im: pl.BlockSpec((ATT_BLOCK, ATT_KV_W), lambda b, n: (im(b, n), 0))
    v_spec = lambda im: pl.BlockSpec((ATT_BLOCK, ATT_KV_W), lambda b, n: (im(b, n), vc))
    return pl.pallas_call(
        functools.partial(_attn_body, seq_len=seq_len),
        grid=(n_batch, nb),
        in_specs=[pl.BlockSpec((ATT_BLOCK, ATT_Q_W), lambda b, n: (cur(b, n), 0)),
                  k_spec(prev), k_spec(cur), k_spec(nxt), v_spec(prev), v_spec(cur), v_spec(nxt),
                  pl.BlockSpec((n_ctx, ATT_KV_W), lambda b, n: (b, 0)),
                  pl.BlockSpec((n_ctx, ATT_KV_W), lambda b, n: (b, vc)),
                  pl.BlockSpec((1, ATT_HEADS), lambda b, n: (0, 0)),
                  pl.BlockSpec(memory_space=pl.ANY)],
        out_specs=pl.BlockSpec((ATT_BLOCK, ATT_Q_W), lambda b, n: (b * nb + n, mix_col // ATT_Q_W)),
        out_shape=jax.ShapeDtypeStruct(mix.shape, mix.dtype),
        input_output_aliases={10: 0},
        compiler_params=_cparams(("parallel", "parallel")),
        name="window_attn",
    )(q, k, k, k, p, p, p, k, p, sink, mix)


def _tile_edges(row0, tm, n_ctx_rows, ctx_len, seq_len):
    in_ctx = row0 < n_ctx_rows
    pos0 = jnp.where(in_ctx, row0 % ctx_len, (row0 - n_ctx_rows) % seq_len)
    seg = jnp.where(in_ctx, ctx_len, seq_len)
    return pos0 == 0, pos0 + tm == seg


def _shifted_rows(cur, k, prev_blk, next_blk, first, last, rows):
    tm = cur.shape[0]
    if k < 0:
        edge = jnp.where(first, 0.0, pltpu.roll(prev_blk, -k, axis=0))
        body = pltpu.roll(cur, -k, axis=0)
        head = jnp.where(rows < -k, edge, body[0:SUBLANES])
        return jnp.concatenate([head, body[SUBLANES:]], axis=0)
    edge = jnp.where(last, 0.0, pltpu.roll(next_blk, SUBLANES - k, axis=0))
    body = pltpu.roll(cur, tm - k, axis=0)
    tail = jnp.where(rows >= SUBLANES - k, edge, body[tm - SUBLANES:])
    return jnp.concatenate([body[:tm - SUBLANES], tail], axis=0)


def _head_sum_mxu(t, blk):
    n = t.shape[1] // LANES
    return jnp.concatenate([_dot_x_ones_nt(t[:, j * LANES:(j + 1) * LANES], blk) for j in range(n)], axis=1)


def _same_head_matrix():
    r = lax.broadcasted_iota(jnp.int32, (LANES, LANES), 0) // RWKV_HD
    c = lax.broadcasted_iota(jnp.int32, (LANES, LANES), 1) // RWKV_HD
    return jnp.where(r == c, 1.0, 0.0).astype(BF16)


def _softplus(x):
    return jnp.maximum(x, 0.0) + jnp.log(1.0 + jnp.exp(-jnp.abs(x)))


def _rwkv_pre_body(p_ref, hp_ref, hn_ref, mu_ref, lora_ref, gw_ref, w0_ref, a0_ref, kk_ref, ka_ref, rk_ref,
                   r_out, v_out, kk_out, g_out, bonus_out, lw_out, kd_out, be_out, *, tm, geo):
    W = RWKV_W
    first, last = _tile_edges(pl.program_id(0) * tm, tm, *geo)
    rows = lax.broadcasted_iota(jnp.int32, (SUBLANES, 1), 0)

    def tshift(c0, c1):
        cur = p_ref[:, c0:c1]
        prev = _shifted_rows(cur, -1, hp_ref[:, c0:c1], hn_ref[:, c0:c1], first, last, rows)
        nxt = _shifted_rows(cur, 1, hp_ref[:, c0:c1], hn_ref[:, c0:c1], first, last, rows)
        return cur + mu_ref[0:1, c0:c1] * (prev - cur) + mu_ref[1:2, c0:c1] * (nxt - cur)

    r = tshift(0, W)
    k = tshift(W, 2 * W)
    v = tshift(2 * W, 3 * W)
    xl = tshift(3 * W, 3 * W + 4 * LANES)
    lane = lax.broadcasted_iota(jnp.int32, (tm, 2 * LANES), 1)
    lin = jnp.where(lane < DECAY_LORA, jnp.tanh(xl[:, :2 * LANES]), xl[:, :2 * LANES])
    lora = _dot(lin, lora_ref[...])
    g_out[...] = _dot(jax.nn.sigmoid(xl), gw_ref[...])
    blk = _same_head_matrix()
    kk = k * kk_ref[...]
    kk = kk / jnp.maximum(jnp.sqrt(_head_sum_mxu(kk * kk, blk)), 1e-12)
    r_out[...] = r
    v_out[...] = v
    kk_out[...] = kk
    kd_sum = jnp.zeros_like(k)
    for d in range(2):
        w_log = -_softplus(-(w0_ref[d:d + 1, :] + lora[:, d * W:(d + 1) * W])) - 0.5
        a = jax.nn.sigmoid(a0_ref[d:d + 1, :] + lora[:, (2 + d) * W:(3 + d) * W])
        kd = k * (1.0 + (a - 1.0) * ka_ref[...])
        lw_out[d] = -jnp.exp(w_log)
        kd_out[d] = kd
        be_out[d] = a * kk
        kd_sum = kd_sum + kd
    bonus_out[...] = _head_sum_mxu(r * kd_sum * rk_ref[...], blk) * v


def _rwkv_pre(p, mu, lora_w, gate_w, w0, a0, k_k, k_a, r_k, *, geo):
    R = p.shape[0]
    W = RWKV_W
    tm = SEQ_BLOCK
    pw = 3 * W + 4 * LANES
    nb8 = R // SUBLANES
    row = lambda i: (i, 0)
    vec = lambda n: pl.BlockSpec((n, W), lambda i: (0, 0))
    out2 = pl.BlockSpec((2, tm, W), lambda i: (0, i, 0))
    f = jax.ShapeDtypeStruct((R, W), F32)
    f2 = jax.ShapeDtypeStruct((2, R, W), F32)
    return pl.pallas_call(
        functools.partial(_rwkv_pre_body, tm=tm, geo=geo),
        grid=(R // tm,),
        in_specs=[pl.BlockSpec((tm, pw), row),
                  pl.BlockSpec((SUBLANES, pw), lambda i: (jnp.maximum(i * (tm // SUBLANES) - 1, 0), 0)),
                  pl.BlockSpec((SUBLANES, pw), lambda i: (jnp.minimum((i + 1) * (tm // SUBLANES), nb8 - 1), 0)),
                  pl.BlockSpec((2, pw), lambda i: (0, 0)),
                  pl.BlockSpec(lora_w.shape, lambda i: (0, 0)),
                  pl.BlockSpec(gate_w.shape, lambda i: (0, 0)),
                  vec(2), vec(2), vec(1), vec(1), vec(1)],
        out_specs=[pl.BlockSpec((tm, W), row)] * 5 + [out2] * 3,
        out_shape=[f] * 5 + [f2] * 3,
        compiler_params=_cparams(("parallel",)),
        name="rwkv_pre",
    )(p, p, p, mu, lora_w, gate_w, w0, a0, k_k, k_a, r_k)


def _rwkv_post_body(yf_ref, yb_ref, bonus_ref, g_ref, lng_ref, lnb_ref, o_ref):
    blk = _same_head_matrix()
    y = yf_ref[...] + yb_ref[...]
    mean = _head_sum_mxu(y, blk) * (1.0 / RWKV_HD)
    d = y - mean
    var = _head_sum_mxu(d * d, blk) * (1.0 / RWKV_HD)
    yn = d * lax.rsqrt(var + GN_EPS)
    o_ref[...] = ((yn * lng_ref[...] + lnb_ref[...] + bonus_ref[...]) * g_ref[...]).astype(o_ref.dtype)


def _rwkv_post(yf, yb, bonus, g, ln_g, ln_b, out_cols):
    R, W = yf.shape
    tm = SEQ_BLOCK
    row = pl.BlockSpec((tm, W), lambda i: (i, 0))
    vec = pl.BlockSpec((1, W), lambda i: (0, 0))
    return pl.pallas_call(
        _rwkv_post_body,
        grid=(R // tm,),
        in_specs=[row, row, row, row, vec, vec],
        out_specs=row,
        out_shape=jax.ShapeDtypeStruct((R, out_cols), BF16),
        compiler_params=_cparams(("parallel",)),
        name="rwkv_post",
    )(yf, yb, bonus, g, ln_g, ln_b)


def _gelu_tanh(x):
    return 0.5 * x * (1.0 + jnp.tanh(math.sqrt(2.0 / math.pi) * (x + 0.044715 * (x * x * x))))


def _s5_post_body(u0_ref, u1_ref, yf_ref, yb_ref, d_ref, w_ref, b_ref, mix_ref, o_ref):
    del mix_ref
    u = jnp.concatenate([u0_ref[...], u1_ref[...]], axis=1)
    y = _gelu_tanh(d_ref[...] * u + yf_ref[...] + yb_ref[...])
    o_ref[...] = (y * jax.nn.sigmoid(_dot(y, w_ref[...]) + b_ref[...])).astype(o_ref.dtype)


def _s5_post(p, col0, yf, yb, d_skip, glu_w, glu_b, mix):
    R, W = yf.shape
    tm = SEQ_BLOCK
    half = W // 2
    row = pl.BlockSpec((tm, W), lambda i: (i, 0))
    vec = pl.BlockSpec((1, W), lambda i: (0, 0))
    return pl.pallas_call(
        _s5_post_body,
        grid=(R // tm,),
        in_specs=[pl.BlockSpec((tm, half), lambda i: (i, col0 // half)),
                  pl.BlockSpec((tm, half), lambda i: (i, col0 // half + 1)), row, row, vec,
                  pl.BlockSpec((W, W), lambda i: (0, 0)), vec, pl.BlockSpec(memory_space=pl.ANY)],
        out_specs=pl.BlockSpec((tm, W), lambda i: (i, 1)),
        out_shape=jax.ShapeDtypeStruct(mix.shape, mix.dtype),
        input_output_aliases={7: 0},
        compiler_params=_cparams(("parallel",)),
        name="s5_post",
    )(p, p, yf, yb, d_skip, glu_w, glu_b, mix)


def _cd_conv_body(x_ref, hp_ref, hn_ref, w_ref, b_ref, o_ref, *, tm, geo, taps):
    first, last = _tile_edges(pl.program_id(0) * tm, tm, *geo)
    rows = lax.broadcasted_iota(jnp.int32, (SUBLANES, 1), 0)
    cur = x_ref[...]
    half = taps // 2
    acc = b_ref[...] + w_ref[half:half + 1, :] * cur
    for j in range(taps):
        if j != half:
            acc = acc + w_ref[j:j + 1, :] * _shifted_rows(cur, j - half, hp_ref[...], hn_ref[...], first, last, rows)
    o_ref[...] = acc * jax.nn.sigmoid(acc)


def _cd_conv(p, col0, width, conv_w, conv_b, *, geo):
    R = p.shape[0]
    tm = SEQ_BLOCK
    tw = 1024
    taps = conv_w.shape[0]
    nb8 = R // SUBLANES
    c0 = col0 // tw
    w8 = jnp.pad(conv_w, ((0, SUBLANES - taps), (0, 0)))
    return pl.pallas_call(
        functools.partial(_cd_conv_body, tm=tm, geo=geo, taps=taps),
        grid=(R // tm, width // tw),
        in_specs=[pl.BlockSpec((tm, tw), lambda i, j: (i, c0 + j)),
                  pl.BlockSpec((SUBLANES, tw), lambda i, j: (jnp.maximum(i * (tm // SUBLANES) - 1, 0), c0 + j)),
                  pl.BlockSpec((SUBLANES, tw), lambda i, j: (jnp.minimum((i + 1) * (tm // SUBLANES), nb8 - 1), c0 + j)),
                  pl.BlockSpec((SUBLANES, tw), lambda i, j: (0, j)),
                  pl.BlockSpec((1, tw), lambda i, j: (0, j))],
        out_specs=pl.BlockSpec((tm, tw), lambda i, j: (i, j)),
        out_shape=jax.ShapeDtypeStruct((R, width), F32),
        compiler_params=_cparams(("parallel", "parallel")),
        name="cd_conv",
    )(p, p, p, w8, conv_b.reshape(1, width))


def _rope_tile(x, cos, sin):
    lane = lax.broadcasted_iota(jnp.int32, x.shape, 1)
    q = ATT_HD // 4
    partner = jnp.where(lane % (2 * q) < q, pltpu.roll(x, LANES - q, axis=1), pltpu.roll(x, q, axis=1))
    return x * cos + partner * sin


def _cd_rope_dt_body(q_ref, k_ref, dtr_ref, cos_ref, sin_ref, bias_ref, a_ref, q_out, k_out, dt_out, dta_out,
                     *, tm, n_ctx_rows):
    in_ctx = pl.program_id(0) * tm < n_ctx_rows
    cos = jnp.where(in_ctx, 1.0, cos_ref[...])
    sin = jnp.where(in_ctx, 0.0, sin_ref[...])
    for j in range(ATT_Q_W // LANES):
        q_out[:, j * LANES:(j + 1) * LANES] = _rope_tile(q_ref[:, j * LANES:(j + 1) * LANES], cos, sin)
    for j in range(ATT_KV_W // LANES):
        k_out[:, j * LANES:(j + 1) * LANES] = _rope_tile(k_ref[:, j * LANES:(j + 1) * LANES], cos, sin)
    dt = _softplus(dtr_ref[...] + bias_ref[...])
    dt_out[...] = dt
    dta_out[...] = dt * a_ref[...]


def _cd_rope_dt(p, q_col, k_col, dt_col, cos, sin, dt_bias, a_neg, *, n_ctx_rows, seq_len):
    R = p.shape[0]
    tm = SEQ_BLOCK
    tab = pl.BlockSpec((tm, LANES), lambda i: (jnp.maximum(i * tm - n_ctx_rows, 0) % seq_len // tm, 0))
    vec = pl.BlockSpec((1, LANES), lambda i: (0, 0))
    return pl.pallas_call(
        functools.partial(_cd_rope_dt_body, tm=tm, n_ctx_rows=n_ctx_rows),
        grid=(R // tm,),
        in_specs=[pl.BlockSpec((tm, ATT_Q_W), lambda i: (i, q_col // ATT_Q_W)),
                  pl.BlockSpec((tm, ATT_KV_W), lambda i: (i, k_col // ATT_KV_W)),
                  pl.BlockSpec((tm, LANES), lambda i: (i, dt_col // LANES)),
                  tab, tab, vec, vec],
        out_specs=[pl.BlockSpec((tm, ATT_Q_W), lambda i: (i, 0)), pl.BlockSpec((tm, ATT_KV_W), lambda i: (i, 0)),
                   pl.BlockSpec((tm, LANES), lambda i: (i, 0)), pl.BlockSpec((tm, LANES), lambda i: (i, 0))],
        out_shape=[jax.ShapeDtypeStruct((R, ATT_Q_W), F32), jax.ShapeDtypeStruct((R, ATT_KV_W), F32),
                   jax.ShapeDtypeStruct((R, LANES), F32), jax.ShapeDtypeStruct((R, LANES), F32)],
        compiler_params=_cparams(("parallel",)),
        name="cd_rope_dt",
    )(p, p, p, cos, sin, dt_bias, a_neg)


def _ssd_post_body(yf_ref, yb_ref, x_ref, z_ref, d_ref, g_ref, o_ref):
    z = z_ref[...]
    y = (yf_ref[...] + yb_ref[...] + d_ref[...] * x_ref[...]) * (z * jax.nn.sigmoid(z))
    gw = SSD_W // SSD_GROUPS
    for g in range(SSD_GROUPS):
        yg = y[:, g * gw:(g + 1) * gw]
        ms = jnp.mean(yg * yg, axis=-1, keepdims=True)
        o_ref[:, g * gw:(g + 1) * gw] = (yg * lax.rsqrt(ms + EPS) * g_ref[:, g * gw:(g + 1) * gw]).astype(o_ref.dtype)


def _ssd_post(yf, yb, xbc, p, d_skip, norm_g, out_cols, *, n_ctx_rows):
    R = yf.shape[0]
    tm = SEQ_BLOCK
    r0 = n_ctx_rows // tm
    row = pl.BlockSpec((tm, SSD_W), lambda i: (r0 + i, 0))
    vec = pl.BlockSpec((1, SSD_W), lambda i: (0, 0))
    return pl.pallas_call(
        _ssd_post_body,
        grid=((R - n_ctx_rows) // tm,),
        in_specs=[row, row, row, row, vec, vec],
        out_specs=pl.BlockSpec((tm, SSD_W), lambda i: (i, 0)),
        out_shape=jax.ShapeDtypeStruct((R - n_ctx_rows, out_cols), BF16),
        compiler_params=_cparams(("parallel",)),
        name="ssd_post",
    )(yf, yb, xbc, p, d_skip, norm_g)


def _segment_pos(R, RC, C, L):
    row = jnp.arange(R, dtype=jnp.int32)
    in_ctx = row < RC
    pos = jnp.where(in_ctx, row % C, (row - RC) % L)
    seg = jnp.where(in_ctx, C, L)
    return pos, seg


def _shift_rows(p, k, pos, seg):
    ok = (pos + k >= 0) & (pos + k < seg)
    return jnp.where(ok[:, None], jnp.roll(p, -k, axis=0), 0.0)


def _head_sum(t):
    R = t.shape[0]
    s = jnp.sum(t.reshape(R, RWKV_HEADS, RWKV_HD), axis=-1, keepdims=True)
    return jnp.broadcast_to(s, (R, RWKV_HEADS, RWKV_HD)).reshape(R, RWKV_W)


def _rope_tables(L):
    nf = ATT_HD // 4
    inv_freq = ROPE_BASE ** (-jnp.arange(nf, dtype=F32) / nf)
    t = jnp.arange(L, dtype=jnp.int32)
    row_id = (t // GRID_W).astype(F32)
    col_id = (t % GRID_W).astype(F32)
    ar = row_id[:, None] * inv_freq
    ac = col_id[:, None] * inv_freq
    cos = jnp.concatenate([jnp.cos(ar), jnp.cos(ar), jnp.cos(ac), jnp.cos(ac)], axis=1)
    sin = jnp.concatenate([-jnp.sin(ar), jnp.sin(ar), -jnp.sin(ac), jnp.sin(ac)], axis=1)
    return cos, sin


def _rope(t, cos, sin, n_heads):
    R = t.shape[0]
    q = ATT_HD // 4
    tr = t.reshape(R, n_heads, 2, 2, q)
    swapped = jnp.stack([tr[..., 1, :], tr[..., 0, :]], axis=-2).reshape(R, n_heads, ATT_HD)
    out = t.reshape(R, n_heads, ATT_HD) * cos[:, None, :] + swapped * sin[:, None, :]
    return out.reshape(R, n_heads * ATT_HD)


def kernel(x, c, ctx, c_ctx, ada_w, ada_b, norm1_g, norm2_g, mlp_w1, mlp_w2, final_g, ab_w_in, ab_w_out, rwkv_mu, rwkv_w0, rwkv_w_up, rwkv_a0, rwkv_a_up, rwkv_g_up, rwkv_k_k, rwkv_k_a, rwkv_r_k, rwkv_ln_g, rwkv_ln_b, s5_A_re, s5_A_im, s5_log_dt, s5_B_re, s5_B_im, s5_C_re, s5_C_im, s5_D, s5_glu_w, s5_glu_b, cd_w_in, cd_w_out, ssd_conv_w, ssd_conv_b, ssd_A_log, ssd_dt_bias, ssd_D, ssd_norm_g, attn_sink):
    B, L, D = x.shape
    C = ctx.shape[1]
    RC, RX = B * C, B * L
    R = RC + RX
    assert C % SEQ_BLOCK == 0 and L % SEQ_BLOCK == 0
    geo = dict(n_batch=B, n_ctx_blk=C // SEQ_BLOCK, n_x_blk=L // SEQ_BLOCK)
    geo_ssd = dict(n_batch=B, n_ctx_blk=C // SSD_CHUNK, n_x_blk=L // SSD_CHUNK)
    group_all = lambda r0: jnp.where(r0 < RC, 0, 1 + (r0 - RC) // L)
    group_x = lambda r0: 1 + r0 // L

    cond = jax.nn.silu(jnp.concatenate([c_ctx[None, :], c], axis=0))
    cond = jnp.pad(cond, ((0, SUBLANES - (B + 1) % SUBLANES), (0, 0))) if (B + 1) % SUBLANES else cond
    n_groups = cond.shape[0]

    def modulation(i):
        mod = _mm(cond, ada_w, b_layer=i, name="ada") + ada_b[i]
        return [m.reshape(n_groups, 1, D) for m in jnp.split(mod, 6, axis=-1)]

    sh1, sc1, g1, sh2, sc2, g2 = modulation(0)
    sh1n, sc1n, g1n, sh2n, sc2n, g2n = modulation(1)
    h, xs = _norm_mod_first(ctx.reshape(RC, D), x.reshape(RX, D), norm1_g[0], sh1, sc1, group_all)
    pad_cols = (-RWKV_COLS) % LANES
    w_in = ab_w_in[0]
    w_in = jnp.concatenate([w_in[:, :RWKV_COLS], jnp.zeros((D, pad_cols), F32), w_in[:, RWKV_COLS:]], axis=1)
    s5_col0 = RWKV_COLS + pad_cols
    p = _mm(h, w_in.astype(BF16), name="ab_in")

    W = RWKV_W
    lora_rows = 2 * LANES
    zw = jnp.zeros((DECAY_LORA, W), F32)
    lora_w = jnp.concatenate([
        jnp.concatenate([rwkv_w_up[0, 0], rwkv_w_up[0, 1], zw, zw], axis=1),
        jnp.concatenate([zw, zw, rwkv_a_up[0, 0], rwkv_a_up[0, 1]], axis=1),
        jnp.zeros((lora_rows - DECAY_LORA - AAA_LORA, 4 * W), F32)], axis=0).astype(BF16)
    gate_w = jnp.pad(rwkv_g_up[0], ((DECAY_LORA + AAA_LORA, pad_cols), (0, 0))).astype(BF16)
    mu = jnp.pad(rwkv_mu[0], ((0, 0), (0, pad_cols)))
    vec = lambda t: t.reshape(1, W)
    r, v, kk, g_gate, bonus, lw, kd, be = _rwkv_pre(
        p, mu, lora_w, gate_w, rwkv_w0[0], rwkv_a0[0], vec(rwkv_k_k[0]), vec(rwkv_k_a[0]), vec(rwkv_r_k[0]),
        geo=(RC, C, L))
    y_dirs = [_rwkv_scan(r, v, kk, lw, kd, be, reverse=(d == 1), **geo) for d in range(2)]
    mix = _rwkv_post(y_dirs[0], y_dirs[1], bonus, g_gate, vec(rwkv_ln_g[0]), vec(rwkv_ln_b[0]), W + S5_W)

    ys_dirs = []
    for d in range(2):
        bw, cw, tab = _s5_weights(s5_A_re[0, d], s5_A_im[0, d], s5_log_dt[0, d], s5_B_re[0, d], s5_B_im[0, d],
                                  s5_C_re[0, d], s5_C_im[0, d])
        ys_dirs.append(_s5_scan(p, s5_col0 // LANES, bw, cw, tab, reverse=(d == 1), **geo))
    mix = _s5_post(p, s5_col0, ys_dirs[0], ys_dirs[1], vec(s5_D[0]), s5_glu_w[0].astype(BF16),
                   vec(s5_glu_b[0]), mix)
    xs, h2 = _mm_res_norm(mix, ab_w_out[0].astype(BF16), xs, g1, norm2_g[0], sh2, sc2, group_all, name="ab_out")
    m1 = _mm(h2, mlp_w1[0].astype(BF16), out_dtype=BF16, act="relu2", name="mlp_up")
    xs = _mm(m1, mlp_w2[0].astype(BF16), res=xs, gate=g2, gate_idx=group_all, name="mlp_down")
    h = _norm_mod(xs, norm1_g[1], sh1n, sc1n, group_all)

    sh1, sc1, g1, sh2, sc2, g2 = sh1n, sc1n, g1n, sh2n, sc2n, g2n
    cd_in = cd_w_in[0]
    o_dt = SSD_W + XBC_W
    o_q = o_dt + 2 * SSD_HEADS
    n_qkv = ATT_Q_W + 2 * ATT_KV_W
    cd_in = jnp.concatenate([cd_in[:, :o_dt], cd_in[:, o_q:o_q + n_qkv], cd_in[:, o_dt:o_q]], axis=1)
    cd_in = jnp.pad(cd_in, ((0, 0), (0, (-cd_in.shape[1]) % 512)))
    q_col = o_dt
    k_col = q_col + ATT_Q_W
    v_col = k_col + ATT_KV_W
    dt_col = v_col + ATT_KV_W
    p = _mm(h, cd_in.astype(BF16), name="cd_in")

    xbc = _cd_conv(p, SSD_W, XBC_W, ssd_conv_w[0], ssd_conv_b[0], geo=(RC, C, L))
    cos, sin = _rope_tables(L)
    lane_pad = lambda t: jnp.pad(t.reshape(1, 2 * SSD_HEADS), ((0, 0), (0, LANES - 2 * SSD_HEADS)))
    q_rot, k_rot, dt, dta = _cd_rope_dt(
        p, q_col, k_col, dt_col, jnp.tile(cos, (1, LANES // ATT_HD)), jnp.tile(sin, (1, LANES // ATT_HD)),
        lane_pad(ssd_dt_bias[0]), lane_pad(-jnp.exp(ssd_A_log[0])), n_ctx_rows=RC, seq_len=L)
    y_dirs = [_ssd_scan(xbc, dt, dta, reverse=(d == 1), **geo_ssd) for d in range(2)]
    mix = _ssd_post(y_dirs[0], y_dirs[1], xbc, p, jnp.repeat(ssd_D[0], SSD_HD).reshape(1, SSD_W),
                    ssd_norm_g[0].reshape(1, SSD_W), SSD_W + ATT_Q_W, n_ctx_rows=RC)
    mix = _window_attn(q_rot, k_rot, p, v_col, attn_sink[0].reshape(1, ATT_HEADS), mix, SSD_W,
                       n_batch=B, n_ctx=C, seq_len=L)
    xo, h2 = _mm_res_norm(mix, cd_w_out[0].astype(BF16), xs, g1, norm2_g[1], sh2, sc2, group_x, res_row0=RC,
                          name="cd_out")
    m1 = _mm(h2, mlp_w1[1].astype(BF16), out_dtype=BF16, act="relu2", name="mlp_up")
    xo = _mm(m1, mlp_w2[1].astype(BF16), res=xo, gate=g2, gate_idx=group_x, name="mlp_down")
    zero = jnp.zeros((1, 1, D), F32)
    out = _norm_mod(xo, final_g, zero, zero, lambda r0: 0, out_dtype=F32)
    return out.reshape(B, L, D)
```

```python
import functools
import math

import jax
import jax.numpy as jnp
from jax import lax
from jax.experimental import pallas as pl
from jax.experimental.pallas import tpu as pltpu

F32 = jnp.float32
BF16 = jnp.bfloat16

EPS = 1e-6
GN_EPS = 64e-5
GRID_W = 64
ROPE_BASE = 10000.0

RWKV_HEADS = 16
RWKV_HD = 64
RWKV_W = RWKV_HEADS * RWKV_HD
DECAY_LORA = 96
AAA_LORA = 96
GATE_LORA = 256
RWKV_COLS = 3 * RWKV_W + DECAY_LORA + AAA_LORA + GATE_LORA
S5_W = 1024
S5_P = 16
S5_G = S5_W // S5_P
S5_N = 64
SSD_HEADS = 32
SSD_HD = 64
SSD_W = SSD_HEADS * SSD_HD
SSD_GROUPS = 4
SSD_N = 128
SSD_CHUNK = 128
XBC_W = SSD_W + 2 * SSD_GROUPS * SSD_N
ATT_HEADS = 16
KV_HEADS = 4
ATT_HD = 64
ATT_Q_W = ATT_HEADS * ATT_HD
ATT_KV_W = KV_HEADS * ATT_HD
WINDOW = 128
ATT_BLOCK = 128

LANES = 128
SUBLANES = 8
SEQ_BLOCK = 256
RWKV_CHUNK = 64
S5_GB = 8
VMEM_LIMIT = 48 * 1024 * 1024
VMEM_INTERNAL = 6 * 1024 * 1024


def _cparams(sem):
    return pltpu.CompilerParams(dimension_semantics=sem, vmem_limit_bytes=VMEM_LIMIT)


def _pick(n, cands):
    for c in cands:
        if n % c == 0:
            return c
    return n


def _dot(a, b):
    return jnp.dot(a.astype(BF16), b.astype(BF16), preferred_element_type=F32)


def _split3(x):
    hi = x.astype(BF16)
    r1 = x - hi.astype(F32)
    mid = r1.astype(BF16)
    lo = (r1 - mid.astype(F32)).astype(BF16)
    return hi, mid, lo


def _dot_ones_x(tri, x):
    hi, mid, lo = _split3(x)
    f = lambda t: jnp.dot(tri, t, preferred_element_type=F32)
    return f(hi) + f(mid) + f(lo)


def _dot_x_ones_nt(x, tri):
    hi, mid, lo = _split3(x)
    f = lambda t: lax.dot_general(t, tri, (((1,), (1,)), ((), ())), preferred_element_type=F32)
    return f(hi) + f(mid) + f(lo)


def _mm_body(*refs, nk, act, residual):
    if residual:
        a_ref, b_ref, res_ref, gate_ref, o_ref, acc_ref = refs
    else:
        a_ref, b_ref, o_ref, acc_ref = refs
    k = pl.program_id(2)
    part = jnp.dot(a_ref[...].astype(BF16), b_ref[...].astype(BF16), preferred_element_type=F32)

    def finish(acc):
        if act == "relu2":
            acc = jnp.square(jnp.maximum(acc, 0.0))
        if residual:
            acc = res_ref[...] + gate_ref[0] * acc
        o_ref[...] = acc.astype(o_ref.dtype)

    if nk == 1:
        finish(part)
    else:
        @pl.when(k == 0)
        def _():
            acc_ref[...] = part

        @pl.when(k > 0)
        def _():
            acc_ref[...] += part

        @pl.when(k == nk - 1)
        def _():
            finish(acc_ref[...])


def _mm(a, b, *, b_layer=0, out_dtype=F32, act=None, res=None, res_row0=0, gate=None, gate_idx=None, name="mm"):
    M, K = a.shape
    N = b.shape[-1]
    tn = _pick(N, (512, 384, 256, 128))
    residual = res is not None
    a_bytes, b_bytes, o_bytes = a.dtype.itemsize, b.dtype.itemsize, jnp.dtype(out_dtype).itemsize

    def vmem_bytes(tm, tk):
        blocks = tm * tk * a_bytes + tk * tn * b_bytes + tm * tn * o_bytes + (tm * tn * 4 if residual else 0)
        return 2 * blocks + (tm * tn * 4 if tk < K else 0)

    tm, tk = next(((tm, tk) for tk in (K, 4096, 2048, 1024, 512, 256, 128) if K % tk == 0
                   for tm in (1024, 512, 256, 128, 64, 32, 16, 8) if M % tm == 0
                   and vmem_bytes(tm, tk) <= VMEM_LIMIT - VMEM_INTERNAL), (8, 128))
    nk = K // tk
    if b.ndim == 3:
        b_spec = pl.BlockSpec((None, tk, tn), lambda i, j, k: (b_layer, k, j))
    else:
        b_spec = pl.BlockSpec((tk, tn), lambda i, j, k: (k, j))
    in_specs = [pl.BlockSpec((tm, tk), lambda i, j, k: (i, k)), b_spec]
    args = [a, b]
    if residual:
        if res_row0 % tm:
            res, res_row0 = res[res_row0:res_row0 + M], 0
        r0 = res_row0 // tm
        in_specs += [pl.BlockSpec((tm, tn), lambda i, j, k: (r0 + i, j)),
                     pl.BlockSpec((1, 1, tn), lambda i, j, k: (gate_idx(i * tm), 0, j))]
        args += [res, gate]
    return pl.pallas_call(
        functools.partial(_mm_body, nk=nk, act=act, residual=residual),
        grid=(M // tm, N // tn, nk),
        in_specs=in_specs,
        out_specs=pl.BlockSpec((tm, tn), lambda i, j, k: (i, j)),
        out_shape=jax.ShapeDtypeStruct((M, N), out_dtype),
        scratch_shapes=[pltpu.VMEM((tm, tn), F32)],
        compiler_params=_cparams(("parallel", "parallel", "arbitrary")),
        name=name,
    )(*args)


def _mm_res_norm_body(a_ref, b_ref, res_ref, gate_ref, g_ref, sh_ref, sc_ref, *out_refs, nk, keep_x):
    if nk > 1:
        *out_refs, acc_ref = out_refs
    h_ref = out_refs[-1]
    k = pl.program_id(1)
    part = jnp.dot(a_ref[...], b_ref[...], preferred_element_type=F32)

    def finish(acc):
        x = res_ref[...] + gate_ref[0] * acc
        if keep_x:
            out_refs[0][...] = x
        ms = jnp.mean(x * x, axis=-1, keepdims=True)
        xn = x * lax.rsqrt(ms + EPS) * g_ref[...]
        h_ref[...] = (xn * (1.0 + sc_ref[0]) + sh_ref[0]).astype(h_ref.dtype)

    if nk == 1:
        finish(part)
        return

    @pl.when(k == 0)
    def _():
        acc_ref[...] = part

    @pl.when(k > 0)
    def _():
        acc_ref[...] += part

    @pl.when(k == nk - 1)
    def _():
        finish(acc_ref[...])


def _mm_res_norm(a, b, res, gate, norm_g, shift, scale, group_idx, *, res_row0=0, keep_x=True, h_dtype=BF16,
                 name="mm_res_norm"):
    M, K = a.shape
    N = b.shape[1]
    h_bytes = jnp.dtype(h_dtype).itemsize

    def vmem_bytes(tm, tk):
        blocks = tm * tk * 2 + tk * N * 2 + tm * N * 4 + (tm * N * 4 if keep_x else 0) + tm * N * h_bytes
        return 2 * blocks + (tm * N * 4 if tk < K else 0)

    tm, tk = next(((tm, tk) for tk in (K, 1024, 512, 256, 128) if K % tk == 0
                   for tm in (512, 256, 128, 64, 32, 16, 8) if M % tm == 0
                   and vmem_bytes(tm, tk) <= VMEM_LIMIT - VMEM_INTERNAL), (8, 128))
    nk = K // tk
    if res_row0 % tm:
        res, res_row0 = res[res_row0:res_row0 + M], 0
    r0 = res_row0 // tm
    grp = lambda i, k: (group_idx(i * tm), 0, 0)
    row = pl.BlockSpec((tm, N), lambda i, k: (i, 0))
    out_specs = ([row] if keep_x else []) + [row]
    out_shape = ([jax.ShapeDtypeStruct((M, N), F32)] if keep_x else []) + [jax.ShapeDtypeStruct((M, N), h_dtype)]
    out = pl.pallas_call(
        functools.partial(_mm_res_norm_body, nk=nk, keep_x=keep_x),
        grid=(M // tm, nk),
        in_specs=[pl.BlockSpec((tm, tk), lambda i, k: (i, k)),
                  pl.BlockSpec((tk, N), lambda i, k: (k, 0)),
                  pl.BlockSpec((tm, N), lambda i, k: (r0 + i, 0)),
                  pl.BlockSpec((1, 1, N), grp),
                  pl.BlockSpec((1, N), lambda i, k: (0, 0)),
                  pl.BlockSpec((1, 1, N), grp), pl.BlockSpec((1, 1, N), grp)],
        out_specs=out_specs,
        out_shape=out_shape,
        scratch_shapes=[pltpu.VMEM((tm, N), F32)] if nk > 1 else [],
        compiler_params=_cparams(("parallel", "arbitrary")),
        name=name,
    )(a, b, res, gate, norm_g.reshape(1, N), shift, scale)
    return out if keep_x else out[0]


def _norm_body(x_ref, g_ref, sh_ref, sc_ref, o_ref):
    x = x_ref[...]
    ms = jnp.mean(x * x, axis=-1, keepdims=True)
    xn = x * lax.rsqrt(ms + EPS) * g_ref[...]
    o_ref[...] = (xn * (1.0 + sc_ref[0]) + sh_ref[0]).astype(o_ref.dtype)


def _norm_mod(x, g, shift, scale, group_idx, out_dtype=BF16):
    M, D = x.shape
    tm = _pick(M, (256, 128, 64, 32, 16, 8))
    return pl.pallas_call(
        _norm_body,
        grid=(M // tm,),
        in_specs=[pl.BlockSpec((tm, D), lambda i: (i, 0)),
                  pl.BlockSpec((1, D), lambda i: (0, 0)),
                  pl.BlockSpec((1, 1, D), lambda i: (group_idx(i * tm), 0, 0)),
                  pl.BlockSpec((1, 1, D), lambda i: (group_idx(i * tm), 0, 0))],
        out_specs=pl.BlockSpec((tm, D), lambda i: (i, 0)),
        out_shape=jax.ShapeDtypeStruct((M, D), out_dtype),
        compiler_params=_cparams(("parallel",)),
        name="norm_mod",
    )(x, g.reshape(1, D), shift, scale)


def _norm_first_body(c_ref, x_ref, g_ref, sh_ref, sc_ref, h_ref, xs_ref, *, n_ctx_tiles):
    x = jnp.where(pl.program_id(0) < n_ctx_tiles, c_ref[...], x_ref[...])
    xs_ref[...] = x
    ms = jnp.mean(x * x, axis=-1, keepdims=True)
    xn = x * lax.rsqrt(ms + EPS) * g_ref[...]
    h_ref[...] = (xn * (1.0 + sc_ref[0]) + sh_ref[0]).astype(h_ref.dtype)


def _norm_mod_first(ctx_rows, x_rows, g, shift, scale, group_idx):
    RC, D = ctx_rows.shape
    RX = x_rows.shape[0]
    tm = SEQ_BLOCK
    nc = RC // tm
    grp = lambda i: (group_idx(i * tm), 0, 0)
    row = pl.BlockSpec((tm, D), lambda i: (i, 0))
    return pl.pallas_call(
        functools.partial(_norm_first_body, n_ctx_tiles=nc),
        grid=((RC + RX) // tm,),
        in_specs=[pl.BlockSpec((tm, D), lambda i: (jnp.minimum(i, nc - 1), 0)),
                  pl.BlockSpec((tm, D), lambda i: (jnp.maximum(i - nc, 0), 0)),
                  pl.BlockSpec((1, D), lambda i: (0, 0)),
                  pl.BlockSpec((1, 1, D), grp), pl.BlockSpec((1, 1, D), grp)],
        out_specs=[row, row],
        out_shape=[jax.ShapeDtypeStruct((RC + RX, D), BF16), jax.ShapeDtypeStruct((RC + RX, D), F32)],
        compiler_params=_cparams(("parallel",)),
        name="norm_first",
    )(ctx_rows, x_rows, g.reshape(1, D), shift, scale)


def _seq_block(b, s, *, n_ctx_blk, n_x_blk, n_batch, reverse):
    if reverse:
        c = b * n_ctx_blk + (n_ctx_blk - 1 - s)
        x = n_batch * n_ctx_blk + b * n_x_blk + (n_x_blk - 1 - (s - n_ctx_blk))
    else:
        c = b * n_ctx_blk + s
        x = n_batch * n_ctx_blk + b * n_x_blk + (s - n_ctx_blk)
    return jnp.where(s < n_ctx_blk, c, x)


def _split2(x):
    hi = x.astype(BF16)
    lo = (x - hi.astype(F32)).astype(BF16)
    return hi, lo


def _rwkv_chunk(lw, r, v, kk, kd, be, st, tri, strict, incl, same_head, head0, eye):
    C = RWKV_CHUNK

    def stack(z):
        return jnp.concatenate([jnp.where(head0, z, 0.0), jnp.where(head0, 0.0, z)], axis=0)

    def unstack(z):
        return z[0:C] + z[C:2 * C]

    bdot = lambda a, b: jnp.dot(a, b, preferred_element_type=F32)
    bdot_nt = lambda a, b: lax.dot_general(a, b, (((1,), (1,)), ((), ())), preferred_element_type=F32)
    bdot_tn = lambda a, b: lax.dot_general(a, b, (((0,), (0,)), ((), ())), preferred_element_type=F32)
    cat = lambda *zs: jnp.concatenate(zs, axis=0)
    P = range(len(lw))
    cum = [_dot_ones_x(tri, lw[p]) for p in P]
    total = [jnp.sum(lw[p], axis=0, keepdims=True) for p in P]
    a_t = [kk[p] * jnp.exp(cum[p] - lw[p]) for p in P]
    r_t = [r[p] * jnp.exp(cum[p]) for p in P]
    a_sp = [_split2(a_t[p]) for p in P]
    r_hi = [r_t[p].astype(BF16) for p in P]
    sa_sp = [_split2(stack(a_t[p])) for p in P]
    sr_hi = [stack(r_t[p]).astype(BF16) for p in P]
    pinv = [jnp.exp(-cum[p]) for p in P]
    bk_sp = [_split2(cat(stack(be[p] * pinv[p]), stack(kd[p] * pinv[p]))) for p in P]
    g1 = [bdot_nt(cat(sa_sp[p][0], sa_sp[p][1], sr_hi[p]), bk_sp[p][0]) for p in P]
    g2 = [bdot_nt(sa_sp[p][0], bk_sp[p][1]) for p in P]
    n_all = [g1[p][0:2 * C] + g1[p][2 * C:4 * C] + g2[p] for p in P]
    x_neg = [jnp.where(strict, -n_all[p][:, 0:2 * C], 0.0) for p in P]
    n_ak = [jnp.where(strict, n_all[p][:, 2 * C:4 * C], 0.0) for p in P]
    m_r = [jnp.concatenate([jnp.where(incl, g1[p][4 * C:6 * C, 0:2 * C], 0.0),
                            jnp.where(incl, g1[p][4 * C:6 * C, 2 * C:4 * C], 0.0)], axis=1).astype(BF16)
           for p in P]
    lcat = lambda *zs: jnp.concatenate(zs, axis=1)
    W2 = 2 * C
    t_inv = [eye + x_neg[p] for p in P]
    xb = [x_neg[p].astype(BF16) for p in P]
    xb = [bdot(xb[p], xb[p]).astype(BF16) for p in P]
    n_pow = int(math.log2(C))
    for it in range(1, n_pow):
        if it + 1 < n_pow:
            both = [bdot(xb[p], lcat(xb[p], t_inv[p].astype(BF16))) for p in P]
            xb = [both[p][:, 0:W2].astype(BF16) for p in P]
            t_inv = [t_inv[p] + both[p][:, W2:2 * W2] for p in P]
        else:
            t_inv = [t_inv[p] + bdot(xb[p], t_inv[p].astype(BF16)) for p in P]
    t_inv = [t_inv[p].astype(BF16) for p in P]
    st_sp = [_split2(st[p]) for p in P]
    as1 = [bdot_nt(a_sp[p][0], cat(st_sp[p][0], st_sp[p][1])) for p in P]
    as2 = [bdot_nt(cat(a_sp[p][1], r_hi[p]), st_sp[p][0]) for p in P]
    v_sp = [_split2(stack(v[p])) for p in P]
    nk_sp = [_split2(n_ak[p]) for p in P]
    nv1 = [bdot(lcat(nk_sp[p][0], nk_sp[p][1]), cat(v_sp[p][0], v_sp[p][0])) for p in P]
    nv2 = [bdot(nk_sp[p][0], v_sp[p][1]) for p in P]
    rhs = [-(stack(as1[p][:, 0:W2] + as1[p][:, W2:2 * W2] + as2[p][0:C]) + nv1[p] + nv2[p]) for p in P]
    u0 = [bdot(t_inv[p], rhs[p].astype(BF16)) for p in P]
    u0_sp = [_split2(u0[p]) for p in P]
    x_sp = [_split2(x_neg[p]) for p in P]
    xu1 = [bdot(lcat(x_sp[p][0], x_sp[p][1]), cat(u0_sp[p][0], u0_sp[p][0])) for p in P]
    xu2 = [bdot(x_sp[p][0], u0_sp[p][1]) for p in P]
    res = [(rhs[p] - u0[p] + xu1[p] + xu2[p]).astype(BF16) for p in P]
    u = [u0[p] + bdot(t_inv[p], res[p]) for p in P]
    y = [stack(as2[p][C:2 * C]) + bdot(m_r[p], cat(u[p].astype(BF16), v_sp[p][0])) for p in P]
    uv_sp = [_split2(cat(unstack(u[p]), v[p])) for p in P]
    pend = [jnp.exp(total[p] - cum[p]) for p in P]
    e_sp = [_split2(cat(be[p] * pend[p], kd[p] * pend[p])) for p in P]
    upd = [bdot_tn(cat(uv_sp[p][0], uv_sp[p][1], uv_sp[p][0]), cat(e_sp[p][0], e_sp[p][0], e_sp[p][1]))
           for p in P]
    st_new = [st[p] * jnp.exp(total[p]) + jnp.where(same_head, upd[p], 0.0) for p in P]
    return [unstack(y[p]) for p in P], st_new


def _rwkv_body(r_ref, v_ref, kk_ref, lw_ref, kd_ref, be_ref, y_ref, st_ref, *, reverse, npair):
    C = RWKV_CHUNK
    nchunk = SEQ_BLOCK // C
    s = pl.program_id(2)

    @pl.when(s == 0)
    def _():
        st_ref[...] = jnp.zeros_like(st_ref)

    sgn = -1 if reverse else 1
    row = lax.broadcasted_iota(jnp.int32, (2 * C, 2 * C), 0)
    col = lax.broadcasted_iota(jnp.int32, (2 * C, 2 * C), 1)
    diff = (row - col) * sgn
    strict = diff > 0
    incl = diff >= 0
    same_head = (row // C) == (col // C)
    eye = jnp.where(row == col, 1.0, 0.0)
    r64 = lax.broadcasted_iota(jnp.int32, (C, C), 0)
    c64 = lax.broadcasted_iota(jnp.int32, (C, C), 1)
    tri = jnp.where((r64 - c64) * sgn >= 0, 1.0, 0.0).astype(BF16)
    head0 = lax.broadcasted_iota(jnp.int32, (C, 2 * C), 1) < C

    def chunk(ci, carry):
        ce = (nchunk - 1 - ci) if reverse else ci
        sl = pl.ds(pl.multiple_of(ce * C, C), C)
        lanes = [slice(p * LANES, (p + 1) * LANES) for p in range(npair)]
        get = lambda ref: [ref[sl, ln] for ln in lanes]
        ys, sts = _rwkv_chunk(get(lw_ref), get(r_ref), get(v_ref), get(kk_ref), get(kd_ref), get(be_ref),
                              [st_ref[p] for p in range(npair)], tri, strict, incl, same_head, head0, eye)
        for p in range(npair):
            y_ref[sl, lanes[p]] = ys[p]
            st_ref[p] = sts[p]
        return carry

    lax.fori_loop(0, nchunk, chunk, 0)


def _rwkv_scan(r, v, kk, lw, kd, be, *, n_batch, n_ctx_blk, n_x_blk, reverse, npair=8):
    R, W = r.shape
    ngrp = W // (LANES * npair)
    ns = n_ctx_blk + n_x_blk
    d = 1 if reverse else 0
    blk = functools.partial(_seq_block, n_ctx_blk=n_ctx_blk, n_x_blk=n_x_blk, n_batch=n_batch, reverse=reverse)
    spec = pl.BlockSpec((SEQ_BLOCK, LANES * npair), lambda b, p, s: (blk(b, s), p))
    dspec = pl.BlockSpec((None, SEQ_BLOCK, LANES * npair), lambda b, p, s: (d, blk(b, s), p))
    return pl.pallas_call(
        functools.partial(_rwkv_body, reverse=reverse, npair=npair),
        grid=(n_batch, ngrp, ns),
        in_specs=[spec] * 3 + [dspec] * 3,
        out_specs=spec,
        out_shape=jax.ShapeDtypeStruct((R, W), F32),
        scratch_shapes=[pltpu.VMEM((npair, LANES, LANES), F32)],
        compiler_params=_cparams(("parallel", "parallel", "arbitrary")),
        name="rwkv_scan_bwd" if reverse else "rwkv_scan_fwd",
    )(r, v, kk, lw, kd, be)


S5_SEG = SEQ_BLOCK // SUBLANES
S5_TAB_SLABS = 2 * S5_SEG + 2 + 8
S5_NGB = 4


def _s5_body(u_ref, bw_ref, cw_ref, tab_ref, y_ref, hb_ref, carry_ref, perm_ref, *, reverse):
    W = S5_GB * S5_N
    T = SEQ_BLOCK
    G = range(S5_NGB)
    s = pl.program_id(2)

    @pl.when(s == 0)
    def _():
        carry_ref[...] = jnp.zeros_like(carry_ref)
        prow = lax.broadcasted_iota(jnp.int32, (T, T), 0)
        pcol = lax.broadcasted_iota(jnp.int32, (T, T), 1)
        t_of_row = S5_SEG * (prow % SUBLANES) + prow // SUBLANES
        if reverse:
            t_of_row = T - 1 - t_of_row
        perm_ref[...] = jnp.where(pcol == t_of_row, 1.0, 0.0).astype(BF16)

    perm = perm_ref[...]
    u_p = jnp.dot(perm, u_ref[...].astype(BF16), preferred_element_type=F32).astype(BF16)
    for g in G:
        hb_ref[g] = jnp.dot(u_p[:, g * LANES:(g + 1) * LANES], bw_ref[g], preferred_element_type=F32)

    slab = lambda j: pl.ds(j * SUBLANES, SUBLANES)
    tab = lambda g, n: tab_ref[g, slab(n), :]
    o_seg = 2 * S5_SEG
    o_pow = o_seg + 2
    re, im = slice(0, W), slice(W, 2 * W)
    a = [(tab(g, o_pow), tab(g, o_pow + 1)) for g in G]
    h = [(hb_ref[g, slab(0), re], hb_ref[g, slab(0), im]) for g in G]
    for j in range(1, S5_SEG):
        h = [(hb_ref[g, slab(j), re] + a[g][0] * h[g][0] - a[g][1] * h[g][1],
              hb_ref[g, slab(j), im] + a[g][0] * h[g][1] + a[g][1] * h[g][0]) for g in G]
        for g in G:
            hb_ref[g, slab(j), re] = h[g][0]
            hb_ref[g, slab(j), im] = h[g][1]
    rows8 = lax.broadcasted_iota(jnp.int32, (SUBLANES, W), 0)
    ent = []
    for g in G:
        gr, gi = h[g]
        for n, k in enumerate((1, 2, 4)):
            pr, pi = tab(g, o_pow + 2 + 2 * n), tab(g, o_pow + 3 + 2 * n)
            sr = jnp.where(rows8 >= k, pltpu.roll(gr, k, axis=0), 0.0)
            si = jnp.where(rows8 >= k, pltpu.roll(gi, k, axis=0), 0.0)
            gr, gi = gr + pr * sr - pi * si, gi + pr * si + pi * sr
        cr, ci = carry_ref[g, :, re], carry_ref[g, :, im]
        qr, qi = tab(g, o_seg), tab(g, o_seg + 1)
        gr, gi = gr + qr * cr - qi * ci, gi + qr * ci + qi * cr
        ent.append((jnp.where(rows8 >= 1, pltpu.roll(gr, 1, axis=0), cr),
                    jnp.where(rows8 >= 1, pltpu.roll(gi, 1, axis=0), ci)))
        carry_ref[g, :, re] = jnp.broadcast_to(gr[SUBLANES - 1:SUBLANES, :], (SUBLANES, W))
        carry_ref[g, :, im] = jnp.broadcast_to(gi[SUBLANES - 1:SUBLANES, :], (SUBLANES, W))
    for g in G:
        in_r, in_i = ent[g]
        for j in range(S5_SEG):
            pr, pi = tab(g, j), tab(g, S5_SEG + j)
            hb_ref[g, slab(j), re] = hb_ref[g, slab(j), re] + pr * in_r - pi * in_i
            hb_ref[g, slab(j), im] = hb_ref[g, slab(j), im] + pr * in_i + pi * in_r
    y_p = jnp.concatenate([jnp.dot(hb_ref[g].astype(BF16), cw_ref[g], preferred_element_type=F32) for g in G],
                          axis=1)
    y_ref[...] = _dot_x_ones_tn(perm, y_p)


def _dot_x_ones_tn(ones, x):
    hi, mid, lo = _split3(x)
    f = lambda t: lax.dot_general(ones, t, (((0,), (0,)), ((), ())), preferred_element_type=F32)
    return f(hi) + f(mid) + f(lo)


def _s5_scan(p, col0_blk, bw, cw, tab, *, n_batch, n_ctx_blk, n_x_blk, reverse):
    R = p.shape[0]
    n = S5_NGB
    ngb = S5_G // S5_GB // n
    ns = n_ctx_blk + n_x_blk
    blk = functools.partial(_seq_block, n_ctx_blk=n_ctx_blk, n_x_blk=n_x_blk, n_batch=n_batch, reverse=reverse)
    W2 = 2 * S5_GB * S5_N
    assert col0_blk % n == 0
    return pl.pallas_call(
        functools.partial(_s5_body, reverse=reverse),
        grid=(n_batch, ngb, ns),
        in_specs=[pl.BlockSpec((SEQ_BLOCK, n * LANES), lambda b, g, s: (blk(b, s), col0_blk // n + g)),
                  pl.BlockSpec((n, LANES, W2), lambda b, g, s: (g, 0, 0)),
                  pl.BlockSpec((n, W2, LANES), lambda b, g, s: (g, 0, 0)),
                  pl.BlockSpec((n, S5_TAB_SLABS * SUBLANES, W2 // 2), lambda b, g, s: (g, 0, 0))],
        out_specs=pl.BlockSpec((SEQ_BLOCK, n * LANES), lambda b, g, s: (blk(b, s), g)),
        out_shape=jax.ShapeDtypeStruct((R, S5_W), F32),
        scratch_shapes=[pltpu.VMEM((n, SEQ_BLOCK, W2), F32), pltpu.VMEM((n, SUBLANES, W2), F32),
                        pltpu.VMEM((SEQ_BLOCK, SEQ_BLOCK), BF16)],
        compiler_params=_cparams(("parallel", "parallel", "arbitrary")),
        name="s5_scan_bwd" if reverse else "s5_scan_fwd",
    )(p, bw, cw, tab)


def _s5_weights(A_re, A_im, log_dt, B_re, B_im, C_re, C_im):
    lam_re = jnp.minimum(A_re, -1e-4)
    dt = jnp.exp(log_dt)[:, None]
    mag = jnp.exp(lam_re * dt)
    ab_re, ab_im = mag * jnp.cos(A_im * dt), mag * jnp.sin(A_im * dt)
    den = lam_re * lam_re + A_im * A_im
    f_re = ((ab_re - 1) * lam_re + ab_im * A_im) / den
    f_im = (ab_im * lam_re - (ab_re - 1) * A_im) / den
    bb_re = f_re[..., None] * B_re - f_im[..., None] * B_im
    bb_im = f_re[..., None] * B_im + f_im[..., None] * B_re
    ngb = S5_G // S5_GB
    eye = jnp.eye(S5_GB, dtype=F32)

    def in_block(bb):
        t = bb.reshape(ngb, S5_GB, S5_N, S5_P)
        return jnp.einsum("agnp,gh->agphn", t, eye).reshape(ngb, S5_GB * S5_P, S5_GB * S5_N)

    def out_block(cc):
        t = cc.reshape(ngb, S5_GB, S5_P, S5_N)
        return jnp.einsum("agpn,gh->agnhp", t, eye).reshape(ngb, S5_GB * S5_N, S5_GB * S5_P)

    bw = jnp.concatenate([in_block(bb_re), in_block(bb_im)], axis=2).astype(BF16)
    cw = jnp.concatenate([out_block(C_re), -out_block(C_im)], axis=1).astype(BF16)

    def power(k):
        m = jnp.exp(k * lam_re * dt)
        return m * jnp.cos(k * A_im * dt), m * jnp.sin(k * A_im * dt)

    rep = lambda t: jnp.broadcast_to(t[None], (SUBLANES,) + t.shape)
    slab_pow = [power(float(j + 1)) for j in range(S5_SEG)]
    seg_pow = [power(float(S5_SEG * (i + 1))) for i in range(SUBLANES)]
    slabs = ([rep(c[0]) for c in slab_pow] + [rep(c[1]) for c in slab_pow]
             + [jnp.stack([c[0] for c in seg_pow]), jnp.stack([c[1] for c in seg_pow])])
    for k in (1.0, float(S5_SEG), 2.0 * S5_SEG, 4.0 * S5_SEG):
        slabs += [rep(t) for t in power(k)]
    tab = jnp.concatenate(slabs, axis=0)
    tab = tab.reshape(S5_TAB_SLABS * SUBLANES, ngb, S5_GB * S5_N).transpose(1, 0, 2)
    return bw, cw, tab


def _ssd_body(x_ref, b_ref, c_ref, dt_ref, dta_ref, sel_ref, y_ref, st_ref, *, reverse):
    Q = SSD_CHUNK
    s = pl.program_id(1)

    @pl.when(s == 0)
    def _():
        st_ref[...] = jnp.zeros_like(st_ref)

    row = lax.broadcasted_iota(jnp.int32, (Q, Q), 0)
    col = lax.broadcasted_iota(jnp.int32, (Q, Q), 1)
    mask = (col >= row) if reverse else (col <= row)
    tri = jnp.where(mask, 1.0, 0.0).astype(BF16)
    head0 = lax.broadcasted_iota(jnp.int32, (Q, LANES), 1) < SSD_HD
    head0_rows = lax.broadcasted_iota(jnp.int32, (LANES, SSD_N), 0) < SSD_HD

    h_lo = SSD_HEADS if reverse else 0
    dt = dt_ref[:, h_lo:h_lo + SSD_HEADS]
    dta = dta_ref[:, h_lo:h_lo + SSD_HEADS]
    dtt = dt_ref[...].T[h_lo:h_lo + SSD_HEADS, :]
    dtat = dta_ref[...].T[h_lo:h_lo + SSD_HEADS, :]
    acum = _dot_ones_x(tri, dta)
    acum_t = _dot_x_ones_nt(dtat, tri)
    total = jnp.sum(dta, axis=0, keepdims=True)
    total_t = jnp.sum(dtat, axis=1, keepdims=True)
    e_tot_t = jnp.exp(total_t)
    dt_wide = jnp.dot(dt.astype(BF16), sel_ref[...], preferred_element_type=F32)
    heads_per_group = SSD_HEADS // SSD_GROUPS
    G = range(SSD_GROUPS)
    pairs = [(g, g * heads_per_group + 2 * pr) for g in G for pr in range(heads_per_group // 2)]
    lanes2 = lambda h0: slice(h0 * SSD_HD, (h0 + 2) * SSD_HD)
    bg = [b_ref[:, g * SSD_N:(g + 1) * SSD_N].astype(BF16) for g in G]
    cg = [c_ref[:, g * SSD_N:(g + 1) * SSD_N].astype(BF16) for g in G]
    cb = [lax.dot_general(cg[g], bg[g], (((1,), (1,)), ((), ())), preferred_element_type=F32) for g in G]
    xp = [x_ref[:, lanes2(h0)] for _, h0 in pairs]
    xb = [t.astype(BF16) for t in xp]
    colb = [jnp.broadcast_to(acum[:, h:h + 1], (Q, LANES)) for h in range(SSD_HEADS)]

    def weights(g, h):
        seg = colb[h] - acum_t[h:h + 1, :]
        return (cb[g] * jnp.exp(jnp.where(mask, seg, -jnp.inf)) * dtt[h:h + 1, :]).astype(BF16)

    yd = [(jnp.dot(weights(g, h0), xb[i], preferred_element_type=F32),
           jnp.dot(weights(g, h0 + 1), xb[i], preferred_element_type=F32)) for i, (g, h0) in enumerate(pairs)]
    st = [st_ref[lanes2(h0), :] for _, h0 in pairs]
    yo = [lax.dot_general(cg[g], st[i].astype(BF16), (((1,), (1,)), ((), ())), preferred_element_type=F32)
          for i, (g, _) in enumerate(pairs)]
    for i, (g, h0) in enumerate(pairs):
        scale = jnp.exp(jnp.where(head0, colb[h0], colb[h0 + 1]))
        y_ref[:, lanes2(h0)] = jnp.where(head0, yd[i][0], yd[i][1]) + yo[i] * scale
    xt = [(xp[i] * dt_wide[:, lanes2(h0)]
           * jnp.exp(jnp.where(head0, total[:, h0:h0 + 1] - colb[h0], total[:, h0 + 1:h0 + 2] - colb[h0 + 1]))
           ).astype(BF16) for i, (_, h0) in enumerate(pairs)]
    upd = [lax.dot_general(xt[i], bg[g], (((0,), (0,)), ((), ())), preferred_element_type=F32)
           for i, (g, _) in enumerate(pairs)]
    for i, (_, h0) in enumerate(pairs):
        dec = jnp.where(head0_rows, e_tot_t[h0:h0 + 1, :], e_tot_t[h0 + 1:h0 + 2, :])
        st_ref[lanes2(h0), :] = st[i] * dec + upd[i]


def _ssd_scan(xbc, dt, dta, *, n_batch, n_ctx_blk, n_x_blk, reverse):
    R = xbc.shape[0]
    Q = SSD_CHUNK
    blk = functools.partial(_seq_block, n_ctx_blk=n_ctx_blk, n_x_blk=n_x_blk, n_batch=n_batch, reverse=reverse)
    ns = n_ctx_blk + n_x_blk
    gw = SSD_GROUPS * SSD_N
    row_spec = lambda w, c: pl.BlockSpec((Q, w), lambda b, s: (blk(b, s), c))
    head_of_lane = jnp.arange(SSD_W, dtype=jnp.int32) // SSD_HD
    sel = (jnp.arange(SSD_HEADS, dtype=jnp.int32)[:, None] == head_of_lane[None, :]).astype(BF16)
    return pl.pallas_call(
        functools.partial(_ssd_body, reverse=reverse),
        grid=(n_batch, ns),
        in_specs=[row_spec(SSD_W, 0), row_spec(gw, SSD_W // gw), row_spec(gw, SSD_W // gw + 1),
                  row_spec(LANES, 0), row_spec(LANES, 0), pl.BlockSpec((SSD_HEADS, SSD_W), lambda b, s: (0, 0))],
        out_specs=row_spec(SSD_W, 0),
        out_shape=jax.ShapeDtypeStruct((R, SSD_W), F32),
        scratch_shapes=[pltpu.VMEM((SSD_W, SSD_N), F32)],
        compiler_params=_cparams(("parallel", "arbitrary")),
        name="ssd_scan_bwd" if reverse else "ssd_scan_fwd",
    )(xbc, xbc, xbc, dt, dta, sel)


def _attn_body(q_ref, kp_ref, kc_ref, kn_ref, vp_ref, vc_ref, vn_ref, kx_ref, vx_ref, sink_ref, mix_ref, o_ref,
               band_ref, *, seq_len):
    del mix_ref
    n = pl.program_id(1)
    blk = ATT_BLOCK
    n_ctx = kx_ref.shape[0]
    rep = ATT_HEADS // KV_HEADS
    nk = n_ctx + 3 * blk

    @pl.when(n == 0)
    def _():
        r_q = lax.broadcasted_iota(jnp.int32, (rep * blk, nk), 0) % blk
        j = lax.broadcasted_iota(jnp.int32, (rep * blk, nk), 1)
        ok = (j < n_ctx) | (jnp.abs(r_q - (j - n_ctx - blk)) <= WINDOW)
        band_ref[...] = jnp.where(ok, 0.0, -jnp.inf)

    k_all = jnp.concatenate([kx_ref[...], kp_ref[...], kc_ref[...], kn_ref[...]], axis=0).astype(BF16)
    v_all = jnp.concatenate([vx_ref[...], vp_ref[...], vc_ref[...], vn_ref[...]], axis=0).astype(BF16)
    jb = lax.broadcasted_iota(jnp.int32, (1, nk), 1) - n_ctx
    k_pos = (n - 1) * blk + jb
    outside = (jb >= 0) & ((k_pos < 0) | (k_pos >= seq_len))
    mask = jnp.where(outside, -jnp.inf, band_ref[...])
    scale = ATT_HD ** -0.5
    head_of_row = lax.broadcasted_iota(jnp.int32, (rep * blk, 1), 0) // blk
    G = range(KV_HEADS)
    hd = lambda t, h: t[:, h * ATT_HD:(h + 1) * ATT_HD]
    q = (q_ref[...] * scale).astype(BF16)
    qs = [jnp.concatenate([hd(q, g * rep + i) for i in range(rep)], axis=0) for g in G]
    sinks = []
    for g in G:
        sk = jnp.zeros((rep * blk, 1), F32)
        for i in range(rep):
            sk = jnp.where(head_of_row == i, sink_ref[0:1, g * rep + i:g * rep + i + 1], sk)
        sinks.append(sk)
    sc = [lax.dot_general(qs[g], hd(k_all, g), (((1,), (1,)), ((), ())), preferred_element_type=F32) + mask
          for g in G]
    m = [jnp.maximum(jnp.max(sc[g], axis=-1, keepdims=True), sinks[g]) for g in G]
    e = [jnp.exp(sc[g] - m[g]) for g in G]
    den = [jnp.sum(e[g], axis=-1, keepdims=True) + jnp.exp(sinks[g] - m[g]) for g in G]
    pv = [jnp.dot(e[g].astype(BF16), hd(v_all, g), preferred_element_type=F32) / den[g] for g in G]
    outs = [pv[g][i * blk:(i + 1) * blk] for g in G for i in range(rep)]
    o_ref[...] = jnp.concatenate(outs, axis=1).astype(o_ref.dtype)


def _window_attn(q, k, p, v_col, sink, mix, mix_col, *, n_batch, n_ctx, seq_len):
    nb = seq_len // ATT_BLOCK
    x0 = n_batch * n_ctx // ATT_BLOCK
    vc = v_col // ATT_KV_W
    cur = lambda b, n: x0 + b * nb + n
    prev = lambda b, n: x0 + b * nb + jnp.maximum(n - 1, 0)
    nxt = lambda b, n: x0 + b * nb + jnp.minimum(n + 1, nb - 1)
    k_spec = lambda im: pl.BlockSpec((ATT_BLOCK, ATT_KV_W), lambda b, n: (im(b, n), 0))
    v_spec = lambda im: pl.BlockSpec((ATT_BLOCK, ATT_KV_W), lambda b, n: (im(b, n), vc))
    return pl.pallas_call(
        functools.partial(_attn_body, seq_len=seq_len),
        grid=(n_batch, nb),
        in_specs=[pl.BlockSpec((ATT_BLOCK, ATT_Q_W), lambda b, n: (cur(b, n), 0)),
                  k_spec(prev), k_spec(cur), k_spec(nxt), v_spec(prev), v_spec(cur), v_spec(nxt),
                  pl.BlockSpec((n_ctx, ATT_KV_W), lambda b, n: (b, 0)),
                  pl.BlockSpec((n_ctx, ATT_KV_W), lambda b, n: (b, vc)),
                  pl.BlockSpec((1, ATT_HEADS), lambda b, n: (0, 0)),
                  pl.BlockSpec(memory_space=pl.ANY)],
        out_specs=pl.BlockSpec((ATT_BLOCK, ATT_Q_W), lambda b, n: (b * nb + n, mix_col // ATT_Q_W)),
        out_shape=jax.ShapeDtypeStruct(mix.shape, mix.dtype),
        input_output_aliases={10: 0},
        scratch_shapes=[pltpu.VMEM((ATT_HEADS // KV_HEADS * ATT_BLOCK, n_ctx + 3 * ATT_BLOCK), F32)],
        compiler_params=_cparams(("parallel", "arbitrary")),
        name="window_attn",
    )(q, k, k, k, p, p, p, k, p, sink, mix)


def _tile_edges(row0, tm, n_ctx_rows, ctx_len, seq_len):
    in_ctx = row0 < n_ctx_rows
    pos0 = jnp.where(in_ctx, row0 % ctx_len, (row0 - n_ctx_rows) % seq_len)
    seg = jnp.where(in_ctx, ctx_len, seq_len)
    return pos0 == 0, pos0 + tm == seg


def _shifted_rows(cur, k, prev_blk, next_blk, first, last, rows):
    tm = cur.shape[0]
    if k < 0:
        edge = jnp.where(first, 0.0, pltpu.roll(prev_blk, -k, axis=0))
        body = pltpu.roll(cur, -k, axis=0)
        head = jnp.where(rows < -k, edge, body[0:SUBLANES])
        return jnp.concatenate([head, body[SUBLANES:]], axis=0)
    edge = jnp.where(last, 0.0, pltpu.roll(next_blk, SUBLANES - k, axis=0))
    body = pltpu.roll(cur, tm - k, axis=0)
    tail = jnp.where(rows >= SUBLANES - k, edge, body[tm - SUBLANES:])
    return jnp.concatenate([body[:tm - SUBLANES], tail], axis=0)


def _head_sum_mxu(t, blk):
    n = t.shape[1] // LANES
    return jnp.concatenate([_dot_x_ones_nt(t[:, j * LANES:(j + 1) * LANES], blk) for j in range(n)], axis=1)


def _same_head_matrix():
    r = lax.broadcasted_iota(jnp.int32, (LANES, LANES), 0) // RWKV_HD
    c = lax.broadcasted_iota(jnp.int32, (LANES, LANES), 1) // RWKV_HD
    return jnp.where(r == c, 1.0, 0.0).astype(BF16)


def _softplus(x):
    return jnp.maximum(x, 0.0) + jnp.log(1.0 + jnp.exp(-jnp.abs(x)))


def _rwkv_pre_body(p_ref, hp_ref, hn_ref, mu_ref, lora_ref, gw_ref, w0_ref, a0_ref, kk_ref, ka_ref, rk_ref,
                   r_out, v_out, kk_out, g_out, bonus_out, lw_out, kd_out, be_out, *, tm, geo):
    W = RWKV_W
    first, last = _tile_edges(pl.program_id(0) * tm, tm, *geo)
    rows = lax.broadcasted_iota(jnp.int32, (SUBLANES, 1), 0)

    def tshift(c0, c1):
        cur = p_ref[:, c0:c1]
        prev = _shifted_rows(cur, -1, hp_ref[:, c0:c1], hn_ref[:, c0:c1], first, last, rows)
        nxt = _shifted_rows(cur, 1, hp_ref[:, c0:c1], hn_ref[:, c0:c1], first, last, rows)
        return cur + mu_ref[0:1, c0:c1] * (prev - cur) + mu_ref[1:2, c0:c1] * (nxt - cur)

    r = tshift(0, W)
    k = tshift(W, 2 * W)
    v = tshift(2 * W, 3 * W)
    xl = tshift(3 * W, 3 * W + 4 * LANES)
    lane = lax.broadcasted_iota(jnp.int32, (tm, 2 * LANES), 1)
    lin = jnp.where(lane < DECAY_LORA, jnp.tanh(xl[:, :2 * LANES]), xl[:, :2 * LANES])
    lora = _dot(lin, lora_ref[...])
    g_out[...] = _dot(jax.nn.sigmoid(xl), gw_ref[...])
    blk = _same_head_matrix()
    kk = k * kk_ref[...]
    kk = kk / jnp.maximum(jnp.sqrt(_head_sum_mxu(kk * kk, blk)), 1e-12)
    r_out[...] = r
    v_out[...] = v
    kk_out[...] = kk
    kd_sum = jnp.zeros_like(k)
    for d in range(2):
        w_log = -_softplus(-(w0_ref[d:d + 1, :] + lora[:, d * W:(d + 1) * W])) - 0.5
        a = jax.nn.sigmoid(a0_ref[d:d + 1, :] + lora[:, (2 + d) * W:(3 + d) * W])
        kd = k * (1.0 + (a - 1.0) * ka_ref[...])
        lw_out[d] = -jnp.exp(w_log)
        kd_out[d] = kd
        be_out[d] = a * kk
        kd_sum = kd_sum + kd
    bonus_out[...] = _head_sum_mxu(r * kd_sum * rk_ref[...], blk) * v


def _rwkv_pre(p, mu, lora_w, gate_w, w0, a0, k_k, k_a, r_k, *, geo):
    R = p.shape[0]
    W = RWKV_W
    tm = SEQ_BLOCK
    pw = 3 * W + 4 * LANES
    nb8 = R // SUBLANES
    row = lambda i: (i, 0)
    vec = lambda n: pl.BlockSpec((n, W), lambda i: (0, 0))
    out2 = pl.BlockSpec((2, tm, W), lambda i: (0, i, 0))
    f = jax.ShapeDtypeStruct((R, W), F32)
    f2 = jax.ShapeDtypeStruct((2, R, W), F32)
    return pl.pallas_call(
        functools.partial(_rwkv_pre_body, tm=tm, geo=geo),
        grid=(R // tm,),
        in_specs=[pl.BlockSpec((tm, pw), row),
                  pl.BlockSpec((SUBLANES, pw), lambda i: (jnp.maximum(i * (tm // SUBLANES) - 1, 0), 0)),
                  pl.BlockSpec((SUBLANES, pw), lambda i: (jnp.minimum((i + 1) * (tm // SUBLANES), nb8 - 1), 0)),
                  pl.BlockSpec((2, pw), lambda i: (0, 0)),
                  pl.BlockSpec(lora_w.shape, lambda i: (0, 0)),
                  pl.BlockSpec(gate_w.shape, lambda i: (0, 0)),
                  vec(2), vec(2), vec(1), vec(1), vec(1)],
        out_specs=[pl.BlockSpec((tm, W), row)] * 5 + [out2] * 3,
        out_shape=[f] * 5 + [f2] * 3,
        compiler_params=_cparams(("parallel",)),
        name="rwkv_pre",
    )(p, p, p, mu, lora_w, gate_w, w0, a0, k_k, k_a, r_k)


def _rwkv_post_body(yf_ref, yb_ref, bonus_ref, g_ref, lng_ref, lnb_ref, o_ref):
    blk = _same_head_matrix()
    y = yf_ref[...] + yb_ref[...]
    mean = _head_sum_mxu(y, blk) * (1.0 / RWKV_HD)
    d = y - mean
    var = _head_sum_mxu(d * d, blk) * (1.0 / RWKV_HD)
    yn = d * lax.rsqrt(var + GN_EPS)
    o_ref[...] = ((yn * lng_ref[...] + lnb_ref[...] + bonus_ref[...]) * g_ref[...]).astype(o_ref.dtype)


def _rwkv_post(yf, yb, bonus, g, ln_g, ln_b, out_cols):
    R, W = yf.shape
    tm = SEQ_BLOCK
    row = pl.BlockSpec((tm, W), lambda i: (i, 0))
    vec = pl.BlockSpec((1, W), lambda i: (0, 0))
    return pl.pallas_call(
        _rwkv_post_body,
        grid=(R // tm,),
        in_specs=[row, row, row, row, vec, vec],
        out_specs=row,
        out_shape=jax.ShapeDtypeStruct((R, out_cols), BF16),
        compiler_params=_cparams(("parallel",)),
        name="rwkv_post",
    )(yf, yb, bonus, g, ln_g, ln_b)


def _gelu_tanh(x):
    return 0.5 * x * (1.0 + jnp.tanh(math.sqrt(2.0 / math.pi) * (x + 0.044715 * (x * x * x))))


def _s5_post_body(u0_ref, u1_ref, yf_ref, yb_ref, d_ref, w_ref, b_ref, mix_ref, o_ref):
    del mix_ref
    u = jnp.concatenate([u0_ref[...], u1_ref[...]], axis=1)
    y = _gelu_tanh(d_ref[...] * u + yf_ref[...] + yb_ref[...])
    o_ref[...] = (y * jax.nn.sigmoid(_dot(y, w_ref[...]) + b_ref[...])).astype(o_ref.dtype)


def _s5_post(p, col0, yf, yb, d_skip, glu_w, glu_b, mix):
    R, W = yf.shape
    tm = SEQ_BLOCK
    half = W // 2
    row = pl.BlockSpec((tm, W), lambda i: (i, 0))
    vec = pl.BlockSpec((1, W), lambda i: (0, 0))
    return pl.pallas_call(
        _s5_post_body,
        grid=(R // tm,),
        in_specs=[pl.BlockSpec((tm, half), lambda i: (i, col0 // half)),
                  pl.BlockSpec((tm, half), lambda i: (i, col0 // half + 1)), row, row, vec,
                  pl.BlockSpec((W, W), lambda i: (0, 0)), vec, pl.BlockSpec(memory_space=pl.ANY)],
        out_specs=pl.BlockSpec((tm, W), lambda i: (i, 1)),
        out_shape=jax.ShapeDtypeStruct(mix.shape, mix.dtype),
        input_output_aliases={7: 0},
        compiler_params=_cparams(("parallel",)),
        name="s5_post",
    )(p, p, yf, yb, d_skip, glu_w, glu_b, mix)


def _cd_conv_body(x_ref, hp_ref, hn_ref, w_ref, b_ref, o_ref, *, tm, geo, taps):
    first, last = _tile_edges(pl.program_id(0) * tm, tm, *geo)
    rows = lax.broadcasted_iota(jnp.int32, (SUBLANES, 1), 0)
    cur = x_ref[...]
    half = taps // 2
    acc = b_ref[...] + w_ref[half:half + 1, :] * cur
    for j in range(taps):
        if j != half:
            acc = acc + w_ref[j:j + 1, :] * _shifted_rows(cur, j - half, hp_ref[...], hn_ref[...], first, last, rows)
    o_ref[...] = acc * jax.nn.sigmoid(acc)


def _cd_conv(p, col0, width, conv_w, conv_b, *, geo):
    R = p.shape[0]
    tm = SEQ_BLOCK
    tw = 1024
    taps = conv_w.shape[0]
    nb8 = R // SUBLANES
    c0 = col0 // tw
    w8 = jnp.pad(conv_w, ((0, SUBLANES - taps), (0, 0)))
    return pl.pallas_call(
        functools.partial(_cd_conv_body, tm=tm, geo=geo, taps=taps),
        grid=(R // tm, width // tw),
        in_specs=[pl.BlockSpec((tm, tw), lambda i, j: (i, c0 + j)),
                  pl.BlockSpec((SUBLANES, tw), lambda i, j: (jnp.maximum(i * (tm // SUBLANES) - 1, 0), c0 + j)),
                  pl.BlockSpec((SUBLANES, tw), lambda i, j: (jnp.minimum((i + 1) * (tm // SUBLANES), nb8 - 1), c0 + j)),
                  pl.BlockSpec((SUBLANES, tw), lambda i, j: (0, j)),
                  pl.BlockSpec((1, tw), lambda i, j: (0, j))],
        out_specs=pl.BlockSpec((tm, tw), lambda i, j: (i, j)),
        out_shape=jax.ShapeDtypeStruct((R, width), F32),
        compiler_params=_cparams(("parallel", "parallel")),
        name="cd_conv",
    )(p, p, p, w8, conv_b.reshape(1, width))


def _rope_tile(x, cos, sin):
    lane = lax.broadcasted_iota(jnp.int32, x.shape, 1)
    q = ATT_HD // 4
    partner = jnp.where(lane % (2 * q) < q, pltpu.roll(x, LANES - q, axis=1), pltpu.roll(x, q, axis=1))
    return x * cos + partner * sin


def _cd_rope_dt_body(q_ref, k_ref, dtr_ref, cos_ref, sin_ref, bias_ref, a_ref, q_out, k_out, dt_out, dta_out,
                     *, tm, n_ctx_rows):
    in_ctx = pl.program_id(0) * tm < n_ctx_rows
    cos = jnp.where(in_ctx, 1.0, cos_ref[...])
    sin = jnp.where(in_ctx, 0.0, sin_ref[...])
    for j in range(ATT_Q_W // LANES):
        q_out[:, j * LANES:(j + 1) * LANES] = _rope_tile(q_ref[:, j * LANES:(j + 1) * LANES], cos, sin)
    for j in range(ATT_KV_W // LANES):
        k_out[:, j * LANES:(j + 1) * LANES] = _rope_tile(k_ref[:, j * LANES:(j + 1) * LANES], cos, sin)
    dt = _softplus(dtr_ref[...] + bias_ref[...])
    dt_out[...] = dt
    dta_out[...] = dt * a_ref[...]


def _cd_rope_dt(p, q_col, k_col, dt_col, cos, sin, dt_bias, a_neg, *, n_ctx_rows, seq_len):
    R = p.shape[0]
    tm = SEQ_BLOCK
    tab = pl.BlockSpec((tm, LANES), lambda i: (jnp.maximum(i * tm - n_ctx_rows, 0) % seq_len // tm, 0))
    vec = pl.BlockSpec((1, LANES), lambda i: (0, 0))
    return pl.pallas_call(
        functools.partial(_cd_rope_dt_body, tm=tm, n_ctx_rows=n_ctx_rows),
        grid=(R // tm,),
        in_specs=[pl.BlockSpec((tm, ATT_Q_W), lambda i: (i, q_col // ATT_Q_W)),
                  pl.BlockSpec((tm, ATT_KV_W), lambda i: (i, k_col // ATT_KV_W)),
                  pl.BlockSpec((tm, LANES), lambda i: (i, dt_col // LANES)),
                  tab, tab, vec, vec],
        out_specs=[pl.BlockSpec((tm, ATT_Q_W), lambda i: (i, 0)), pl.BlockSpec((tm, ATT_KV_W), lambda i: (i, 0)),
                   pl.BlockSpec((tm, LANES), lambda i: (i, 0)), pl.BlockSpec((tm, LANES), lambda i: (i, 0))],
        out_shape=[jax.ShapeDtypeStruct((R, ATT_Q_W), F32), jax.ShapeDtypeStruct((R, ATT_KV_W), F32),
                   jax.ShapeDtypeStruct((R, LANES), F32), jax.ShapeDtypeStruct((R, LANES), F32)],
        compiler_params=_cparams(("parallel",)),
        name="cd_rope_dt",
    )(p, p, p, cos, sin, dt_bias, a_neg)


def _ssd_post_body(yf_ref, yb_ref, x_ref, z_ref, d_ref, g_ref, o_ref):
    z = z_ref[...]
    y = (yf_ref[...] + yb_ref[...] + d_ref[...] * x_ref[...]) * (z * jax.nn.sigmoid(z))
    gw = SSD_W // SSD_GROUPS
    for g in range(SSD_GROUPS):
        yg = y[:, g * gw:(g + 1) * gw]
        ms = jnp.mean(yg * yg, axis=-1, keepdims=True)
        o_ref[:, g * gw:(g + 1) * gw] = (yg * lax.rsqrt(ms + EPS) * g_ref[:, g * gw:(g + 1) * gw]).astype(o_ref.dtype)


def _ssd_post(yf, yb, xbc, p, d_skip, norm_g, out_cols, *, n_ctx_rows):
    R = yf.shape[0]
    tm = SEQ_BLOCK
    r0 = n_ctx_rows // tm
    row = pl.BlockSpec((tm, SSD_W), lambda i: (r0 + i, 0))
    vec = pl.BlockSpec((1, SSD_W), lambda i: (0, 0))
    return pl.pallas_call(
        _ssd_post_body,
        grid=((R - n_ctx_rows) // tm,),
        in_specs=[row, row, row, row, vec, vec],
        out_specs=pl.BlockSpec((tm, SSD_W), lambda i: (i, 0)),
        out_shape=jax.ShapeDtypeStruct((R - n_ctx_rows, out_cols), BF16),
        compiler_params=_cparams(("parallel",)),
        name="ssd_post",
    )(yf, yb, xbc, p, d_skip, norm_g)


def _rope_tables(L):
    nf = ATT_HD // 4
    inv_freq = ROPE_BASE ** (-jnp.arange(nf, dtype=F32) / nf)
    t = jnp.arange(L, dtype=jnp.int32)
    row_id = (t // GRID_W).astype(F32)
    col_id = (t % GRID_W).astype(F32)
    ar = row_id[:, None] * inv_freq
    ac = col_id[:, None] * inv_freq
    cos = jnp.concatenate([jnp.cos(ar), jnp.cos(ar), jnp.cos(ac), jnp.cos(ac)], axis=1)
    sin = jnp.concatenate([-jnp.sin(ar), jnp.sin(ar), -jnp.sin(ac), jnp.sin(ac)], axis=1)
    return cos, sin


def kernel(x, c, ctx, c_ctx, ada_w, ada_b, norm1_g, norm2_g, mlp_w1, mlp_w2, final_g, ab_w_in, ab_w_out, rwkv_mu, rwkv_w0, rwkv_w_up, rwkv_a0, rwkv_a_up, rwkv_g_up, rwkv_k_k, rwkv_k_a, rwkv_r_k, rwkv_ln_g, rwkv_ln_b, s5_A_re, s5_A_im, s5_log_dt, s5_B_re, s5_B_im, s5_C_re, s5_C_im, s5_D, s5_glu_w, s5_glu_b, cd_w_in, cd_w_out, ssd_conv_w, ssd_conv_b, ssd_A_log, ssd_dt_bias, ssd_D, ssd_norm_g, attn_sink):
    B, L, D = x.shape
    C = ctx.shape[1]
    RC, RX = B * C, B * L
    R = RC + RX
    assert C % SEQ_BLOCK == 0 and L % SEQ_BLOCK == 0
    geo = dict(n_batch=B, n_ctx_blk=C // SEQ_BLOCK, n_x_blk=L // SEQ_BLOCK)
    geo_ssd = dict(n_batch=B, n_ctx_blk=C // SSD_CHUNK, n_x_blk=L // SSD_CHUNK)
    group_all = lambda r0: jnp.where(r0 < RC, 0, 1 + (r0 - RC) // L)
    group_x = lambda r0: 1 + r0 // L

    cond = jax.nn.silu(jnp.concatenate([c_ctx[None, :], c], axis=0))
    cond = jnp.pad(cond, ((0, SUBLANES - (B + 1) % SUBLANES), (0, 0))) if (B + 1) % SUBLANES else cond
    n_groups = cond.shape[0]

    def modulation(i):
        mod = _mm(cond, ada_w, b_layer=i, name="ada") + ada_b[i]
        return [m.reshape(n_groups, 1, D) for m in jnp.split(mod, 6, axis=-1)]

    sh1, sc1, g1, sh2, sc2, g2 = modulation(0)
    sh1n, sc1n, g1n, sh2n, sc2n, g2n = modulation(1)
    h, xs = _norm_mod_first(ctx.reshape(RC, D), x.reshape(RX, D), norm1_g[0], sh1, sc1, group_all)
    pad_cols = (-RWKV_COLS) % LANES
    w_in = ab_w_in[0]
    w_in = jnp.concatenate([w_in[:, :RWKV_COLS], jnp.zeros((D, pad_cols), F32), w_in[:, RWKV_COLS:]], axis=1)
    s5_col0 = RWKV_COLS + pad_cols
    p = _mm(h, w_in.astype(BF16), name="ab_in")

    W = RWKV_W
    lora_rows = 2 * LANES
    zw = jnp.zeros((DECAY_LORA, W), F32)
    lora_w = jnp.concatenate([
        jnp.concatenate([rwkv_w_up[0, 0], rwkv_w_up[0, 1], zw, zw], axis=1),
        jnp.concatenate([zw, zw, rwkv_a_up[0, 0], rwkv_a_up[0, 1]], axis=1),
        jnp.zeros((lora_rows - DECAY_LORA - AAA_LORA, 4 * W), F32)], axis=0).astype(BF16)
    gate_w = jnp.pad(rwkv_g_up[0], ((DECAY_LORA + AAA_LORA, pad_cols), (0, 0))).astype(BF16)
    mu = jnp.pad(rwkv_mu[0], ((0, 0), (0, pad_cols)))
    vec = lambda t: t.reshape(1, W)
    r, v, kk, g_gate, bonus, lw, kd, be = _rwkv_pre(
        p, mu, lora_w, gate_w, rwkv_w0[0], rwkv_a0[0], vec(rwkv_k_k[0]), vec(rwkv_k_a[0]), vec(rwkv_r_k[0]),
        geo=(RC, C, L))
    y_dirs = [_rwkv_scan(r, v, kk, lw, kd, be, reverse=(d == 1), **geo) for d in range(2)]
    mix = _rwkv_post(y_dirs[0], y_dirs[1], bonus, g_gate, vec(rwkv_ln_g[0]), vec(rwkv_ln_b[0]), W + S5_W)

    ys_dirs = []
    for d in range(2):
        bw, cw, tab = _s5_weights(s5_A_re[0, d], s5_A_im[0, d], s5_log_dt[0, d], s5_B_re[0, d], s5_B_im[0, d],
                                  s5_C_re[0, d], s5_C_im[0, d])
        ys_dirs.append(_s5_scan(p, s5_col0 // LANES, bw, cw, tab, reverse=(d == 1), **geo))
    mix = _s5_post(p, s5_col0, ys_dirs[0], ys_dirs[1], vec(s5_D[0]), s5_glu_w[0].astype(BF16),
                   vec(s5_glu_b[0]), mix)
    xs, h2 = _mm_res_norm(mix, ab_w_out[0].astype(BF16), xs, g1, norm2_g[0], sh2, sc2, group_all, name="ab_out")
    m1 = _mm(h2, mlp_w1[0].astype(BF16), out_dtype=BF16, act="relu2", name="mlp_up")
    xs = _mm(m1, mlp_w2[0].astype(BF16), res=xs, gate=g2, gate_idx=group_all, name="mlp_down")
    h = _norm_mod(xs, norm1_g[1], sh1n, sc1n, group_all)

    sh1, sc1, g1, sh2, sc2, g2 = sh1n, sc1n, g1n, sh2n, sc2n, g2n
    cd_in = cd_w_in[0]
    o_dt = SSD_W + XBC_W
    o_q = o_dt + 2 * SSD_HEADS
    n_qkv = ATT_Q_W + 2 * ATT_KV_W
    cd_in = jnp.concatenate([cd_in[:, :o_dt], cd_in[:, o_q:o_q + n_qkv], cd_in[:, o_dt:o_q]], axis=1)
    cd_in = jnp.pad(cd_in, ((0, 0), (0, (-cd_in.shape[1]) % 512)))
    q_col = o_dt
    k_col = q_col + ATT_Q_W
    v_col = k_col + ATT_KV_W
    dt_col = v_col + ATT_KV_W
    p = _mm(h, cd_in.astype(BF16), name="cd_in")

    xbc = _cd_conv(p, SSD_W, XBC_W, ssd_conv_w[0], ssd_conv_b[0], geo=(RC, C, L))
    cos, sin = _rope_tables(L)
    lane_pad = lambda t: jnp.pad(t.reshape(1, 2 * SSD_HEADS), ((0, 0), (0, LANES - 2 * SSD_HEADS)))
    q_rot, k_rot, dt, dta = _cd_rope_dt(
        p, q_col, k_col, dt_col, jnp.tile(cos, (1, LANES // ATT_HD)), jnp.tile(sin, (1, LANES // ATT_HD)),
        lane_pad(ssd_dt_bias[0]), lane_pad(-jnp.exp(ssd_A_log[0])), n_ctx_rows=RC, seq_len=L)
    y_dirs = [_ssd_scan(xbc, dt, dta, reverse=(d == 1), **geo_ssd) for d in range(2)]
    mix = _ssd_post(y_dirs[0], y_dirs[1], xbc, p, jnp.repeat(ssd_D[0], SSD_HD).reshape(1, SSD_W),
                    ssd_norm_g[0].reshape(1, SSD_W), SSD_W + ATT_Q_W, n_ctx_rows=RC)
    mix = _window_attn(q_rot, k_rot, p, v_col, attn_sink[0].reshape(1, ATT_HEADS), mix, SSD_W,
                       n_batch=B, n_ctx=C, seq_len=L)
    xo, h2 = _mm_res_norm(mix, cd_w_out[0].astype(BF16), xs, g1, norm2_g[1], sh2, sc2, group_x, res_row0=RC,
                          name="cd_out")
    m1 = _mm(h2, mlp_w1[1].astype(BF16), out_dtype=BF16, act="relu2", name="mlp_up")
    xo = _mm(m1, mlp_w2[1].astype(BF16), res=xo, gate=g2, gate_idx=group_x, name="mlp_down")
    zero = jnp.zeros((1, 1, D), F32)
    out = _norm_mod(xo, final_g, zero, zero, lambda r0: 0, out_dtype=F32)
    return out.reshape(B, L, D)
```

```python
import functools
import math

import jax
import jax.numpy as jnp
from jax import lax
from jax.experimental import pallas as pl
from jax.experimental.pallas import tpu as pltpu

F32 = jnp.float32
BF16 = jnp.bfloat16

EPS = 1e-6
GN_EPS = 64e-5
GRID_W = 64
ROPE_BASE = 10000.0

RWKV_HEADS = 16
RWKV_HD = 64
RWKV_W = RWKV_HEADS * RWKV_HD
DECAY_LORA = 96
AAA_LORA = 96
GATE_LORA = 256
RWKV_COLS = 3 * RWKV_W + DECAY_LORA + AAA_LORA + GATE_LORA
S5_W = 1024
S5_P = 16
S5_G = S5_W // S5_P
S5_N = 64
SSD_HEADS = 32
SSD_HD = 64
SSD_W = SSD_HEADS * SSD_HD
SSD_GROUPS = 4
SSD_N = 128
SSD_CHUNK = 128
XBC_W = SSD_W + 2 * SSD_GROUPS * SSD_N
ATT_HEADS = 16
KV_HEADS = 4
ATT_HD = 64
ATT_Q_W = ATT_HEADS * ATT_HD
ATT_KV_W = KV_HEADS * ATT_HD
WINDOW = 128
ATT_BLOCK = 128

LANES = 128
SUBLANES = 8
SEQ_BLOCK = 256
RWKV_CHUNK = 64
S5_GB = 8
VMEM_LIMIT = 48 * 1024 * 1024
VMEM_INTERNAL = 6 * 1024 * 1024
CAST_BLOCK_BYTES = 8 * 1024 * 1024


def _cparams(sem):
    return pltpu.CompilerParams(dimension_semantics=sem, vmem_limit_bytes=VMEM_LIMIT)


def _pick(n, cands):
    for c in cands:
        if n % c == 0:
            return c
    return n


def _dot(a, b):
    return jnp.dot(a.astype(BF16), b.astype(BF16), preferred_element_type=F32)


def _split3(x):
    hi = x.astype(BF16)
    r1 = x - hi.astype(F32)
    mid = r1.astype(BF16)
    lo = (r1 - mid.astype(F32)).astype(BF16)
    return hi, mid, lo


def _dot_ones_x(tri, x):
    hi, mid, lo = _split3(x)
    f = lambda t: jnp.dot(tri, t, preferred_element_type=F32)
    return f(hi) + f(mid) + f(lo)


def _dot_x_ones_nt(x, tri):
    hi, mid, lo = _split3(x)
    f = lambda t: lax.dot_general(t, tri, (((1,), (1,)), ((), ())), preferred_element_type=F32)
    return f(hi) + f(mid) + f(lo)


def _mm_body(*refs, nk, act, residual):
    if residual:
        a_ref, b_ref, res_ref, gate_ref, o_ref, acc_ref = refs
    else:
        a_ref, b_ref, o_ref, acc_ref = refs
    k = pl.program_id(2)
    part = jnp.dot(a_ref[...].astype(BF16), b_ref[...].astype(BF16), preferred_element_type=F32)

    def finish(acc):
        if act == "relu2":
            acc = jnp.square(jnp.maximum(acc, 0.0))
        if residual:
            acc = res_ref[...] + gate_ref[0] * acc
        o_ref[...] = acc.astype(o_ref.dtype)

    if nk == 1:
        finish(part)
    else:
        @pl.when(k == 0)
        def _():
            acc_ref[...] = part

        @pl.when(k > 0)
        def _():
            acc_ref[...] += part

        @pl.when(k == nk - 1)
        def _():
            finish(acc_ref[...])


def _mm(a, b, *, b_layer=0, out_dtype=F32, act=None, res=None, res_row0=0, gate=None, gate_idx=None, name="mm"):
    M, K = a.shape
    N = b.shape[-1]
    residual = res is not None
    a_bytes, b_bytes, o_bytes = a.dtype.itemsize, b.dtype.itemsize, jnp.dtype(out_dtype).itemsize

    def vmem_bytes(tm, tn, tk):
        blocks = tm * tk * a_bytes + tk * tn * b_bytes + tm * tn * o_bytes + (tm * tn * 4 if residual else 0)
        return 2 * blocks + tm * tn * 4

    fits = [(tk == K, tm * tn, tm, tn, tk)
            for tk in (K, 4096, 2048, 1024, 512, 256, 128) if K % tk == 0
            for tn in (1536, 1024, 512, 384, 256, 128) if N % tn == 0
            for tm in (1024, 512, 256, 128, 64, 32, 16, 8) if M % tm == 0
            and vmem_bytes(tm, tn, tk) <= VMEM_LIMIT - VMEM_INTERNAL]
    _, _, tm, tn, tk = max(fits)
    nk = K // tk
    if b.ndim == 3:
        b_spec = pl.BlockSpec((None, tk, tn), lambda i, j, k: (b_layer, k, j))
    else:
        b_spec = pl.BlockSpec((tk, tn), lambda i, j, k: (k, j))
    in_specs = [pl.BlockSpec((tm, tk), lambda i, j, k: (i, k)), b_spec]
    args = [a, b]
    if residual:
        if res_row0 % tm:
            res, res_row0 = res[res_row0:res_row0 + M], 0
        r0 = res_row0 // tm
        in_specs += [pl.BlockSpec((tm, tn), lambda i, j, k: (r0 + i, j)),
                     pl.BlockSpec((1, 1, tn), lambda i, j, k: (gate_idx(i * tm), 0, j))]
        args += [res, gate]
    return pl.pallas_call(
        functools.partial(_mm_body, nk=nk, act=act, residual=residual),
        grid=(M // tm, N // tn, nk),
        in_specs=in_specs,
        out_specs=pl.BlockSpec((tm, tn), lambda i, j, k: (i, j)),
        out_shape=jax.ShapeDtypeStruct((M, N), out_dtype),
        scratch_shapes=[pltpu.VMEM((tm, tn), F32)],
        compiler_params=_cparams(("parallel", "parallel", "arbitrary")),
        name=name,
    )(*args)


def _cast_body(w_ref, o_ref):
    o_ref[...] = w_ref[...].astype(o_ref.dtype)


def _layer_bf16(w, layer):
    _, K, N = w.shape
    tk = next(t for t in (2048, 1024, 512, 256, 128, 64, 32, 16) if K % t == 0 and t * N * 4 <= CAST_BLOCK_BYTES)
    return pl.pallas_call(
        _cast_body,
        grid=(K // tk,),
        in_specs=[pl.BlockSpec((None, tk, N), lambda i: (layer, i, 0))],
        out_specs=pl.BlockSpec((tk, N), lambda i: (i, 0)),
        out_shape=jax.ShapeDtypeStruct((K, N), BF16),
        compiler_params=_cparams(("parallel",)),
        name="weight_bf16",
    )(w)


def _mm_res_norm_body(a_ref, b_ref, res_ref, gate_ref, g_ref, sh_ref, sc_ref, *out_refs, nk, keep_x):
    if nk > 1:
        *out_refs, acc_ref = out_refs
    h_ref = out_refs[-1]
    k = pl.program_id(1)
    part = jnp.dot(a_ref[...], b_ref[...], preferred_element_type=F32)

    def finish(acc):
        x = res_ref[...] + gate_ref[0] * acc
        if keep_x:
            out_refs[0][...] = x
        ms = jnp.mean(x * x, axis=-1, keepdims=True)
        xn = x * lax.rsqrt(ms + EPS) * g_ref[...]
        h_ref[...] = (xn * (1.0 + sc_ref[0]) + sh_ref[0]).astype(h_ref.dtype)

    if nk == 1:
        finish(part)
        return

    @pl.when(k == 0)
    def _():
        acc_ref[...] = part

    @pl.when(k > 0)
    def _():
        acc_ref[...] += part

    @pl.when(k == nk - 1)
    def _():
        finish(acc_ref[...])


def _mm_res_norm(a, b, res, gate, norm_g, shift, scale, group_idx, *, res_row0=0, keep_x=True, h_dtype=BF16,
                 name="mm_res_norm"):
    M, K = a.shape
    N = b.shape[1]
    h_bytes = jnp.dtype(h_dtype).itemsize

    def vmem_bytes(tm, tk):
        blocks = tm * tk * 2 + tk * N * 2 + tm * N * 4 + (tm * N * 4 if keep_x else 0) + tm * N * h_bytes
        return 2 * blocks + (tm * N * 4 if tk < K else 0)

    tm, tk = next(((tm, tk) for tk in (K, 1024, 512, 256, 128) if K % tk == 0
                   for tm in (512, 256, 128, 64, 32, 16, 8) if M % tm == 0
                   and vmem_bytes(tm, tk) <= VMEM_LIMIT - VMEM_INTERNAL), (8, 128))
    nk = K // tk
    if res_row0 % tm:
        res, res_row0 = res[res_row0:res_row0 + M], 0
    r0 = res_row0 // tm
    grp = lambda i, k: (group_idx(i * tm), 0, 0)
    row = pl.BlockSpec((tm, N), lambda i, k: (i, 0))
    out_specs = ([row] if keep_x else []) + [row]
    out_shape = ([jax.ShapeDtypeStruct((M, N), F32)] if keep_x else []) + [jax.ShapeDtypeStruct((M, N), h_dtype)]
    out = pl.pallas_call(
        functools.partial(_mm_res_norm_body, nk=nk, keep_x=keep_x),
        grid=(M // tm, nk),
        in_specs=[pl.BlockSpec((tm, tk), lambda i, k: (i, k)),
                  pl.BlockSpec((tk, N), lambda i, k: (k, 0)),
                  pl.BlockSpec((tm, N), lambda i, k: (r0 + i, 0)),
                  pl.BlockSpec((1, 1, N), grp),
                  pl.BlockSpec((1, N), lambda i, k: (0, 0)),
                  pl.BlockSpec((1, 1, N), grp), pl.BlockSpec((1, 1, N), grp)],
        out_specs=out_specs,
        out_shape=out_shape,
        scratch_shapes=[pltpu.VMEM((tm, N), F32)] if nk > 1 else [],
        compiler_params=_cparams(("parallel", "arbitrary")),
        name=name,
    )(a, b, res, gate, norm_g.reshape(1, N), shift, scale)
    return out if keep_x else out[0]


def _norm_body(x_ref, g_ref, sh_ref, sc_ref, o_ref):
    x = x_ref[...]
    ms = jnp.mean(x * x, axis=-1, keepdims=True)
    xn = x * lax.rsqrt(ms + EPS) * g_ref[...]
    o_ref[...] = (xn * (1.0 + sc_ref[0]) + sh_ref[0]).astype(o_ref.dtype)


def _norm_mod(x, g, shift, scale, group_idx, out_dtype=BF16):
    M, D = x.shape
    tm = _pick(M, (256, 128, 64, 32, 16, 8))
    return pl.pallas_call(
        _norm_body,
        grid=(M // tm,),
        in_specs=[pl.BlockSpec((tm, D), lambda i: (i, 0)),
                  pl.BlockSpec((1, D), lambda i: (0, 0)),
                  pl.BlockSpec((1, 1, D), lambda i: (group_idx(i * tm), 0, 0)),
                  pl.BlockSpec((1, 1, D), lambda i: (group_idx(i * tm), 0, 0))],
        out_specs=pl.BlockSpec((tm, D), lambda i: (i, 0)),
        out_shape=jax.ShapeDtypeStruct((M, D), out_dtype),
        compiler_params=_cparams(("parallel",)),
        name="norm_mod",
    )(x, g.reshape(1, D), shift, scale)


def _norm_first_body(c_ref, x_ref, g_ref, sh_ref, sc_ref, h_ref, xs_ref, *, n_ctx_tiles):
    x = jnp.where(pl.program_id(0) < n_ctx_tiles, c_ref[...], x_ref[...])
    xs_ref[...] = x
    ms = jnp.mean(x * x, axis=-1, keepdims=True)
    xn = x * lax.rsqrt(ms + EPS) * g_ref[...]
    h_ref[...] = (xn * (1.0 + sc_ref[0]) + sh_ref[0]).astype(h_ref.dtype)


def _norm_mod_first(ctx_rows, x_rows, g, shift, scale, group_idx):
    RC, D = ctx_rows.shape
    RX = x_rows.shape[0]
    tm = SEQ_BLOCK
    nc = RC // tm
    grp = lambda i: (group_idx(i * tm), 0, 0)
    row = pl.BlockSpec((tm, D), lambda i: (i, 0))
    return pl.pallas_call(
        functools.partial(_norm_first_body, n_ctx_tiles=nc),
        grid=((RC + RX) // tm,),
        in_specs=[pl.BlockSpec((tm, D), lambda i: (jnp.minimum(i, nc - 1), 0)),
                  pl.BlockSpec((tm, D), lambda i: (jnp.maximum(i - nc, 0), 0)),
                  pl.BlockSpec((1, D), lambda i: (0, 0)),
                  pl.BlockSpec((1, 1, D), grp), pl.BlockSpec((1, 1, D), grp)],
        out_specs=[row, row],
        out_shape=[jax.ShapeDtypeStruct((RC + RX, D), BF16), jax.ShapeDtypeStruct((RC + RX, D), F32)],
        compiler_params=_cparams(("parallel",)),
        name="norm_first",
    )(ctx_rows, x_rows, g.reshape(1, D), shift, scale)


def _seq_block(b, s, *, n_ctx_blk, n_x_blk, n_batch, reverse):
    if reverse:
        c = b * n_ctx_blk + (n_ctx_blk - 1 - s)
        x = n_batch * n_ctx_blk + b * n_x_blk + (n_x_blk - 1 - (s - n_ctx_blk))
    else:
        c = b * n_ctx_blk + s
        x = n_batch * n_ctx_blk + b * n_x_blk + (s - n_ctx_blk)
    return jnp.where(s < n_ctx_blk, c, x)


def _split2(x):
    hi = x.astype(BF16)
    lo = (x - hi.astype(F32)).astype(BF16)
    return hi, lo


def _rwkv_chunk(lw, r, v, kk, kd, be, st, tri, strict, incl, same_head, head0, eye):
    C = RWKV_CHUNK

    def stack(z):
        return jnp.concatenate([jnp.where(head0, z, 0.0), jnp.where(head0, 0.0, z)], axis=0)

    def unstack(z):
        return z[0:C] + z[C:2 * C]

    bdot = lambda a, b: jnp.dot(a, b, preferred_element_type=F32)
    bdot_nt = lambda a, b: lax.dot_general(a, b, (((1,), (1,)), ((), ())), preferred_element_type=F32)
    bdot_tn = lambda a, b: lax.dot_general(a, b, (((0,), (0,)), ((), ())), preferred_element_type=F32)
    cat = lambda *zs: jnp.concatenate(zs, axis=0)
    P = range(len(lw))
    cum = [_dot_ones_x(tri, lw[p]) for p in P]
    total = [jnp.sum(lw[p], axis=0, keepdims=True) for p in P]
    a_t = [kk[p] * jnp.exp(cum[p] - lw[p]) for p in P]
    r_t = [r[p] * jnp.exp(cum[p]) for p in P]
    a_sp = [_split2(a_t[p]) for p in P]
    r_hi = [r_t[p].astype(BF16) for p in P]
    sa_sp = [_split2(stack(a_t[p])) for p in P]
    sr_hi = [stack(r_t[p]).astype(BF16) for p in P]
    pinv = [jnp.exp(-cum[p]) for p in P]
    bk_sp = [_split2(cat(stack(be[p] * pinv[p]), stack(kd[p] * pinv[p]))) for p in P]
    g1 = [bdot_nt(cat(sa_sp[p][0], sa_sp[p][1], sr_hi[p]), bk_sp[p][0]) for p in P]
    g2 = [bdot_nt(sa_sp[p][0], bk_sp[p][1]) for p in P]
    n_all = [g1[p][0:2 * C] + g1[p][2 * C:4 * C] + g2[p] for p in P]
    x_neg = [jnp.where(strict, -n_all[p][:, 0:2 * C], 0.0) for p in P]
    n_ak = [jnp.where(strict, n_all[p][:, 2 * C:4 * C], 0.0) for p in P]
    m_r = [jnp.concatenate([jnp.where(incl, g1[p][4 * C:6 * C, 0:2 * C], 0.0),
                            jnp.where(incl, g1[p][4 * C:6 * C, 2 * C:4 * C], 0.0)], axis=1).astype(BF16)
           for p in P]
    lcat = lambda *zs: jnp.concatenate(zs, axis=1)
    W2 = 2 * C
    t_inv = [eye + x_neg[p] for p in P]
    xb = [x_neg[p].astype(BF16) for p in P]
    xb = [bdot(xb[p], xb[p]).astype(BF16) for p in P]
    n_pow = int(math.log2(C))
    for it in range(1, n_pow):
        if it + 1 < n_pow:
            both = [bdot(xb[p], lcat(xb[p], t_inv[p].astype(BF16))) for p in P]
            xb = [both[p][:, 0:W2].astype(BF16) for p in P]
            t_inv = [t_inv[p] + both[p][:, W2:2 * W2] for p in P]
        else:
            t_inv = [t_inv[p] + bdot(xb[p], t_inv[p].astype(BF16)) for p in P]
    t_inv = [t_inv[p].astype(BF16) for p in P]
    st_sp = [_split2(st[p]) for p in P]
    as1 = [bdot_nt(a_sp[p][0], cat(st_sp[p][0], st_sp[p][1])) for p in P]
    as2 = [bdot_nt(cat(a_sp[p][1], r_hi[p]), st_sp[p][0]) for p in P]
    v_sp = [_split2(stack(v[p])) for p in P]
    nk_sp = [_split2(n_ak[p]) for p in P]
    nv1 = [bdot(lcat(nk_sp[p][0], nk_sp[p][1]), cat(v_sp[p][0], v_sp[p][0])) for p in P]
    nv2 = [bdot(nk_sp[p][0], v_sp[p][1]) for p in P]
    rhs = [-(stack(as1[p][:, 0:W2] + as1[p][:, W2:2 * W2] + as2[p][0:C]) + nv1[p] + nv2[p]) for p in P]
    u0 = [bdot(t_inv[p], rhs[p].astype(BF16)) for p in P]
    u0_sp = [_split2(u0[p]) for p in P]
    x_sp = [_split2(x_neg[p]) for p in P]
    xu1 = [bdot(lcat(x_sp[p][0], x_sp[p][1]), cat(u0_sp[p][0], u0_sp[p][0])) for p in P]
    xu2 = [bdot(x_sp[p][0], u0_sp[p][1]) for p in P]
    res = [(rhs[p] - u0[p] + xu1[p] + xu2[p]).astype(BF16) for p in P]
    u = [u0[p] + bdot(t_inv[p], res[p]) for p in P]
    y = [stack(as2[p][C:2 * C]) + bdot(m_r[p], cat(u[p].astype(BF16), v_sp[p][0])) for p in P]
    uv_sp = [_split2(cat(unstack(u[p]), v[p])) for p in P]
    pend = [jnp.exp(total[p] - cum[p]) for p in P]
    e_sp = [_split2(cat(be[p] * pend[p], kd[p] * pend[p])) for p in P]
    upd = [bdot_tn(cat(uv_sp[p][0], uv_sp[p][1], uv_sp[p][0]), cat(e_sp[p][0], e_sp[p][0], e_sp[p][1]))
           for p in P]
    st_new = [st[p] * jnp.exp(total[p]) + jnp.where(same_head, upd[p], 0.0) for p in P]
    return [unstack(y[p]) for p in P], st_new


def _rwkv_body(r_ref, v_ref, kk_ref, lw_ref, kd_ref, be_ref, y_ref, st_ref, *, reverse, npair):
    C = RWKV_CHUNK
    nchunk = SEQ_BLOCK // C
    s = pl.program_id(2)

    @pl.when(s == 0)
    def _():
        st_ref[...] = jnp.zeros_like(st_ref)

    sgn = -1 if reverse else 1
    row = lax.broadcasted_iota(jnp.int32, (2 * C, 2 * C), 0)
    col = lax.broadcasted_iota(jnp.int32, (2 * C, 2 * C), 1)
    diff = (row - col) * sgn
    strict = diff > 0
    incl = diff >= 0
    same_head = (row // C) == (col // C)
    eye = jnp.where(row == col, 1.0, 0.0)
    r64 = lax.broadcasted_iota(jnp.int32, (C, C), 0)
    c64 = lax.broadcasted_iota(jnp.int32, (C, C), 1)
    tri = jnp.where((r64 - c64) * sgn >= 0, 1.0, 0.0).astype(BF16)
    head0 = lax.broadcasted_iota(jnp.int32, (C, 2 * C), 1) < C

    def chunk(ci, carry):
        ce = (nchunk - 1 - ci) if reverse else ci
        sl = pl.ds(pl.multiple_of(ce * C, C), C)
        lanes = [slice(p * LANES, (p + 1) * LANES) for p in range(npair)]
        get = lambda ref: [ref[sl, ln] for ln in lanes]
        ys, sts = _rwkv_chunk(get(lw_ref), get(r_ref), get(v_ref), get(kk_ref), get(kd_ref), get(be_ref),
                              [st_ref[p] for p in range(npair)], tri, strict, incl, same_head, head0, eye)
        for p in range(npair):
            y_ref[sl, lanes[p]] = ys[p]
            st_ref[p] = sts[p]
        return carry

    lax.fori_loop(0, nchunk, chunk, 0)


def _rwkv_scan(r, v, kk, lw, kd, be, *, n_batch, n_ctx_blk, n_x_blk, reverse, npair=8):
    R, W = r.shape
    ngrp = W // (LANES * npair)
    ns = n_ctx_blk + n_x_blk
    d = 1 if reverse else 0
    blk = functools.partial(_seq_block, n_ctx_blk=n_ctx_blk, n_x_blk=n_x_blk, n_batch=n_batch, reverse=reverse)
    spec = pl.BlockSpec((SEQ_BLOCK, LANES * npair), lambda b, p, s: (blk(b, s), p))
    dspec = pl.BlockSpec((None, SEQ_BLOCK, LANES * npair), lambda b, p, s: (d, blk(b, s), p))
    return pl.pallas_call(
        functools.partial(_rwkv_body, reverse=reverse, npair=npair),
        grid=(n_batch, ngrp, ns),
        in_specs=[spec] * 3 + [dspec] * 3,
        out_specs=spec,
        out_shape=jax.ShapeDtypeStruct((R, W), F32),
        scratch_shapes=[pltpu.VMEM((npair, LANES, LANES), F32)],
        compiler_params=_cparams(("parallel", "parallel", "arbitrary")),
        name="rwkv_scan_bwd" if reverse else "rwkv_scan_fwd",
    )(r, v, kk, lw, kd, be)


S5_SEG = SEQ_BLOCK // SUBLANES
S5_TAB_SLABS = 2 * S5_SEG + 2 + 8
S5_NGB = 4


def _s5_body(u_ref, bw_ref, cw_ref, tab_ref, y_ref, hb_ref, carry_ref, perm_ref, *, reverse):
    W = S5_GB * S5_N
    T = SEQ_BLOCK
    G = range(S5_NGB)
    s = pl.program_id(2)

    @pl.when(s == 0)
    def _():
        carry_ref[...] = jnp.zeros_like(carry_ref)
        prow = lax.broadcasted_iota(jnp.int32, (T, T), 0)
        pcol = lax.broadcasted_iota(jnp.int32, (T, T), 1)
        t_of_row = S5_SEG * (prow % SUBLANES) + prow // SUBLANES
        if reverse:
            t_of_row = T - 1 - t_of_row
        perm_ref[...] = jnp.where(pcol == t_of_row, 1.0, 0.0).astype(BF16)

    perm = perm_ref[...]
    u_p = jnp.dot(perm, u_ref[...].astype(BF16), preferred_element_type=F32).astype(BF16)
    for g in G:
        hb_ref[g] = jnp.dot(u_p[:, g * LANES:(g + 1) * LANES], bw_ref[g], preferred_element_type=F32)

    slab = lambda j: pl.ds(j * SUBLANES, SUBLANES)
    tab = lambda g, n: tab_ref[g, slab(n), :]
    o_seg = 2 * S5_SEG
    o_pow = o_seg + 2
    re, im = slice(0, W), slice(W, 2 * W)
    a = [(tab(g, o_pow), tab(g, o_pow + 1)) for g in G]
    h = [(hb_ref[g, slab(0), re], hb_ref[g, slab(0), im]) for g in G]
    for j in range(1, S5_SEG):
        h = [(hb_ref[g, slab(j), re] + a[g][0] * h[g][0] - a[g][1] * h[g][1],
              hb_ref[g, slab(j), im] + a[g][0] * h[g][1] + a[g][1] * h[g][0]) for g in G]
        for g in G:
            hb_ref[g, slab(j), re] = h[g][0]
            hb_ref[g, slab(j), im] = h[g][1]
    rows8 = lax.broadcasted_iota(jnp.int32, (SUBLANES, W), 0)
    ent = []
    for g in G:
        gr, gi = h[g]
        for n, k in enumerate((1, 2, 4)):
            pr, pi = tab(g, o_pow + 2 + 2 * n), tab(g, o_pow + 3 + 2 * n)
            sr = jnp.where(rows8 >= k, pltpu.roll(gr, k, axis=0), 0.0)
            si = jnp.where(rows8 >= k, pltpu.roll(gi, k, axis=0), 0.0)
            gr, gi = gr + pr * sr - pi * si, gi + pr * si + pi * sr
        cr, ci = carry_ref[g, :, re], carry_ref[g, :, im]
        qr, qi = tab(g, o_seg), tab(g, o_seg + 1)
        gr, gi = gr + qr * cr - qi * ci, gi + qr * ci + qi * cr
        ent.append((jnp.where(rows8 >= 1, pltpu.roll(gr, 1, axis=0), cr),
                    jnp.where(rows8 >= 1, pltpu.roll(gi, 1, axis=0), ci)))
        carry_ref[g, :, re] = jnp.broadcast_to(gr[SUBLANES - 1:SUBLANES, :], (SUBLANES, W))
        carry_ref[g, :, im] = jnp.broadcast_to(gi[SUBLANES - 1:SUBLANES, :], (SUBLANES, W))
    for g in G:
        in_r, in_i = ent[g]
        for j in range(S5_SEG):
            pr, pi = tab(g, j), tab(g, S5_SEG + j)
            hb_ref[g, slab(j), re] = hb_ref[g, slab(j), re] + pr * in_r - pi * in_i
            hb_ref[g, slab(j), im] = hb_ref[g, slab(j), im] + pr * in_i + pi * in_r
    y_p = jnp.concatenate([jnp.dot(hb_ref[g].astype(BF16), cw_ref[g], preferred_element_type=F32) for g in G],
                          axis=1)
    y_ref[...] = _dot_x_ones_tn(perm, y_p)


def _dot_x_ones_tn(ones, x):
    hi, mid, lo = _split3(x)
    f = lambda t: lax.dot_general(ones, t, (((0,), (0,)), ((), ())), preferred_element_type=F32)
    return f(hi) + f(mid) + f(lo)


def _s5_scan(p, col0_blk, bw, cw, tab, *, n_batch, n_ctx_blk, n_x_blk, reverse):
    R = p.shape[0]
    n = S5_NGB
    ngb = S5_G // S5_GB // n
    ns = n_ctx_blk + n_x_blk
    blk = functools.partial(_seq_block, n_ctx_blk=n_ctx_blk, n_x_blk=n_x_blk, n_batch=n_batch, reverse=reverse)
    W2 = 2 * S5_GB * S5_N
    assert col0_blk % n == 0
    return pl.pallas_call(
        functools.partial(_s5_body, reverse=reverse),
        grid=(n_batch, ngb, ns),
        in_specs=[pl.BlockSpec((SEQ_BLOCK, n * LANES), lambda b, g, s: (blk(b, s), col0_blk // n + g)),
                  pl.BlockSpec((n, LANES, W2), lambda b, g, s: (g, 0, 0)),
                  pl.BlockSpec((n, W2, LANES), lambda b, g, s: (g, 0, 0)),
                  pl.BlockSpec((n, S5_TAB_SLABS * SUBLANES, W2 // 2), lambda b, g, s: (g, 0, 0))],
        out_specs=pl.BlockSpec((SEQ_BLOCK, n * LANES), lambda b, g, s: (blk(b, s), g)),
        out_shape=jax.ShapeDtypeStruct((R, S5_W), F32),
        scratch_shapes=[pltpu.VMEM((n, SEQ_BLOCK, W2), F32), pltpu.VMEM((n, SUBLANES, W2), F32),
                        pltpu.VMEM((SEQ_BLOCK, SEQ_BLOCK), BF16)],
        compiler_params=_cparams(("parallel", "parallel", "arbitrary")),
        name="s5_scan_bwd" if reverse else "s5_scan_fwd",
    )(p, bw, cw, tab)


def _s5_weights(A_re, A_im, log_dt, B_re, B_im, C_re, C_im):
    lam_re = jnp.minimum(A_re, -1e-4)
    dt = jnp.exp(log_dt)[:, None]
    mag = jnp.exp(lam_re * dt)
    ab_re, ab_im = mag * jnp.cos(A_im * dt), mag * jnp.sin(A_im * dt)
    den = lam_re * lam_re + A_im * A_im
    f_re = ((ab_re - 1) * lam_re + ab_im * A_im) / den
    f_im = (ab_im * lam_re - (ab_re - 1) * A_im) / den
    bb_re = f_re[..., None] * B_re - f_im[..., None] * B_im
    bb_im = f_re[..., None] * B_im + f_im[..., None] * B_re
    ngb = S5_G // S5_GB
    eye = jnp.eye(S5_GB, dtype=F32)

    def in_block(bb):
        t = bb.reshape(ngb, S5_GB, S5_N, S5_P)
        return jnp.einsum("agnp,gh->agphn", t, eye).reshape(ngb, S5_GB * S5_P, S5_GB * S5_N)

    def out_block(cc):
        t = cc.reshape(ngb, S5_GB, S5_P, S5_N)
        return jnp.einsum("agpn,gh->agnhp", t, eye).reshape(ngb, S5_GB * S5_N, S5_GB * S5_P)

    bw = jnp.concatenate([in_block(bb_re), in_block(bb_im)], axis=2).astype(BF16)
    cw = jnp.concatenate([out_block(C_re), -out_block(C_im)], axis=1).astype(BF16)

    def power(k):
        m = jnp.exp(k * lam_re * dt)
        return m * jnp.cos(k * A_im * dt), m * jnp.sin(k * A_im * dt)

    rep = lambda t: jnp.broadcast_to(t[None], (SUBLANES,) + t.shape)
    slab_pow = [power(float(j + 1)) for j in range(S5_SEG)]
    seg_pow = [power(float(S5_SEG * (i + 1))) for i in range(SUBLANES)]
    slabs = ([rep(c[0]) for c in slab_pow] + [rep(c[1]) for c in slab_pow]
             + [jnp.stack([c[0] for c in seg_pow]), jnp.stack([c[1] for c in seg_pow])])
    for k in (1.0, float(S5_SEG), 2.0 * S5_SEG, 4.0 * S5_SEG):
        slabs += [rep(t) for t in power(k)]
    tab = jnp.concatenate(slabs, axis=0)
    tab = tab.reshape(S5_TAB_SLABS * SUBLANES, ngb, S5_GB * S5_N).transpose(1, 0, 2)
    return bw, cw, tab


def _ssd_body(x_ref, b_ref, c_ref, dt_ref, dta_ref, sel_ref, y_ref, st_ref, *, reverse):
    Q = SSD_CHUNK
    s = pl.program_id(1)

    @pl.when(s == 0)
    def _():
        st_ref[...] = jnp.zeros_like(st_ref)

    row = lax.broadcasted_iota(jnp.int32, (Q, Q), 0)
    col = lax.broadcasted_iota(jnp.int32, (Q, Q), 1)
    mask = (col >= row) if reverse else (col <= row)
    tri = jnp.where(mask, 1.0, 0.0).astype(BF16)
    head0 = lax.broadcasted_iota(jnp.int32, (Q, LANES), 1) < SSD_HD
    head0_rows = lax.broadcasted_iota(jnp.int32, (LANES, SSD_N), 0) < SSD_HD

    h_lo = SSD_HEADS if reverse else 0
    dt = dt_ref[:, h_lo:h_lo + SSD_HEADS]
    dta = dta_ref[:, h_lo:h_lo + SSD_HEADS]
    dtt = dt_ref[...].T[h_lo:h_lo + SSD_HEADS, :]
    dtat = dta_ref[...].T[h_lo:h_lo + SSD_HEADS, :]
    acum = _dot_ones_x(tri, dta)
    acum_t = _dot_x_ones_nt(dtat, tri)
    total = jnp.sum(dta, axis=0, keepdims=True)
    total_t = jnp.sum(dtat, axis=1, keepdims=True)
    e_tot_t = jnp.exp(total_t)
    dt_wide = jnp.dot(dt.astype(BF16), sel_ref[...], preferred_element_type=F32)
    heads_per_group = SSD_HEADS // SSD_GROUPS
    G = range(SSD_GROUPS)
    pairs = [(g, g * heads_per_group + 2 * pr) for g in G for pr in range(heads_per_group // 2)]
    lanes2 = lambda h0: slice(h0 * SSD_HD, (h0 + 2) * SSD_HD)
    bg = [b_ref[:, g * SSD_N:(g + 1) * SSD_N].astype(BF16) for g in G]
    cg = [c_ref[:, g * SSD_N:(g + 1) * SSD_N].astype(BF16) for g in G]
    cb = [lax.dot_general(cg[g], bg[g], (((1,), (1,)), ((), ())), preferred_element_type=F32) for g in G]
    xp = [x_ref[:, lanes2(h0)] for _, h0 in pairs]
    xb = [t.astype(BF16) for t in xp]
    colb = [jnp.broadcast_to(acum[:, h:h + 1], (Q, LANES)) for h in range(SSD_HEADS)]

    def weights(g, h):
        seg = colb[h] - acum_t[h:h + 1, :]
        return (cb[g] * jnp.exp(jnp.where(mask, seg, -jnp.inf)) * dtt[h:h + 1, :]).astype(BF16)

    yd = [(jnp.dot(weights(g, h0), xb[i], preferred_element_type=F32),
           jnp.dot(weights(g, h0 + 1), xb[i], preferred_element_type=F32)) for i, (g, h0) in enumerate(pairs)]
    st = [st_ref[lanes2(h0), :] for _, h0 in pairs]
    yo = [lax.dot_general(cg[g], st[i].astype(BF16), (((1,), (1,)), ((), ())), preferred_element_type=F32)
          for i, (g, _) in enumerate(pairs)]
    for i, (g, h0) in enumerate(pairs):
        scale = jnp.exp(jnp.where(head0, colb[h0], colb[h0 + 1]))
        y_ref[:, lanes2(h0)] = jnp.where(head0, yd[i][0], yd[i][1]) + yo[i] * scale
    xt = [(xp[i] * dt_wide[:, lanes2(h0)]
           * jnp.exp(jnp.where(head0, total[:, h0:h0 + 1] - colb[h0], total[:, h0 + 1:h0 + 2] - colb[h0 + 1]))
           ).astype(BF16) for i, (_, h0) in enumerate(pairs)]
    upd = [lax.dot_general(xt[i], bg[g], (((0,), (0,)), ((), ())), preferred_element_type=F32)
           for i, (g, _) in enumerate(pairs)]
    for i, (_, h0) in enumerate(pairs):
        dec = jnp.where(head0_rows, e_tot_t[h0:h0 + 1, :], e_tot_t[h0 + 1:h0 + 2, :])
        st_ref[lanes2(h0), :] = st[i] * dec + upd[i]


def _ssd_scan(xbc, dt, dta, *, n_batch, n_ctx_blk, n_x_blk, reverse):
    R = xbc.shape[0]
    Q = SSD_CHUNK
    blk = functools.partial(_seq_block, n_ctx_blk=n_ctx_blk, n_x_blk=n_x_blk, n_batch=n_batch, reverse=reverse)
    ns = n_ctx_blk + n_x_blk
    gw = SSD_GROUPS * SSD_N
    row_spec = lambda w, c: pl.BlockSpec((Q, w), lambda b, s: (blk(b, s), c))
    head_of_lane = jnp.arange(SSD_W, dtype=jnp.int32) // SSD_HD
    sel = (jnp.arange(SSD_HEADS, dtype=jnp.int32)[:, None] == head_of_lane[None, :]).astype(BF16)
    return pl.pallas_call(
        functools.partial(_ssd_body, reverse=reverse),
        grid=(n_batch, ns),
        in_specs=[row_spec(SSD_W, 0), row_spec(gw, SSD_W // gw), row_spec(gw, SSD_W // gw + 1),
                  row_spec(LANES, 0), row_spec(LANES, 0), pl.BlockSpec((SSD_HEADS, SSD_W), lambda b, s: (0, 0))],
        out_specs=row_spec(SSD_W, 0),
        out_shape=jax.ShapeDtypeStruct((R, SSD_W), F32),
        scratch_shapes=[pltpu.VMEM((SSD_W, SSD_N), F32)],
        compiler_params=_cparams(("parallel", "arbitrary")),
        name="ssd_scan_bwd" if reverse else "ssd_scan_fwd",
    )(xbc, xbc, xbc, dt, dta, sel)


def _attn_body(q_ref, kp_ref, kc_ref, kn_ref, vp_ref, vc_ref, vn_ref, kx_ref, vx_ref, sink_ref, mix_ref, o_ref,
               band_ref, *, seq_len):
    del mix_ref
    n = pl.program_id(1)
    blk = ATT_BLOCK
    n_ctx = kx_ref.shape[0]
    rep = ATT_HEADS // KV_HEADS
    nk = n_ctx + 3 * blk

    @pl.when(n == 0)
    def _():
        r_q = lax.broadcasted_iota(jnp.int32, (rep * blk, nk), 0) % blk
        j = lax.broadcasted_iota(jnp.int32, (rep * blk, nk), 1)
        ok = (j < n_ctx) | (jnp.abs(r_q - (j - n_ctx - blk)) <= WINDOW)
        band_ref[...] = jnp.where(ok, 0.0, -jnp.inf)

    k_all = jnp.concatenate([kx_ref[...], kp_ref[...], kc_ref[...], kn_ref[...]], axis=0).astype(BF16)
    v_all = jnp.concatenate([vx_ref[...], vp_ref[...], vc_ref[...], vn_ref[...]], axis=0).astype(BF16)
    jb = lax.broadcasted_iota(jnp.int32, (1, nk), 1) - n_ctx
    k_pos = (n - 1) * blk + jb
    outside = (jb >= 0) & ((k_pos < 0) | (k_pos >= seq_len))
    mask = jnp.where(outside, -jnp.inf, band_ref[...])
    scale = ATT_HD ** -0.5
    head_of_row = lax.broadcasted_iota(jnp.int32, (rep * blk, 1), 0) // blk
    G = range(KV_HEADS)
    hd = lambda t, h: t[:, h * ATT_HD:(h + 1) * ATT_HD]
    q = (q_ref[...] * scale).astype(BF16)
    qs = [jnp.concatenate([hd(q, g * rep + i) for i in range(rep)], axis=0) for g in G]
    sinks = []
    for g in G:
        sk = jnp.zeros((rep * blk, 1), F32)
        for i in range(rep):
            sk = jnp.where(head_of_row == i, sink_ref[0:1, g * rep + i:g * rep + i + 1], sk)
        sinks.append(sk)
    sc = [lax.dot_general(qs[g], hd(k_all, g), (((1,), (1,)), ((), ())), preferred_element_type=F32) + mask
          for g in G]
    m = [jnp.maximum(jnp.max(sc[g], axis=-1, keepdims=True), sinks[g]) for g in G]
    e = [jnp.exp(sc[g] - m[g]) for g in G]
    den = [jnp.sum(e[g], axis=-1, keepdims=True) + jnp.exp(sinks[g] - m[g]) for g in G]
    pv = [jnp.dot(e[g].astype(BF16), hd(v_all, g), preferred_element_type=F32) / den[g] for g in G]
    outs = [pv[g][i * blk:(i + 1) * blk] for g in G for i in range(rep)]
    o_ref[...] = jnp.concatenate(outs, axis=1).astype(o_ref.dtype)


def _window_attn(q, k, p, v_col, sink, mix, mix_col, *, n_batch, n_ctx, seq_len):
    nb = seq_len // ATT_BLOCK
    x0 = n_batch * n_ctx // ATT_BLOCK
    vc = v_col // ATT_KV_W
    cur = lambda b, n: x0 + b * nb + n
    prev = lambda b, n: x0 + b * nb + jnp.maximum(n - 1, 0)
    nxt = lambda b, n: x0 + b * nb + jnp.minimum(n + 1, nb - 1)
    k_spec = lambda im: pl.BlockSpec((ATT_BLOCK, ATT_KV_W), lambda b, n: (im(b, n), 0))
    v_spec = lambda im: pl.BlockSpec((ATT_BLOCK, ATT_KV_W), lambda b, n: (im(b, n), vc))
    return pl.pallas_call(
        functools.partial(_attn_body, seq_len=seq_len),
        grid=(n_batch, nb),
        in_specs=[pl.BlockSpec((ATT_BLOCK, ATT_Q_W), lambda b, n: (cur(b, n), 0)),
                  k_spec(prev), k_spec(cur), k_spec(nxt), v_spec(prev), v_spec(cur), v_spec(nxt),
                  pl.BlockSpec((n_ctx, ATT_KV_W), lambda b, n: (b, 0)),
                  pl.BlockSpec((n_ctx, ATT_KV_W), lambda b, n: (b, vc)),
                  pl.BlockSpec((1, ATT_HEADS), lambda b, n: (0, 0)),
                  pl.BlockSpec(memory_space=pl.ANY)],
        out_specs=pl.BlockSpec((ATT_BLOCK, ATT_Q_W), lambda b, n: (b * nb + n, mix_col // ATT_Q_W)),
        out_shape=jax.ShapeDtypeStruct(mix.shape, mix.dtype),
        input_output_aliases={10: 0},
        scratch_shapes=[pltpu.VMEM((ATT_HEADS // KV_HEADS * ATT_BLOCK, n_ctx + 3 * ATT_BLOCK), F32)],
        compiler_params=_cparams(("parallel", "arbitrary")),
        name="window_attn",
    )(q, k, k, k, p, p, p, k, p, sink, mix)


def _tile_edges(row0, tm, n_ctx_rows, ctx_len, seq_len):
    in_ctx = row0 < n_ctx_rows
    pos0 = jnp.where(in_ctx, row0 % ctx_len, (row0 - n_ctx_rows) % seq_len)
    seg = jnp.where(in_ctx, ctx_len, seq_len)
    return pos0 == 0, pos0 + tm == seg


def _shifted_rows(cur, k, prev_blk, next_blk, first, last, rows):
    tm = cur.shape[0]
    if k < 0:
        edge = jnp.where(first, 0.0, pltpu.roll(prev_blk, -k, axis=0))
        body = pltpu.roll(cur, -k, axis=0)
        head = jnp.where(rows < -k, edge, body[0:SUBLANES])
        return jnp.concatenate([head, body[SUBLANES:]], axis=0)
    edge = jnp.where(last, 0.0, pltpu.roll(next_blk, SUBLANES - k, axis=0))
    body = pltpu.roll(cur, tm - k, axis=0)
    tail = jnp.where(rows >= SUBLANES - k, edge, body[tm - SUBLANES:])
    return jnp.concatenate([body[:tm - SUBLANES], tail], axis=0)


def _head_sum_mxu(t, blk):
    n = t.shape[1] // LANES
    hi, lo = _split2(t)
    both = jnp.concatenate([hi, lo], axis=0)
    rows = t.shape[0]
    out = []
    for j in range(n):
        s = jnp.dot(both[:, j * LANES:(j + 1) * LANES], blk, preferred_element_type=F32)
        out.append(s[0:rows] + s[rows:2 * rows])
    return jnp.concatenate(out, axis=1)


def _same_head_matrix():
    r = lax.broadcasted_iota(jnp.int32, (LANES, LANES), 0) // RWKV_HD
    c = lax.broadcasted_iota(jnp.int32, (LANES, LANES), 1) // RWKV_HD
    return jnp.where(r == c, 1.0, 0.0).astype(BF16)


def _softplus(x):
    return jnp.maximum(x, 0.0) + jnp.log(1.0 + jnp.exp(-jnp.abs(x)))


def _rwkv_pre_body(p_ref, hp_ref, hn_ref, mu_ref, lora_ref, gw_ref, w0_ref, a0_ref, kk_ref, ka_ref, rk_ref,
                   r_out, v_out, kk_out, g_out, bonus_out, lw_out, kd_out, be_out, *, tm, geo):
    W = RWKV_W
    first, last = _tile_edges(pl.program_id(0) * tm, tm, *geo)
    rows = lax.broadcasted_iota(jnp.int32, (SUBLANES, 1), 0)

    def tshift(c0, c1):
        cur = p_ref[:, c0:c1]
        prev = _shifted_rows(cur, -1, hp_ref[:, c0:c1], hn_ref[:, c0:c1], first, last, rows)
        nxt = _shifted_rows(cur, 1, hp_ref[:, c0:c1], hn_ref[:, c0:c1], first, last, rows)
        return cur + mu_ref[0:1, c0:c1] * (prev - cur) + mu_ref[1:2, c0:c1] * (nxt - cur)

    r = tshift(0, W)
    k = tshift(W, 2 * W)
    v = tshift(2 * W, 3 * W)
    xl = tshift(3 * W, 3 * W + 4 * LANES)
    lane = lax.broadcasted_iota(jnp.int32, (tm, 2 * LANES), 1)
    lin = jnp.where(lane < DECAY_LORA, jnp.tanh(xl[:, :2 * LANES]), xl[:, :2 * LANES])
    lora = _dot(lin, lora_ref[...])
    g_out[...] = _dot(jax.nn.sigmoid(xl), gw_ref[...])
    blk = _same_head_matrix()
    kk = k * kk_ref[...]
    kk = kk / jnp.maximum(jnp.sqrt(_head_sum_mxu(kk * kk, blk)), 1e-12)
    r_out[...] = r
    v_out[...] = v
    kk_out[...] = kk
    kd_sum = jnp.zeros_like(k)
    for d in range(2):
        w_log = -_softplus(-(w0_ref[d:d + 1, :] + lora[:, d * W:(d + 1) * W])) - 0.5
        a = jax.nn.sigmoid(a0_ref[d:d + 1, :] + lora[:, (2 + d) * W:(3 + d) * W])
        kd = k * (1.0 + (a - 1.0) * ka_ref[...])
        lw_out[d] = -jnp.exp(w_log)
        kd_out[d] = kd
        be_out[d] = a * kk
        kd_sum = kd_sum + kd
    bonus_out[...] = _head_sum_mxu(r * kd_sum * rk_ref[...], blk) * v


def _rwkv_pre(p, mu, lora_w, gate_w, w0, a0, k_k, k_a, r_k, *, geo):
    R = p.shape[0]
    W = RWKV_W
    tm = SEQ_BLOCK
    pw = 3 * W + 4 * LANES
    nb8 = R // SUBLANES
    row = lambda i: (i, 0)
    vec = lambda n: pl.BlockSpec((n, W), lambda i: (0, 0))
    out2 = pl.BlockSpec((2, tm, W), lambda i: (0, i, 0))
    f = jax.ShapeDtypeStruct((R, W), F32)
    f2 = jax.ShapeDtypeStruct((2, R, W), F32)
    return pl.pallas_call(
        functools.partial(_rwkv_pre_body, tm=tm, geo=geo),
        grid=(R // tm,),
        in_specs=[pl.BlockSpec((tm, pw), row),
                  pl.BlockSpec((SUBLANES, pw), lambda i: (jnp.maximum(i * (tm // SUBLANES) - 1, 0), 0)),
                  pl.BlockSpec((SUBLANES, pw), lambda i: (jnp.minimum((i + 1) * (tm // SUBLANES), nb8 - 1), 0)),
                  pl.BlockSpec((2, pw), lambda i: (0, 0)),
                  pl.BlockSpec(lora_w.shape, lambda i: (0, 0)),
                  pl.BlockSpec(gate_w.shape, lambda i: (0, 0)),
                  vec(2), vec(2), vec(1), vec(1), vec(1)],
        out_specs=[pl.BlockSpec((tm, W), row)] * 5 + [out2] * 3,
        out_shape=[f] * 5 + [f2] * 3,
        compiler_params=_cparams(("parallel",)),
        name="rwkv_pre",
    )(p, p, p, mu, lora_w, gate_w, w0, a0, k_k, k_a, r_k)


def _rwkv_post_body(yf_ref, yb_ref, bonus_ref, g_ref, lng_ref, lnb_ref, o_ref):
    blk = _same_head_matrix()
    y = yf_ref[...] + yb_ref[...]
    mean = _head_sum_mxu(y, blk) * (1.0 / RWKV_HD)
    d = y - mean
    var = _head_sum_mxu(d * d, blk) * (1.0 / RWKV_HD)
    yn = d * lax.rsqrt(var + GN_EPS)
    o_ref[...] = ((yn * lng_ref[...] + lnb_ref[...] + bonus_ref[...]) * g_ref[...]).astype(o_ref.dtype)


def _rwkv_post(yf, yb, bonus, g, ln_g, ln_b, out_cols):
    R, W = yf.shape
    tm = SEQ_BLOCK
    row = pl.BlockSpec((tm, W), lambda i: (i, 0))
    vec = pl.BlockSpec((1, W), lambda i: (0, 0))
    return pl.pallas_call(
        _rwkv_post_body,
        grid=(R // tm,),
        in_specs=[row, row, row, row, vec, vec],
        out_specs=row,
        out_shape=jax.ShapeDtypeStruct((R, out_cols), BF16),
        compiler_params=_cparams(("parallel",)),
        name="rwkv_post",
    )(yf, yb, bonus, g, ln_g, ln_b)


def _gelu_tanh(x):
    return 0.5 * x * (1.0 + jnp.tanh(math.sqrt(2.0 / math.pi) * (x + 0.044715 * (x * x * x))))


def _s5_post_body(u0_ref, u1_ref, yf_ref, yb_ref, d_ref, w_ref, b_ref, mix_ref, o_ref):
    del mix_ref
    u = jnp.concatenate([u0_ref[...], u1_ref[...]], axis=1)
    y = _gelu_tanh(d_ref[...] * u + yf_ref[...] + yb_ref[...])
    o_ref[...] = (y * jax.nn.sigmoid(_dot(y, w_ref[...]) + b_ref[...])).astype(o_ref.dtype)


def _s5_post(p, col0, yf, yb, d_skip, glu_w, glu_b, mix):
    R, W = yf.shape
    tm = SEQ_BLOCK
    half = W // 2
    row = pl.BlockSpec((tm, W), lambda i: (i, 0))
    vec = pl.BlockSpec((1, W), lambda i: (0, 0))
    return pl.pallas_call(
        _s5_post_body,
        grid=(R // tm,),
        in_specs=[pl.BlockSpec((tm, half), lambda i: (i, col0 // half)),
                  pl.BlockSpec((tm, half), lambda i: (i, col0 // half + 1)), row, row, vec,
                  pl.BlockSpec((W, W), lambda i: (0, 0)), vec, pl.BlockSpec(memory_space=pl.ANY)],
        out_specs=pl.BlockSpec((tm, W), lambda i: (i, 1)),
        out_shape=jax.ShapeDtypeStruct(mix.shape, mix.dtype),
        input_output_aliases={7: 0},
        compiler_params=_cparams(("parallel",)),
        name="s5_post",
    )(p, p, yf, yb, d_skip, glu_w, glu_b, mix)


def _cd_conv_body(x_ref, hp_ref, hn_ref, w_ref, b_ref, o_ref, *, tm, geo, taps):
    first, last = _tile_edges(pl.program_id(0) * tm, tm, *geo)
    rows = lax.broadcasted_iota(jnp.int32, (SUBLANES, 1), 0)
    cur = x_ref[...]
    half = taps // 2
    acc = b_ref[...] + w_ref[half:half + 1, :] * cur
    for j in range(taps):
        if j != half:
            acc = acc + w_ref[j:j + 1, :] * _shifted_rows(cur, j - half, hp_ref[...], hn_ref[...], first, last, rows)
    o_ref[...] = acc * jax.nn.sigmoid(acc)


def _cd_conv(p, col0, width, conv_w, conv_b, *, geo):
    R = p.shape[0]
    tm = SEQ_BLOCK
    tw = 1024
    taps = conv_w.shape[0]
    nb8 = R // SUBLANES
    c0 = col0 // tw
    w8 = jnp.pad(conv_w, ((0, SUBLANES - taps), (0, 0)))
    return pl.pallas_call(
        functools.partial(_cd_conv_body, tm=tm, geo=geo, taps=taps),
        grid=(R // tm, width // tw),
        in_specs=[pl.BlockSpec((tm, tw), lambda i, j: (i, c0 + j)),
                  pl.BlockSpec((SUBLANES, tw), lambda i, j: (jnp.maximum(i * (tm // SUBLANES) - 1, 0), c0 + j)),
                  pl.BlockSpec((SUBLANES, tw), lambda i, j: (jnp.minimum((i + 1) * (tm // SUBLANES), nb8 - 1), c0 + j)),
                  pl.BlockSpec((SUBLANES, tw), lambda i, j: (0, j)),
                  pl.BlockSpec((1, tw), lambda i, j: (0, j))],
        out_specs=pl.BlockSpec((tm, tw), lambda i, j: (i, j)),
        out_shape=jax.ShapeDtypeStruct((R, width), F32),
        compiler_params=_cparams(("parallel", "parallel")),
        name="cd_conv",
    )(p, p, p, w8, conv_b.reshape(1, width))


def _rope_tile(x, cos, sin):
    lane = lax.broadcasted_iota(jnp.int32, x.shape, 1)
    q = ATT_HD // 4
    partner = jnp.where(lane % (2 * q) < q, pltpu.roll(x, LANES - q, axis=1), pltpu.roll(x, q, axis=1))
    return x * cos + partner * sin


def _cd_rope_dt_body(q_ref, k_ref, dtr_ref, cos_ref, sin_ref, bias_ref, a_ref, q_out, k_out, dt_out, dta_out,
                     *, tm, n_ctx_rows):
    in_ctx = pl.program_id(0) * tm < n_ctx_rows
    cos = jnp.where(in_ctx, 1.0, cos_ref[...])
    sin = jnp.where(in_ctx, 0.0, sin_ref[...])
    for j in range(ATT_Q_W // LANES):
        q_out[:, j * LANES:(j + 1) * LANES] = _rope_tile(q_ref[:, j * LANES:(j + 1) * LANES], cos, sin)
    for j in range(ATT_KV_W // LANES):
        k_out[:, j * LANES:(j + 1) * LANES] = _rope_tile(k_ref[:, j * LANES:(j + 1) * LANES], cos, sin)
    dt = _softplus(dtr_ref[...] + bias_ref[...])
    dt_out[...] = dt
    dta_out[...] = dt * a_ref[...]


def _cd_rope_dt(p, q_col, k_col, dt_col, cos, sin, dt_bias, a_neg, *, n_ctx_rows, seq_len):
    R = p.shape[0]
    tm = SEQ_BLOCK
    tab = pl.BlockSpec((tm, LANES), lambda i: (jnp.maximum(i * tm - n_ctx_rows, 0) % seq_len // tm, 0))
    vec = pl.BlockSpec((1, LANES), lambda i: (0, 0))
    return pl.pallas_call(
        functools.partial(_cd_rope_dt_body, tm=tm, n_ctx_rows=n_ctx_rows),
        grid=(R // tm,),
        in_specs=[pl.BlockSpec((tm, ATT_Q_W), lambda i: (i, q_col // ATT_Q_W)),
                  pl.BlockSpec((tm, ATT_KV_W), lambda i: (i, k_col // ATT_KV_W)),
                  pl.BlockSpec((tm, LANES), lambda i: (i, dt_col // LANES)),
                  tab, tab, vec, vec],
        out_specs=[pl.BlockSpec((tm, ATT_Q_W), lambda i: (i, 0)), pl.BlockSpec((tm, ATT_KV_W), lambda i: (i, 0)),
                   pl.BlockSpec((tm, LANES), lambda i: (i, 0)), pl.BlockSpec((tm, LANES), lambda i: (i, 0))],
        out_shape=[jax.ShapeDtypeStruct((R, ATT_Q_W), F32), jax.ShapeDtypeStruct((R, ATT_KV_W), F32),
                   jax.ShapeDtypeStruct((R, LANES), F32), jax.ShapeDtypeStruct((R, LANES), F32)],
        compiler_params=_cparams(("parallel",)),
        name="cd_rope_dt",
    )(p, p, p, cos, sin, dt_bias, a_neg)


def _ssd_post_body(yf_ref, yb_ref, x_ref, z_ref, d_ref, g_ref, o_ref):
    z = z_ref[...]
    y = (yf_ref[...] + yb_ref[...] + d_ref[...] * x_ref[...]) * (z * jax.nn.sigmoid(z))
    gw = SSD_W // SSD_GROUPS
    for g in range(SSD_GROUPS):
        yg = y[:, g * gw:(g + 1) * gw]
        ms = jnp.mean(yg * yg, axis=-1, keepdims=True)
        o_ref[:, g * gw:(g + 1) * gw] = (yg * lax.rsqrt(ms + EPS) * g_ref[:, g * gw:(g + 1) * gw]).astype(o_ref.dtype)


def _ssd_post(yf, yb, xbc, p, d_skip, norm_g, out_cols, *, n_ctx_rows):
    R = yf.shape[0]
    tm = SEQ_BLOCK
    r0 = n_ctx_rows // tm
    row = pl.BlockSpec((tm, SSD_W), lambda i: (r0 + i, 0))
    vec = pl.BlockSpec((1, SSD_W), lambda i: (0, 0))
    return pl.pallas_call(
        _ssd_post_body,
        grid=((R - n_ctx_rows) // tm,),
        in_specs=[row, row, row, row, vec, vec],
        out_specs=pl.BlockSpec((tm, SSD_W), lambda i: (i, 0)),
        out_shape=jax.ShapeDtypeStruct((R - n_ctx_rows, out_cols), BF16),
        compiler_params=_cparams(("parallel",)),
        name="ssd_post",
    )(yf, yb, xbc, p, d_skip, norm_g)


def _rope_tables(L):
    nf = ATT_HD // 4
    inv_freq = ROPE_BASE ** (-jnp.arange(nf, dtype=F32) / nf)
    t = jnp.arange(L, dtype=jnp.int32)
    row_id = (t // GRID_W).astype(F32)
    col_id = (t % GRID_W).astype(F32)
    ar = row_id[:, None] * inv_freq
    ac = col_id[:, None] * inv_freq
    cos = jnp.concatenate([jnp.cos(ar), jnp.cos(ar), jnp.cos(ac), jnp.cos(ac)], axis=1)
    sin = jnp.concatenate([-jnp.sin(ar), jnp.sin(ar), -jnp.sin(ac), jnp.sin(ac)], axis=1)
    return cos, sin


def kernel(x, c, ctx, c_ctx, ada_w, ada_b, norm1_g, norm2_g, mlp_w1, mlp_w2, final_g, ab_w_in, ab_w_out, rwkv_mu, rwkv_w0, rwkv_w_up, rwkv_a0, rwkv_a_up, rwkv_g_up, rwkv_k_k, rwkv_k_a, rwkv_r_k, rwkv_ln_g, rwkv_ln_b, s5_A_re, s5_A_im, s5_log_dt, s5_B_re, s5_B_im, s5_C_re, s5_C_im, s5_D, s5_glu_w, s5_glu_b, cd_w_in, cd_w_out, ssd_conv_w, ssd_conv_b, ssd_A_log, ssd_dt_bias, ssd_D, ssd_norm_g, attn_sink):
    B, L, D = x.shape
    C = ctx.shape[1]
    RC, RX = B * C, B * L
    R = RC + RX
    assert C % SEQ_BLOCK == 0 and L % SEQ_BLOCK == 0
    geo = dict(n_batch=B, n_ctx_blk=C // SEQ_BLOCK, n_x_blk=L // SEQ_BLOCK)
    geo_ssd = dict(n_batch=B, n_ctx_blk=C // SSD_CHUNK, n_x_blk=L // SSD_CHUNK)
    group_all = lambda r0: jnp.where(r0 < RC, 0, 1 + (r0 - RC) // L)
    group_x = lambda r0: 1 + r0 // L

    cond = jax.nn.silu(jnp.concatenate([c_ctx[None, :], c], axis=0))
    cond = jnp.pad(cond, ((0, SUBLANES - (B + 1) % SUBLANES), (0, 0))) if (B + 1) % SUBLANES else cond
    n_groups = cond.shape[0]

    def modulation(i):
        mod = _mm(cond, ada_w, b_layer=i, name="ada") + ada_b[i]
        return [m.reshape(n_groups, 1, D) for m in jnp.split(mod, 6, axis=-1)]

    sh1, sc1, g1, sh2, sc2, g2 = modulation(0)
    sh1n, sc1n, g1n, sh2n, sc2n, g2n = modulation(1)
    h, xs = _norm_mod_first(ctx.reshape(RC, D), x.reshape(RX, D), norm1_g[0], sh1, sc1, group_all)
    pad_cols = (-RWKV_COLS) % LANES
    w_in = ab_w_in[0]
    w_in = jnp.concatenate([w_in[:, :RWKV_COLS], jnp.zeros((D, pad_cols), F32), w_in[:, RWKV_COLS:]], axis=1)
    s5_col0 = RWKV_COLS + pad_cols
    p = _mm(h, w_in.astype(BF16), name="ab_in")

    W = RWKV_W
    lora_rows = 2 * LANES
    zw = jnp.zeros((DECAY_LORA, W), F32)
    lora_w = jnp.concatenate([
        jnp.concatenate([rwkv_w_up[0, 0], rwkv_w_up[0, 1], zw, zw], axis=1),
        jnp.concatenate([zw, zw, rwkv_a_up[0, 0], rwkv_a_up[0, 1]], axis=1),
        jnp.zeros((lora_rows - DECAY_LORA - AAA_LORA, 4 * W), F32)], axis=0).astype(BF16)
    gate_w = jnp.pad(rwkv_g_up[0], ((DECAY_LORA + AAA_LORA, pad_cols), (0, 0))).astype(BF16)
    mu = jnp.pad(rwkv_mu[0], ((0, 0), (0, pad_cols)))
    vec = lambda t: t.reshape(1, W)
    r, v, kk, g_gate, bonus, lw, kd, be = _rwkv_pre(
        p, mu, lora_w, gate_w, rwkv_w0[0], rwkv_a0[0], vec(rwkv_k_k[0]), vec(rwkv_k_a[0]), vec(rwkv_r_k[0]),
        geo=(RC, C, L))
    y_dirs = [_rwkv_scan(r, v, kk, lw, kd, be, reverse=(d == 1), **geo) for d in range(2)]
    mix = _rwkv_post(y_dirs[0], y_dirs[1], bonus, g_gate, vec(rwkv_ln_g[0]), vec(rwkv_ln_b[0]), W + S5_W)

    ys_dirs = []
    for d in range(2):
        bw, cw, tab = _s5_weights(s5_A_re[0, d], s5_A_im[0, d], s5_log_dt[0, d], s5_B_re[0, d], s5_B_im[0, d],
                                  s5_C_re[0, d], s5_C_im[0, d])
        ys_dirs.append(_s5_scan(p, s5_col0 // LANES, bw, cw, tab, reverse=(d == 1), **geo))
    mix = _s5_post(p, s5_col0, ys_dirs[0], ys_dirs[1], vec(s5_D[0]), _layer_bf16(s5_glu_w, 0),
                   vec(s5_glu_b[0]), mix)
    xs, h2 = _mm_res_norm(mix, _layer_bf16(ab_w_out, 0), xs, g1, norm2_g[0], sh2, sc2, group_all, name="ab_out")
    m1 = _mm(h2, _layer_bf16(mlp_w1, 0), out_dtype=BF16, act="relu2", name="mlp_up")
    xs = _mm(m1, _layer_bf16(mlp_w2, 0), res=xs, gate=g2, gate_idx=group_all, name="mlp_down")
    h = _norm_mod(xs, norm1_g[1], sh1n, sc1n, group_all)

    sh1, sc1, g1, sh2, sc2, g2 = sh1n, sc1n, g1n, sh2n, sc2n, g2n
    cd_in = cd_w_in[0]
    o_dt = SSD_W + XBC_W
    o_q = o_dt + 2 * SSD_HEADS
    n_qkv = ATT_Q_W + 2 * ATT_KV_W
    cd_in = jnp.concatenate([cd_in[:, :o_dt], cd_in[:, o_q:o_q + n_qkv], cd_in[:, o_dt:o_q]], axis=1)
    cd_in = jnp.pad(cd_in, ((0, 0), (0, (-cd_in.shape[1]) % 512)))
    q_col = o_dt
    k_col = q_col + ATT_Q_W
    v_col = k_col + ATT_KV_W
    dt_col = v_col + ATT_KV_W
    p = _mm(h, cd_in.astype(BF16), name="cd_in")

    xbc = _cd_conv(p, SSD_W, XBC_W, ssd_conv_w[0], ssd_conv_b[0], geo=(RC, C, L))
    cos, sin = _rope_tables(L)
    lane_pad = lambda t: jnp.pad(t.reshape(1, 2 * SSD_HEADS), ((0, 0), (0, LANES - 2 * SSD_HEADS)))
    q_rot, k_rot, dt, dta = _cd_rope_dt(
        p, q_col, k_col, dt_col, jnp.tile(cos, (1, LANES // ATT_HD)), jnp.tile(sin, (1, LANES // ATT_HD)),
        lane_pad(ssd_dt_bias[0]), lane_pad(-jnp.exp(ssd_A_log[0])), n_ctx_rows=RC, seq_len=L)
    y_dirs = [_ssd_scan(xbc, dt, dta, reverse=(d == 1), **geo_ssd) for d in range(2)]
    mix = _ssd_post(y_dirs[0], y_dirs[1], xbc, p, jnp.repeat(ssd_D[0], SSD_HD).reshape(1, SSD_W),
                    ssd_norm_g[0].reshape(1, SSD_W), SSD_W + ATT_Q_W, n_ctx_rows=RC)
    mix = _window_attn(q_rot, k_rot, p, v_col, attn_sink[0].reshape(1, ATT_HEADS), mix, SSD_W,
                       n_batch=B, n_ctx=C, seq_len=L)
    xo, h2 = _mm_res_norm(mix, _layer_bf16(cd_w_out, 0), xs, g1, norm2_g[1], sh2, sc2, group_x, res_row0=RC,
                          name="cd_out")
    m1 = _mm(h2, _layer_bf16(mlp_w1, 1), out_dtype=BF16, act="relu2", name="mlp_up")
    xo = _mm(m1, _layer_bf16(mlp_w2, 1), res=xo, gate=g2, gate_idx=group_x, name="mlp_down")
    zero = jnp.zeros((1, 1, D), F32)
    out = _norm_mod(xo, final_g, zero, zero, lambda r0: 0, out_dtype=F32)
    return out.reshape(B, L, D)
```

```python
import functools
import math

import jax
import jax.numpy as jnp
from jax import lax
from jax.experimental import pallas as pl
from jax.experimental.pallas import tpu as pltpu

F32 = jnp.float32
BF16 = jnp.bfloat16

EPS = 1e-6
GN_EPS = 64e-5
GRID_W = 64
ROPE_BASE = 10000.0

RWKV_HEADS = 16
RWKV_HD = 64
RWKV_W = RWKV_HEADS * RWKV_HD
DECAY_LORA = 96
AAA_LORA = 96
GATE_LORA = 256
RWKV_COLS = 3 * RWKV_W + DECAY_LORA + AAA_LORA + GATE_LORA
S5_W = 1024
S5_P = 16
S5_G = S5_W // S5_P
S5_N = 64
SSD_HEADS = 32
SSD_HD = 64
SSD_W = SSD_HEADS * SSD_HD
SSD_GROUPS = 4
SSD_N = 128
SSD_CHUNK = 128
XBC_W = SSD_W + 2 * SSD_GROUPS * SSD_N
ATT_HEADS = 16
KV_HEADS = 4
ATT_HD = 64
ATT_Q_W = ATT_HEADS * ATT_HD
ATT_KV_W = KV_HEADS * ATT_HD
WINDOW = 128
ATT_BLOCK = 128

LANES = 128
SUBLANES = 8
SEQ_BLOCK = 256
RWKV_CHUNK = 64
S5_GB = 8
VMEM_LIMIT = 48 * 1024 * 1024
VMEM_INTERNAL = 6 * 1024 * 1024
CAST_BLOCK_BYTES = 8 * 1024 * 1024
STAGE_ROWS = (512, 256, 128, 64, 32, 16, 8)


def _cparams(sem):
    return pltpu.CompilerParams(dimension_semantics=sem, vmem_limit_bytes=VMEM_LIMIT)


def _pick(n, cands):
    for c in cands:
        if n % c == 0:
            return c
    return n


def _dot(a, b):
    return jnp.dot(a.astype(BF16), b.astype(BF16), preferred_element_type=F32)


def _split3(x):
    hi = x.astype(BF16)
    r1 = x - hi.astype(F32)
    mid = r1.astype(BF16)
    lo = (r1 - mid.astype(F32)).astype(BF16)
    return hi, mid, lo


def _dot_ones_x(tri, x):
    hi, mid, lo = _split3(x)
    f = lambda t: jnp.dot(tri, t, preferred_element_type=F32)
    return f(hi) + f(mid) + f(lo)


def _dot_x_ones_nt(x, tri):
    hi, mid, lo = _split3(x)
    f = lambda t: lax.dot_general(t, tri, (((1,), (1,)), ((), ())), preferred_element_type=F32)
    return f(hi) + f(mid) + f(lo)


def _mm_body(*refs, nk, act, residual):
    if nk > 1:
        *refs, acc_ref = refs
    if residual:
        a_ref, b_ref, res_ref, gate_ref, o_ref = refs
    else:
        a_ref, b_ref, o_ref = refs
    k = pl.program_id(2)
    part = jnp.dot(a_ref[...].astype(BF16), b_ref[...].astype(BF16), preferred_element_type=F32)

    def finish(acc):
        if act == "relu2":
            acc = jnp.square(jnp.maximum(acc, 0.0))
        if residual:
            acc = res_ref[...] + gate_ref[0] * acc
        o_ref[...] = acc.astype(o_ref.dtype)

    if nk == 1:
        finish(part)
    else:
        @pl.when(k == 0)
        def _():
            acc_ref[...] = part

        @pl.when(k > 0)
        def _():
            acc_ref[...] += part

        @pl.when(k == nk - 1)
        def _():
            finish(acc_ref[...])


def _mm(a, b, *, b_layer=0, out_dtype=F32, act=None, res=None, res_row0=0, gate=None, gate_idx=None, name="mm"):
    M, K = a.shape
    N = b.shape[-1]
    residual = res is not None
    a_bytes, b_bytes, o_bytes = a.dtype.itemsize, b.dtype.itemsize, jnp.dtype(out_dtype).itemsize

    def vmem_bytes(tm, tn, tk):
        blocks = tm * tk * a_bytes + tk * tn * b_bytes + tm * tn * o_bytes + (tm * tn * 4 if residual else 0)
        return 2 * blocks + tm * tn * 4

    fits = [(tk == K, tm * tn, tm, tn, tk)
            for tk in (K, 4096, 2048, 1024, 512, 256, 128) if K % tk == 0
            for tn in (2048, 1792, 1536, 1024, 512, 384, 256, 128) if N % tn == 0
            for tm in (1024, 512, 256, 128, 64, 32, 16, 8) if M % tm == 0
            and vmem_bytes(tm, tn, tk) <= VMEM_LIMIT - VMEM_INTERNAL]
    _, _, tm, tn, tk = max(fits)
    nk = K // tk
    if b.ndim == 3:
        b_spec = pl.BlockSpec((None, tk, tn), lambda i, j, k: (b_layer, k, j))
    else:
        b_spec = pl.BlockSpec((tk, tn), lambda i, j, k: (k, j))
    in_specs = [pl.BlockSpec((tm, tk), lambda i, j, k: (i, k)), b_spec]
    args = [a, b]
    if residual:
        if res_row0 % tm:
            res, res_row0 = res[res_row0:res_row0 + M], 0
        r0 = res_row0 // tm
        in_specs += [pl.BlockSpec((tm, tn), lambda i, j, k: (r0 + i, j)),
                     pl.BlockSpec((1, 1, tn), lambda i, j, k: (gate_idx(i * tm), 0, j))]
        args += [res, gate]
    return pl.pallas_call(
        functools.partial(_mm_body, nk=nk, act=act, residual=residual),
        grid=(M // tm, N // tn, nk),
        in_specs=in_specs,
        out_specs=pl.BlockSpec((tm, tn), lambda i, j, k: (i, j)),
        out_shape=jax.ShapeDtypeStruct((M, N), out_dtype),
        scratch_shapes=[pltpu.VMEM((tm, tn), F32)] if nk > 1 else [],
        compiler_params=_cparams(("parallel", "parallel", "arbitrary")),
        name=name,
    )(*args)


def _cast_body(w_ref, o_ref):
    o_ref[...] = w_ref[...].astype(o_ref.dtype)


def _layer_bf16(w, layer):
    _, K, N = w.shape
    tk = next(t for t in (2048, 1024, 512, 256, 128, 64, 32, 16) if K % t == 0 and t * N * 4 <= CAST_BLOCK_BYTES)
    return pl.pallas_call(
        _cast_body,
        grid=(K // tk,),
        in_specs=[pl.BlockSpec((None, tk, N), lambda i: (layer, i, 0))],
        out_specs=pl.BlockSpec((tk, N), lambda i: (i, 0)),
        out_shape=jax.ShapeDtypeStruct((K, N), BF16),
        compiler_params=_cparams(("parallel",)),
        name="weight_bf16",
    )(w)


def _mm_res_norm_body(a_ref, b_ref, res_ref, gate_ref, g_ref, sh_ref, sc_ref, *out_refs, nk, keep_x):
    if nk > 1:
        *out_refs, acc_ref = out_refs
    h_ref = out_refs[-1]
    k = pl.program_id(1)
    part = jnp.dot(a_ref[...], b_ref[...], preferred_element_type=F32)

    def finish(acc):
        x = res_ref[...] + gate_ref[0] * acc
        if keep_x:
            out_refs[0][...] = x
        ms = jnp.mean(x * x, axis=-1, keepdims=True)
        xn = x * lax.rsqrt(ms + EPS) * g_ref[...]
        h_ref[...] = (xn * (1.0 + sc_ref[0]) + sh_ref[0]).astype(h_ref.dtype)

    if nk == 1:
        finish(part)
        return

    @pl.when(k == 0)
    def _():
        acc_ref[...] = part

    @pl.when(k > 0)
    def _():
        acc_ref[...] += part

    @pl.when(k == nk - 1)
    def _():
        finish(acc_ref[...])


def _mm_res_norm(a, b, res, gate, norm_g, shift, scale, group_idx, *, res_row0=0, keep_x=True, h_dtype=BF16,
                 name="mm_res_norm"):
    M, K = a.shape
    N = b.shape[1]
    h_bytes = jnp.dtype(h_dtype).itemsize

    def vmem_bytes(tm, tk):
        blocks = tm * tk * 2 + tk * N * 2 + tm * N * 4 + (tm * N * 4 if keep_x else 0) + tm * N * h_bytes
        return 2 * blocks + (tm * N * 4 if tk < K else 0)

    tm, tk = next(((tm, tk) for tk in (K, 1024, 512, 256, 128) if K % tk == 0
                   for tm in (512, 256, 128, 64, 32, 16, 8) if M % tm == 0
                   and vmem_bytes(tm, tk) <= VMEM_LIMIT - VMEM_INTERNAL), (8, 128))
    nk = K // tk
    if res_row0 % tm:
        res, res_row0 = res[res_row0:res_row0 + M], 0
    r0 = res_row0 // tm
    grp = lambda i, k: (group_idx(i * tm), 0, 0)
    row = pl.BlockSpec((tm, N), lambda i, k: (i, 0))
    out_specs = ([row] if keep_x else []) + [row]
    out_shape = ([jax.ShapeDtypeStruct((M, N), F32)] if keep_x else []) + [jax.ShapeDtypeStruct((M, N), h_dtype)]
    out = pl.pallas_call(
        functools.partial(_mm_res_norm_body, nk=nk, keep_x=keep_x),
        grid=(M // tm, nk),
        in_specs=[pl.BlockSpec((tm, tk), lambda i, k: (i, k)),
                  pl.BlockSpec((tk, N), lambda i, k: (k, 0)),
                  pl.BlockSpec((tm, N), lambda i, k: (r0 + i, 0)),
                  pl.BlockSpec((1, 1, N), grp),
                  pl.BlockSpec((1, N), lambda i, k: (0, 0)),
                  pl.BlockSpec((1, 1, N), grp), pl.BlockSpec((1, 1, N), grp)],
        out_specs=out_specs,
        out_shape=out_shape,
        scratch_shapes=[pltpu.VMEM((tm, N), F32)] if nk > 1 else [],
        compiler_params=_cparams(("parallel", "arbitrary")),
        name=name,
    )(a, b, res, gate, norm_g.reshape(1, N), shift, scale)
    return out if keep_x else out[0]


def _norm_body(x_ref, g_ref, sh_ref, sc_ref, o_ref):
    x = x_ref[...]
    ms = jnp.mean(x * x, axis=-1, keepdims=True)
    xn = x * lax.rsqrt(ms + EPS) * g_ref[...]
    o_ref[...] = (xn * (1.0 + sc_ref[0]) + sh_ref[0]).astype(o_ref.dtype)


def _norm_mod(x, g, shift, scale, group_idx, out_dtype=BF16):
    M, D = x.shape
    tm = _pick(M, STAGE_ROWS)
    return pl.pallas_call(
        _norm_body,
        grid=(M // tm,),
        in_specs=[pl.BlockSpec((tm, D), lambda i: (i, 0)),
                  pl.BlockSpec((1, D), lambda i: (0, 0)),
                  pl.BlockSpec((1, 1, D), lambda i: (group_idx(i * tm), 0, 0)),
                  pl.BlockSpec((1, 1, D), lambda i: (group_idx(i * tm), 0, 0))],
        out_specs=pl.BlockSpec((tm, D), lambda i: (i, 0)),
        out_shape=jax.ShapeDtypeStruct((M, D), out_dtype),
        compiler_params=_cparams(("parallel",)),
        name="norm_mod",
    )(x, g.reshape(1, D), shift, scale)


def _norm_first_body(c_ref, x_ref, g_ref, sh_ref, sc_ref, h_ref, xs_ref, *, n_ctx_tiles):
    x = jnp.where(pl.program_id(0) < n_ctx_tiles, c_ref[...], x_ref[...])
    xs_ref[...] = x
    ms = jnp.mean(x * x, axis=-1, keepdims=True)
    xn = x * lax.rsqrt(ms + EPS) * g_ref[...]
    h_ref[...] = (xn * (1.0 + sc_ref[0]) + sh_ref[0]).astype(h_ref.dtype)


def _norm_mod_first(ctx_rows, x_rows, g, shift, scale, group_idx):
    RC, D = ctx_rows.shape
    RX = x_rows.shape[0]
    tm = _pick(math.gcd(RC, RX), STAGE_ROWS)
    nc = RC // tm
    grp = lambda i: (group_idx(i * tm), 0, 0)
    row = pl.BlockSpec((tm, D), lambda i: (i, 0))
    return pl.pallas_call(
        functools.partial(_norm_first_body, n_ctx_tiles=nc),
        grid=((RC + RX) // tm,),
        in_specs=[pl.BlockSpec((tm, D), lambda i: (jnp.minimum(i, nc - 1), 0)),
                  pl.BlockSpec((tm, D), lambda i: (jnp.maximum(i - nc, 0), 0)),
                  pl.BlockSpec((1, D), lambda i: (0, 0)),
                  pl.BlockSpec((1, 1, D), grp), pl.BlockSpec((1, 1, D), grp)],
        out_specs=[row, row],
        out_shape=[jax.ShapeDtypeStruct((RC + RX, D), BF16), jax.ShapeDtypeStruct((RC + RX, D), F32)],
        compiler_params=_cparams(("parallel",)),
        name="norm_first",
    )(ctx_rows, x_rows, g.reshape(1, D), shift, scale)


def _seq_block(b, s, *, n_ctx_blk, n_x_blk, n_batch, reverse):
    if reverse:
        c = b * n_ctx_blk + (n_ctx_blk - 1 - s)
        x = n_batch * n_ctx_blk + b * n_x_blk + (n_x_blk - 1 - (s - n_ctx_blk))
    else:
        c = b * n_ctx_blk + s
        x = n_batch * n_ctx_blk + b * n_x_blk + (s - n_ctx_blk)
    return jnp.where(s < n_ctx_blk, c, x)


def _split2(x):
    hi = x.astype(BF16)
    lo = (x - hi.astype(F32)).astype(BF16)
    return hi, lo


def _rwkv_chunk(lw, r, v, kk, kd, be, st, tri, strict, incl, same_head, head0, eye):
    C = RWKV_CHUNK

    def stack(z):
        return jnp.concatenate([jnp.where(head0, z, 0.0), jnp.where(head0, 0.0, z)], axis=0)

    def unstack(z):
        return z[0:C] + z[C:2 * C]

    bdot = lambda a, b: jnp.dot(a, b, preferred_element_type=F32)
    bdot_nt = lambda a, b: lax.dot_general(a, b, (((1,), (1,)), ((), ())), preferred_element_type=F32)
    bdot_tn = lambda a, b: lax.dot_general(a, b, (((0,), (0,)), ((), ())), preferred_element_type=F32)
    cat = lambda *zs: jnp.concatenate(zs, axis=0)
    P = range(len(lw))
    cum = [_dot_ones_x(tri, lw[p]) for p in P]
    total = [jnp.sum(lw[p], axis=0, keepdims=True) for p in P]
    a_t = [kk[p] * jnp.exp(cum[p] - lw[p]) for p in P]
    r_t = [r[p] * jnp.exp(cum[p]) for p in P]
    a_sp = [_split2(a_t[p]) for p in P]
    r_hi = [r_t[p].astype(BF16) for p in P]
    sa_sp = [_split2(stack(a_t[p])) for p in P]
    sr_hi = [stack(r_t[p]).astype(BF16) for p in P]
    pinv = [jnp.exp(-cum[p]) for p in P]
    bk_sp = [_split2(cat(stack(be[p] * pinv[p]), stack(kd[p] * pinv[p]))) for p in P]
    g1 = [bdot_nt(cat(sa_sp[p][0], sa_sp[p][1], sr_hi[p]), bk_sp[p][0]) for p in P]
    g2 = [bdot_nt(sa_sp[p][0], bk_sp[p][1]) for p in P]
    n_all = [g1[p][0:2 * C] + g1[p][2 * C:4 * C] + g2[p] for p in P]
    x_neg = [jnp.where(strict, -n_all[p][:, 0:2 * C], 0.0) for p in P]
    n_ak = [jnp.where(strict, n_all[p][:, 2 * C:4 * C], 0.0) for p in P]
    m_r = [jnp.concatenate([jnp.where(incl, g1[p][4 * C:6 * C, 0:2 * C], 0.0),
                            jnp.where(incl, g1[p][4 * C:6 * C, 2 * C:4 * C], 0.0)], axis=1).astype(BF16)
           for p in P]
    lcat = lambda *zs: jnp.concatenate(zs, axis=1)
    W2 = 2 * C
    t_inv = [eye + x_neg[p] for p in P]
    xb = [x_neg[p].astype(BF16) for p in P]
    xb = [bdot(xb[p], xb[p]).astype(BF16) for p in P]
    n_pow = int(math.log2(C))
    for it in range(1, n_pow):
        if it + 1 < n_pow:
            both = [bdot(xb[p], lcat(xb[p], t_inv[p].astype(BF16))) for p in P]
            xb = [both[p][:, 0:W2].astype(BF16) for p in P]
            t_inv = [t_inv[p] + both[p][:, W2:2 * W2] for p in P]
        else:
            t_inv = [t_inv[p] + bdot(xb[p], t_inv[p].astype(BF16)) for p in P]
    t_inv = [t_inv[p].astype(BF16) for p in P]
    st_sp = [_split2(st[p]) for p in P]
    as1 = [bdot_nt(a_sp[p][0], cat(st_sp[p][0], st_sp[p][1])) for p in P]
    as2 = [bdot_nt(cat(a_sp[p][1], r_hi[p]), st_sp[p][0]) for p in P]
    v_sp = [_split2(stack(v[p])) for p in P]
    nk_sp = [_split2(n_ak[p]) for p in P]
    nv1 = [bdot(lcat(nk_sp[p][0], nk_sp[p][1]), cat(v_sp[p][0], v_sp[p][0])) for p in P]
    nv2 = [bdot(nk_sp[p][0], v_sp[p][1]) for p in P]
    rhs = [-(stack(as1[p][:, 0:W2] + as1[p][:, W2:2 * W2] + as2[p][0:C]) + nv1[p] + nv2[p]) for p in P]
    u0 = [bdot(t_inv[p], rhs[p].astype(BF16)) for p in P]
    u0_sp = [_split2(u0[p]) for p in P]
    x_sp = [_split2(x_neg[p]) for p in P]
    xu1 = [bdot(lcat(x_sp[p][0], x_sp[p][1]), cat(u0_sp[p][0], u0_sp[p][0])) for p in P]
    xu2 = [bdot(x_sp[p][0], u0_sp[p][1]) for p in P]
    res = [(rhs[p] - u0[p] + xu1[p] + xu2[p]).astype(BF16) for p in P]
    u = [u0[p] + bdot(t_inv[p], res[p]) for p in P]
    y = [stack(as2[p][C:2 * C]) + bdot(m_r[p], cat(u[p].astype(BF16), v_sp[p][0])) for p in P]
    uv_sp = [_split2(cat(unstack(u[p]), v[p])) for p in P]
    pend = [jnp.exp(total[p] - cum[p]) for p in P]
    e_sp = [_split2(cat(be[p] * pend[p], kd[p] * pend[p])) for p in P]
    upd = [bdot_tn(cat(uv_sp[p][0], uv_sp[p][1], uv_sp[p][0]), cat(e_sp[p][0], e_sp[p][0], e_sp[p][1]))
           for p in P]
    st_new = [st[p] * jnp.exp(total[p]) + jnp.where(same_head, upd[p], 0.0) for p in P]
    return [unstack(y[p]) for p in P], st_new


def _rwkv_body(r_ref, v_ref, kk_ref, lw_ref, kd_ref, be_ref, y_ref, st_ref, *, reverse, npair):
    C = RWKV_CHUNK
    nchunk = SEQ_BLOCK // C
    s = pl.program_id(2)

    @pl.when(s == 0)
    def _():
        st_ref[...] = jnp.zeros_like(st_ref)

    sgn = -1 if reverse else 1
    row = lax.broadcasted_iota(jnp.int32, (2 * C, 2 * C), 0)
    col = lax.broadcasted_iota(jnp.int32, (2 * C, 2 * C), 1)
    diff = (row - col) * sgn
    strict = diff > 0
    incl = diff >= 0
    same_head = (row // C) == (col // C)
    eye = jnp.where(row == col, 1.0, 0.0)
    r64 = lax.broadcasted_iota(jnp.int32, (C, C), 0)
    c64 = lax.broadcasted_iota(jnp.int32, (C, C), 1)
    tri = jnp.where((r64 - c64) * sgn >= 0, 1.0, 0.0).astype(BF16)
    head0 = lax.broadcasted_iota(jnp.int32, (C, 2 * C), 1) < C

    def chunk(ci, carry):
        ce = (nchunk - 1 - ci) if reverse else ci
        sl = pl.ds(pl.multiple_of(ce * C, C), C)
        lanes = [slice(p * LANES, (p + 1) * LANES) for p in range(npair)]
        get = lambda ref: [ref[sl, ln] for ln in lanes]
        ys, sts = _rwkv_chunk(get(lw_ref), get(r_ref), get(v_ref), get(kk_ref), get(kd_ref), get(be_ref),
                              [st_ref[p] for p in range(npair)], tri, strict, incl, same_head, head0, eye)
        for p in range(npair):
            y_ref[sl, lanes[p]] = ys[p]
            st_ref[p] = sts[p]
        return carry

    lax.fori_loop(0, nchunk, chunk, 0)


def _rwkv_scan(r, v, kk, lw, kd, be, *, n_batch, n_ctx_blk, n_x_blk, reverse, npair=8):
    R, W = r.shape
    ngrp = W // (LANES * npair)
    ns = n_ctx_blk + n_x_blk
    d = 1 if reverse else 0
    blk = functools.partial(_seq_block, n_ctx_blk=n_ctx_blk, n_x_blk=n_x_blk, n_batch=n_batch, reverse=reverse)
    spec = pl.BlockSpec((SEQ_BLOCK, LANES * npair), lambda b, p, s: (blk(b, s), p))
    dspec = pl.BlockSpec((None, SEQ_BLOCK, LANES * npair), lambda b, p, s: (d, blk(b, s), p))
    return pl.pallas_call(
        functools.partial(_rwkv_body, reverse=reverse, npair=npair),
        grid=(n_batch, ngrp, ns),
        in_specs=[spec] * 3 + [dspec] * 3,
        out_specs=spec,
        out_shape=jax.ShapeDtypeStruct((R, W), F32),
        scratch_shapes=[pltpu.VMEM((npair, LANES, LANES), F32)],
        compiler_params=_cparams(("parallel", "parallel", "arbitrary")),
        name="rwkv_scan_bwd" if reverse else "rwkv_scan_fwd",
    )(r, v, kk, lw, kd, be)


S5_SEG = SEQ_BLOCK // SUBLANES
S5_TAB_SLABS = 2 * S5_SEG + 2 + 8
S5_NGB = 4


def _s5_body(u_ref, bw_ref, cw_ref, tab_ref, y_ref, hb_ref, carry_ref, perm_ref, *, reverse):
    W = S5_GB * S5_N
    T = SEQ_BLOCK
    G = range(S5_NGB)
    s = pl.program_id(2)

    @pl.when(s == 0)
    def _():
        carry_ref[...] = jnp.zeros_like(carry_ref)
        prow = lax.broadcasted_iota(jnp.int32, (T, T), 0)
        pcol = lax.broadcasted_iota(jnp.int32, (T, T), 1)
        t_of_row = S5_SEG * (prow % SUBLANES) + prow // SUBLANES
        if reverse:
            t_of_row = T - 1 - t_of_row
        perm_ref[...] = jnp.where(pcol == t_of_row, 1.0, 0.0).astype(BF16)

    perm = perm_ref[...]
    u_p = jnp.dot(perm, u_ref[...].astype(BF16), preferred_element_type=F32).astype(BF16)
    for g in G:
        hb_ref[g] = jnp.dot(u_p[:, g * LANES:(g + 1) * LANES], bw_ref[g], preferred_element_type=F32)

    slab = lambda j: pl.ds(j * SUBLANES, SUBLANES)
    tab = lambda g, n: tab_ref[g, slab(n), :]
    o_seg = 2 * S5_SEG
    o_pow = o_seg + 2
    re, im = slice(0, W), slice(W, 2 * W)
    a = [(tab(g, o_pow), tab(g, o_pow + 1)) for g in G]
    h = [(hb_ref[g, slab(0), re], hb_ref[g, slab(0), im]) for g in G]
    for j in range(1, S5_SEG):
        h = [(hb_ref[g, slab(j), re] + a[g][0] * h[g][0] - a[g][1] * h[g][1],
              hb_ref[g, slab(j), im] + a[g][0] * h[g][1] + a[g][1] * h[g][0]) for g in G]
        for g in G:
            hb_ref[g, slab(j), re] = h[g][0]
            hb_ref[g, slab(j), im] = h[g][1]
    rows8 = lax.broadcasted_iota(jnp.int32, (SUBLANES, W), 0)
    ent = []
    for g in G:
        gr, gi = h[g]
        for n, k in enumerate((1, 2, 4)):
            pr, pi = tab(g, o_pow + 2 + 2 * n), tab(g, o_pow + 3 + 2 * n)
            sr = jnp.where(rows8 >= k, pltpu.roll(gr, k, axis=0), 0.0)
            si = jnp.where(rows8 >= k, pltpu.roll(gi, k, axis=0), 0.0)
            gr, gi = gr + pr * sr - pi * si, gi + pr * si + pi * sr
        cr, ci = carry_ref[g, :, re], carry_ref[g, :, im]
        qr, qi = tab(g, o_seg), tab(g, o_seg + 1)
        gr, gi = gr + qr * cr - qi * ci, gi + qr * ci + qi * cr
        ent.append((jnp.where(rows8 >= 1, pltpu.roll(gr, 1, axis=0), cr),
                    jnp.where(rows8 >= 1, pltpu.roll(gi, 1, axis=0), ci)))
        carry_ref[g, :, re] = jnp.broadcast_to(gr[SUBLANES - 1:SUBLANES, :], (SUBLANES, W))
        carry_ref[g, :, im] = jnp.broadcast_to(gi[SUBLANES - 1:SUBLANES, :], (SUBLANES, W))
    for g in G:
        in_r, in_i = ent[g]
        for j in range(S5_SEG):
            pr, pi = tab(g, j), tab(g, S5_SEG + j)
            hb_ref[g, slab(j), re] = hb_ref[g, slab(j), re] + pr * in_r - pi * in_i
            hb_ref[g, slab(j), im] = hb_ref[g, slab(j), im] + pr * in_i + pi * in_r
    y_p = jnp.concatenate([jnp.dot(hb_ref[g].astype(BF16), cw_ref[g], preferred_element_type=F32) for g in G],
                          axis=1)
    y_ref[...] = _dot_x_ones_tn(perm, y_p)


def _dot_x_ones_tn(ones, x):
    hi, mid, lo = _split3(x)
    f = lambda t: lax.dot_general(ones, t, (((0,), (0,)), ((), ())), preferred_element_type=F32)
    return f(hi) + f(mid) + f(lo)


def _s5_scan(p, col0_blk, bw, cw, tab, *, n_batch, n_ctx_blk, n_x_blk, reverse):
    R = p.shape[0]
    n = S5_NGB
    ngb = S5_G // S5_GB // n
    ns = n_ctx_blk + n_x_blk
    blk = functools.partial(_seq_block, n_ctx_blk=n_ctx_blk, n_x_blk=n_x_blk, n_batch=n_batch, reverse=reverse)
    W2 = 2 * S5_GB * S5_N
    assert col0_blk % n == 0
    return pl.pallas_call(
        functools.partial(_s5_body, reverse=reverse),
        grid=(n_batch, ngb, ns),
        in_specs=[pl.BlockSpec((SEQ_BLOCK, n * LANES), lambda b, g, s: (blk(b, s), col0_blk // n + g)),
                  pl.BlockSpec((n, LANES, W2), lambda b, g, s: (g, 0, 0)),
                  pl.BlockSpec((n, W2, LANES), lambda b, g, s: (g, 0, 0)),
                  pl.BlockSpec((n, S5_TAB_SLABS * SUBLANES, W2 // 2), lambda b, g, s: (g, 0, 0))],
        out_specs=pl.BlockSpec((SEQ_BLOCK, n * LANES), lambda b, g, s: (blk(b, s), g)),
        out_shape=jax.ShapeDtypeStruct((R, S5_W), F32),
        scratch_shapes=[pltpu.VMEM((n, SEQ_BLOCK, W2), F32), pltpu.VMEM((n, SUBLANES, W2), F32),
                        pltpu.VMEM((SEQ_BLOCK, SEQ_BLOCK), BF16)],
        compiler_params=_cparams(("parallel", "parallel", "arbitrary")),
        name="s5_scan_bwd" if reverse else "s5_scan_fwd",
    )(p, bw, cw, tab)


def _s5_weights(A_re, A_im, log_dt, B_re, B_im, C_re, C_im):
    lam_re = jnp.minimum(A_re, -1e-4)
    dt = jnp.exp(log_dt)[:, None]
    mag = jnp.exp(lam_re * dt)
    ab_re, ab_im = mag * jnp.cos(A_im * dt), mag * jnp.sin(A_im * dt)
    den = lam_re * lam_re + A_im * A_im
    f_re = ((ab_re - 1) * lam_re + ab_im * A_im) / den
    f_im = (ab_im * lam_re - (ab_re - 1) * A_im) / den
    bb_re = f_re[..., None] * B_re - f_im[..., None] * B_im
    bb_im = f_re[..., None] * B_im + f_im[..., None] * B_re
    ngb = S5_G // S5_GB
    eye = jnp.eye(S5_GB, dtype=F32)

    def in_block(bb):
        t = bb.reshape(ngb, S5_GB, S5_N, S5_P)
        return jnp.einsum("agnp,gh->agphn", t, eye).reshape(ngb, S5_GB * S5_P, S5_GB * S5_N)

    def out_block(cc):
        t = cc.reshape(ngb, S5_GB, S5_P, S5_N)
        return jnp.einsum("agpn,gh->agnhp", t, eye).reshape(ngb, S5_GB * S5_N, S5_GB * S5_P)

    bw = jnp.concatenate([in_block(bb_re), in_block(bb_im)], axis=2).astype(BF16)
    cw = jnp.concatenate([out_block(C_re), -out_block(C_im)], axis=1).astype(BF16)

    def power(k):
        m = jnp.exp(k * lam_re * dt)
        return m * jnp.cos(k * A_im * dt), m * jnp.sin(k * A_im * dt)

    rep = lambda t: jnp.broadcast_to(t[None], (SUBLANES,) + t.shape)
    slab_pow = [power(float(j + 1)) for j in range(S5_SEG)]
    seg_pow = [power(float(S5_SEG * (i + 1))) for i in range(SUBLANES)]
    slabs = ([rep(c[0]) for c in slab_pow] + [rep(c[1]) for c in slab_pow]
             + [jnp.stack([c[0] for c in seg_pow]), jnp.stack([c[1] for c in seg_pow])])
    for k in (1.0, float(S5_SEG), 2.0 * S5_SEG, 4.0 * S5_SEG):
        slabs += [rep(t) for t in power(k)]
    tab = jnp.concatenate(slabs, axis=0)
    tab = tab.reshape(S5_TAB_SLABS * SUBLANES, ngb, S5_GB * S5_N).transpose(1, 0, 2)
    return bw, cw, tab


def _ssd_body(x_ref, b_ref, c_ref, dt_ref, dta_ref, sel_ref, y_ref, st_ref, *, reverse):
    Q = SSD_CHUNK
    s = pl.program_id(1)

    @pl.when(s == 0)
    def _():
        st_ref[...] = jnp.zeros_like(st_ref)

    row = lax.broadcasted_iota(jnp.int32, (Q, Q), 0)
    col = lax.broadcasted_iota(jnp.int32, (Q, Q), 1)
    mask = (col >= row) if reverse else (col <= row)
    tri = jnp.where(mask, 1.0, 0.0).astype(BF16)
    head0 = lax.broadcasted_iota(jnp.int32, (Q, LANES), 1) < SSD_HD
    head0_rows = lax.broadcasted_iota(jnp.int32, (LANES, SSD_N), 0) < SSD_HD

    h_lo = SSD_HEADS if reverse else 0
    dt = dt_ref[:, h_lo:h_lo + SSD_HEADS]
    dta = dta_ref[:, h_lo:h_lo + SSD_HEADS]
    dtt = dt_ref[...].T[h_lo:h_lo + SSD_HEADS, :]
    dtat = dta_ref[...].T[h_lo:h_lo + SSD_HEADS, :]
    acum = _dot_ones_x(tri, dta)
    acum_t = _dot_x_ones_nt(dtat, tri)
    total = jnp.sum(dta, axis=0, keepdims=True)
    total_t = jnp.sum(dtat, axis=1, keepdims=True)
    e_tot_t = jnp.exp(total_t)
    dt_wide = jnp.dot(dt.astype(BF16), sel_ref[...], preferred_element_type=F32)
    heads_per_group = SSD_HEADS // SSD_GROUPS
    G = range(SSD_GROUPS)
    pairs = [(g, g * heads_per_group + 2 * pr) for g in G for pr in range(heads_per_group // 2)]
    lanes2 = lambda h0: slice(h0 * SSD_HD, (h0 + 2) * SSD_HD)
    bg = [b_ref[:, g * SSD_N:(g + 1) * SSD_N].astype(BF16) for g in G]
    cg = [c_ref[:, g * SSD_N:(g + 1) * SSD_N].astype(BF16) for g in G]
    cb = [lax.dot_general(cg[g], bg[g], (((1,), (1,)), ((), ())), preferred_element_type=F32) for g in G]
    xp = [x_ref[:, lanes2(h0)] for _, h0 in pairs]
    xb = [t.astype(BF16) for t in xp]
    colb = [jnp.broadcast_to(acum[:, h:h + 1], (Q, LANES)) for h in range(SSD_HEADS)]

    def weights(g, h):
        seg = colb[h] - acum_t[h:h + 1, :]
        return (cb[g] * jnp.exp(jnp.where(mask, seg, -jnp.inf)) * dtt[h:h + 1, :]).astype(BF16)

    yd = [(jnp.dot(weights(g, h0), xb[i], preferred_element_type=F32),
           jnp.dot(weights(g, h0 + 1), xb[i], preferred_element_type=F32)) for i, (g, h0) in enumerate(pairs)]
    st = [st_ref[lanes2(h0), :] for _, h0 in pairs]
    yo = [lax.dot_general(cg[g], st[i].astype(BF16), (((1,), (1,)), ((), ())), preferred_element_type=F32)
          for i, (g, _) in enumerate(pairs)]
    for i, (g, h0) in enumerate(pairs):
        scale = jnp.exp(jnp.where(head0, colb[h0], colb[h0 + 1]))
        y_ref[:, lanes2(h0)] = jnp.where(head0, yd[i][0], yd[i][1]) + yo[i] * scale
    xt = [(xp[i] * dt_wide[:, lanes2(h0)]
           * jnp.exp(jnp.where(head0, total[:, h0:h0 + 1] - colb[h0], total[:, h0 + 1:h0 + 2] - colb[h0 + 1]))
           ).astype(BF16) for i, (_, h0) in enumerate(pairs)]
    upd = [lax.dot_general(xt[i], bg[g], (((0,), (0,)), ((), ())), preferred_element_type=F32)
           for i, (g, _) in enumerate(pairs)]
    for i, (_, h0) in enumerate(pairs):
        dec = jnp.where(head0_rows, e_tot_t[h0:h0 + 1, :], e_tot_t[h0 + 1:h0 + 2, :])
        st_ref[lanes2(h0), :] = st[i] * dec + upd[i]


def _ssd_scan(xbc, dt, dta, *, n_batch, n_ctx_blk, n_x_blk, reverse):
    R = xbc.shape[0]
    Q = SSD_CHUNK
    blk = functools.partial(_seq_block, n_ctx_blk=n_ctx_blk, n_x_blk=n_x_blk, n_batch=n_batch, reverse=reverse)
    ns = n_ctx_blk + n_x_blk
    gw = SSD_GROUPS * SSD_N
    row_spec = lambda w, c: pl.BlockSpec((Q, w), lambda b, s: (blk(b, s), c))
    head_of_lane = jnp.arange(SSD_W, dtype=jnp.int32) // SSD_HD
    sel = (jnp.arange(SSD_HEADS, dtype=jnp.int32)[:, None] == head_of_lane[None, :]).astype(BF16)
    return pl.pallas_call(
        functools.partial(_ssd_body, reverse=reverse),
        grid=(n_batch, ns),
        in_specs=[row_spec(SSD_W, 0), row_spec(gw, SSD_W // gw), row_spec(gw, SSD_W // gw + 1),
                  row_spec(LANES, 0), row_spec(LANES, 0), pl.BlockSpec((SSD_HEADS, SSD_W), lambda b, s: (0, 0))],
        out_specs=row_spec(SSD_W, 0),
        out_shape=jax.ShapeDtypeStruct((R, SSD_W), F32),
        scratch_shapes=[pltpu.VMEM((SSD_W, SSD_N), F32)],
        compiler_params=_cparams(("parallel", "arbitrary")),
        name="ssd_scan_bwd" if reverse else "ssd_scan_fwd",
    )(xbc, xbc, xbc, dt, dta, sel)


def _attn_body(q_ref, kp_ref, kc_ref, kn_ref, vp_ref, vc_ref, vn_ref, kx_ref, vx_ref, sink_ref, mix_ref, o_ref,
               band_ref, *, seq_len):
    del mix_ref
    n = pl.program_id(1)
    blk = ATT_BLOCK
    n_ctx = kx_ref.shape[0]
    rep = ATT_HEADS // KV_HEADS
    nk = n_ctx + 3 * blk

    @pl.when(n == 0)
    def _():
        r_q = lax.broadcasted_iota(jnp.int32, (rep * blk, nk), 0) % blk
        j = lax.broadcasted_iota(jnp.int32, (rep * blk, nk), 1)
        ok = (j < n_ctx) | (jnp.abs(r_q - (j - n_ctx - blk)) <= WINDOW)
        band_ref[...] = jnp.where(ok, 0.0, -jnp.inf)

    k_all = jnp.concatenate([kx_ref[...], kp_ref[...], kc_ref[...], kn_ref[...]], axis=0).astype(BF16)
    v_all = jnp.concatenate([vx_ref[...], vp_ref[...], vc_ref[...], vn_ref[...]], axis=0).astype(BF16)
    jb = lax.broadcasted_iota(jnp.int32, (1, nk), 1) - n_ctx
    k_pos = (n - 1) * blk + jb
    outside = (jb >= 0) & ((k_pos < 0) | (k_pos >= seq_len))
    mask = jnp.where(outside, -jnp.inf, band_ref[...])
    scale = ATT_HD ** -0.5
    head_of_row = lax.broadcasted_iota(jnp.int32, (rep * blk, 1), 0) // blk
    G = range(KV_HEADS)
    hd = lambda t, h: t[:, h * ATT_HD:(h + 1) * ATT_HD]
    q = (q_ref[...] * scale).astype(BF16)
    qs = [jnp.concatenate([hd(q, g * rep + i) for i in range(rep)], axis=0) for g in G]
    sinks = []
    for g in G:
        sk = jnp.zeros((rep * blk, 1), F32)
        for i in range(rep):
            sk = jnp.where(head_of_row == i, sink_ref[0:1, g * rep + i:g * rep + i + 1], sk)
        sinks.append(sk)
    sc = [lax.dot_general(qs[g], hd(k_all, g), (((1,), (1,)), ((), ())), preferred_element_type=F32) + mask
          for g in G]
    m = [jnp.maximum(jnp.max(sc[g], axis=-1, keepdims=True), sinks[g]) for g in G]
    e = [jnp.exp(sc[g] - m[g]) for g in G]
    den = [jnp.sum(e[g], axis=-1, keepdims=True) + jnp.exp(sinks[g] - m[g]) for g in G]
    pv = [jnp.dot(e[g].astype(BF16), hd(v_all, g), preferred_element_type=F32) / den[g] for g in G]
    outs = [pv[g][i * blk:(i + 1) * blk] for g in G for i in range(rep)]
    o_ref[...] = jnp.concatenate(outs, axis=1).astype(o_ref.dtype)


def _window_attn(q, k, p, v_col, sink, mix, mix_col, *, n_batch, n_ctx, seq_len):
    nb = seq_len // ATT_BLOCK
    x0 = n_batch * n_ctx // ATT_BLOCK
    vc = v_col // ATT_KV_W
    cur = lambda b, n: x0 + b * nb + n
    prev = lambda b, n: x0 + b * nb + jnp.maximum(n - 1, 0)
    nxt = lambda b, n: x0 + b * nb + jnp.minimum(n + 1, nb - 1)
    k_spec = lambda im: pl.BlockSpec((ATT_BLOCK, ATT_KV_W), lambda b, n: (im(b, n), 0))
    v_spec = lambda im: pl.BlockSpec((ATT_BLOCK, ATT_KV_W), lambda b, n: (im(b, n), vc))
    return pl.pallas_call(
        functools.partial(_attn_body, seq_len=seq_len),
        grid=(n_batch, nb),
        in_specs=[pl.BlockSpec((ATT_BLOCK, ATT_Q_W), lambda b, n: (cur(b, n), 0)),
                  k_spec(prev), k_spec(cur), k_spec(nxt), v_spec(prev), v_spec(cur), v_spec(nxt),
                  pl.BlockSpec((n_ctx, ATT_KV_W), lambda b, n: (b, 0)),
                  pl.BlockSpec((n_ctx, ATT_KV_W), lambda b, n: (b, vc)),
                  pl.BlockSpec((1, ATT_HEADS), lambda b, n: (0, 0)),
                  pl.BlockSpec(memory_space=pl.ANY)],
        out_specs=pl.BlockSpec((ATT_BLOCK, ATT_Q_W), lambda b, n: (b * nb + n, mix_col // ATT_Q_W)),
        out_shape=jax.ShapeDtypeStruct(mix.shape, mix.dtype),
        input_output_aliases={10: 0},
        scratch_shapes=[pltpu.VMEM((ATT_HEADS // KV_HEADS * ATT_BLOCK, n_ctx + 3 * ATT_BLOCK), F32)],
        compiler_params=_cparams(("parallel", "arbitrary")),
        name="window_attn",
    )(q, k, k, k, p, p, p, k, p, sink, mix)


def _tile_edges(row0, tm, n_ctx_rows, ctx_len, seq_len):
    in_ctx = row0 < n_ctx_rows
    pos0 = jnp.where(in_ctx, row0 % ctx_len, (row0 - n_ctx_rows) % seq_len)
    seg = jnp.where(in_ctx, ctx_len, seq_len)
    return pos0 == 0, pos0 + tm == seg


def _shifted_rows(cur, k, prev_blk, next_blk, first, last, rows):
    tm = cur.shape[0]
    if k < 0:
        edge = jnp.where(first, 0.0, pltpu.roll(prev_blk, -k, axis=0))
        body = pltpu.roll(cur, -k, axis=0)
        head = jnp.where(rows < -k, edge, body[0:SUBLANES])
        return jnp.concatenate([head, body[SUBLANES:]], axis=0)
    edge = jnp.where(last, 0.0, pltpu.roll(next_blk, SUBLANES - k, axis=0))
    body = pltpu.roll(cur, tm - k, axis=0)
    tail = jnp.where(rows >= SUBLANES - k, edge, body[tm - SUBLANES:])
    return jnp.concatenate([body[:tm - SUBLANES], tail], axis=0)


def _head_sum_mxu(t, blk):
    n = t.shape[1] // LANES
    hi, lo = _split2(t)
    both = jnp.concatenate([hi, lo], axis=0)
    rows = t.shape[0]
    out = []
    for j in range(n):
        s = jnp.dot(both[:, j * LANES:(j + 1) * LANES], blk, preferred_element_type=F32)
        out.append(s[0:rows] + s[rows:2 * rows])
    return jnp.concatenate(out, axis=1)


def _same_head_matrix():
    r = lax.broadcasted_iota(jnp.int32, (LANES, LANES), 0) // RWKV_HD
    c = lax.broadcasted_iota(jnp.int32, (LANES, LANES), 1) // RWKV_HD
    return jnp.where(r == c, 1.0, 0.0).astype(BF16)


def _softplus(x):
    return jnp.maximum(x, 0.0) + jnp.log(1.0 + jnp.exp(-jnp.abs(x)))


def _rwkv_pre_body(p_ref, hp_ref, hn_ref, mu_ref, lora_ref, gw_ref, w0_ref, a0_ref, kk_ref, ka_ref, rk_ref,
                   r_out, v_out, kk_out, g_out, bonus_out, lw_out, kd_out, be_out, *, tm, geo):
    W = RWKV_W
    first, last = _tile_edges(pl.program_id(0) * tm, tm, *geo)
    rows = lax.broadcasted_iota(jnp.int32, (SUBLANES, 1), 0)

    def tshift(c0, c1):
        cur = p_ref[:, c0:c1]
        prev = _shifted_rows(cur, -1, hp_ref[:, c0:c1], hn_ref[:, c0:c1], first, last, rows)
        nxt = _shifted_rows(cur, 1, hp_ref[:, c0:c1], hn_ref[:, c0:c1], first, last, rows)
        return cur + mu_ref[0:1, c0:c1] * (prev - cur) + mu_ref[1:2, c0:c1] * (nxt - cur)

    r = tshift(0, W)
    k = tshift(W, 2 * W)
    v = tshift(2 * W, 3 * W)
    xl = tshift(3 * W, 3 * W + 4 * LANES)
    lane = lax.broadcasted_iota(jnp.int32, (tm, 2 * LANES), 1)
    lin = jnp.where(lane < DECAY_LORA, jnp.tanh(xl[:, :2 * LANES]), xl[:, :2 * LANES])
    lora = _dot(lin, lora_ref[...])
    g_out[...] = _dot(jax.nn.sigmoid(xl), gw_ref[...])
    blk = _same_head_matrix()
    kk = k * kk_ref[...]
    kk = kk / jnp.maximum(jnp.sqrt(_head_sum_mxu(kk * kk, blk)), 1e-12)
    r_out[...] = r
    v_out[...] = v
    kk_out[...] = kk
    kd_sum = jnp.zeros_like(k)
    for d in range(2):
        w_log = -_softplus(-(w0_ref[d:d + 1, :] + lora[:, d * W:(d + 1) * W])) - 0.5
        a = jax.nn.sigmoid(a0_ref[d:d + 1, :] + lora[:, (2 + d) * W:(3 + d) * W])
        kd = k * (1.0 + (a - 1.0) * ka_ref[...])
        lw_out[d] = -jnp.exp(w_log)
        kd_out[d] = kd
        be_out[d] = a * kk
        kd_sum = kd_sum + kd
    bonus_out[...] = _head_sum_mxu(r * kd_sum * rk_ref[...], blk) * v


def _rwkv_pre(p, mu, lora_w, gate_w, w0, a0, k_k, k_a, r_k, *, geo):
    R = p.shape[0]
    W = RWKV_W
    tm = SEQ_BLOCK
    pw = 3 * W + 4 * LANES
    nb8 = R // SUBLANES
    row = lambda i: (i, 0)
    vec = lambda n: pl.BlockSpec((n, W), lambda i: (0, 0))
    out2 = pl.BlockSpec((2, tm, W), lambda i: (0, i, 0))
    f = jax.ShapeDtypeStruct((R, W), F32)
    f2 = jax.ShapeDtypeStruct((2, R, W), F32)
    return pl.pallas_call(
        functools.partial(_rwkv_pre_body, tm=tm, geo=geo),
        grid=(R // tm,),
        in_specs=[pl.BlockSpec((tm, pw), row),
                  pl.BlockSpec((SUBLANES, pw), lambda i: (jnp.maximum(i * (tm // SUBLANES) - 1, 0), 0)),
                  pl.BlockSpec((SUBLANES, pw), lambda i: (jnp.minimum((i + 1) * (tm // SUBLANES), nb8 - 1), 0)),
                  pl.BlockSpec((2, pw), lambda i: (0, 0)),
                  pl.BlockSpec(lora_w.shape, lambda i: (0, 0)),
                  pl.BlockSpec(gate_w.shape, lambda i: (0, 0)),
                  vec(2), vec(2), vec(1), vec(1), vec(1)],
        out_specs=[pl.BlockSpec((tm, W), row)] * 5 + [out2] * 3,
        out_shape=[f] * 5 + [f2] * 3,
        compiler_params=_cparams(("parallel",)),
        name="rwkv_pre",
    )(p, p, p, mu, lora_w, gate_w, w0, a0, k_k, k_a, r_k)


def _rwkv_post_body(yf_ref, yb_ref, bonus_ref, g_ref, lng_ref, lnb_ref, o_ref):
    blk = _same_head_matrix()
    y = yf_ref[...] + yb_ref[...]
    mean = _head_sum_mxu(y, blk) * (1.0 / RWKV_HD)
    d = y - mean
    var = _head_sum_mxu(d * d, blk) * (1.0 / RWKV_HD)
    yn = d * lax.rsqrt(var + GN_EPS)
    o_ref[...] = ((yn * lng_ref[...] + lnb_ref[...] + bonus_ref[...]) * g_ref[...]).astype(o_ref.dtype)


def _rwkv_post(yf, yb, bonus, g, ln_g, ln_b, out_cols):
    R, W = yf.shape
    tm = _pick(R, STAGE_ROWS)
    row = pl.BlockSpec((tm, W), lambda i: (i, 0))
    vec = pl.BlockSpec((1, W), lambda i: (0, 0))
    return pl.pallas_call(
        _rwkv_post_body,
        grid=(R // tm,),
        in_specs=[row, row, row, row, vec, vec],
        out_specs=row,
        out_shape=jax.ShapeDtypeStruct((R, out_cols), BF16),
        compiler_params=_cparams(("parallel",)),
        name="rwkv_post",
    )(yf, yb, bonus, g, ln_g, ln_b)


def _gelu_tanh(x):
    return 0.5 * x * (1.0 + jnp.tanh(math.sqrt(2.0 / math.pi) * (x + 0.044715 * (x * x * x))))


def _s5_post_body(u0_ref, u1_ref, yf_ref, yb_ref, d_ref, w_ref, b_ref, mix_ref, o_ref):
    del mix_ref
    u = jnp.concatenate([u0_ref[...], u1_ref[...]], axis=1)
    y = _gelu_tanh(d_ref[...] * u + yf_ref[...] + yb_ref[...])
    o_ref[...] = (y * jax.nn.sigmoid(_dot(y, w_ref[...]) + b_ref[...])).astype(o_ref.dtype)


def _s5_post(p, col0, yf, yb, d_skip, glu_w, glu_b, mix):
    R, W = yf.shape
    tm = _pick(R, STAGE_ROWS)
    half = W // 2
    row = pl.BlockSpec((tm, W), lambda i: (i, 0))
    vec = pl.BlockSpec((1, W), lambda i: (0, 0))
    return pl.pallas_call(
        _s5_post_body,
        grid=(R // tm,),
        in_specs=[pl.BlockSpec((tm, half), lambda i: (i, col0 // half)),
                  pl.BlockSpec((tm, half), lambda i: (i, col0 // half + 1)), row, row, vec,
                  pl.BlockSpec((W, W), lambda i: (0, 0)), vec, pl.BlockSpec(memory_space=pl.ANY)],
        out_specs=pl.BlockSpec((tm, W), lambda i: (i, 1)),
        out_shape=jax.ShapeDtypeStruct(mix.shape, mix.dtype),
        input_output_aliases={7: 0},
        compiler_params=_cparams(("parallel",)),
        name="s5_post",
    )(p, p, yf, yb, d_skip, glu_w, glu_b, mix)


def _cd_conv_body(x_ref, hp_ref, hn_ref, w_ref, b_ref, o_ref, *, tm, geo, taps):
    first, last = _tile_edges(pl.program_id(0) * tm, tm, *geo)
    rows = lax.broadcasted_iota(jnp.int32, (SUBLANES, 1), 0)
    cur = x_ref[...]
    half = taps // 2
    acc = b_ref[...] + w_ref[half:half + 1, :] * cur
    for j in range(taps):
        if j != half:
            acc = acc + w_ref[j:j + 1, :] * _shifted_rows(cur, j - half, hp_ref[...], hn_ref[...], first, last, rows)
    o_ref[...] = acc * jax.nn.sigmoid(acc)


def _cd_conv(p, col0, width, conv_w, conv_b, *, geo):
    R = p.shape[0]
    tm = SEQ_BLOCK
    tw = 1024
    taps = conv_w.shape[0]
    nb8 = R // SUBLANES
    c0 = col0 // tw
    w8 = jnp.pad(conv_w, ((0, SUBLANES - taps), (0, 0)))
    return pl.pallas_call(
        functools.partial(_cd_conv_body, tm=tm, geo=geo, taps=taps),
        grid=(R // tm, width // tw),
        in_specs=[pl.BlockSpec((tm, tw), lambda i, j: (i, c0 + j)),
                  pl.BlockSpec((SUBLANES, tw), lambda i, j: (jnp.maximum(i * (tm // SUBLANES) - 1, 0), c0 + j)),
                  pl.BlockSpec((SUBLANES, tw), lambda i, j: (jnp.minimum((i + 1) * (tm // SUBLANES), nb8 - 1), c0 + j)),
                  pl.BlockSpec((SUBLANES, tw), lambda i, j: (0, j)),
                  pl.BlockSpec((1, tw), lambda i, j: (0, j))],
        out_specs=pl.BlockSpec((tm, tw), lambda i, j: (i, j)),
        out_shape=jax.ShapeDtypeStruct((R, width), F32),
        compiler_params=_cparams(("parallel", "parallel")),
        name="cd_conv",
    )(p, p, p, w8, conv_b.reshape(1, width))


def _rope_tile(x, cos, sin):
    lane = lax.broadcasted_iota(jnp.int32, x.shape, 1)
    q = ATT_HD // 4
    partner = jnp.where(lane % (2 * q) < q, pltpu.roll(x, LANES - q, axis=1), pltpu.roll(x, q, axis=1))
    return x * cos + partner * sin


def _cd_rope_dt_body(q_ref, k_ref, dtr_ref, cos_ref, sin_ref, bias_ref, a_ref, q_out, k_out, dt_out, dta_out,
                     *, tm, n_ctx_rows):
    in_ctx = pl.program_id(0) * tm < n_ctx_rows
    cos = jnp.where(in_ctx, 1.0, cos_ref[...])
    sin = jnp.where(in_ctx, 0.0, sin_ref[...])
    for j in range(ATT_Q_W // LANES):
        q_out[:, j * LANES:(j + 1) * LANES] = _rope_tile(q_ref[:, j * LANES:(j + 1) * LANES], cos, sin)
    for j in range(ATT_KV_W // LANES):
        k_out[:, j * LANES:(j + 1) * LANES] = _rope_tile(k_ref[:, j * LANES:(j + 1) * LANES], cos, sin)
    dt = _softplus(dtr_ref[...] + bias_ref[...])
    dt_out[...] = dt
    dta_out[...] = dt * a_ref[...]


def _cd_rope_dt(p, q_col, k_col, dt_col, cos, sin, dt_bias, a_neg, *, n_ctx_rows, seq_len):
    R = p.shape[0]
    tm = _pick(math.gcd(n_ctx_rows, seq_len), STAGE_ROWS)
    tab = pl.BlockSpec((tm, LANES), lambda i: (jnp.maximum(i * tm - n_ctx_rows, 0) % seq_len // tm, 0))
    vec = pl.BlockSpec((1, LANES), lambda i: (0, 0))
    return pl.pallas_call(
        functools.partial(_cd_rope_dt_body, tm=tm, n_ctx_rows=n_ctx_rows),
        grid=(R // tm,),
        in_specs=[pl.BlockSpec((tm, ATT_Q_W), lambda i: (i, q_col // ATT_Q_W)),
                  pl.BlockSpec((tm, ATT_KV_W), lambda i: (i, k_col // ATT_KV_W)),
                  pl.BlockSpec((tm, LANES), lambda i: (i, dt_col // LANES)),
                  tab, tab, vec, vec],
        out_specs=[pl.BlockSpec((tm, ATT_Q_W), lambda i: (i, 0)), pl.BlockSpec((tm, ATT_KV_W), lambda i: (i, 0)),
                   pl.BlockSpec((tm, LANES), lambda i: (i, 0)), pl.BlockSpec((tm, LANES), lambda i: (i, 0))],
        out_shape=[jax.ShapeDtypeStruct((R, ATT_Q_W), F32), jax.ShapeDtypeStruct((R, ATT_KV_W), F32),
                   jax.ShapeDtypeStruct((R, LANES), F32), jax.ShapeDtypeStruct((R, LANES), F32)],
        compiler_params=_cparams(("parallel",)),
        name="cd_rope_dt",
    )(p, p, p, cos, sin, dt_bias, a_neg)


def _ssd_post_body(yf_ref, yb_ref, x_ref, z_ref, d_ref, g_ref, o_ref):
    z = z_ref[...]
    y = (yf_ref[...] + yb_ref[...] + d_ref[...] * x_ref[...]) * (z * jax.nn.sigmoid(z))
    gw = SSD_W // SSD_GROUPS
    for g in range(SSD_GROUPS):
        yg = y[:, g * gw:(g + 1) * gw]
        ms = jnp.mean(yg * yg, axis=-1, keepdims=True)
        o_ref[:, g * gw:(g + 1) * gw] = (yg * lax.rsqrt(ms + EPS) * g_ref[:, g * gw:(g + 1) * gw]).astype(o_ref.dtype)


def _ssd_post(yf, yb, xbc, p, d_skip, norm_g, out_cols, *, n_ctx_rows):
    R = yf.shape[0]
    tm = _pick(math.gcd(n_ctx_rows, R - n_ctx_rows), STAGE_ROWS)
    r0 = n_ctx_rows // tm
    row = pl.BlockSpec((tm, SSD_W), lambda i: (r0 + i, 0))
    vec = pl.BlockSpec((1, SSD_W), lambda i: (0, 0))
    return pl.pallas_call(
        _ssd_post_body,
        grid=((R - n_ctx_rows) // tm,),
        in_specs=[row, row, row, row, vec, vec],
        out_specs=pl.BlockSpec((tm, SSD_W), lambda i: (i, 0)),
        out_shape=jax.ShapeDtypeStruct((R - n_ctx_rows, out_cols), BF16),
        compiler_params=_cparams(("parallel",)),
        name="ssd_post",
    )(yf, yb, xbc, p, d_skip, norm_g)


def _rope_tables(L):
    nf = ATT_HD // 4
    inv_freq = ROPE_BASE ** (-jnp.arange(nf, dtype=F32) / nf)
    t = jnp.arange(L, dtype=jnp.int32)
    row_id = (t // GRID_W).astype(F32)
    col_id = (t % GRID_W).astype(F32)
    ar = row_id[:, None] * inv_freq
    ac = col_id[:, None] * inv_freq
    cos = jnp.concatenate([jnp.cos(ar), jnp.cos(ar), jnp.cos(ac), jnp.cos(ac)], axis=1)
    sin = jnp.concatenate([-jnp.sin(ar), jnp.sin(ar), -jnp.sin(ac), jnp.sin(ac)], axis=1)
    return cos, sin


def kernel(x, c, ctx, c_ctx, ada_w, ada_b, norm1_g, norm2_g, mlp_w1, mlp_w2, final_g, ab_w_in, ab_w_out, rwkv_mu, rwkv_w0, rwkv_w_up, rwkv_a0, rwkv_a_up, rwkv_g_up, rwkv_k_k, rwkv_k_a, rwkv_r_k, rwkv_ln_g, rwkv_ln_b, s5_A_re, s5_A_im, s5_log_dt, s5_B_re, s5_B_im, s5_C_re, s5_C_im, s5_D, s5_glu_w, s5_glu_b, cd_w_in, cd_w_out, ssd_conv_w, ssd_conv_b, ssd_A_log, ssd_dt_bias, ssd_D, ssd_norm_g, attn_sink):
    B, L, D = x.shape
    C = ctx.shape[1]
    RC, RX = B * C, B * L
    R = RC + RX
    assert C % SEQ_BLOCK == 0 and L % SEQ_BLOCK == 0
    geo = dict(n_batch=B, n_ctx_blk=C // SEQ_BLOCK, n_x_blk=L // SEQ_BLOCK)
    geo_ssd = dict(n_batch=B, n_ctx_blk=C // SSD_CHUNK, n_x_blk=L // SSD_CHUNK)
    group_all = lambda r0: jnp.where(r0 < RC, 0, 1 + (r0 - RC) // L)
    group_x = lambda r0: 1 + r0 // L

    cond = jax.nn.silu(jnp.concatenate([c_ctx[None, :], c], axis=0))
    cond = jnp.pad(cond, ((0, SUBLANES - (B + 1) % SUBLANES), (0, 0))) if (B + 1) % SUBLANES else cond
    n_groups = cond.shape[0]

    def modulation(i):
        mod = _mm(cond, ada_w, b_layer=i, name="ada") + ada_b[i]
        return [m.reshape(n_groups, 1, D) for m in jnp.split(mod, 6, axis=-1)]

    sh1, sc1, g1, sh2, sc2, g2 = modulation(0)
    sh1n, sc1n, g1n, sh2n, sc2n, g2n = modulation(1)
    h, xs = _norm_mod_first(ctx.reshape(RC, D), x.reshape(RX, D), norm1_g[0], sh1, sc1, group_all)
    pad_cols = (-RWKV_COLS) % LANES
    w_in = ab_w_in[0]
    w_in = jnp.concatenate([w_in[:, :RWKV_COLS], jnp.zeros((D, pad_cols), F32), w_in[:, RWKV_COLS:]], axis=1)
    s5_col0 = RWKV_COLS + pad_cols
    p = _mm(h, w_in.astype(BF16), name="ab_in")

    W = RWKV_W
    lora_rows = 2 * LANES
    zw = jnp.zeros((DECAY_LORA, W), F32)
    lora_w = jnp.concatenate([
        jnp.concatenate([rwkv_w_up[0, 0], rwkv_w_up[0, 1], zw, zw], axis=1),
        jnp.concatenate([zw, zw, rwkv_a_up[0, 0], rwkv_a_up[0, 1]], axis=1),
        jnp.zeros((lora_rows - DECAY_LORA - AAA_LORA, 4 * W), F32)], axis=0).astype(BF16)
    gate_w = jnp.pad(rwkv_g_up[0], ((DECAY_LORA + AAA_LORA, pad_cols), (0, 0))).astype(BF16)
    mu = jnp.pad(rwkv_mu[0], ((0, 0), (0, pad_cols)))
    vec = lambda t: t.reshape(1, W)
    r, v, kk, g_gate, bonus, lw, kd, be = _rwkv_pre(
        p, mu, lora_w, gate_w, rwkv_w0[0], rwkv_a0[0], vec(rwkv_k_k[0]), vec(rwkv_k_a[0]), vec(rwkv_r_k[0]),
        geo=(RC, C, L))
    y_dirs = [_rwkv_scan(r, v, kk, lw, kd, be, reverse=(d == 1), **geo) for d in range(2)]
    mix = _rwkv_post(y_dirs[0], y_dirs[1], bonus, g_gate, vec(rwkv_ln_g[0]), vec(rwkv_ln_b[0]), W + S5_W)

    ys_dirs = []
    for d in range(2):
        bw, cw, tab = _s5_weights(s5_A_re[0, d], s5_A_im[0, d], s5_log_dt[0, d], s5_B_re[0, d], s5_B_im[0, d],
                                  s5_C_re[0, d], s5_C_im[0, d])
        ys_dirs.append(_s5_scan(p, s5_col0 // LANES, bw, cw, tab, reverse=(d == 1), **geo))
    mix = _s5_post(p, s5_col0, ys_dirs[0], ys_dirs[1], vec(s5_D[0]), _layer_bf16(s5_glu_w, 0),
                   vec(s5_glu_b[0]), mix)
    xs, h2 = _mm_res_norm(mix, _layer_bf16(ab_w_out, 0), xs, g1, norm2_g[0], sh2, sc2, group_all, name="ab_out")
    m1 = _mm(h2, _layer_bf16(mlp_w1, 0), out_dtype=BF16, act="relu2", name="mlp_up")
    xs = _mm(m1, _layer_bf16(mlp_w2, 0), res=xs, gate=g2, gate_idx=group_all, name="mlp_down")
    h = _norm_mod(xs, norm1_g[1], sh1n, sc1n, group_all)

    sh1, sc1, g1, sh2, sc2, g2 = sh1n, sc1n, g1n, sh2n, sc2n, g2n
    cd_in = cd_w_in[0]
    o_dt = SSD_W + XBC_W
    o_q = o_dt + 2 * SSD_HEADS
    n_qkv = ATT_Q_W + 2 * ATT_KV_W
    cd_in = jnp.concatenate([cd_in[:, :o_dt], cd_in[:, o_q:o_q + n_qkv], cd_in[:, o_dt:o_q]], axis=1)
    cd_in = jnp.pad(cd_in, ((0, 0), (0, (-cd_in.shape[1]) % 512)))
    q_col = o_dt
    k_col = q_col + ATT_Q_W
    v_col = k_col + ATT_KV_W
    dt_col = v_col + ATT_KV_W
    p = _mm(h, cd_in.astype(BF16), name="cd_in")

    xbc = _cd_conv(p, SSD_W, XBC_W, ssd_conv_w[0], ssd_conv_b[0], geo=(RC, C, L))
    cos, sin = _rope_tables(L)
    lane_pad = lambda t: jnp.pad(t.reshape(1, 2 * SSD_HEADS), ((0, 0), (0, LANES - 2 * SSD_HEADS)))
    q_rot, k_rot, dt, dta = _cd_rope_dt(
        p, q_col, k_col, dt_col, jnp.tile(cos, (1, LANES // ATT_HD)), jnp.tile(sin, (1, LANES // ATT_HD)),
        lane_pad(ssd_dt_bias[0]), lane_pad(-jnp.exp(ssd_A_log[0])), n_ctx_rows=RC, seq_len=L)
    y_dirs = [_ssd_scan(xbc, dt, dta, reverse=(d == 1), **geo_ssd) for d in range(2)]
    mix = _ssd_post(y_dirs[0], y_dirs[1], xbc, p, jnp.repeat(ssd_D[0], SSD_HD).reshape(1, SSD_W),
                    ssd_norm_g[0].reshape(1, SSD_W), SSD_W + ATT_Q_W, n_ctx_rows=RC)
    mix = _window_attn(q_rot, k_rot, p, v_col, attn_sink[0].reshape(1, ATT_HEADS), mix, SSD_W,
                       n_batch=B, n_ctx=C, seq_len=L)
    xo, h2 = _mm_res_norm(mix, _layer_bf16(cd_w_out, 0), xs, g1, norm2_g[1], sh2, sc2, group_x, res_row0=RC,
                          name="cd_out")
    m1 = _mm(h2, _layer_bf16(mlp_w1, 1), out_dtype=BF16, act="relu2", name="mlp_up")
    xo = _mm(m1, _layer_bf16(mlp_w2, 1), res=xo, gate=g2, gate_idx=group_x, name="mlp_down")
    zero = jnp.zeros((1, 1, D), F32)
    out = _norm_mod(xo, final_g, zero, zero, lambda r0: 0, out_dtype=F32)
    return out.reshape(B, L, D)
```

```python
import functools
import math

import jax
import jax.numpy as jnp
from jax import lax
from jax.experimental import pallas as pl
from jax.experimental.pallas import tpu as pltpu

F32 = jnp.float32
BF16 = jnp.bfloat16

EPS = 1e-6
GN_EPS = 64e-5
GRID_W = 64
ROPE_BASE = 10000.0

RWKV_HEADS = 16
RWKV_HD = 64
RWKV_W = RWKV_HEADS * RWKV_HD
DECAY_LORA = 96
AAA_LORA = 96
GATE_LORA = 256
RWKV_COLS = 3 * RWKV_W + DECAY_LORA + AAA_LORA + GATE_LORA
S5_W = 1024
S5_P = 16
S5_G = S5_W // S5_P
S5_N = 64
SSD_HEADS = 32
SSD_HD = 64
SSD_W = SSD_HEADS * SSD_HD
SSD_GROUPS = 4
SSD_N = 128
SSD_CHUNK = 128
XBC_W = SSD_W + 2 * SSD_GROUPS * SSD_N
ATT_HEADS = 16
KV_HEADS = 4
ATT_HD = 64
ATT_Q_W = ATT_HEADS * ATT_HD
ATT_KV_W = KV_HEADS * ATT_HD
WINDOW = 128
ATT_BLOCK = 128

LANES = 128
SUBLANES = 8
SEQ_BLOCK = 256
RWKV_CHUNK = 64
S5_GB = 8
VMEM_LIMIT = 48 * 1024 * 1024
VMEM_INTERNAL = 6 * 1024 * 1024
CAST_BLOCK_BYTES = 8 * 1024 * 1024
STAGE_ROWS = (512, 256, 128, 64, 32, 16, 8)


def _cparams(sem):
    return pltpu.CompilerParams(dimension_semantics=sem, vmem_limit_bytes=VMEM_LIMIT)


def _pick(n, cands):
    for c in cands:
        if n % c == 0:
            return c
    return n


def _dot(a, b):
    return jnp.dot(a.astype(BF16), b.astype(BF16), preferred_element_type=F32)


def _split3(x):
    hi = x.astype(BF16)
    r1 = x - hi.astype(F32)
    mid = r1.astype(BF16)
    lo = (r1 - mid.astype(F32)).astype(BF16)
    return hi, mid, lo


def _dot_ones_x(tri, x):
    hi, mid, lo = _split3(x)
    f = lambda t: jnp.dot(tri, t, preferred_element_type=F32)
    return f(hi) + f(mid) + f(lo)


def _dot_x_ones_nt(x, tri):
    hi, mid, lo = _split3(x)
    f = lambda t: lax.dot_general(t, tri, (((1,), (1,)), ((), ())), preferred_element_type=F32)
    return f(hi) + f(mid) + f(lo)


def _mm_body(*refs, nk, act, residual):
    if nk > 1:
        *refs, acc_ref = refs
    if residual:
        a_ref, b_ref, res_ref, gate_ref, o_ref = refs
    else:
        a_ref, b_ref, o_ref = refs
    k = pl.program_id(2)
    part = jnp.dot(a_ref[...].astype(BF16), b_ref[...].astype(BF16), preferred_element_type=F32)

    def finish(acc):
        if act == "relu2":
            acc = jnp.square(jnp.maximum(acc, 0.0))
        if residual:
            acc = res_ref[...] + gate_ref[0] * acc
        o_ref[...] = acc.astype(o_ref.dtype)

    if nk == 1:
        finish(part)
    else:
        @pl.when(k == 0)
        def _():
            acc_ref[...] = part

        @pl.when(k > 0)
        def _():
            acc_ref[...] += part

        @pl.when(k == nk - 1)
        def _():
            finish(acc_ref[...])


def _mm(a, b, *, b_layer=0, out_dtype=F32, act=None, res=None, res_row0=0, gate=None, gate_idx=None, name="mm"):
    M, K = a.shape
    N = b.shape[-1]
    residual = res is not None
    a_bytes, b_bytes, o_bytes = a.dtype.itemsize, b.dtype.itemsize, jnp.dtype(out_dtype).itemsize

    def vmem_bytes(tm, tn, tk):
        blocks = tm * tk * a_bytes + tk * tn * b_bytes + tm * tn * o_bytes + (tm * tn * 4 if residual else 0)
        return 2 * blocks + tm * tn * 4

    fits = [(tk == K, tm * tn, tm, tn, tk)
            for tk in (K, 4096, 2048, 1024, 512, 256, 128) if K % tk == 0
            for tn in (2048, 1792, 1536, 1024, 512, 384, 256, 128) if N % tn == 0
            for tm in (1024, 512, 256, 128, 64, 32, 16, 8) if M % tm == 0
            and vmem_bytes(tm, tn, tk) <= VMEM_LIMIT - VMEM_INTERNAL]
    _, _, tm, tn, tk = max(fits)
    nk = K // tk
    if b.ndim == 3:
        b_spec = pl.BlockSpec((None, tk, tn), lambda i, j, k: (b_layer, k, j))
    else:
        b_spec = pl.BlockSpec((tk, tn), lambda i, j, k: (k, j))
    in_specs = [pl.BlockSpec((tm, tk), lambda i, j, k: (i, k)), b_spec]
    args = [a, b]
    if residual:
        if res_row0 % tm:
            res, res_row0 = res[res_row0:res_row0 + M], 0
        r0 = res_row0 // tm
        in_specs += [pl.BlockSpec((tm, tn), lambda i, j, k: (r0 + i, j)),
                     pl.BlockSpec((1, 1, tn), lambda i, j, k: (gate_idx(i * tm), 0, j))]
        args += [res, gate]
    return pl.pallas_call(
        functools.partial(_mm_body, nk=nk, act=act, residual=residual),
        grid=(M // tm, N // tn, nk),
        in_specs=in_specs,
        out_specs=pl.BlockSpec((tm, tn), lambda i, j, k: (i, j)),
        out_shape=jax.ShapeDtypeStruct((M, N), out_dtype),
        scratch_shapes=[pltpu.VMEM((tm, tn), F32)] if nk > 1 else [],
        compiler_params=_cparams(("parallel", "parallel", "arbitrary")),
        name=name,
    )(*args)


def _cast_body(w_ref, o_ref):
    o_ref[...] = w_ref[...].astype(o_ref.dtype)


def _layer_bf16(w, layer):
    _, K, N = w.shape
    tk = next(t for t in (2048, 1024, 512, 256, 128, 64, 32, 16) if K % t == 0 and t * N * 4 <= CAST_BLOCK_BYTES)
    return pl.pallas_call(
        _cast_body,
        grid=(K // tk,),
        in_specs=[pl.BlockSpec((None, tk, N), lambda i: (layer, i, 0))],
        out_specs=pl.BlockSpec((tk, N), lambda i: (i, 0)),
        out_shape=jax.ShapeDtypeStruct((K, N), BF16),
        compiler_params=_cparams(("parallel",)),
        name="weight_bf16",
    )(w)


def _pack_body(w_ref, o_ref, *, pieces):
    w = w_ref[...]
    parts = []
    for src, width in pieces:
        parts.append(jnp.zeros((w.shape[0], width), F32) if src is None else w[:, src:src + width])
    o_ref[...] = jnp.concatenate(parts, axis=1).astype(o_ref.dtype)


def _layer_packed_bf16(w, layer, pieces):
    _, K, N = w.shape
    n_out = sum(width for _, width in pieces)
    tk = next(t for t in (256, 128, 64, 32, 16) if K % t == 0 and t * N * 4 <= CAST_BLOCK_BYTES)
    return pl.pallas_call(
        functools.partial(_pack_body, pieces=tuple(pieces)),
        grid=(K // tk,),
        in_specs=[pl.BlockSpec((None, tk, N), lambda i: (layer, i, 0))],
        out_specs=pl.BlockSpec((tk, n_out), lambda i: (i, 0)),
        out_shape=jax.ShapeDtypeStruct((K, n_out), BF16),
        compiler_params=_cparams(("parallel",)),
        name="weight_pack",
    )(w)


def _mm_res_norm_body(a_ref, b_ref, res_ref, gate_ref, g_ref, sh_ref, sc_ref, *out_refs, nk, keep_x):
    if nk > 1:
        *out_refs, acc_ref = out_refs
    h_ref = out_refs[-1]
    k = pl.program_id(1)
    part = jnp.dot(a_ref[...], b_ref[...], preferred_element_type=F32)

    def finish(acc):
        x = res_ref[...] + gate_ref[0] * acc
        if keep_x:
            out_refs[0][...] = x
        ms = jnp.mean(x * x, axis=-1, keepdims=True)
        xn = x * lax.rsqrt(ms + EPS) * g_ref[...]
        h_ref[...] = (xn * (1.0 + sc_ref[0]) + sh_ref[0]).astype(h_ref.dtype)

    if nk == 1:
        finish(part)
        return

    @pl.when(k == 0)
    def _():
        acc_ref[...] = part

    @pl.when(k > 0)
    def _():
        acc_ref[...] += part

    @pl.when(k == nk - 1)
    def _():
        finish(acc_ref[...])


def _mm_res_norm(a, b, res, gate, norm_g, shift, scale, group_idx, *, res_row0=0, keep_x=True, h_dtype=BF16,
                 name="mm_res_norm"):
    M, K = a.shape
    N = b.shape[1]
    h_bytes = jnp.dtype(h_dtype).itemsize

    def vmem_bytes(tm, tk):
        blocks = tm * tk * 2 + tk * N * 2 + tm * N * 4 + (tm * N * 4 if keep_x else 0) + tm * N * h_bytes
        return 2 * blocks + (tm * N * 4 if tk < K else 0)

    tm, tk = next(((tm, tk) for tk in (K, 1024, 512, 256, 128) if K % tk == 0
                   for tm in (512, 256, 128, 64, 32, 16, 8) if M % tm == 0
                   and vmem_bytes(tm, tk) <= VMEM_LIMIT - VMEM_INTERNAL), (8, 128))
    nk = K // tk
    if res_row0 % tm:
        res, res_row0 = res[res_row0:res_row0 + M], 0
    r0 = res_row0 // tm
    grp = lambda i, k: (group_idx(i * tm), 0, 0)
    row = pl.BlockSpec((tm, N), lambda i, k: (i, 0))
    out_specs = ([row] if keep_x else []) + [row]
    out_shape = ([jax.ShapeDtypeStruct((M, N), F32)] if keep_x else []) + [jax.ShapeDtypeStruct((M, N), h_dtype)]
    out = pl.pallas_call(
        functools.partial(_mm_res_norm_body, nk=nk, keep_x=keep_x),
        grid=(M // tm, nk),
        in_specs=[pl.BlockSpec((tm, tk), lambda i, k: (i, k)),
                  pl.BlockSpec((tk, N), lambda i, k: (k, 0)),
                  pl.BlockSpec((tm, N), lambda i, k: (r0 + i, 0)),
                  pl.BlockSpec((1, 1, N), grp),
                  pl.BlockSpec((1, N), lambda i, k: (0, 0)),
                  pl.BlockSpec((1, 1, N), grp), pl.BlockSpec((1, 1, N), grp)],
        out_specs=out_specs,
        out_shape=out_shape,
        scratch_shapes=[pltpu.VMEM((tm, N), F32)] if nk > 1 else [],
        compiler_params=_cparams(("parallel", "arbitrary")),
        name=name,
    )(a, b, res, gate, norm_g.reshape(1, N), shift, scale)
    return out if keep_x else out[0]


def _norm_body(x_ref, g_ref, sh_ref, sc_ref, o_ref):
    x = x_ref[...]
    ms = jnp.mean(x * x, axis=-1, keepdims=True)
    xn = x * lax.rsqrt(ms + EPS) * g_ref[...]
    o_ref[...] = (xn * (1.0 + sc_ref[0]) + sh_ref[0]).astype(o_ref.dtype)


def _norm_mod(x, g, shift, scale, group_idx, out_dtype=BF16):
    M, D = x.shape
    tm = _pick(M, STAGE_ROWS)
    return pl.pallas_call(
        _norm_body,
        grid=(M // tm,),
        in_specs=[pl.BlockSpec((tm, D), lambda i: (i, 0)),
                  pl.BlockSpec((1, D), lambda i: (0, 0)),
                  pl.BlockSpec((1, 1, D), lambda i: (group_idx(i * tm), 0, 0)),
                  pl.BlockSpec((1, 1, D), lambda i: (group_idx(i * tm), 0, 0))],
        out_specs=pl.BlockSpec((tm, D), lambda i: (i, 0)),
        out_shape=jax.ShapeDtypeStruct((M, D), out_dtype),
        compiler_params=_cparams(("parallel",)),
        name="norm_mod",
    )(x, g.reshape(1, D), shift, scale)


def _norm_first_body(c_ref, x_ref, g_ref, sh_ref, sc_ref, h_ref, xs_ref, *, n_ctx_tiles):
    x = jnp.where(pl.program_id(0) < n_ctx_tiles, c_ref[...], x_ref[...])
    xs_ref[...] = x
    ms = jnp.mean(x * x, axis=-1, keepdims=True)
    xn = x * lax.rsqrt(ms + EPS) * g_ref[...]
    h_ref[...] = (xn * (1.0 + sc_ref[0]) + sh_ref[0]).astype(h_ref.dtype)


def _norm_mod_first(ctx_rows, x_rows, g, shift, scale, group_idx):
    RC, D = ctx_rows.shape
    RX = x_rows.shape[0]
    tm = _pick(math.gcd(RC, RX), STAGE_ROWS)
    nc = RC // tm
    grp = lambda i: (group_idx(i * tm), 0, 0)
    row = pl.BlockSpec((tm, D), lambda i: (i, 0))
    return pl.pallas_call(
        functools.partial(_norm_first_body, n_ctx_tiles=nc),
        grid=((RC + RX) // tm,),
        in_specs=[pl.BlockSpec((tm, D), lambda i: (jnp.minimum(i, nc - 1), 0)),
                  pl.BlockSpec((tm, D), lambda i: (jnp.maximum(i - nc, 0), 0)),
                  pl.BlockSpec((1, D), lambda i: (0, 0)),
                  pl.BlockSpec((1, 1, D), grp), pl.BlockSpec((1, 1, D), grp)],
        out_specs=[row, row],
        out_shape=[jax.ShapeDtypeStruct((RC + RX, D), BF16), jax.ShapeDtypeStruct((RC + RX, D), F32)],
        compiler_params=_cparams(("parallel",)),
        name="norm_first",
    )(ctx_rows, x_rows, g.reshape(1, D), shift, scale)


def _seq_block(b, s, *, n_ctx_blk, n_x_blk, n_batch, reverse):
    if reverse:
        c = b * n_ctx_blk + (n_ctx_blk - 1 - s)
        x = n_batch * n_ctx_blk + b * n_x_blk + (n_x_blk - 1 - (s - n_ctx_blk))
    else:
        c = b * n_ctx_blk + s
        x = n_batch * n_ctx_blk + b * n_x_blk + (s - n_ctx_blk)
    return jnp.where(s < n_ctx_blk, c, x)


def _split2(x):
    hi = x.astype(BF16)
    lo = (x - hi.astype(F32)).astype(BF16)
    return hi, lo


def _rwkv_chunk(lw, r, v, kk, kd, be, st, tri, strict, incl, same_head, head0, eye):
    C = RWKV_CHUNK

    def stack(z):
        return jnp.concatenate([jnp.where(head0, z, 0.0), jnp.where(head0, 0.0, z)], axis=0)

    def unstack(z):
        return z[0:C] + z[C:2 * C]

    bdot = lambda a, b: jnp.dot(a, b, preferred_element_type=F32)
    bdot_nt = lambda a, b: lax.dot_general(a, b, (((1,), (1,)), ((), ())), preferred_element_type=F32)
    bdot_tn = lambda a, b: lax.dot_general(a, b, (((0,), (0,)), ((), ())), preferred_element_type=F32)
    cat = lambda *zs: jnp.concatenate(zs, axis=0)
    P = range(len(lw))
    cum = [_dot_ones_x(tri, lw[p]) for p in P]
    total = [jnp.sum(lw[p], axis=0, keepdims=True) for p in P]
    a_t = [kk[p] * jnp.exp(cum[p] - lw[p]) for p in P]
    r_t = [r[p] * jnp.exp(cum[p]) for p in P]
    a_sp = [_split2(a_t[p]) for p in P]
    r_hi = [r_t[p].astype(BF16) for p in P]
    sa_sp = [_split2(stack(a_t[p])) for p in P]
    sr_hi = [stack(r_t[p]).astype(BF16) for p in P]
    pinv = [jnp.exp(-cum[p]) for p in P]
    bk_sp = [_split2(cat(stack(be[p] * pinv[p]), stack(kd[p] * pinv[p]))) for p in P]
    g1 = [bdot_nt(cat(sa_sp[p][0], sa_sp[p][1], sr_hi[p]), bk_sp[p][0]) for p in P]
    g2 = [bdot_nt(sa_sp[p][0], bk_sp[p][1]) for p in P]
    n_all = [g1[p][0:2 * C] + g1[p][2 * C:4 * C] + g2[p] for p in P]
    x_neg = [jnp.where(strict, -n_all[p][:, 0:2 * C], 0.0) for p in P]
    n_ak = [jnp.where(strict, n_all[p][:, 2 * C:4 * C], 0.0) for p in P]
    m_r = [jnp.concatenate([jnp.where(incl, g1[p][4 * C:6 * C, 0:2 * C], 0.0),
                            jnp.where(incl, g1[p][4 * C:6 * C, 2 * C:4 * C], 0.0)], axis=1).astype(BF16)
           for p in P]
    lcat = lambda *zs: jnp.concatenate(zs, axis=1)
    W2 = 2 * C
    t_inv = [eye + x_neg[p] for p in P]
    xb = [x_neg[p].astype(BF16) for p in P]
    xb = [bdot(xb[p], xb[p]).astype(BF16) for p in P]
    n_pow = int(math.log2(C))
    for it in range(1, n_pow):
        if it + 1 < n_pow:
            both = [bdot(xb[p], lcat(xb[p], t_inv[p].astype(BF16))) for p in P]
            xb = [both[p][:, 0:W2].astype(BF16) for p in P]
            t_inv = [t_inv[p] + both[p][:, W2:2 * W2] for p in P]
        else:
            t_inv = [t_inv[p] + bdot(xb[p], t_inv[p].astype(BF16)) for p in P]
    t_inv = [t_inv[p].astype(BF16) for p in P]
    st_sp = [_split2(st[p]) for p in P]
    as1 = [bdot_nt(a_sp[p][0], cat(st_sp[p][0], st_sp[p][1])) for p in P]
    as2 = [bdot_nt(cat(a_sp[p][1], r_hi[p]), st_sp[p][0]) for p in P]
    v_sp = [_split2(stack(v[p])) for p in P]
    nk_sp = [_split2(n_ak[p]) for p in P]
    nv1 = [bdot(lcat(nk_sp[p][0], nk_sp[p][1]), cat(v_sp[p][0], v_sp[p][0])) for p in P]
    nv2 = [bdot(nk_sp[p][0], v_sp[p][1]) for p in P]
    rhs = [-(stack(as1[p][:, 0:W2] + as1[p][:, W2:2 * W2] + as2[p][0:C]) + nv1[p] + nv2[p]) for p in P]
    u0 = [bdot(t_inv[p], rhs[p].astype(BF16)) for p in P]
    u0_sp = [_split2(u0[p]) for p in P]
    x_sp = [_split2(x_neg[p]) for p in P]
    xu1 = [bdot(lcat(x_sp[p][0], x_sp[p][1]), cat(u0_sp[p][0], u0_sp[p][0])) for p in P]
    xu2 = [bdot(x_sp[p][0], u0_sp[p][1]) for p in P]
    res = [(rhs[p] - u0[p] + xu1[p] + xu2[p]).astype(BF16) for p in P]
    u = [u0[p] + bdot(t_inv[p], res[p]) for p in P]
    y = [stack(as2[p][C:2 * C]) + bdot(m_r[p], cat(u[p].astype(BF16), v_sp[p][0])) for p in P]
    uv_sp = [_split2(cat(unstack(u[p]), v[p])) for p in P]
    pend = [jnp.exp(total[p] - cum[p]) for p in P]
    e_sp = [_split2(cat(be[p] * pend[p], kd[p] * pend[p])) for p in P]
    upd = [bdot_tn(cat(uv_sp[p][0], uv_sp[p][1], uv_sp[p][0]), cat(e_sp[p][0], e_sp[p][0], e_sp[p][1]))
           for p in P]
    st_new = [st[p] * jnp.exp(total[p]) + jnp.where(same_head, upd[p], 0.0) for p in P]
    return [unstack(y[p]) for p in P], st_new


def _rwkv_body(r_ref, v_ref, kk_ref, lw_ref, kd_ref, be_ref, y_ref, st_ref, *, reverse, npair):
    C = RWKV_CHUNK
    nchunk = SEQ_BLOCK // C
    s = pl.program_id(2)

    @pl.when(s == 0)
    def _():
        st_ref[...] = jnp.zeros_like(st_ref)

    sgn = -1 if reverse else 1
    row = lax.broadcasted_iota(jnp.int32, (2 * C, 2 * C), 0)
    col = lax.broadcasted_iota(jnp.int32, (2 * C, 2 * C), 1)
    diff = (row - col) * sgn
    strict = diff > 0
    incl = diff >= 0
    same_head = (row // C) == (col // C)
    eye = jnp.where(row == col, 1.0, 0.0)
    r64 = lax.broadcasted_iota(jnp.int32, (C, C), 0)
    c64 = lax.broadcasted_iota(jnp.int32, (C, C), 1)
    tri = jnp.where((r64 - c64) * sgn >= 0, 1.0, 0.0).astype(BF16)
    head0 = lax.broadcasted_iota(jnp.int32, (C, 2 * C), 1) < C

    def chunk(ci, carry):
        ce = (nchunk - 1 - ci) if reverse else ci
        sl = pl.ds(pl.multiple_of(ce * C, C), C)
        lanes = [slice(p * LANES, (p + 1) * LANES) for p in range(npair)]
        get = lambda ref: [ref[sl, ln] for ln in lanes]
        ys, sts = _rwkv_chunk(get(lw_ref), get(r_ref), get(v_ref), get(kk_ref), get(kd_ref), get(be_ref),
                              [st_ref[p] for p in range(npair)], tri, strict, incl, same_head, head0, eye)
        for p in range(npair):
            y_ref[sl, lanes[p]] = ys[p]
            st_ref[p] = sts[p]
        return carry

    lax.fori_loop(0, nchunk, chunk, 0)


def _rwkv_scan(r, v, kk, lw, kd, be, *, n_batch, n_ctx_blk, n_x_blk, reverse, npair=8):
    R, W = r.shape
    ngrp = W // (LANES * npair)
    ns = n_ctx_blk + n_x_blk
    d = 1 if reverse else 0
    blk = functools.partial(_seq_block, n_ctx_blk=n_ctx_blk, n_x_blk=n_x_blk, n_batch=n_batch, reverse=reverse)
    spec = pl.BlockSpec((SEQ_BLOCK, LANES * npair), lambda b, p, s: (blk(b, s), p))
    dspec = pl.BlockSpec((None, SEQ_BLOCK, LANES * npair), lambda b, p, s: (d, blk(b, s), p))
    return pl.pallas_call(
        functools.partial(_rwkv_body, reverse=reverse, npair=npair),
        grid=(n_batch, ngrp, ns),
        in_specs=[spec] * 3 + [dspec] * 3,
        out_specs=spec,
        out_shape=jax.ShapeDtypeStruct((R, W), F32),
        scratch_shapes=[pltpu.VMEM((npair, LANES, LANES), F32)],
        compiler_params=_cparams(("parallel", "parallel", "arbitrary")),
        name="rwkv_scan_bwd" if reverse else "rwkv_scan_fwd",
    )(r, v, kk, lw, kd, be)


S5_SEG = SEQ_BLOCK // SUBLANES
S5_TAB_SLABS = 2 * S5_SEG + 2 + 8
S5_NGB = 4


def _s5_body(u_ref, bw_ref, cw_ref, tab_ref, y_ref, hb_ref, carry_ref, perm_ref, *, reverse):
    W = S5_GB * S5_N
    T = SEQ_BLOCK
    G = range(S5_NGB)
    s = pl.program_id(2)

    @pl.when(s == 0)
    def _():
        carry_ref[...] = jnp.zeros_like(carry_ref)
        prow = lax.broadcasted_iota(jnp.int32, (T, T), 0)
        pcol = lax.broadcasted_iota(jnp.int32, (T, T), 1)
        t_of_row = S5_SEG * (prow % SUBLANES) + prow // SUBLANES
        if reverse:
            t_of_row = T - 1 - t_of_row
        perm_ref[...] = jnp.where(pcol == t_of_row, 1.0, 0.0).astype(BF16)

    perm = perm_ref[...]
    u_p = jnp.dot(perm, u_ref[...].astype(BF16), preferred_element_type=F32).astype(BF16)
    for g in G:
        hb_ref[g] = jnp.dot(u_p[:, g * LANES:(g + 1) * LANES], bw_ref[g], preferred_element_type=F32)

    slab = lambda j: pl.ds(j * SUBLANES, SUBLANES)
    tab = lambda g, n: tab_ref[g, slab(n), :]
    o_seg = 2 * S5_SEG
    o_pow = o_seg + 2
    re, im = slice(0, W), slice(W, 2 * W)
    a = [(tab(g, o_pow), tab(g, o_pow + 1)) for g in G]
    h = [(hb_ref[g, slab(0), re], hb_ref[g, slab(0), im]) for g in G]
    for j in range(1, S5_SEG):
        h = [(hb_ref[g, slab(j), re] + a[g][0] * h[g][0] - a[g][1] * h[g][1],
              hb_ref[g, slab(j), im] + a[g][0] * h[g][1] + a[g][1] * h[g][0]) for g in G]
        for g in G:
            hb_ref[g, slab(j), re] = h[g][0]
            hb_ref[g, slab(j), im] = h[g][1]
    rows8 = lax.broadcasted_iota(jnp.int32, (SUBLANES, W), 0)
    ent = []
    for g in G:
        gr, gi = h[g]
        for n, k in enumerate((1, 2, 4)):
            pr, pi = tab(g, o_pow + 2 + 2 * n), tab(g, o_pow + 3 + 2 * n)
            sr = jnp.where(rows8 >= k, pltpu.roll(gr, k, axis=0), 0.0)
            si = jnp.where(rows8 >= k, pltpu.roll(gi, k, axis=0), 0.0)
            gr, gi = gr + pr * sr - pi * si, gi + pr * si + pi * sr
        cr, ci = carry_ref[g, :, re], carry_ref[g, :, im]
        qr, qi = tab(g, o_seg), tab(g, o_seg + 1)
        gr, gi = gr + qr * cr - qi * ci, gi + qr * ci + qi * cr
        ent.append((jnp.where(rows8 >= 1, pltpu.roll(gr, 1, axis=0), cr),
                    jnp.where(rows8 >= 1, pltpu.roll(gi, 1, axis=0), ci)))
        carry_ref[g, :, re] = jnp.broadcast_to(gr[SUBLANES - 1:SUBLANES, :], (SUBLANES, W))
        carry_ref[g, :, im] = jnp.broadcast_to(gi[SUBLANES - 1:SUBLANES, :], (SUBLANES, W))
    for g in G:
        in_r, in_i = ent[g]
        for j in range(S5_SEG):
            pr, pi = tab(g, j), tab(g, S5_SEG + j)
            hb_ref[g, slab(j), re] = hb_ref[g, slab(j), re] + pr * in_r - pi * in_i
            hb_ref[g, slab(j), im] = hb_ref[g, slab(j), im] + pr * in_i + pi * in_r
    y_p = jnp.concatenate([jnp.dot(hb_ref[g].astype(BF16), cw_ref[g], preferred_element_type=F32) for g in G],
                          axis=1)
    y_ref[...] = _dot_x_ones_tn(perm, y_p)


def _dot_x_ones_tn(ones, x):
    hi, mid, lo = _split3(x)
    f = lambda t: lax.dot_general(ones, t, (((0,), (0,)), ((), ())), preferred_element_type=F32)
    return f(hi) + f(mid) + f(lo)


def _s5_scan(p, col0_blk, bw, cw, tab, *, n_batch, n_ctx_blk, n_x_blk, reverse):
    R = p.shape[0]
    n = S5_NGB
    ngb = S5_G // S5_GB // n
    ns = n_ctx_blk + n_x_blk
    blk = functools.partial(_seq_block, n_ctx_blk=n_ctx_blk, n_x_blk=n_x_blk, n_batch=n_batch, reverse=reverse)
    W2 = 2 * S5_GB * S5_N
    assert col0_blk % n == 0
    return pl.pallas_call(
        functools.partial(_s5_body, reverse=reverse),
        grid=(n_batch, ngb, ns),
        in_specs=[pl.BlockSpec((SEQ_BLOCK, n * LANES), lambda b, g, s: (blk(b, s), col0_blk // n + g)),
                  pl.BlockSpec((n, LANES, W2), lambda b, g, s: (g, 0, 0)),
                  pl.BlockSpec((n, W2, LANES), lambda b, g, s: (g, 0, 0)),
                  pl.BlockSpec((n, S5_TAB_SLABS * SUBLANES, W2 // 2), lambda b, g, s: (g, 0, 0))],
        out_specs=pl.BlockSpec((SEQ_BLOCK, n * LANES), lambda b, g, s: (blk(b, s), g)),
        out_shape=jax.ShapeDtypeStruct((R, S5_W), F32),
        scratch_shapes=[pltpu.VMEM((n, SEQ_BLOCK, W2), F32), pltpu.VMEM((n, SUBLANES, W2), F32),
                        pltpu.VMEM((SEQ_BLOCK, SEQ_BLOCK), BF16)],
        compiler_params=_cparams(("parallel", "parallel", "arbitrary")),
        name="s5_scan_bwd" if reverse else "s5_scan_fwd",
    )(p, bw, cw, tab)


def _s5_weights(A_re, A_im, log_dt, B_re, B_im, C_re, C_im):
    lam_re = jnp.minimum(A_re, -1e-4)
    dt = jnp.exp(log_dt)[:, None]
    mag = jnp.exp(lam_re * dt)
    ab_re, ab_im = mag * jnp.cos(A_im * dt), mag * jnp.sin(A_im * dt)
    den = lam_re * lam_re + A_im * A_im
    f_re = ((ab_re - 1) * lam_re + ab_im * A_im) / den
    f_im = (ab_im * lam_re - (ab_re - 1) * A_im) / den
    bb_re = f_re[..., None] * B_re - f_im[..., None] * B_im
    bb_im = f_re[..., None] * B_im + f_im[..., None] * B_re
    ngb = S5_G // S5_GB
    eye = jnp.eye(S5_GB, dtype=F32)

    def in_block(bb):
        t = bb.reshape(ngb, S5_GB, S5_N, S5_P)
        return jnp.einsum("agnp,gh->agphn", t, eye).reshape(ngb, S5_GB * S5_P, S5_GB * S5_N)

    def out_block(cc):
        t = cc.reshape(ngb, S5_GB, S5_P, S5_N)
        return jnp.einsum("agpn,gh->agnhp", t, eye).reshape(ngb, S5_GB * S5_N, S5_GB * S5_P)

    bw = jnp.concatenate([in_block(bb_re), in_block(bb_im)], axis=2).astype(BF16)
    cw = jnp.concatenate([out_block(C_re), -out_block(C_im)], axis=1).astype(BF16)

    def power(k):
        m = jnp.exp(k * lam_re * dt)
        return m * jnp.cos(k * A_im * dt), m * jnp.sin(k * A_im * dt)

    rep = lambda t: jnp.broadcast_to(t[None], (SUBLANES,) + t.shape)
    slab_pow = [power(float(j + 1)) for j in range(S5_SEG)]
    seg_pow = [power(float(S5_SEG * (i + 1))) for i in range(SUBLANES)]
    slabs = ([rep(c[0]) for c in slab_pow] + [rep(c[1]) for c in slab_pow]
             + [jnp.stack([c[0] for c in seg_pow]), jnp.stack([c[1] for c in seg_pow])])
    for k in (1.0, float(S5_SEG), 2.0 * S5_SEG, 4.0 * S5_SEG):
        slabs += [rep(t) for t in power(k)]
    tab = jnp.concatenate(slabs, axis=0)
    tab = tab.reshape(S5_TAB_SLABS * SUBLANES, ngb, S5_GB * S5_N).transpose(1, 0, 2)
    return bw, cw, tab


def _ssd_body(x_ref, b_ref, c_ref, dt_ref, dta_ref, sel_ref, y_ref, st_ref, *, reverse):
    Q = SSD_CHUNK
    s = pl.program_id(1)

    @pl.when(s == 0)
    def _():
        st_ref[...] = jnp.zeros_like(st_ref)

    row = lax.broadcasted_iota(jnp.int32, (Q, Q), 0)
    col = lax.broadcasted_iota(jnp.int32, (Q, Q), 1)
    mask = (col >= row) if reverse else (col <= row)
    tri = jnp.where(mask, 1.0, 0.0).astype(BF16)
    head0 = lax.broadcasted_iota(jnp.int32, (Q, LANES), 1) < SSD_HD
    head0_rows = lax.broadcasted_iota(jnp.int32, (LANES, SSD_N), 0) < SSD_HD

    h_lo = SSD_HEADS if reverse else 0
    dt = dt_ref[:, h_lo:h_lo + SSD_HEADS]
    dta = dta_ref[:, h_lo:h_lo + SSD_HEADS]
    dtt = dt_ref[...].T[h_lo:h_lo + SSD_HEADS, :]
    dtat = dta_ref[...].T[h_lo:h_lo + SSD_HEADS, :]
    acum = _dot_ones_x(tri, dta)
    acum_t = _dot_x_ones_nt(dtat, tri)
    total = jnp.sum(dta, axis=0, keepdims=True)
    total_t = jnp.sum(dtat, axis=1, keepdims=True)
    e_tot_t = jnp.exp(total_t)
    dt_wide = jnp.dot(dt.astype(BF16), sel_ref[...], preferred_element_type=F32)
    heads_per_group = SSD_HEADS // SSD_GROUPS
    G = range(SSD_GROUPS)
    pairs = [(g, g * heads_per_group + 2 * pr) for g in G for pr in range(heads_per_group // 2)]
    lanes2 = lambda h0: slice(h0 * SSD_HD, (h0 + 2) * SSD_HD)
    bg = [b_ref[:, g * SSD_N:(g + 1) * SSD_N].astype(BF16) for g in G]
    cg = [c_ref[:, g * SSD_N:(g + 1) * SSD_N].astype(BF16) for g in G]
    cb = [lax.dot_general(cg[g], bg[g], (((1,), (1,)), ((), ())), preferred_element_type=F32) for g in G]
    xp = [x_ref[:, lanes2(h0)] for _, h0 in pairs]
    xb = [t.astype(BF16) for t in xp]
    colb = [jnp.broadcast_to(acum[:, h:h + 1], (Q, LANES)) for h in range(SSD_HEADS)]

    def weights(g, h):
        seg = colb[h] - acum_t[h:h + 1, :]
        return (cb[g] * jnp.exp(jnp.where(mask, seg, -jnp.inf)) * dtt[h:h + 1, :]).astype(BF16)

    yd = [(jnp.dot(weights(g, h0), xb[i], preferred_element_type=F32),
           jnp.dot(weights(g, h0 + 1), xb[i], preferred_element_type=F32)) for i, (g, h0) in enumerate(pairs)]
    st = [st_ref[lanes2(h0), :] for _, h0 in pairs]
    yo = [lax.dot_general(cg[g], st[i].astype(BF16), (((1,), (1,)), ((), ())), preferred_element_type=F32)
          for i, (g, _) in enumerate(pairs)]
    for i, (g, h0) in enumerate(pairs):
        scale = jnp.exp(jnp.where(head0, colb[h0], colb[h0 + 1]))
        y_ref[:, lanes2(h0)] = jnp.where(head0, yd[i][0], yd[i][1]) + yo[i] * scale
    xt = [(xp[i] * dt_wide[:, lanes2(h0)]
           * jnp.exp(jnp.where(head0, total[:, h0:h0 + 1] - colb[h0], total[:, h0 + 1:h0 + 2] - colb[h0 + 1]))
           ).astype(BF16) for i, (_, h0) in enumerate(pairs)]
    upd = [lax.dot_general(xt[i], bg[g], (((0,), (0,)), ((), ())), preferred_element_type=F32)
           for i, (g, _) in enumerate(pairs)]
    for i, (_, h0) in enumerate(pairs):
        dec = jnp.where(head0_rows, e_tot_t[h0:h0 + 1, :], e_tot_t[h0 + 1:h0 + 2, :])
        st_ref[lanes2(h0), :] = st[i] * dec + upd[i]


def _ssd_scan(xbc, dt, dta, *, n_batch, n_ctx_blk, n_x_blk, reverse):
    R = xbc.shape[0]
    Q = SSD_CHUNK
    blk = functools.partial(_seq_block, n_ctx_blk=n_ctx_blk, n_x_blk=n_x_blk, n_batch=n_batch, reverse=reverse)
    ns = n_ctx_blk + n_x_blk
    gw = SSD_GROUPS * SSD_N
    row_spec = lambda w, c: pl.BlockSpec((Q, w), lambda b, s: (blk(b, s), c))
    head_of_lane = jnp.arange(SSD_W, dtype=jnp.int32) // SSD_HD
    sel = (jnp.arange(SSD_HEADS, dtype=jnp.int32)[:, None] == head_of_lane[None, :]).astype(BF16)
    return pl.pallas_call(
        functools.partial(_ssd_body, reverse=reverse),
        grid=(n_batch, ns),
        in_specs=[row_spec(SSD_W, 0), row_spec(gw, SSD_W // gw), row_spec(gw, SSD_W // gw + 1),
                  row_spec(LANES, 0), row_spec(LANES, 0), pl.BlockSpec((SSD_HEADS, SSD_W), lambda b, s: (0, 0))],
        out_specs=row_spec(SSD_W, 0),
        out_shape=jax.ShapeDtypeStruct((R, SSD_W), F32),
        scratch_shapes=[pltpu.VMEM((SSD_W, SSD_N), F32)],
        compiler_params=_cparams(("parallel", "arbitrary")),
        name="ssd_scan_bwd" if reverse else "ssd_scan_fwd",
    )(xbc, xbc, xbc, dt, dta, sel)


def _attn_body(q_ref, kp_ref, kc_ref, kn_ref, vp_ref, vc_ref, vn_ref, kx_ref, vx_ref, sink_ref, mix_ref, o_ref,
               band_ref, *, seq_len):
    del mix_ref
    n = pl.program_id(1)
    blk = ATT_BLOCK
    n_ctx = kx_ref.shape[0]
    rep = ATT_HEADS // KV_HEADS
    nk = n_ctx + 3 * blk

    @pl.when(n == 0)
    def _():
        r_q = lax.broadcasted_iota(jnp.int32, (rep * blk, nk), 0) % blk
        j = lax.broadcasted_iota(jnp.int32, (rep * blk, nk), 1)
        ok = (j < n_ctx) | (jnp.abs(r_q - (j - n_ctx - blk)) <= WINDOW)
        band_ref[...] = jnp.where(ok, 0.0, -jnp.inf)

    k_all = jnp.concatenate([kx_ref[...], kp_ref[...], kc_ref[...], kn_ref[...]], axis=0).astype(BF16)
    v_all = jnp.concatenate([vx_ref[...], vp_ref[...], vc_ref[...], vn_ref[...]], axis=0).astype(BF16)
    jb = lax.broadcasted_iota(jnp.int32, (1, nk), 1) - n_ctx
    k_pos = (n - 1) * blk + jb
    outside = (jb >= 0) & ((k_pos < 0) | (k_pos >= seq_len))
    mask = jnp.where(outside, -jnp.inf, band_ref[...])
    scale = ATT_HD ** -0.5
    head_of_row = lax.broadcasted_iota(jnp.int32, (rep * blk, 1), 0) // blk
    G = range(KV_HEADS)
    hd = lambda t, h: t[:, h * ATT_HD:(h + 1) * ATT_HD]
    q = (q_ref[...] * scale).astype(BF16)
    qs = [jnp.concatenate([hd(q, g * rep + i) for i in range(rep)], axis=0) for g in G]
    sinks = []
    for g in G:
        sk = jnp.zeros((rep * blk, 1), F32)
        for i in range(rep):
            sk = jnp.where(head_of_row == i, sink_ref[0:1, g * rep + i:g * rep + i + 1], sk)
        sinks.append(sk)
    sc = [lax.dot_general(qs[g], hd(k_all, g), (((1,), (1,)), ((), ())), preferred_element_type=F32) + mask
          for g in G]
    m = [jnp.maximum(jnp.max(sc[g], axis=-1, keepdims=True), sinks[g]) for g in G]
    e = [jnp.exp(sc[g] - m[g]) for g in G]
    den = [jnp.sum(e[g], axis=-1, keepdims=True) + jnp.exp(sinks[g] - m[g]) for g in G]
    pv = [jnp.dot(e[g].astype(BF16), hd(v_all, g), preferred_element_type=F32) / den[g] for g in G]
    outs = [pv[g][i * blk:(i + 1) * blk] for g in G for i in range(rep)]
    o_ref[...] = jnp.concatenate(outs, axis=1).astype(o_ref.dtype)


def _window_attn(q, k, p, v_col, sink, mix, mix_col, *, n_batch, n_ctx, seq_len):
    nb = seq_len // ATT_BLOCK
    x0 = n_batch * n_ctx // ATT_BLOCK
    vc = v_col // ATT_KV_W
    cur = lambda b, n: x0 + b * nb + n
    prev = lambda b, n: x0 + b * nb + jnp.maximum(n - 1, 0)
    nxt = lambda b, n: x0 + b * nb + jnp.minimum(n + 1, nb - 1)
    k_spec = lambda im: pl.BlockSpec((ATT_BLOCK, ATT_KV_W), lambda b, n: (im(b, n), 0))
    v_spec = lambda im: pl.BlockSpec((ATT_BLOCK, ATT_KV_W), lambda b, n: (im(b, n), vc))
    return pl.pallas_call(
        functools.partial(_attn_body, seq_len=seq_len),
        grid=(n_batch, nb),
        in_specs=[pl.BlockSpec((ATT_BLOCK, ATT_Q_W), lambda b, n: (cur(b, n), 0)),
                  k_spec(prev), k_spec(cur), k_spec(nxt), v_spec(prev), v_spec(cur), v_spec(nxt),
                  pl.BlockSpec((n_ctx, ATT_KV_W), lambda b, n: (b, 0)),
                  pl.BlockSpec((n_ctx, ATT_KV_W), lambda b, n: (b, vc)),
                  pl.BlockSpec((1, ATT_HEADS), lambda b, n: (0, 0)),
                  pl.BlockSpec(memory_space=pl.ANY)],
        out_specs=pl.BlockSpec((ATT_BLOCK, ATT_Q_W), lambda b, n: (b * nb + n, mix_col // ATT_Q_W)),
        out_shape=jax.ShapeDtypeStruct(mix.shape, mix.dtype),
        input_output_aliases={10: 0},
        scratch_shapes=[pltpu.VMEM((ATT_HEADS // KV_HEADS * ATT_BLOCK, n_ctx + 3 * ATT_BLOCK), F32)],
        compiler_params=_cparams(("parallel", "arbitrary")),
        name="window_attn",
    )(q, k, k, k, p, p, p, k, p, sink, mix)


def _tile_edges(row0, tm, n_ctx_rows, ctx_len, seq_len):
    in_ctx = row0 < n_ctx_rows
    pos0 = jnp.where(in_ctx, row0 % ctx_len, (row0 - n_ctx_rows) % seq_len)
    seg = jnp.where(in_ctx, ctx_len, seq_len)
    return pos0 == 0, pos0 + tm == seg


def _shifted_rows(cur, k, prev_blk, next_blk, first, last, rows):
    tm = cur.shape[0]
    if k < 0:
        edge = jnp.where(first, 0.0, pltpu.roll(prev_blk, -k, axis=0))
        body = pltpu.roll(cur, -k, axis=0)
        head = jnp.where(rows < -k, edge, body[0:SUBLANES])
        return jnp.concatenate([head, body[SUBLANES:]], axis=0)
    edge = jnp.where(last, 0.0, pltpu.roll(next_blk, SUBLANES - k, axis=0))
    body = pltpu.roll(cur, tm - k, axis=0)
    tail = jnp.where(rows >= SUBLANES - k, edge, body[tm - SUBLANES:])
    return jnp.concatenate([body[:tm - SUBLANES], tail], axis=0)


def _head_sum_mxu(t, blk):
    n = t.shape[1] // LANES
    hi, lo = _split2(t)
    both = jnp.concatenate([hi, lo], axis=0)
    rows = t.shape[0]
    out = []
    for j in range(n):
        s = jnp.dot(both[:, j * LANES:(j + 1) * LANES], blk, preferred_element_type=F32)
        out.append(s[0:rows] + s[rows:2 * rows])
    return jnp.concatenate(out, axis=1)


def _same_head_matrix():
    r = lax.broadcasted_iota(jnp.int32, (LANES, LANES), 0) // RWKV_HD
    c = lax.broadcasted_iota(jnp.int32, (LANES, LANES), 1) // RWKV_HD
    return jnp.where(r == c, 1.0, 0.0).astype(BF16)


def _softplus(x):
    return jnp.maximum(x, 0.0) + jnp.log(1.0 + jnp.exp(-jnp.abs(x)))


def _rwkv_pre_body(p_ref, hp_ref, hn_ref, mu_ref, lora_ref, gw_ref, w0_ref, a0_ref, kk_ref, ka_ref, rk_ref,
                   r_out, v_out, kk_out, g_out, bonus_out, lw_out, kd_out, be_out, *, tm, geo):
    W = RWKV_W
    first, last = _tile_edges(pl.program_id(0) * tm, tm, *geo)
    rows = lax.broadcasted_iota(jnp.int32, (SUBLANES, 1), 0)

    def tshift(c0, c1):
        cur = p_ref[:, c0:c1]
        prev = _shifted_rows(cur, -1, hp_ref[:, c0:c1], hn_ref[:, c0:c1], first, last, rows)
        nxt = _shifted_rows(cur, 1, hp_ref[:, c0:c1], hn_ref[:, c0:c1], first, last, rows)
        return cur + mu_ref[0:1, c0:c1] * (prev - cur) + mu_ref[1:2, c0:c1] * (nxt - cur)

    r = tshift(0, W)
    k = tshift(W, 2 * W)
    v = tshift(2 * W, 3 * W)
    xl = tshift(3 * W, 3 * W + 4 * LANES)
    lane = lax.broadcasted_iota(jnp.int32, (tm, 2 * LANES), 1)
    lin = jnp.where(lane < DECAY_LORA, jnp.tanh(xl[:, :2 * LANES]), xl[:, :2 * LANES])
    lora = _dot(lin, lora_ref[...])
    g_out[...] = _dot(jax.nn.sigmoid(xl), gw_ref[...])
    blk = _same_head_matrix()
    kk = k * kk_ref[...]
    kk = kk / jnp.maximum(jnp.sqrt(_head_sum_mxu(kk * kk, blk)), 1e-12)
    r_out[...] = r
    v_out[...] = v
    kk_out[...] = kk
    kd_sum = jnp.zeros_like(k)
    for d in range(2):
        w_log = -_softplus(-(w0_ref[d:d + 1, :] + lora[:, d * W:(d + 1) * W])) - 0.5
        a = jax.nn.sigmoid(a0_ref[d:d + 1, :] + lora[:, (2 + d) * W:(3 + d) * W])
        kd = k * (1.0 + (a - 1.0) * ka_ref[...])
        lw_out[d] = -jnp.exp(w_log)
        kd_out[d] = kd
        be_out[d] = a * kk
        kd_sum = kd_sum + kd
    bonus_out[...] = _head_sum_mxu(r * kd_sum * rk_ref[...], blk) * v


def _rwkv_pre(p, mu, lora_w, gate_w, w0, a0, k_k, k_a, r_k, *, geo):
    R = p.shape[0]
    W = RWKV_W
    tm = SEQ_BLOCK
    pw = 3 * W + 4 * LANES
    nb8 = R // SUBLANES
    row = lambda i: (i, 0)
    vec = lambda n: pl.BlockSpec((n, W), lambda i: (0, 0))
    out2 = pl.BlockSpec((2, tm, W), lambda i: (0, i, 0))
    f = jax.ShapeDtypeStruct((R, W), F32)
    f2 = jax.ShapeDtypeStruct((2, R, W), F32)
    return pl.pallas_call(
        functools.partial(_rwkv_pre_body, tm=tm, geo=geo),
        grid=(R // tm,),
        in_specs=[pl.BlockSpec((tm, pw), row),
                  pl.BlockSpec((SUBLANES, pw), lambda i: (jnp.maximum(i * (tm // SUBLANES) - 1, 0), 0)),
                  pl.BlockSpec((SUBLANES, pw), lambda i: (jnp.minimum((i + 1) * (tm // SUBLANES), nb8 - 1), 0)),
                  pl.BlockSpec((2, pw), lambda i: (0, 0)),
                  pl.BlockSpec(lora_w.shape, lambda i: (0, 0)),
                  pl.BlockSpec(gate_w.shape, lambda i: (0, 0)),
                  vec(2), vec(2), vec(1), vec(1), vec(1)],
        out_specs=[pl.BlockSpec((tm, W), row)] * 5 + [out2] * 3,
        out_shape=[f] * 5 + [f2] * 3,
        compiler_params=_cparams(("parallel",)),
        name="rwkv_pre",
    )(p, p, p, mu, lora_w, gate_w, w0, a0, k_k, k_a, r_k)


def _rwkv_post_body(yf_ref, yb_ref, bonus_ref, g_ref, lng_ref, lnb_ref, o_ref):
    blk = _same_head_matrix()
    y = yf_ref[...] + yb_ref[...]
    mean = _head_sum_mxu(y, blk) * (1.0 / RWKV_HD)
    d = y - mean
    var = _head_sum_mxu(d * d, blk) * (1.0 / RWKV_HD)
    yn = d * lax.rsqrt(var + GN_EPS)
    o_ref[...] = ((yn * lng_ref[...] + lnb_ref[...] + bonus_ref[...]) * g_ref[...]).astype(o_ref.dtype)


def _rwkv_post(yf, yb, bonus, g, ln_g, ln_b, out_cols):
    R, W = yf.shape
    tm = _pick(R, STAGE_ROWS)
    row = pl.BlockSpec((tm, W), lambda i: (i, 0))
    vec = pl.BlockSpec((1, W), lambda i: (0, 0))
    return pl.pallas_call(
        _rwkv_post_body,
        grid=(R // tm,),
        in_specs=[row, row, row, row, vec, vec],
        out_specs=row,
        out_shape=jax.ShapeDtypeStruct((R, out_cols), BF16),
        compiler_params=_cparams(("parallel",)),
        name="rwkv_post",
    )(yf, yb, bonus, g, ln_g, ln_b)


def _gelu_tanh(x):
    return 0.5 * x * (1.0 + jnp.tanh(math.sqrt(2.0 / math.pi) * (x + 0.044715 * (x * x * x))))


def _s5_post_body(u0_ref, u1_ref, yf_ref, yb_ref, d_ref, w_ref, b_ref, mix_ref, o_ref):
    del mix_ref
    u = jnp.concatenate([u0_ref[...], u1_ref[...]], axis=1)
    y = _gelu_tanh(d_ref[...] * u + yf_ref[...] + yb_ref[...])
    o_ref[...] = (y * jax.nn.sigmoid(_dot(y, w_ref[...]) + b_ref[...])).astype(o_ref.dtype)


def _s5_post(p, col0, yf, yb, d_skip, glu_w, glu_b, mix):
    R, W = yf.shape
    tm = _pick(R, STAGE_ROWS)
    half = W // 2
    row = pl.BlockSpec((tm, W), lambda i: (i, 0))
    vec = pl.BlockSpec((1, W), lambda i: (0, 0))
    return pl.pallas_call(
        _s5_post_body,
        grid=(R // tm,),
        in_specs=[pl.BlockSpec((tm, half), lambda i: (i, col0 // half)),
                  pl.BlockSpec((tm, half), lambda i: (i, col0 // half + 1)), row, row, vec,
                  pl.BlockSpec((W, W), lambda i: (0, 0)), vec, pl.BlockSpec(memory_space=pl.ANY)],
        out_specs=pl.BlockSpec((tm, W), lambda i: (i, 1)),
        out_shape=jax.ShapeDtypeStruct(mix.shape, mix.dtype),
        input_output_aliases={7: 0},
        compiler_params=_cparams(("parallel",)),
        name="s5_post",
    )(p, p, yf, yb, d_skip, glu_w, glu_b, mix)


def _cd_conv_body(x_ref, hp_ref, hn_ref, w_ref, b_ref, o_ref, *, tm, geo, taps):
    first, last = _tile_edges(pl.program_id(0) * tm, tm, *geo)
    rows = lax.broadcasted_iota(jnp.int32, (SUBLANES, 1), 0)
    cur = x_ref[...]
    half = taps // 2
    acc = b_ref[...] + w_ref[half:half + 1, :] * cur
    for j in range(taps):
        if j != half:
            acc = acc + w_ref[j:j + 1, :] * _shifted_rows(cur, j - half, hp_ref[...], hn_ref[...], first, last, rows)
    o_ref[...] = acc * jax.nn.sigmoid(acc)


def _cd_conv(p, col0, width, conv_w, conv_b, *, geo):
    R = p.shape[0]
    tm = SEQ_BLOCK
    tw = 1024
    taps = conv_w.shape[0]
    nb8 = R // SUBLANES
    c0 = col0 // tw
    w8 = jnp.pad(conv_w, ((0, SUBLANES - taps), (0, 0)))
    return pl.pallas_call(
        functools.partial(_cd_conv_body, tm=tm, geo=geo, taps=taps),
        grid=(R // tm, width // tw),
        in_specs=[pl.BlockSpec((tm, tw), lambda i, j: (i, c0 + j)),
                  pl.BlockSpec((SUBLANES, tw), lambda i, j: (jnp.maximum(i * (tm // SUBLANES) - 1, 0), c0 + j)),
                  pl.BlockSpec((SUBLANES, tw), lambda i, j: (jnp.minimum((i + 1) * (tm // SUBLANES), nb8 - 1), c0 + j)),
                  pl.BlockSpec((SUBLANES, tw), lambda i, j: (0, j)),
                  pl.BlockSpec((1, tw), lambda i, j: (0, j))],
        out_specs=pl.BlockSpec((tm, tw), lambda i, j: (i, j)),
        out_shape=jax.ShapeDtypeStruct((R, width), F32),
        compiler_params=_cparams(("parallel", "parallel")),
        name="cd_conv",
    )(p, p, p, w8, conv_b.reshape(1, width))


def _rope_tile(x, cos, sin):
    lane = lax.broadcasted_iota(jnp.int32, x.shape, 1)
    q = ATT_HD // 4
    partner = jnp.where(lane % (2 * q) < q, pltpu.roll(x, LANES - q, axis=1), pltpu.roll(x, q, axis=1))
    return x * cos + partner * sin


def _cd_rope_dt_body(q_ref, k_ref, dtr_ref, cos_ref, sin_ref, bias_ref, a_ref, q_out, k_out, dt_out, dta_out,
                     *, tm, n_ctx_rows):
    in_ctx = pl.program_id(0) * tm < n_ctx_rows
    cos = jnp.where(in_ctx, 1.0, cos_ref[...])
    sin = jnp.where(in_ctx, 0.0, sin_ref[...])
    for j in range(ATT_Q_W // LANES):
        q_out[:, j * LANES:(j + 1) * LANES] = _rope_tile(q_ref[:, j * LANES:(j + 1) * LANES], cos, sin)
    for j in range(ATT_KV_W // LANES):
        k_out[:, j * LANES:(j + 1) * LANES] = _rope_tile(k_ref[:, j * LANES:(j + 1) * LANES], cos, sin)
    dt = _softplus(dtr_ref[...] + bias_ref[...])
    dt_out[...] = dt
    dta_out[...] = dt * a_ref[...]


def _cd_rope_dt(p, q_col, k_col, dt_col, cos, sin, dt_bias, a_neg, *, n_ctx_rows, seq_len):
    R = p.shape[0]
    tm = _pick(math.gcd(n_ctx_rows, seq_len), STAGE_ROWS)
    tab = pl.BlockSpec((tm, LANES), lambda i: (jnp.maximum(i * tm - n_ctx_rows, 0) % seq_len // tm, 0))
    vec = pl.BlockSpec((1, LANES), lambda i: (0, 0))
    return pl.pallas_call(
        functools.partial(_cd_rope_dt_body, tm=tm, n_ctx_rows=n_ctx_rows),
        grid=(R // tm,),
        in_specs=[pl.BlockSpec((tm, ATT_Q_W), lambda i: (i, q_col // ATT_Q_W)),
                  pl.BlockSpec((tm, ATT_KV_W), lambda i: (i, k_col // ATT_KV_W)),
                  pl.BlockSpec((tm, LANES), lambda i: (i, dt_col // LANES)),
                  tab, tab, vec, vec],
        out_specs=[pl.BlockSpec((tm, ATT_Q_W), lambda i: (i, 0)), pl.BlockSpec((tm, ATT_KV_W), lambda i: (i, 0)),
                   pl.BlockSpec((tm, LANES), lambda i: (i, 0)), pl.BlockSpec((tm, LANES), lambda i: (i, 0))],
        out_shape=[jax.ShapeDtypeStruct((R, ATT_Q_W), F32), jax.ShapeDtypeStruct((R, ATT_KV_W), F32),
                   jax.ShapeDtypeStruct((R, LANES), F32), jax.ShapeDtypeStruct((R, LANES), F32)],
        compiler_params=_cparams(("parallel",)),
        name="cd_rope_dt",
    )(p, p, p, cos, sin, dt_bias, a_neg)


def _ssd_post_body(yf_ref, yb_ref, x_ref, z_ref, d_ref, g_ref, o_ref):
    z = z_ref[...]
    y = (yf_ref[...] + yb_ref[...] + d_ref[...] * x_ref[...]) * (z * jax.nn.sigmoid(z))
    gw = SSD_W // SSD_GROUPS
    for g in range(SSD_GROUPS):
        yg = y[:, g * gw:(g + 1) * gw]
        ms = jnp.mean(yg * yg, axis=-1, keepdims=True)
        o_ref[:, g * gw:(g + 1) * gw] = (yg * lax.rsqrt(ms + EPS) * g_ref[:, g * gw:(g + 1) * gw]).astype(o_ref.dtype)


def _ssd_post(yf, yb, xbc, p, d_skip, norm_g, out_cols, *, n_ctx_rows):
    R = yf.shape[0]
    tm = _pick(math.gcd(n_ctx_rows, R - n_ctx_rows), STAGE_ROWS)
    r0 = n_ctx_rows // tm
    row = pl.BlockSpec((tm, SSD_W), lambda i: (r0 + i, 0))
    vec = pl.BlockSpec((1, SSD_W), lambda i: (0, 0))
    return pl.pallas_call(
        _ssd_post_body,
        grid=((R - n_ctx_rows) // tm,),
        in_specs=[row, row, row, row, vec, vec],
        out_specs=pl.BlockSpec((tm, SSD_W), lambda i: (i, 0)),
        out_shape=jax.ShapeDtypeStruct((R - n_ctx_rows, out_cols), BF16),
        compiler_params=_cparams(("parallel",)),
        name="ssd_post",
    )(yf, yb, xbc, p, d_skip, norm_g)


def _rope_tables(L):
    nf = ATT_HD // 4
    inv_freq = ROPE_BASE ** (-jnp.arange(nf, dtype=F32) / nf)
    t = jnp.arange(L, dtype=jnp.int32)
    row_id = (t // GRID_W).astype(F32)
    col_id = (t % GRID_W).astype(F32)
    ar = row_id[:, None] * inv_freq
    ac = col_id[:, None] * inv_freq
    cos = jnp.concatenate([jnp.cos(ar), jnp.cos(ar), jnp.cos(ac), jnp.cos(ac)], axis=1)
    sin = jnp.concatenate([-jnp.sin(ar), jnp.sin(ar), -jnp.sin(ac), jnp.sin(ac)], axis=1)
    return cos, sin


def kernel(x, c, ctx, c_ctx, ada_w, ada_b, norm1_g, norm2_g, mlp_w1, mlp_w2, final_g, ab_w_in, ab_w_out, rwkv_mu, rwkv_w0, rwkv_w_up, rwkv_a0, rwkv_a_up, rwkv_g_up, rwkv_k_k, rwkv_k_a, rwkv_r_k, rwkv_ln_g, rwkv_ln_b, s5_A_re, s5_A_im, s5_log_dt, s5_B_re, s5_B_im, s5_C_re, s5_C_im, s5_D, s5_glu_w, s5_glu_b, cd_w_in, cd_w_out, ssd_conv_w, ssd_conv_b, ssd_A_log, ssd_dt_bias, ssd_D, ssd_norm_g, attn_sink):
    B, L, D = x.shape
    C = ctx.shape[1]
    RC, RX = B * C, B * L
    R = RC + RX
    assert C % SEQ_BLOCK == 0 and L % SEQ_BLOCK == 0
    geo = dict(n_batch=B, n_ctx_blk=C // SEQ_BLOCK, n_x_blk=L // SEQ_BLOCK)
    geo_ssd = dict(n_batch=B, n_ctx_blk=C // SSD_CHUNK, n_x_blk=L // SSD_CHUNK)
    group_all = lambda r0: jnp.where(r0 < RC, 0, 1 + (r0 - RC) // L)
    group_x = lambda r0: 1 + r0 // L

    cond = jax.nn.silu(jnp.concatenate([c_ctx[None, :], c], axis=0))
    cond = jnp.pad(cond, ((0, SUBLANES - (B + 1) % SUBLANES), (0, 0))) if (B + 1) % SUBLANES else cond
    n_groups = cond.shape[0]

    def modulation(i):
        mod = _mm(cond, ada_w, b_layer=i, name="ada") + ada_b[i]
        return [m.reshape(n_groups, 1, D) for m in jnp.split(mod, 6, axis=-1)]

    sh1, sc1, g1, sh2, sc2, g2 = modulation(0)
    sh1n, sc1n, g1n, sh2n, sc2n, g2n = modulation(1)
    h, xs = _norm_mod_first(ctx.reshape(RC, D), x.reshape(RX, D), norm1_g[0], sh1, sc1, group_all)
    pad_cols = (-RWKV_COLS) % LANES
    w_in = _layer_packed_bf16(ab_w_in, 0, [(0, RWKV_COLS), (None, pad_cols), (RWKV_COLS, S5_W)])
    s5_col0 = RWKV_COLS + pad_cols
    p = _mm(h, w_in, name="ab_in")

    W = RWKV_W
    lora_rows = 2 * LANES
    zw = jnp.zeros((DECAY_LORA, W), F32)
    lora_w = jnp.concatenate([
        jnp.concatenate([rwkv_w_up[0, 0], rwkv_w_up[0, 1], zw, zw], axis=1),
        jnp.concatenate([zw, zw, rwkv_a_up[0, 0], rwkv_a_up[0, 1]], axis=1),
        jnp.zeros((lora_rows - DECAY_LORA - AAA_LORA, 4 * W), F32)], axis=0).astype(BF16)
    gate_w = jnp.pad(rwkv_g_up[0], ((DECAY_LORA + AAA_LORA, pad_cols), (0, 0))).astype(BF16)
    mu = jnp.pad(rwkv_mu[0], ((0, 0), (0, pad_cols)))
    vec = lambda t: t.reshape(1, W)
    r, v, kk, g_gate, bonus, lw, kd, be = _rwkv_pre(
        p, mu, lora_w, gate_w, rwkv_w0[0], rwkv_a0[0], vec(rwkv_k_k[0]), vec(rwkv_k_a[0]), vec(rwkv_r_k[0]),
        geo=(RC, C, L))
    y_dirs = [_rwkv_scan(r, v, kk, lw, kd, be, reverse=(d == 1), **geo) for d in range(2)]
    mix = _rwkv_post(y_dirs[0], y_dirs[1], bonus, g_gate, vec(rwkv_ln_g[0]), vec(rwkv_ln_b[0]), W + S5_W)

    ys_dirs = []
    for d in range(2):
        bw, cw, tab = _s5_weights(s5_A_re[0, d], s5_A_im[0, d], s5_log_dt[0, d], s5_B_re[0, d], s5_B_im[0, d],
                                  s5_C_re[0, d], s5_C_im[0, d])
        ys_dirs.append(_s5_scan(p, s5_col0 // LANES, bw, cw, tab, reverse=(d == 1), **geo))
    mix = _s5_post(p, s5_col0, ys_dirs[0], ys_dirs[1], vec(s5_D[0]), _layer_bf16(s5_glu_w, 0),
                   vec(s5_glu_b[0]), mix)
    xs, h2 = _mm_res_norm(mix, _layer_bf16(ab_w_out, 0), xs, g1, norm2_g[0], sh2, sc2, group_all, name="ab_out")
    m1 = _mm(h2, _layer_bf16(mlp_w1, 0), out_dtype=BF16, act="relu2", name="mlp_up")
    xs = _mm(m1, _layer_bf16(mlp_w2, 0), res=xs, gate=g2, gate_idx=group_all, name="mlp_down")
    h = _norm_mod(xs, norm1_g[1], sh1n, sc1n, group_all)

    sh1, sc1, g1, sh2, sc2, g2 = sh1n, sc1n, g1n, sh2n, sc2n, g2n
    o_dt = SSD_W + XBC_W
    o_q = o_dt + 2 * SSD_HEADS
    n_qkv = ATT_Q_W + 2 * ATT_KV_W
    cd_cols = o_q + n_qkv
    cd_in = _layer_packed_bf16(cd_w_in, 0, [(0, o_dt), (o_q, n_qkv), (o_dt, o_q - o_dt), (None, (-cd_cols) % 512)])
    q_col = o_dt
    k_col = q_col + ATT_Q_W
    v_col = k_col + ATT_KV_W
    dt_col = v_col + ATT_KV_W
    p = _mm(h, cd_in, name="cd_in")

    xbc = _cd_conv(p, SSD_W, XBC_W, ssd_conv_w[0], ssd_conv_b[0], geo=(RC, C, L))
    cos, sin = _rope_tables(L)
    lane_pad = lambda t: jnp.pad(t.reshape(1, 2 * SSD_HEADS), ((0, 0), (0, LANES - 2 * SSD_HEADS)))
    q_rot, k_rot, dt, dta = _cd_rope_dt(
        p, q_col, k_col, dt_col, jnp.tile(cos, (1, LANES // ATT_HD)), jnp.tile(sin, (1, LANES // ATT_HD)),
        lane_pad(ssd_dt_bias[0]), lane_pad(-jnp.exp(ssd_A_log[0])), n_ctx_rows=RC, seq_len=L)
    y_dirs = [_ssd_scan(xbc, dt, dta, reverse=(d == 1), **geo_ssd) for d in range(2)]
    mix = _ssd_post(y_dirs[0], y_dirs[1], xbc, p, jnp.repeat(ssd_D[0], SSD_HD).reshape(1, SSD_W),
                    ssd_norm_g[0].reshape(1, SSD_W), SSD_W + ATT_Q_W, n_ctx_rows=RC)
    mix = _window_attn(q_rot, k_rot, p, v_col, attn_sink[0].reshape(1, ATT_HEADS), mix, SSD_W,
                       n_batch=B, n_ctx=C, seq_len=L)
    xo, h2 = _mm_res_norm(mix, _layer_bf16(cd_w_out, 0), xs, g1, norm2_g[1], sh2, sc2, group_x, res_row0=RC,
                          name="cd_out")
    m1 = _mm(h2, _layer_bf16(mlp_w1, 1), out_dtype=BF16, act="relu2", name="mlp_up")
    xo = _mm(m1, _layer_bf16(mlp_w2, 1), res=xo, gate=g2, gate_idx=group_x, name="mlp_down")
    zero = jnp.zeros((1, 1, D), F32)
    out = _norm_mod(xo, final_g, zero, zero, lambda r0: 0, out_dtype=F32)
    return out.reshape(B, L, D)
```

```python
import functools
import math

import jax
import jax.numpy as jnp
from jax import lax
from jax.experimental import pallas as pl
from jax.experimental.pallas import tpu as pltpu

F32 = jnp.float32
BF16 = jnp.bfloat16

EPS = 1e-6
GN_EPS = 64e-5
GRID_W = 64
ROPE_BASE = 10000.0

RWKV_HEADS = 16
RWKV_HD = 64
RWKV_W = RWKV_HEADS * RWKV_HD
DECAY_LORA = 96
AAA_LORA = 96
GATE_LORA = 256
RWKV_COLS = 3 * RWKV_W + DECAY_LORA + AAA_LORA + GATE_LORA
S5_W = 1024
S5_P = 16
S5_G = S5_W // S5_P
S5_N = 64
SSD_HEADS = 32
SSD_HD = 64
SSD_W = SSD_HEADS * SSD_HD
SSD_GROUPS = 4
SSD_N = 128
SSD_CHUNK = 128
XBC_W = SSD_W + 2 * SSD_GROUPS * SSD_N
ATT_HEADS = 16
KV_HEADS = 4
ATT_HD = 64
ATT_Q_W = ATT_HEADS * ATT_HD
ATT_KV_W = KV_HEADS * ATT_HD
WINDOW = 128
ATT_BLOCK = 128

LANES = 128
SUBLANES = 8
SEQ_BLOCK = 256
RWKV_CHUNK = 64
S5_GB = 8
VMEM_LIMIT = 48 * 1024 * 1024
VMEM_INTERNAL = 6 * 1024 * 1024
CAST_BLOCK_BYTES = 8 * 1024 * 1024
STAGE_ROWS = (512, 256, 128, 64, 32, 16, 8)


def _cparams(sem):
    return pltpu.CompilerParams(dimension_semantics=sem, vmem_limit_bytes=VMEM_LIMIT)


def _pick(n, cands):
    for c in cands:
        if n % c == 0:
            return c
    return n


def _dot(a, b):
    return jnp.dot(a.astype(BF16), b.astype(BF16), preferred_element_type=F32)


def _split3(x):
    hi = x.astype(BF16)
    r1 = x - hi.astype(F32)
    mid = r1.astype(BF16)
    lo = (r1 - mid.astype(F32)).astype(BF16)
    return hi, mid, lo


def _dot_ones_x(tri, x):
    hi, mid, lo = _split3(x)
    f = lambda t: jnp.dot(tri, t, preferred_element_type=F32)
    return f(hi) + f(mid) + f(lo)


def _dot_x_ones_nt(x, tri):
    hi, mid, lo = _split3(x)
    f = lambda t: lax.dot_general(t, tri, (((1,), (1,)), ((), ())), preferred_element_type=F32)
    return f(hi) + f(mid) + f(lo)


def _mm_body(*refs, nk, act, residual):
    if nk > 1:
        *refs, acc_ref = refs
    if residual:
        a_ref, b_ref, res_ref, gate_ref, o_ref = refs
    else:
        a_ref, b_ref, o_ref = refs
    k = pl.program_id(2)
    part = jnp.dot(a_ref[...].astype(BF16), b_ref[...].astype(BF16), preferred_element_type=F32)

    def finish(acc):
        if act == "relu2":
            acc = jnp.square(jnp.maximum(acc, 0.0))
        if residual:
            acc = res_ref[...] + gate_ref[0] * acc
        o_ref[...] = acc.astype(o_ref.dtype)

    if nk == 1:
        finish(part)
    else:
        @pl.when(k == 0)
        def _():
            acc_ref[...] = part

        @pl.when(k > 0)
        def _():
            acc_ref[...] += part

        @pl.when(k == nk - 1)
        def _():
            finish(acc_ref[...])


def _mm(a, b, *, b_layer=0, out_dtype=F32, act=None, res=None, res_row0=0, gate=None, gate_idx=None, name="mm"):
    M, K = a.shape
    N = b.shape[-1]
    residual = res is not None
    a_bytes, b_bytes, o_bytes = a.dtype.itemsize, b.dtype.itemsize, jnp.dtype(out_dtype).itemsize

    def vmem_bytes(tm, tn, tk):
        blocks = tm * tk * a_bytes + tk * tn * b_bytes + tm * tn * o_bytes + (tm * tn * 4 if residual else 0)
        return 2 * blocks + tm * tn * 4

    fits = [(tk == K, tm * tn, tm, tn, tk)
            for tk in (K, 4096, 2048, 1024, 512, 256, 128) if K % tk == 0
            for tn in (2048, 1792, 1536, 1024, 512, 384, 256, 128) if N % tn == 0
            for tm in (1024, 512, 256, 128, 64, 32, 16, 8) if M % tm == 0
            and vmem_bytes(tm, tn, tk) <= VMEM_LIMIT - VMEM_INTERNAL]
    _, _, tm, tn, tk = max(fits)
    nk = K // tk
    if b.ndim == 3:
        b_spec = pl.BlockSpec((None, tk, tn), lambda i, j, k: (b_layer, k, j))
    else:
        b_spec = pl.BlockSpec((tk, tn), lambda i, j, k: (k, j))
    in_specs = [pl.BlockSpec((tm, tk), lambda i, j, k: (i, k)), b_spec]
    args = [a, b]
    if residual:
        if res_row0 % tm:
            res, res_row0 = res[res_row0:res_row0 + M], 0
        r0 = res_row0 // tm
        in_specs += [pl.BlockSpec((tm, tn), lambda i, j, k: (r0 + i, j)),
                     pl.BlockSpec((1, 1, tn), lambda i, j, k: (gate_idx(i * tm), 0, j))]
        args += [res, gate]
    return pl.pallas_call(
        functools.partial(_mm_body, nk=nk, act=act, residual=residual),
        grid=(M // tm, N // tn, nk),
        in_specs=in_specs,
        out_specs=pl.BlockSpec((tm, tn), lambda i, j, k: (i, j)),
        out_shape=jax.ShapeDtypeStruct((M, N), out_dtype),
        scratch_shapes=[pltpu.VMEM((tm, tn), F32)] if nk > 1 else [],
        compiler_params=_cparams(("parallel", "parallel", "arbitrary")),
        name=name,
    )(*args)


def _cast_body(w_ref, o_ref):
    o_ref[...] = w_ref[...].astype(o_ref.dtype)


def _layer_bf16(w, layer):
    _, K, N = w.shape
    tk = next(t for t in (2048, 1024, 512, 256, 128, 64, 32, 16) if K % t == 0 and t * N * 4 <= CAST_BLOCK_BYTES)
    return pl.pallas_call(
        _cast_body,
        grid=(K // tk,),
        in_specs=[pl.BlockSpec((None, tk, N), lambda i: (layer, i, 0))],
        out_specs=pl.BlockSpec((tk, N), lambda i: (i, 0)),
        out_shape=jax.ShapeDtypeStruct((K, N), BF16),
        compiler_params=_cparams(("parallel",)),
        name="weight_bf16",
    )(w)


def _pack_body(w_ref, o_ref, *, pieces):
    w = w_ref[...]
    parts = []
    for src, width in pieces:
        parts.append(jnp.zeros((w.shape[0], width), F32) if src is None else w[:, src:src + width])
    o_ref[...] = jnp.concatenate(parts, axis=1).astype(o_ref.dtype)


def _layer_packed_bf16(w, layer, pieces):
    _, K, N = w.shape
    n_out = sum(width for _, width in pieces)
    tk = next(t for t in (256, 128, 64, 32, 16) if K % t == 0 and t * N * 4 <= CAST_BLOCK_BYTES)
    return pl.pallas_call(
        functools.partial(_pack_body, pieces=tuple(pieces)),
        grid=(K // tk,),
        in_specs=[pl.BlockSpec((None, tk, N), lambda i: (layer, i, 0))],
        out_specs=pl.BlockSpec((tk, n_out), lambda i: (i, 0)),
        out_shape=jax.ShapeDtypeStruct((K, n_out), BF16),
        compiler_params=_cparams(("parallel",)),
        name="weight_pack",
    )(w)


def _mm_res_norm_body(a_ref, b_ref, res_ref, gate_ref, g_ref, sh_ref, sc_ref, *out_refs, nk, keep_x):
    if nk > 1:
        *out_refs, acc_ref = out_refs
    h_ref = out_refs[-1]
    k = pl.program_id(1)
    part = jnp.dot(a_ref[...], b_ref[...], preferred_element_type=F32)

    def finish(acc):
        x = res_ref[...] + gate_ref[0] * acc
        if keep_x:
            out_refs[0][...] = x
        ms = jnp.mean(x * x, axis=-1, keepdims=True)
        xn = x * lax.rsqrt(ms + EPS) * g_ref[...]
        h_ref[...] = (xn * (1.0 + sc_ref[0]) + sh_ref[0]).astype(h_ref.dtype)

    if nk == 1:
        finish(part)
        return

    @pl.when(k == 0)
    def _():
        acc_ref[...] = part

    @pl.when(k > 0)
    def _():
        acc_ref[...] += part

    @pl.when(k == nk - 1)
    def _():
        finish(acc_ref[...])


def _mm_res_norm(a, b, res, gate, norm_g, shift, scale, group_idx, *, res_row0=0, keep_x=True, h_dtype=BF16,
                 name="mm_res_norm"):
    M, K = a.shape
    N = b.shape[1]
    h_bytes = jnp.dtype(h_dtype).itemsize

    def vmem_bytes(tm, tk):
        blocks = tm * tk * 2 + tk * N * 2 + tm * N * 4 + (tm * N * 4 if keep_x else 0) + tm * N * h_bytes
        return 2 * blocks + (tm * N * 4 if tk < K else 0)

    tm, tk = next(((tm, tk) for tk in (K, 1024, 512, 256, 128) if K % tk == 0
                   for tm in (512, 256, 128, 64, 32, 16, 8) if M % tm == 0
                   and vmem_bytes(tm, tk) <= VMEM_LIMIT - VMEM_INTERNAL), (8, 128))
    nk = K // tk
    if res_row0 % tm:
        res, res_row0 = res[res_row0:res_row0 + M], 0
    r0 = res_row0 // tm
    grp = lambda i, k: (group_idx(i * tm), 0, 0)
    row = pl.BlockSpec((tm, N), lambda i, k: (i, 0))
    out_specs = ([row] if keep_x else []) + [row]
    out_shape = ([jax.ShapeDtypeStruct((M, N), F32)] if keep_x else []) + [jax.ShapeDtypeStruct((M, N), h_dtype)]
    out = pl.pallas_call(
        functools.partial(_mm_res_norm_body, nk=nk, keep_x=keep_x),
        grid=(M // tm, nk),
        in_specs=[pl.BlockSpec((tm, tk), lambda i, k: (i, k)),
                  pl.BlockSpec((tk, N), lambda i, k: (k, 0)),
                  pl.BlockSpec((tm, N), lambda i, k: (r0 + i, 0)),
                  pl.BlockSpec((1, 1, N), grp),
                  pl.BlockSpec((1, N), lambda i, k: (0, 0)),
                  pl.BlockSpec((1, 1, N), grp), pl.BlockSpec((1, 1, N), grp)],
        out_specs=out_specs,
        out_shape=out_shape,
        scratch_shapes=[pltpu.VMEM((tm, N), F32)] if nk > 1 else [],
        compiler_params=_cparams(("parallel", "arbitrary")),
        name=name,
    )(a, b, res, gate, norm_g.reshape(1, N), shift, scale)
    return out if keep_x else out[0]


def _norm_body(x_ref, g_ref, sh_ref, sc_ref, o_ref):
    x = x_ref[...]
    ms = jnp.mean(x * x, axis=-1, keepdims=True)
    xn = x * lax.rsqrt(ms + EPS) * g_ref[...]
    o_ref[...] = (xn * (1.0 + sc_ref[0]) + sh_ref[0]).astype(o_ref.dtype)


def _norm_mod(x, g, shift, scale, group_idx, out_dtype=BF16):
    M, D = x.shape
    tm = _pick(M, STAGE_ROWS)
    return pl.pallas_call(
        _norm_body,
        grid=(M // tm,),
        in_specs=[pl.BlockSpec((tm, D), lambda i: (i, 0)),
                  pl.BlockSpec((1, D), lambda i: (0, 0)),
                  pl.BlockSpec((1, 1, D), lambda i: (group_idx(i * tm), 0, 0)),
                  pl.BlockSpec((1, 1, D), lambda i: (group_idx(i * tm), 0, 0))],
        out_specs=pl.BlockSpec((tm, D), lambda i: (i, 0)),
        out_shape=jax.ShapeDtypeStruct((M, D), out_dtype),
        compiler_params=_cparams(("parallel",)),
        name="norm_mod",
    )(x, g.reshape(1, D), shift, scale)


def _norm_first_body(c_ref, x_ref, g_ref, sh_ref, sc_ref, h_ref, xs_ref, *, n_ctx_tiles):
    x = jnp.where(pl.program_id(0) < n_ctx_tiles, c_ref[...], x_ref[...])
    xs_ref[...] = x
    ms = jnp.mean(x * x, axis=-1, keepdims=True)
    xn = x * lax.rsqrt(ms + EPS) * g_ref[...]
    h_ref[...] = (xn * (1.0 + sc_ref[0]) + sh_ref[0]).astype(h_ref.dtype)


def _norm_mod_first(ctx_rows, x_rows, g, shift, scale, group_idx):
    RC, D = ctx_rows.shape
    RX = x_rows.shape[0]
    tm = _pick(math.gcd(RC, RX), STAGE_ROWS)
    nc = RC // tm
    grp = lambda i: (group_idx(i * tm), 0, 0)
    row = pl.BlockSpec((tm, D), lambda i: (i, 0))
    return pl.pallas_call(
        functools.partial(_norm_first_body, n_ctx_tiles=nc),
        grid=((RC + RX) // tm,),
        in_specs=[pl.BlockSpec((tm, D), lambda i: (jnp.minimum(i, nc - 1), 0)),
                  pl.BlockSpec((tm, D), lambda i: (jnp.maximum(i - nc, 0), 0)),
                  pl.BlockSpec((1, D), lambda i: (0, 0)),
                  pl.BlockSpec((1, 1, D), grp), pl.BlockSpec((1, 1, D), grp)],
        out_specs=[row, row],
        out_shape=[jax.ShapeDtypeStruct((RC + RX, D), BF16), jax.ShapeDtypeStruct((RC + RX, D), F32)],
        compiler_params=_cparams(("parallel",)),
        name="norm_first",
    )(ctx_rows, x_rows, g.reshape(1, D), shift, scale)


def _seq_block(b, s, *, n_ctx_blk, n_x_blk, n_batch, reverse):
    if reverse:
        c = b * n_ctx_blk + (n_ctx_blk - 1 - s)
        x = n_batch * n_ctx_blk + b * n_x_blk + (n_x_blk - 1 - (s - n_ctx_blk))
    else:
        c = b * n_ctx_blk + s
        x = n_batch * n_ctx_blk + b * n_x_blk + (s - n_ctx_blk)
    return jnp.where(s < n_ctx_blk, c, x)


def _split2(x):
    hi = x.astype(BF16)
    lo = (x - hi.astype(F32)).astype(BF16)
    return hi, lo


def _rwkv_chunk(lw, r, v, kk, kd, be, st, tri, strict, incl, same_head, head0, eye):
    C = RWKV_CHUNK

    def stack(z):
        return jnp.concatenate([jnp.where(head0, z, 0.0), jnp.where(head0, 0.0, z)], axis=0)

    def unstack(z):
        return z[0:C] + z[C:2 * C]

    bdot = lambda a, b: jnp.dot(a, b, preferred_element_type=F32)
    bdot_nt = lambda a, b: lax.dot_general(a, b, (((1,), (1,)), ((), ())), preferred_element_type=F32)
    bdot_tn = lambda a, b: lax.dot_general(a, b, (((0,), (0,)), ((), ())), preferred_element_type=F32)
    cat = lambda *zs: jnp.concatenate(zs, axis=0)
    P = range(len(lw))
    cum = [_dot_ones_x(tri, lw[p]) for p in P]
    total = [jnp.sum(lw[p], axis=0, keepdims=True) for p in P]
    a_t = [kk[p] * jnp.exp(cum[p] - lw[p]) for p in P]
    r_t = [r[p] * jnp.exp(cum[p]) for p in P]
    a_sp = [_split2(a_t[p]) for p in P]
    r_hi = [r_t[p].astype(BF16) for p in P]
    sa_sp = [_split2(stack(a_t[p])) for p in P]
    sr_hi = [stack(r_t[p]).astype(BF16) for p in P]
    pinv = [jnp.exp(-cum[p]) for p in P]
    bk_sp = [_split2(cat(stack(be[p] * pinv[p]), stack(kd[p] * pinv[p]))) for p in P]
    g1 = [bdot_nt(cat(sa_sp[p][0], sa_sp[p][1], sr_hi[p]), bk_sp[p][0]) for p in P]
    g2 = [bdot_nt(sa_sp[p][0], bk_sp[p][1]) for p in P]
    n_all = [g1[p][0:2 * C] + g1[p][2 * C:4 * C] + g2[p] for p in P]
    x_neg = [jnp.where(strict, -n_all[p][:, 0:2 * C], 0.0) for p in P]
    n_ak = [jnp.where(strict, n_all[p][:, 2 * C:4 * C], 0.0) for p in P]
    m_r = [jnp.concatenate([jnp.where(incl, g1[p][4 * C:6 * C, 0:2 * C], 0.0),
                            jnp.where(incl, g1[p][4 * C:6 * C, 2 * C:4 * C], 0.0)], axis=1).astype(BF16)
           for p in P]
    lcat = lambda *zs: jnp.concatenate(zs, axis=1)
    W2 = 2 * C
    t_inv = [eye + x_neg[p] for p in P]
    xb = [x_neg[p].astype(BF16) for p in P]
    xb = [bdot(xb[p], xb[p]).astype(BF16) for p in P]
    n_pow = int(math.log2(C))
    for it in range(1, n_pow):
        if it + 1 < n_pow:
            both = [bdot(xb[p], lcat(xb[p], t_inv[p].astype(BF16))) for p in P]
            xb = [both[p][:, 0:W2].astype(BF16) for p in P]
            t_inv = [t_inv[p] + both[p][:, W2:2 * W2] for p in P]
        else:
            t_inv = [t_inv[p] + bdot(xb[p], t_inv[p].astype(BF16)) for p in P]
    t_inv = [t_inv[p].astype(BF16) for p in P]
    st_sp = [_split2(st[p]) for p in P]
    as1 = [bdot_nt(a_sp[p][0], cat(st_sp[p][0], st_sp[p][1])) for p in P]
    as2 = [bdot_nt(cat(a_sp[p][1], r_hi[p]), st_sp[p][0]) for p in P]
    v_sp = [_split2(stack(v[p])) for p in P]
    nk_sp = [_split2(n_ak[p]) for p in P]
    nv1 = [bdot(lcat(nk_sp[p][0], nk_sp[p][1]), cat(v_sp[p][0], v_sp[p][0])) for p in P]
    nv2 = [bdot(nk_sp[p][0], v_sp[p][1]) for p in P]
    rhs = [-(stack(as1[p][:, 0:W2] + as1[p][:, W2:2 * W2] + as2[p][0:C]) + nv1[p] + nv2[p]) for p in P]
    u0 = [bdot(t_inv[p], rhs[p].astype(BF16)) for p in P]
    u0_sp = [_split2(u0[p]) for p in P]
    x_sp = [_split2(x_neg[p]) for p in P]
    xu1 = [bdot(lcat(x_sp[p][0], x_sp[p][1]), cat(u0_sp[p][0], u0_sp[p][0])) for p in P]
    xu2 = [bdot(x_sp[p][0], u0_sp[p][1]) for p in P]
    res = [(rhs[p] - u0[p] + xu1[p] + xu2[p]).astype(BF16) for p in P]
    u = [u0[p] + bdot(t_inv[p], res[p]) for p in P]
    y = [stack(as2[p][C:2 * C]) + bdot(m_r[p], cat(u[p].astype(BF16), v_sp[p][0])) for p in P]
    uv_sp = [_split2(cat(unstack(u[p]), v[p])) for p in P]
    pend = [jnp.exp(total[p] - cum[p]) for p in P]
    e_sp = [_split2(cat(be[p] * pend[p], kd[p] * pend[p])) for p in P]
    upd = [bdot_tn(cat(uv_sp[p][0], uv_sp[p][1], uv_sp[p][0]), cat(e_sp[p][0], e_sp[p][0], e_sp[p][1]))
           for p in P]
    st_new = [st[p] * jnp.exp(total[p]) + jnp.where(same_head, upd[p], 0.0) for p in P]
    return [unstack(y[p]) for p in P], st_new


def _rwkv_body(r_ref, v_ref, kk_ref, lw_ref, kd_ref, be_ref, y_ref, st_ref, *, reverse, npair):
    C = RWKV_CHUNK
    nchunk = SEQ_BLOCK // C
    s = pl.program_id(2)

    @pl.when(s == 0)
    def _():
        st_ref[...] = jnp.zeros_like(st_ref)

    sgn = -1 if reverse else 1
    row = lax.broadcasted_iota(jnp.int32, (2 * C, 2 * C), 0)
    col = lax.broadcasted_iota(jnp.int32, (2 * C, 2 * C), 1)
    diff = (row - col) * sgn
    strict = diff > 0
    incl = diff >= 0
    same_head = (row // C) == (col // C)
    eye = jnp.where(row == col, 1.0, 0.0)
    r64 = lax.broadcasted_iota(jnp.int32, (C, C), 0)
    c64 = lax.broadcasted_iota(jnp.int32, (C, C), 1)
    tri = jnp.where((r64 - c64) * sgn >= 0, 1.0, 0.0).astype(BF16)
    head0 = lax.broadcasted_iota(jnp.int32, (C, 2 * C), 1) < C

    def chunk(ci, carry):
        ce = (nchunk - 1 - ci) if reverse else ci
        sl = pl.ds(pl.multiple_of(ce * C, C), C)
        lanes = [slice(p * LANES, (p + 1) * LANES) for p in range(npair)]
        get = lambda ref: [ref[sl, ln] for ln in lanes]
        ys, sts = _rwkv_chunk(get(lw_ref), get(r_ref), get(v_ref), get(kk_ref), get(kd_ref), get(be_ref),
                              [st_ref[p] for p in range(npair)], tri, strict, incl, same_head, head0, eye)
        for p in range(npair):
            y_ref[sl, lanes[p]] = ys[p]
            st_ref[p] = sts[p]
        return carry

    lax.fori_loop(0, nchunk, chunk, 0)


def _rwkv_scan(r, v, kk, lw, kd, be, *, n_batch, n_ctx_blk, n_x_blk, reverse, npair=8):
    R, W = r.shape
    ngrp = W // (LANES * npair)
    ns = n_ctx_blk + n_x_blk
    d = 1 if reverse else 0
    blk = functools.partial(_seq_block, n_ctx_blk=n_ctx_blk, n_x_blk=n_x_blk, n_batch=n_batch, reverse=reverse)
    spec = pl.BlockSpec((SEQ_BLOCK, LANES * npair), lambda b, p, s: (blk(b, s), p))
    dspec = pl.BlockSpec((None, SEQ_BLOCK, LANES * npair), lambda b, p, s: (d, blk(b, s), p))
    return pl.pallas_call(
        functools.partial(_rwkv_body, reverse=reverse, npair=npair),
        grid=(n_batch, ngrp, ns),
        in_specs=[spec] * 3 + [dspec] * 3,
        out_specs=spec,
        out_shape=jax.ShapeDtypeStruct((R, W), F32),
        scratch_shapes=[pltpu.VMEM((npair, LANES, LANES), F32)],
        compiler_params=_cparams(("parallel", "parallel", "arbitrary")),
        name="rwkv_scan_bwd" if reverse else "rwkv_scan_fwd",
    )(r, v, kk, lw, kd, be)


S5_SEG = SEQ_BLOCK // SUBLANES
S5_TAB_SLABS = 2 * S5_SEG + 2 + 8
S5_NGB = 4


def _s5_body(u_ref, bw_ref, cw_ref, tab_ref, y_ref, hb_ref, carry_ref, perm_ref, *, reverse):
    W = S5_GB * S5_N
    T = SEQ_BLOCK
    G = range(S5_NGB)
    s = pl.program_id(2)

    @pl.when(s == 0)
    def _():
        carry_ref[...] = jnp.zeros_like(carry_ref)
        prow = lax.broadcasted_iota(jnp.int32, (T, T), 0)
        pcol = lax.broadcasted_iota(jnp.int32, (T, T), 1)
        t_of_row = S5_SEG * (prow % SUBLANES) + prow // SUBLANES
        if reverse:
            t_of_row = T - 1 - t_of_row
        perm_ref[...] = jnp.where(pcol == t_of_row, 1.0, 0.0).astype(BF16)

    perm = perm_ref[...]
    u_p = jnp.dot(perm, u_ref[...].astype(BF16), preferred_element_type=F32).astype(BF16)
    for g in G:
        hb_ref[g] = jnp.dot(u_p[:, g * LANES:(g + 1) * LANES], bw_ref[g], preferred_element_type=F32)

    slab = lambda j: pl.ds(j * SUBLANES, SUBLANES)
    tab = lambda g, n: tab_ref[g, slab(n), :]
    o_seg = 2 * S5_SEG
    o_pow = o_seg + 2
    re, im = slice(0, W), slice(W, 2 * W)
    a = [(tab(g, o_pow), tab(g, o_pow + 1)) for g in G]
    h = [(hb_ref[g, slab(0), re], hb_ref[g, slab(0), im]) for g in G]
    for j in range(1, S5_SEG):
        h = [(hb_ref[g, slab(j), re] + a[g][0] * h[g][0] - a[g][1] * h[g][1],
              hb_ref[g, slab(j), im] + a[g][0] * h[g][1] + a[g][1] * h[g][0]) for g in G]
        for g in G:
            hb_ref[g, slab(j), re] = h[g][0]
            hb_ref[g, slab(j), im] = h[g][1]
    rows8 = lax.broadcasted_iota(jnp.int32, (SUBLANES, W), 0)
    ent = []
    for g in G:
        gr, gi = h[g]
        for n, k in enumerate((1, 2, 4)):
            pr, pi = tab(g, o_pow + 2 + 2 * n), tab(g, o_pow + 3 + 2 * n)
            sr = jnp.where(rows8 >= k, pltpu.roll(gr, k, axis=0), 0.0)
            si = jnp.where(rows8 >= k, pltpu.roll(gi, k, axis=0), 0.0)
            gr, gi = gr + pr * sr - pi * si, gi + pr * si + pi * sr
        cr, ci = carry_ref[g, :, re], carry_ref[g, :, im]
        qr, qi = tab(g, o_seg), tab(g, o_seg + 1)
        gr, gi = gr + qr * cr - qi * ci, gi + qr * ci + qi * cr
        ent.append((jnp.where(rows8 >= 1, pltpu.roll(gr, 1, axis=0), cr),
                    jnp.where(rows8 >= 1, pltpu.roll(gi, 1, axis=0), ci)))
        carry_ref[g, :, re] = jnp.broadcast_to(gr[SUBLANES - 1:SUBLANES, :], (SUBLANES, W))
        carry_ref[g, :, im] = jnp.broadcast_to(gi[SUBLANES - 1:SUBLANES, :], (SUBLANES, W))
    for g in G:
        in_r, in_i = ent[g]
        for j in range(S5_SEG):
            pr, pi = tab(g, j), tab(g, S5_SEG + j)
            hb_ref[g, slab(j), re] = hb_ref[g, slab(j), re] + pr * in_r - pi * in_i
            hb_ref[g, slab(j), im] = hb_ref[g, slab(j), im] + pr * in_i + pi * in_r
    y_p = jnp.concatenate([jnp.dot(hb_ref[g].astype(BF16), cw_ref[g], preferred_element_type=F32) for g in G],
                          axis=1)
    y_ref[...] = _dot_x_ones_tn(perm, y_p)


def _dot_x_ones_tn(ones, x):
    hi, mid, lo = _split3(x)
    f = lambda t: lax.dot_general(ones, t, (((0,), (0,)), ((), ())), preferred_element_type=F32)
    return f(hi) + f(mid) + f(lo)


def _s5_scan(p, col0_blk, bw, cw, tab, *, n_batch, n_ctx_blk, n_x_blk, reverse):
    R = p.shape[0]
    n = S5_NGB
    ngb = S5_G // S5_GB // n
    ns = n_ctx_blk + n_x_blk
    blk = functools.partial(_seq_block, n_ctx_blk=n_ctx_blk, n_x_blk=n_x_blk, n_batch=n_batch, reverse=reverse)
    W2 = 2 * S5_GB * S5_N
    assert col0_blk % n == 0
    return pl.pallas_call(
        functools.partial(_s5_body, reverse=reverse),
        grid=(n_batch, ngb, ns),
        in_specs=[pl.BlockSpec((SEQ_BLOCK, n * LANES), lambda b, g, s: (blk(b, s), col0_blk // n + g)),
                  pl.BlockSpec((n, LANES, W2), lambda b, g, s: (g, 0, 0)),
                  pl.BlockSpec((n, W2, LANES), lambda b, g, s: (g, 0, 0)),
                  pl.BlockSpec((n, S5_TAB_SLABS * SUBLANES, W2 // 2), lambda b, g, s: (g, 0, 0))],
        out_specs=pl.BlockSpec((SEQ_BLOCK, n * LANES), lambda b, g, s: (blk(b, s), g)),
        out_shape=jax.ShapeDtypeStruct((R, S5_W), F32),
        scratch_shapes=[pltpu.VMEM((n, SEQ_BLOCK, W2), F32), pltpu.VMEM((n, SUBLANES, W2), F32),
                        pltpu.VMEM((SEQ_BLOCK, SEQ_BLOCK), BF16)],
        compiler_params=_cparams(("parallel", "parallel", "arbitrary")),
        name="s5_scan_bwd" if reverse else "s5_scan_fwd",
    )(p, bw, cw, tab)


def _s5_weights(A_re, A_im, log_dt, B_re, B_im, C_re, C_im):
    lam_re = jnp.minimum(A_re, -1e-4)
    dt = jnp.exp(log_dt)[:, None]
    mag = jnp.exp(lam_re * dt)
    ab_re, ab_im = mag * jnp.cos(A_im * dt), mag * jnp.sin(A_im * dt)
    den = lam_re * lam_re + A_im * A_im
    f_re = ((ab_re - 1) * lam_re + ab_im * A_im) / den
    f_im = (ab_im * lam_re - (ab_re - 1) * A_im) / den
    bb_re = f_re[..., None] * B_re - f_im[..., None] * B_im
    bb_im = f_re[..., None] * B_im + f_im[..., None] * B_re
    ngb = S5_G // S5_GB
    eye = jnp.eye(S5_GB, dtype=F32)

    def in_block(bb):
        t = bb.reshape(ngb, S5_GB, S5_N, S5_P)
        return jnp.einsum("agnp,gh->agphn", t, eye).reshape(ngb, S5_GB * S5_P, S5_GB * S5_N)

    def out_block(cc):
        t = cc.reshape(ngb, S5_GB, S5_P, S5_N)
        return jnp.einsum("agpn,gh->agnhp", t, eye).reshape(ngb, S5_GB * S5_N, S5_GB * S5_P)

    bw = jnp.concatenate([in_block(bb_re), in_block(bb_im)], axis=2).astype(BF16)
    cw = jnp.concatenate([out_block(C_re), -out_block(C_im)], axis=1).astype(BF16)

    ones8 = [1.0] * SUBLANES
    slab_e = [[float(j + 1)] * SUBLANES for j in range(S5_SEG)]
    seg_e = [float(S5_SEG * (i + 1)) for i in range(SUBLANES)]
    expo, is_im = [], []
    for group in (slab_e, [seg_e]):
        expo += group + group
        is_im += [False] * len(group) + [True] * len(group)
    for k in (1.0, float(S5_SEG), 2.0 * S5_SEG, 4.0 * S5_SEG):
        expo += [[k * o for o in ones8]] * 2
        is_im += [False, True]
    e = jnp.asarray(expo, F32).reshape(S5_TAB_SLABS * SUBLANES, 1, 1)
    im = jnp.repeat(jnp.asarray(is_im), SUBLANES).reshape(S5_TAB_SLABS * SUBLANES, 1, 1)
    m = jnp.exp(e * (lam_re * dt))
    ang = e * (A_im * dt)
    tab = m * jnp.where(im, jnp.sin(ang), jnp.cos(ang))
    tab = tab.reshape(S5_TAB_SLABS * SUBLANES, ngb, S5_GB * S5_N).transpose(1, 0, 2)
    return bw, cw, tab


def _ssd_body(x_ref, b_ref, c_ref, dt_ref, dta_ref, sel_ref, y_ref, st_ref, *, reverse):
    Q = SSD_CHUNK
    s = pl.program_id(1)

    @pl.when(s == 0)
    def _():
        st_ref[...] = jnp.zeros_like(st_ref)

    row = lax.broadcasted_iota(jnp.int32, (Q, Q), 0)
    col = lax.broadcasted_iota(jnp.int32, (Q, Q), 1)
    mask = (col >= row) if reverse else (col <= row)
    tri = jnp.where(mask, 1.0, 0.0).astype(BF16)
    head0 = lax.broadcasted_iota(jnp.int32, (Q, LANES), 1) < SSD_HD
    head0_rows = lax.broadcasted_iota(jnp.int32, (LANES, SSD_N), 0) < SSD_HD

    h_lo = SSD_HEADS if reverse else 0
    dt = dt_ref[:, h_lo:h_lo + SSD_HEADS]
    dta = dta_ref[:, h_lo:h_lo + SSD_HEADS]
    dtt = dt_ref[...].T[h_lo:h_lo + SSD_HEADS, :]
    dtat = dta_ref[...].T[h_lo:h_lo + SSD_HEADS, :]
    acum = _dot_ones_x(tri, dta)
    acum_t = _dot_x_ones_nt(dtat, tri)
    total = jnp.sum(dta, axis=0, keepdims=True)
    total_t = jnp.sum(dtat, axis=1, keepdims=True)
    e_tot_t = jnp.exp(total_t)
    dt_wide = jnp.dot(dt.astype(BF16), sel_ref[...], preferred_element_type=F32)
    heads_per_group = SSD_HEADS // SSD_GROUPS
    G = range(SSD_GROUPS)
    pairs = [(g, g * heads_per_group + 2 * pr) for g in G for pr in range(heads_per_group // 2)]
    lanes2 = lambda h0: slice(h0 * SSD_HD, (h0 + 2) * SSD_HD)
    bg = [b_ref[:, g * SSD_N:(g + 1) * SSD_N].astype(BF16) for g in G]
    cg = [c_ref[:, g * SSD_N:(g + 1) * SSD_N].astype(BF16) for g in G]
    cb = [lax.dot_general(cg[g], bg[g], (((1,), (1,)), ((), ())), preferred_element_type=F32) for g in G]
    xp = [x_ref[:, lanes2(h0)] for _, h0 in pairs]
    xb = [t.astype(BF16) for t in xp]
    colb = [jnp.broadcast_to(acum[:, h:h + 1], (Q, LANES)) for h in range(SSD_HEADS)]

    def weights(g, h):
        seg = colb[h] - acum_t[h:h + 1, :]
        return (cb[g] * jnp.exp(jnp.where(mask, seg, -jnp.inf)) * dtt[h:h + 1, :]).astype(BF16)

    yd = [(jnp.dot(weights(g, h0), xb[i], preferred_element_type=F32),
           jnp.dot(weights(g, h0 + 1), xb[i], preferred_element_type=F32)) for i, (g, h0) in enumerate(pairs)]
    st = [st_ref[lanes2(h0), :] for _, h0 in pairs]
    yo = [lax.dot_general(cg[g], st[i].astype(BF16), (((1,), (1,)), ((), ())), preferred_element_type=F32)
          for i, (g, _) in enumerate(pairs)]
    for i, (g, h0) in enumerate(pairs):
        scale = jnp.exp(jnp.where(head0, colb[h0], colb[h0 + 1]))
        y_ref[:, lanes2(h0)] = jnp.where(head0, yd[i][0], yd[i][1]) + yo[i] * scale
    xt = [(xp[i] * dt_wide[:, lanes2(h0)]
           * jnp.exp(jnp.where(head0, total[:, h0:h0 + 1] - colb[h0], total[:, h0 + 1:h0 + 2] - colb[h0 + 1]))
           ).astype(BF16) for i, (_, h0) in enumerate(pairs)]
    upd = [lax.dot_general(xt[i], bg[g], (((0,), (0,)), ((), ())), preferred_element_type=F32)
           for i, (g, _) in enumerate(pairs)]
    for i, (_, h0) in enumerate(pairs):
        dec = jnp.where(head0_rows, e_tot_t[h0:h0 + 1, :], e_tot_t[h0 + 1:h0 + 2, :])
        st_ref[lanes2(h0), :] = st[i] * dec + upd[i]


def _ssd_scan(xbc, dt, dta, *, n_batch, n_ctx_blk, n_x_blk, reverse):
    R = xbc.shape[0]
    Q = SSD_CHUNK
    blk = functools.partial(_seq_block, n_ctx_blk=n_ctx_blk, n_x_blk=n_x_blk, n_batch=n_batch, reverse=reverse)
    ns = n_ctx_blk + n_x_blk
    gw = SSD_GROUPS * SSD_N
    row_spec = lambda w, c: pl.BlockSpec((Q, w), lambda b, s: (blk(b, s), c))
    head_of_lane = jnp.arange(SSD_W, dtype=jnp.int32) // SSD_HD
    sel = (jnp.arange(SSD_HEADS, dtype=jnp.int32)[:, None] == head_of_lane[None, :]).astype(BF16)
    return pl.pallas_call(
        functools.partial(_ssd_body, reverse=reverse),
        grid=(n_batch, ns),
        in_specs=[row_spec(SSD_W, 0), row_spec(gw, SSD_W // gw), row_spec(gw, SSD_W // gw + 1),
                  row_spec(LANES, 0), row_spec(LANES, 0), pl.BlockSpec((SSD_HEADS, SSD_W), lambda b, s: (0, 0))],
        out_specs=row_spec(SSD_W, 0),
        out_shape=jax.ShapeDtypeStruct((R, SSD_W), F32),
        scratch_shapes=[pltpu.VMEM((SSD_W, SSD_N), F32)],
        compiler_params=_cparams(("parallel", "arbitrary")),
        name="ssd_scan_bwd" if reverse else "ssd_scan_fwd",
    )(xbc, xbc, xbc, dt, dta, sel)


def _attn_body(q_ref, kp_ref, kc_ref, kn_ref, vp_ref, vc_ref, vn_ref, kx_ref, vx_ref, sink_ref, mix_ref, o_ref,
               band_ref, *, seq_len):
    del mix_ref
    n = pl.program_id(1)
    blk = ATT_BLOCK
    n_ctx = kx_ref.shape[0]
    rep = ATT_HEADS // KV_HEADS
    nk = n_ctx + 3 * blk

    @pl.when(n == 0)
    def _():
        r_q = lax.broadcasted_iota(jnp.int32, (rep * blk, nk), 0) % blk
        j = lax.broadcasted_iota(jnp.int32, (rep * blk, nk), 1)
        ok = (j < n_ctx) | (jnp.abs(r_q - (j - n_ctx - blk)) <= WINDOW)
        band_ref[...] = jnp.where(ok, 0.0, -jnp.inf)

    k_all = jnp.concatenate([kx_ref[...], kp_ref[...], kc_ref[...], kn_ref[...]], axis=0).astype(BF16)
    v_all = jnp.concatenate([vx_ref[...], vp_ref[...], vc_ref[...], vn_ref[...]], axis=0).astype(BF16)
    jb = lax.broadcasted_iota(jnp.int32, (1, nk), 1) - n_ctx
    k_pos = (n - 1) * blk + jb
    outside = (jb >= 0) & ((k_pos < 0) | (k_pos >= seq_len))
    mask = jnp.where(outside, -jnp.inf, band_ref[...])
    scale = ATT_HD ** -0.5
    head_of_row = lax.broadcasted_iota(jnp.int32, (rep * blk, 1), 0) // blk
    G = range(KV_HEADS)
    hd = lambda t, h: t[:, h * ATT_HD:(h + 1) * ATT_HD]
    q = (q_ref[...] * scale).astype(BF16)
    qs = [jnp.concatenate([hd(q, g * rep + i) for i in range(rep)], axis=0) for g in G]
    sinks = []
    for g in G:
        sk = jnp.zeros((rep * blk, 1), F32)
        for i in range(rep):
            sk = jnp.where(head_of_row == i, sink_ref[0:1, g * rep + i:g * rep + i + 1], sk)
        sinks.append(sk)
    sc = [lax.dot_general(qs[g], hd(k_all, g), (((1,), (1,)), ((), ())), preferred_element_type=F32) + mask
          for g in G]
    m = [jnp.maximum(jnp.max(sc[g], axis=-1, keepdims=True), sinks[g]) for g in G]
    e = [jnp.exp(sc[g] - m[g]) for g in G]
    den = [jnp.sum(e[g], axis=-1, keepdims=True) + jnp.exp(sinks[g] - m[g]) for g in G]
    pv = [jnp.dot(e[g].astype(BF16), hd(v_all, g), preferred_element_type=F32) / den[g] for g in G]
    outs = [pv[g][i * blk:(i + 1) * blk] for g in G for i in range(rep)]
    o_ref[...] = jnp.concatenate(outs, axis=1).astype(o_ref.dtype)


def _window_attn(q, k, p, v_col, sink, mix, mix_col, *, n_batch, n_ctx, seq_len):
    nb = seq_len // ATT_BLOCK
    x0 = n_batch * n_ctx // ATT_BLOCK
    vc = v_col // ATT_KV_W
    cur = lambda b, n: x0 + b * nb + n
    prev = lambda b, n: x0 + b * nb + jnp.maximum(n - 1, 0)
    nxt = lambda b, n: x0 + b * nb + jnp.minimum(n + 1, nb - 1)
    k_spec = lambda im: pl.BlockSpec((ATT_BLOCK, ATT_KV_W), lambda b, n: (im(b, n), 0))
    v_spec = lambda im: pl.BlockSpec((ATT_BLOCK, ATT_KV_W), lambda b, n: (im(b, n), vc))
    return pl.pallas_call(
        functools.partial(_attn_body, seq_len=seq_len),
        grid=(n_batch, nb),
        in_specs=[pl.BlockSpec((ATT_BLOCK, ATT_Q_W), lambda b, n: (cur(b, n), 0)),
                  k_spec(prev), k_spec(cur), k_spec(nxt), v_spec(prev), v_spec(cur), v_spec(nxt),
                  pl.BlockSpec((n_ctx, ATT_KV_W), lambda b, n: (b, 0)),
                  pl.BlockSpec((n_ctx, ATT_KV_W), lambda b, n: (b, vc)),
                  pl.BlockSpec((1, ATT_HEADS), lambda b, n: (0, 0)),
                  pl.BlockSpec(memory_space=pl.ANY)],
        out_specs=pl.BlockSpec((ATT_BLOCK, ATT_Q_W), lambda b, n: (b * nb + n, mix_col // ATT_Q_W)),
        out_shape=jax.ShapeDtypeStruct(mix.shape, mix.dtype),
        input_output_aliases={10: 0},
        scratch_shapes=[pltpu.VMEM((ATT_HEADS // KV_HEADS * ATT_BLOCK, n_ctx + 3 * ATT_BLOCK), F32)],
        compiler_params=_cparams(("parallel", "arbitrary")),
        name="window_attn",
    )(q, k, k, k, p, p, p, k, p, sink, mix)


def _tile_edges(row0, tm, n_ctx_rows, ctx_len, seq_len):
    in_ctx = row0 < n_ctx_rows
    pos0 = jnp.where(in_ctx, row0 % ctx_len, (row0 - n_ctx_rows) % seq_len)
    seg = jnp.where(in_ctx, ctx_len, seq_len)
    return pos0 == 0, pos0 + tm == seg


def _shifted_rows(cur, k, prev_blk, next_blk, first, last, rows):
    tm = cur.shape[0]
    if k < 0:
        edge = jnp.where(first, 0.0, pltpu.roll(prev_blk, -k, axis=0))
        body = pltpu.roll(cur, -k, axis=0)
        head = jnp.where(rows < -k, edge, body[0:SUBLANES])
        return jnp.concatenate([head, body[SUBLANES:]], axis=0)
    edge = jnp.where(last, 0.0, pltpu.roll(next_blk, SUBLANES - k, axis=0))
    body = pltpu.roll(cur, tm - k, axis=0)
    tail = jnp.where(rows >= SUBLANES - k, edge, body[tm - SUBLANES:])
    return jnp.concatenate([body[:tm - SUBLANES], tail], axis=0)


def _head_sum_mxu(t, blk):
    n = t.shape[1] // LANES
    hi, lo = _split2(t)
    both = jnp.concatenate([hi, lo], axis=0)
    rows = t.shape[0]
    out = []
    for j in range(n):
        s = jnp.dot(both[:, j * LANES:(j + 1) * LANES], blk, preferred_element_type=F32)
        out.append(s[0:rows] + s[rows:2 * rows])
    return jnp.concatenate(out, axis=1)


def _same_head_matrix():
    r = lax.broadcasted_iota(jnp.int32, (LANES, LANES), 0) // RWKV_HD
    c = lax.broadcasted_iota(jnp.int32, (LANES, LANES), 1) // RWKV_HD
    return jnp.where(r == c, 1.0, 0.0).astype(BF16)


def _softplus(x):
    return jnp.maximum(x, 0.0) + jnp.log(1.0 + jnp.exp(-jnp.abs(x)))


def _rwkv_pre_body(p_ref, hp_ref, hn_ref, mu_ref, lora_ref, gw_ref, w0_ref, a0_ref, kk_ref, ka_ref, rk_ref,
                   r_out, v_out, kk_out, g_out, bonus_out, lw_out, kd_out, be_out, *, tm, geo):
    W = RWKV_W
    first, last = _tile_edges(pl.program_id(0) * tm, tm, *geo)
    rows = lax.broadcasted_iota(jnp.int32, (SUBLANES, 1), 0)

    def tshift(c0, c1):
        cur = p_ref[:, c0:c1]
        prev = _shifted_rows(cur, -1, hp_ref[:, c0:c1], hn_ref[:, c0:c1], first, last, rows)
        nxt = _shifted_rows(cur, 1, hp_ref[:, c0:c1], hn_ref[:, c0:c1], first, last, rows)
        return cur + mu_ref[0:1, c0:c1] * (prev - cur) + mu_ref[1:2, c0:c1] * (nxt - cur)

    r = tshift(0, W)
    k = tshift(W, 2 * W)
    v = tshift(2 * W, 3 * W)
    xl = tshift(3 * W, 3 * W + 4 * LANES)
    lane = lax.broadcasted_iota(jnp.int32, (tm, 2 * LANES), 1)
    lin = jnp.where(lane < DECAY_LORA, jnp.tanh(xl[:, :2 * LANES]), xl[:, :2 * LANES])
    lora = _dot(lin, lora_ref[...])
    g_out[...] = _dot(jax.nn.sigmoid(xl), gw_ref[...])
    blk = _same_head_matrix()
    kk = k * kk_ref[...]
    kk = kk / jnp.maximum(jnp.sqrt(_head_sum_mxu(kk * kk, blk)), 1e-12)
    r_out[...] = r
    v_out[...] = v
    kk_out[...] = kk
    kd_sum = jnp.zeros_like(k)
    for d in range(2):
        w_log = -_softplus(-(w0_ref[d:d + 1, :] + lora[:, d * W:(d + 1) * W])) - 0.5
        a = jax.nn.sigmoid(a0_ref[d:d + 1, :] + lora[:, (2 + d) * W:(3 + d) * W])
        kd = k * (1.0 + (a - 1.0) * ka_ref[...])
        lw_out[d] = -jnp.exp(w_log)
        kd_out[d] = kd
        be_out[d] = a * kk
        kd_sum = kd_sum + kd
    bonus_out[...] = _head_sum_mxu(r * kd_sum * rk_ref[...], blk) * v


def _rwkv_pre(p, mu, lora_w, gate_w, w0, a0, k_k, k_a, r_k, *, geo):
    R = p.shape[0]
    W = RWKV_W
    tm = SEQ_BLOCK
    pw = 3 * W + 4 * LANES
    nb8 = R // SUBLANES
    row = lambda i: (i, 0)
    vec = lambda n: pl.BlockSpec((n, W), lambda i: (0, 0))
    out2 = pl.BlockSpec((2, tm, W), lambda i: (0, i, 0))
    f = jax.ShapeDtypeStruct((R, W), F32)
    f2 = jax.ShapeDtypeStruct((2, R, W), F32)
    return pl.pallas_call(
        functools.partial(_rwkv_pre_body, tm=tm, geo=geo),
        grid=(R // tm,),
        in_specs=[pl.BlockSpec((tm, pw), row),
                  pl.BlockSpec((SUBLANES, pw), lambda i: (jnp.maximum(i * (tm // SUBLANES) - 1, 0), 0)),
                  pl.BlockSpec((SUBLANES, pw), lambda i: (jnp.minimum((i + 1) * (tm // SUBLANES), nb8 - 1), 0)),
                  pl.BlockSpec((2, pw), lambda i: (0, 0)),
                  pl.BlockSpec(lora_w.shape, lambda i: (0, 0)),
                  pl.BlockSpec(gate_w.shape, lambda i: (0, 0)),
                  vec(2), vec(2), vec(1), vec(1), vec(1)],
        out_specs=[pl.BlockSpec((tm, W), row)] * 5 + [out2] * 3,
        out_shape=[f] * 5 + [f2] * 3,
        compiler_params=_cparams(("parallel",)),
        name="rwkv_pre",
    )(p, p, p, mu, lora_w, gate_w, w0, a0, k_k, k_a, r_k)


def _rwkv_post_body(yf_ref, yb_ref, bonus_ref, g_ref, lng_ref, lnb_ref, o_ref):
    blk = _same_head_matrix()
    y = yf_ref[...] + yb_ref[...]
    mean = _head_sum_mxu(y, blk) * (1.0 / RWKV_HD)
    d = y - mean
    var = _head_sum_mxu(d * d, blk) * (1.0 / RWKV_HD)
    yn = d * lax.rsqrt(var + GN_EPS)
    o_ref[...] = ((yn * lng_ref[...] + lnb_ref[...] + bonus_ref[...]) * g_ref[...]).astype(o_ref.dtype)


def _rwkv_post(yf, yb, bonus, g, ln_g, ln_b, out_cols):
    R, W = yf.shape
    tm = _pick(R, STAGE_ROWS)
    row = pl.BlockSpec((tm, W), lambda i: (i, 0))
    vec = pl.BlockSpec((1, W), lambda i: (0, 0))
    return pl.pallas_call(
        _rwkv_post_body,
        grid=(R // tm,),
        in_specs=[row, row, row, row, vec, vec],
        out_specs=row,
        out_shape=jax.ShapeDtypeStruct((R, out_cols), BF16),
        compiler_params=_cparams(("parallel",)),
        name="rwkv_post",
    )(yf, yb, bonus, g, ln_g, ln_b)


def _gelu_tanh(x):
    return 0.5 * x * (1.0 + jnp.tanh(math.sqrt(2.0 / math.pi) * (x + 0.044715 * (x * x * x))))


def _s5_post_body(u0_ref, u1_ref, yf_ref, yb_ref, d_ref, w_ref, b_ref, mix_ref, o_ref):
    del mix_ref
    u = jnp.concatenate([u0_ref[...], u1_ref[...]], axis=1)
    y = _gelu_tanh(d_ref[...] * u + yf_ref[...] + yb_ref[...])
    o_ref[...] = (y * jax.nn.sigmoid(_dot(y, w_ref[...]) + b_ref[...])).astype(o_ref.dtype)


def _s5_post(p, col0, yf, yb, d_skip, glu_w, glu_b, mix):
    R, W = yf.shape
    tm = _pick(R, STAGE_ROWS)
    half = W // 2
    row = pl.BlockSpec((tm, W), lambda i: (i, 0))
    vec = pl.BlockSpec((1, W), lambda i: (0, 0))
    return pl.pallas_call(
        _s5_post_body,
        grid=(R // tm,),
        in_specs=[pl.BlockSpec((tm, half), lambda i: (i, col0 // half)),
                  pl.BlockSpec((tm, half), lambda i: (i, col0 // half + 1)), row, row, vec,
                  pl.BlockSpec((W, W), lambda i: (0, 0)), vec, pl.BlockSpec(memory_space=pl.ANY)],
        out_specs=pl.BlockSpec((tm, W), lambda i: (i, 1)),
        out_shape=jax.ShapeDtypeStruct(mix.shape, mix.dtype),
        input_output_aliases={7: 0},
        compiler_params=_cparams(("parallel",)),
        name="s5_post",
    )(p, p, yf, yb, d_skip, glu_w, glu_b, mix)


def _cd_conv_body(x_ref, hp_ref, hn_ref, w_ref, b_ref, o_ref, *, tm, geo, taps):
    first, last = _tile_edges(pl.program_id(0) * tm, tm, *geo)
    rows = lax.broadcasted_iota(jnp.int32, (SUBLANES, 1), 0)
    cur = x_ref[...]
    half = taps // 2
    acc = b_ref[...] + w_ref[half:half + 1, :] * cur
    for j in range(taps):
        if j != half:
            acc = acc + w_ref[j:j + 1, :] * _shifted_rows(cur, j - half, hp_ref[...], hn_ref[...], first, last, rows)
    o_ref[...] = acc * jax.nn.sigmoid(acc)


def _cd_conv(p, col0, width, conv_w, conv_b, *, geo):
    R = p.shape[0]
    tm = SEQ_BLOCK
    tw = 1024
    taps = conv_w.shape[0]
    nb8 = R // SUBLANES
    c0 = col0 // tw
    w8 = jnp.pad(conv_w, ((0, SUBLANES - taps), (0, 0)))
    return pl.pallas_call(
        functools.partial(_cd_conv_body, tm=tm, geo=geo, taps=taps),
        grid=(R // tm, width // tw),
        in_specs=[pl.BlockSpec((tm, tw), lambda i, j: (i, c0 + j)),
                  pl.BlockSpec((SUBLANES, tw), lambda i, j: (jnp.maximum(i * (tm // SUBLANES) - 1, 0), c0 + j)),
                  pl.BlockSpec((SUBLANES, tw), lambda i, j: (jnp.minimum((i + 1) * (tm // SUBLANES), nb8 - 1), c0 + j)),
                  pl.BlockSpec((SUBLANES, tw), lambda i, j: (0, j)),
                  pl.BlockSpec((1, tw), lambda i, j: (0, j))],
        out_specs=pl.BlockSpec((tm, tw), lambda i, j: (i, j)),
        out_shape=jax.ShapeDtypeStruct((R, width), F32),
        compiler_params=_cparams(("parallel", "parallel")),
        name="cd_conv",
    )(p, p, p, w8, conv_b.reshape(1, width))


def _rope_tile(x, cos, sin):
    lane = lax.broadcasted_iota(jnp.int32, x.shape, 1)
    q = ATT_HD // 4
    partner = jnp.where(lane % (2 * q) < q, pltpu.roll(x, LANES - q, axis=1), pltpu.roll(x, q, axis=1))
    return x * cos + partner * sin


def _cd_rope_dt_body(q_ref, k_ref, dtr_ref, cos_ref, sin_ref, bias_ref, a_ref, q_out, k_out, dt_out, dta_out,
                     *, tm, n_ctx_rows):
    in_ctx = pl.program_id(0) * tm < n_ctx_rows
    cos = jnp.where(in_ctx, 1.0, cos_ref[...])
    sin = jnp.where(in_ctx, 0.0, sin_ref[...])
    for j in range(ATT_Q_W // LANES):
        q_out[:, j * LANES:(j + 1) * LANES] = _rope_tile(q_ref[:, j * LANES:(j + 1) * LANES], cos, sin)
    for j in range(ATT_KV_W // LANES):
        k_out[:, j * LANES:(j + 1) * LANES] = _rope_tile(k_ref[:, j * LANES:(j + 1) * LANES], cos, sin)
    dt = _softplus(dtr_ref[...] + bias_ref[...])
    dt_out[...] = dt
    dta_out[...] = dt * a_ref[...]


def _cd_rope_dt(p, q_col, k_col, dt_col, cos, sin, dt_bias, a_neg, *, n_ctx_rows, seq_len):
    R = p.shape[0]
    tm = _pick(math.gcd(n_ctx_rows, seq_len), STAGE_ROWS)
    tab = pl.BlockSpec((tm, LANES), lambda i: (jnp.maximum(i * tm - n_ctx_rows, 0) % seq_len // tm, 0))
    vec = pl.BlockSpec((1, LANES), lambda i: (0, 0))
    return pl.pallas_call(
        functools.partial(_cd_rope_dt_body, tm=tm, n_ctx_rows=n_ctx_rows),
        grid=(R // tm,),
        in_specs=[pl.BlockSpec((tm, ATT_Q_W), lambda i: (i, q_col // ATT_Q_W)),
                  pl.BlockSpec((tm, ATT_KV_W), lambda i: (i, k_col // ATT_KV_W)),
                  pl.BlockSpec((tm, LANES), lambda i: (i, dt_col // LANES)),
                  tab, tab, vec, vec],
        out_specs=[pl.BlockSpec((tm, ATT_Q_W), lambda i: (i, 0)), pl.BlockSpec((tm, ATT_KV_W), lambda i: (i, 0)),
                   pl.BlockSpec((tm, LANES), lambda i: (i, 0)), pl.BlockSpec((tm, LANES), lambda i: (i, 0))],
        out_shape=[jax.ShapeDtypeStruct((R, ATT_Q_W), F32), jax.ShapeDtypeStruct((R, ATT_KV_W), F32),
                   jax.ShapeDtypeStruct((R, LANES), F32), jax.ShapeDtypeStruct((R, LANES), F32)],
        compiler_params=_cparams(("parallel",)),
        name="cd_rope_dt",
    )(p, p, p, cos, sin, dt_bias, a_neg)


def _ssd_post_body(yf_ref, yb_ref, x_ref, z_ref, d_ref, g_ref, o_ref):
    z = z_ref[...]
    y = (yf_ref[...] + yb_ref[...] + d_ref[...] * x_ref[...]) * (z * jax.nn.sigmoid(z))
    gw = SSD_W // SSD_GROUPS
    for g in range(SSD_GROUPS):
        yg = y[:, g * gw:(g + 1) * gw]
        ms = jnp.mean(yg * yg, axis=-1, keepdims=True)
        o_ref[:, g * gw:(g + 1) * gw] = (yg * lax.rsqrt(ms + EPS) * g_ref[:, g * gw:(g + 1) * gw]).astype(o_ref.dtype)


def _ssd_post(yf, yb, xbc, p, d_skip, norm_g, out_cols, *, n_ctx_rows):
    R = yf.shape[0]
    tm = _pick(math.gcd(n_ctx_rows, R - n_ctx_rows), STAGE_ROWS)
    r0 = n_ctx_rows // tm
    row = pl.BlockSpec((tm, SSD_W), lambda i: (r0 + i, 0))
    vec = pl.BlockSpec((1, SSD_W), lambda i: (0, 0))
    return pl.pallas_call(
        _ssd_post_body,
        grid=((R - n_ctx_rows) // tm,),
        in_specs=[row, row, row, row, vec, vec],
        out_specs=pl.BlockSpec((tm, SSD_W), lambda i: (i, 0)),
        out_shape=jax.ShapeDtypeStruct((R - n_ctx_rows, out_cols), BF16),
        compiler_params=_cparams(("parallel",)),
        name="ssd_post",
    )(yf, yb, xbc, p, d_skip, norm_g)


def _rope_tables(L):
    nf = ATT_HD // 4
    inv_freq = ROPE_BASE ** (-jnp.arange(nf, dtype=F32) / nf)
    t = jnp.arange(L, dtype=jnp.int32)
    row_id = (t // GRID_W).astype(F32)
    col_id = (t % GRID_W).astype(F32)
    ar = row_id[:, None] * inv_freq
    ac = col_id[:, None] * inv_freq
    cos = jnp.concatenate([jnp.cos(ar), jnp.cos(ar), jnp.cos(ac), jnp.cos(ac)], axis=1)
    sin = jnp.concatenate([-jnp.sin(ar), jnp.sin(ar), -jnp.sin(ac), jnp.sin(ac)], axis=1)
    return cos, sin


def kernel(x, c, ctx, c_ctx, ada_w, ada_b, norm1_g, norm2_g, mlp_w1, mlp_w2, final_g, ab_w_in, ab_w_out, rwkv_mu, rwkv_w0, rwkv_w_up, rwkv_a0, rwkv_a_up, rwkv_g_up, rwkv_k_k, rwkv_k_a, rwkv_r_k, rwkv_ln_g, rwkv_ln_b, s5_A_re, s5_A_im, s5_log_dt, s5_B_re, s5_B_im, s5_C_re, s5_C_im, s5_D, s5_glu_w, s5_glu_b, cd_w_in, cd_w_out, ssd_conv_w, ssd_conv_b, ssd_A_log, ssd_dt_bias, ssd_D, ssd_norm_g, attn_sink):
    B, L, D = x.shape
    C = ctx.shape[1]
    RC, RX = B * C, B * L
    R = RC + RX
    assert C % SEQ_BLOCK == 0 and L % SEQ_BLOCK == 0
    geo = dict(n_batch=B, n_ctx_blk=C // SEQ_BLOCK, n_x_blk=L // SEQ_BLOCK)
    geo_ssd = dict(n_batch=B, n_ctx_blk=C // SSD_CHUNK, n_x_blk=L // SSD_CHUNK)
    group_all = lambda r0: jnp.where(r0 < RC, 0, 1 + (r0 - RC) // L)
    group_x = lambda r0: 1 + r0 // L

    cond = jax.nn.silu(jnp.concatenate([c_ctx[None, :], c], axis=0))
    cond = jnp.pad(cond, ((0, SUBLANES - (B + 1) % SUBLANES), (0, 0))) if (B + 1) % SUBLANES else cond
    n_groups = cond.shape[0]

    def modulation(i):
        mod = _mm(cond, ada_w, b_layer=i, name="ada") + ada_b[i]
        return [m.reshape(n_groups, 1, D) for m in jnp.split(mod, 6, axis=-1)]

    sh1, sc1, g1, sh2, sc2, g2 = modulation(0)
    sh1n, sc1n, g1n, sh2n, sc2n, g2n = modulation(1)
    h, xs = _norm_mod_first(ctx.reshape(RC, D), x.reshape(RX, D), norm1_g[0], sh1, sc1, group_all)
    pad_cols = (-RWKV_COLS) % LANES
    w_in = _layer_packed_bf16(ab_w_in, 0, [(0, RWKV_COLS), (None, pad_cols), (RWKV_COLS, S5_W)])
    s5_col0 = RWKV_COLS + pad_cols
    p = _mm(h, w_in, name="ab_in")

    W = RWKV_W
    lora_rows = 2 * LANES
    zw = jnp.zeros((DECAY_LORA, W), F32)
    lora_w = jnp.concatenate([
        jnp.concatenate([rwkv_w_up[0, 0], rwkv_w_up[0, 1], zw, zw], axis=1),
        jnp.concatenate([zw, zw, rwkv_a_up[0, 0], rwkv_a_up[0, 1]], axis=1),
        jnp.zeros((lora_rows - DECAY_LORA - AAA_LORA, 4 * W), F32)], axis=0).astype(BF16)
    gate_w = jnp.pad(rwkv_g_up[0], ((DECAY_LORA + AAA_LORA, pad_cols), (0, 0))).astype(BF16)
    mu = jnp.pad(rwkv_mu[0], ((0, 0), (0, pad_cols)))
    vec = lambda t: t.reshape(1, W)
    r, v, kk, g_gate, bonus, lw, kd, be = _rwkv_pre(
        p, mu, lora_w, gate_w, rwkv_w0[0], rwkv_a0[0], vec(rwkv_k_k[0]), vec(rwkv_k_a[0]), vec(rwkv_r_k[0]),
        geo=(RC, C, L))
    y_dirs = [_rwkv_scan(r, v, kk, lw, kd, be, reverse=(d == 1), **geo) for d in range(2)]
    mix = _rwkv_post(y_dirs[0], y_dirs[1], bonus, g_gate, vec(rwkv_ln_g[0]), vec(rwkv_ln_b[0]), W + S5_W)

    ys_dirs = []
    for d in range(2):
        bw, cw, tab = _s5_weights(s5_A_re[0, d], s5_A_im[0, d], s5_log_dt[0, d], s5_B_re[0, d], s5_B_im[0, d],
                                  s5_C_re[0, d], s5_C_im[0, d])
        ys_dirs.append(_s5_scan(p, s5_col0 // LANES, bw, cw, tab, reverse=(d == 1), **geo))
    mix = _s5_post(p, s5_col0, ys_dirs[0], ys_dirs[1], vec(s5_D[0]), _layer_bf16(s5_glu_w, 0),
                   vec(s5_glu_b[0]), mix)
    xs, h2 = _mm_res_norm(mix, _layer_bf16(ab_w_out, 0), xs, g1, norm2_g[0], sh2, sc2, group_all, name="ab_out")
    m1 = _mm(h2, _layer_bf16(mlp_w1, 0), out_dtype=BF16, act="relu2", name="mlp_up")
    xs = _mm(m1, _layer_bf16(mlp_w2, 0), res=xs, gate=g2, gate_idx=group_all, name="mlp_down")
    h = _norm_mod(xs, norm1_g[1], sh1n, sc1n, group_all)

    sh1, sc1, g1, sh2, sc2, g2 = sh1n, sc1n, g1n, sh2n, sc2n, g2n
    o_dt = SSD_W + XBC_W
    o_q = o_dt + 2 * SSD_HEADS
    n_qkv = ATT_Q_W + 2 * ATT_KV_W
    cd_cols = o_q + n_qkv
    cd_in = _layer_packed_bf16(cd_w_in, 0, [(0, o_dt), (o_q, n_qkv), (o_dt, o_q - o_dt), (None, (-cd_cols) % 512)])
    q_col = o_dt
    k_col = q_col + ATT_Q_W
    v_col = k_col + ATT_KV_W
    dt_col = v_col + ATT_KV_W
    p = _mm(h, cd_in, name="cd_in")

    xbc = _cd_conv(p, SSD_W, XBC_W, ssd_conv_w[0], ssd_conv_b[0], geo=(RC, C, L))
    cos, sin = _rope_tables(L)
    lane_pad = lambda t: jnp.pad(t.reshape(1, 2 * SSD_HEADS), ((0, 0), (0, LANES - 2 * SSD_HEADS)))
    q_rot, k_rot, dt, dta = _cd_rope_dt(
        p, q_col, k_col, dt_col, jnp.tile(cos, (1, LANES // ATT_HD)), jnp.tile(sin, (1, LANES // ATT_HD)),
        lane_pad(ssd_dt_bias[0]), lane_pad(-jnp.exp(ssd_A_log[0])), n_ctx_rows=RC, seq_len=L)
    y_dirs = [_ssd_scan(xbc, dt, dta, reverse=(d == 1), **geo_ssd) for d in range(2)]
    mix = _ssd_post(y_dirs[0], y_dirs[1], xbc, p, jnp.repeat(ssd_D[0], SSD_HD).reshape(1, SSD_W),
                    ssd_norm_g[0].reshape(1, SSD_W), SSD_W + ATT_Q_W, n_ctx_rows=RC)
    mix = _window_attn(q_rot, k_rot, p, v_col, attn_sink[0].reshape(1, ATT_HEADS), mix, SSD_W,
                       n_batch=B, n_ctx=C, seq_len=L)
    xo, h2 = _mm_res_norm(mix, _layer_bf16(cd_w_out, 0), xs, g1, norm2_g[1], sh2, sc2, group_x, res_row0=RC,
                          name="cd_out")
    m1 = _mm(h2, _layer_bf16(mlp_w1, 1), out_dtype=BF16, act="relu2", name="mlp_up")
    xo = _mm(m1, _layer_bf16(mlp_w2, 1), res=xo, gate=g2, gate_idx=group_x, name="mlp_down")
    zero = jnp.zeros((1, 1, D), F32)
    out = _norm_mod(xo, final_g, zero, zero, lambda r0: 0, out_dtype=F32)
    return out.reshape(B, L, D)
```

```python
import functools
import math

import jax
import jax.numpy as jnp
from jax import lax
from jax.experimental import pallas as pl
from jax.experimental.pallas import tpu as pltpu

F32 = jnp.float32
BF16 = jnp.bfloat16

EPS = 1e-6
GN_EPS = 64e-5
GRID_W = 64
ROPE_BASE = 10000.0

RWKV_HEADS = 16
RWKV_HD = 64
RWKV_W = RWKV_HEADS * RWKV_HD
DECAY_LORA = 96
AAA_LORA = 96
GATE_LORA = 256
RWKV_COLS = 3 * RWKV_W + DECAY_LORA + AAA_LORA + GATE_LORA
S5_W = 1024
S5_P = 16
S5_G = S5_W // S5_P
S5_N = 64
SSD_HEADS = 32
SSD_HD = 64
SSD_W = SSD_HEADS * SSD_HD
SSD_GROUPS = 4
SSD_N = 128
SSD_CHUNK = 128
XBC_W = SSD_W + 2 * SSD_GROUPS * SSD_N
ATT_HEADS = 16
KV_HEADS = 4
ATT_HD = 64
ATT_Q_W = ATT_HEADS * ATT_HD
ATT_KV_W = KV_HEADS * ATT_HD
WINDOW = 128
ATT_BLOCK = 128

LANES = 128
SUBLANES = 8
SEQ_BLOCK = 256
RWKV_CHUNK = 64
S5_GB = 8
VMEM_LIMIT = 48 * 1024 * 1024
VMEM_INTERNAL = 6 * 1024 * 1024
CAST_BLOCK_BYTES = 8 * 1024 * 1024
STAGE_ROWS = (512, 256, 128, 64, 32, 16, 8)


def _cparams(sem):
    return pltpu.CompilerParams(dimension_semantics=sem, vmem_limit_bytes=VMEM_LIMIT)


def _pick(n, cands):
    for c in cands:
        if n % c == 0:
            return c
    return n


def _dot(a, b):
    return jnp.dot(a.astype(BF16), b.astype(BF16), preferred_element_type=F32)


def _split3(x):
    hi = x.astype(BF16)
    r1 = x - hi.astype(F32)
    mid = r1.astype(BF16)
    lo = (r1 - mid.astype(F32)).astype(BF16)
    return hi, mid, lo


def _dot_ones_x(tri, x):
    hi, mid, lo = _split3(x)
    f = lambda t: jnp.dot(tri, t, preferred_element_type=F32)
    return f(hi) + f(mid) + f(lo)


def _dot_x_ones_nt(x, tri):
    hi, mid, lo = _split3(x)
    f = lambda t: lax.dot_general(t, tri, (((1,), (1,)), ((), ())), preferred_element_type=F32)
    return f(hi) + f(mid) + f(lo)


def _mm_body(*refs, nk, act, residual):
    if nk > 1:
        *refs, acc_ref = refs
    if residual:
        a_ref, b_ref, res_ref, gate_ref, o_ref = refs
    else:
        a_ref, b_ref, o_ref = refs
    k = pl.program_id(2)
    part = jnp.dot(a_ref[...].astype(BF16), b_ref[...].astype(BF16), preferred_element_type=F32)

    def finish(acc):
        if act == "relu2":
            acc = jnp.square(jnp.maximum(acc, 0.0))
        if residual:
            acc = res_ref[...] + gate_ref[0] * acc
        o_ref[...] = acc.astype(o_ref.dtype)

    if nk == 1:
        finish(part)
    else:
        @pl.when(k == 0)
        def _():
            acc_ref[...] = part

        @pl.when(k > 0)
        def _():
            acc_ref[...] += part

        @pl.when(k == nk - 1)
        def _():
            finish(acc_ref[...])


def _mm(a, b, *, b_layer=0, out_dtype=F32, act=None, res=None, res_row0=0, gate=None, gate_idx=None, name="mm"):
    M, K = a.shape
    N = b.shape[-1]
    residual = res is not None
    a_bytes, b_bytes, o_bytes = a.dtype.itemsize, b.dtype.itemsize, jnp.dtype(out_dtype).itemsize

    def vmem_bytes(tm, tn, tk):
        blocks = tm * tk * a_bytes + tk * tn * b_bytes + tm * tn * o_bytes + (tm * tn * 4 if residual else 0)
        return 2 * blocks + tm * tn * 4

    fits = [(tk == K, tm * tn, tm, tn, tk)
            for tk in (K, 4096, 2048, 1024, 512, 256, 128) if K % tk == 0
            for tn in (2048, 1792, 1536, 1024, 512, 384, 256, 128) if N % tn == 0
            for tm in (1024, 512, 256, 128, 64, 32, 16, 8) if M % tm == 0
            and vmem_bytes(tm, tn, tk) <= VMEM_LIMIT - VMEM_INTERNAL]
    _, _, tm, tn, tk = max(fits)
    nk = K // tk
    if b.ndim == 3:
        b_spec = pl.BlockSpec((None, tk, tn), lambda i, j, k: (b_layer, k, j))
    else:
        b_spec = pl.BlockSpec((tk, tn), lambda i, j, k: (k, j))
    in_specs = [pl.BlockSpec((tm, tk), lambda i, j, k: (i, k)), b_spec]
    args = [a, b]
    if residual:
        if res_row0 % tm:
            res, res_row0 = res[res_row0:res_row0 + M], 0
        r0 = res_row0 // tm
        in_specs += [pl.BlockSpec((tm, tn), lambda i, j, k: (r0 + i, j)),
                     pl.BlockSpec((1, 1, tn), lambda i, j, k: (gate_idx(i * tm), 0, j))]
        args += [res, gate]
    return pl.pallas_call(
        functools.partial(_mm_body, nk=nk, act=act, residual=residual),
        grid=(M // tm, N // tn, nk),
        in_specs=in_specs,
        out_specs=pl.BlockSpec((tm, tn), lambda i, j, k: (i, j)),
        out_shape=jax.ShapeDtypeStruct((M, N), out_dtype),
        scratch_shapes=[pltpu.VMEM((tm, tn), F32)] if nk > 1 else [],
        compiler_params=_cparams(("parallel", "parallel", "arbitrary")),
        name=name,
    )(*args)


def _cast_body(w_ref, o_ref):
    o_ref[...] = w_ref[...].astype(o_ref.dtype)


def _layer_bf16(w, layer):
    _, K, N = w.shape
    tk = next(t for t in (2048, 1024, 512, 256, 128, 64, 32, 16) if K % t == 0 and t * N * 4 <= CAST_BLOCK_BYTES)
    return pl.pallas_call(
        _cast_body,
        grid=(K // tk,),
        in_specs=[pl.BlockSpec((None, tk, N), lambda i: (layer, i, 0))],
        out_specs=pl.BlockSpec((tk, N), lambda i: (i, 0)),
        out_shape=jax.ShapeDtypeStruct((K, N), BF16),
        compiler_params=_cparams(("parallel",)),
        name="weight_bf16",
    )(w)


def _pack_body(w_ref, o_ref, *, pieces):
    w = w_ref[...]
    parts = []
    for src, width in pieces:
        parts.append(jnp.zeros((w.shape[0], width), F32) if src is None else w[:, src:src + width])
    o_ref[...] = jnp.concatenate(parts, axis=1).astype(o_ref.dtype)


def _layer_packed_bf16(w, layer, pieces):
    _, K, N = w.shape
    n_out = sum(width for _, width in pieces)
    tk = next(t for t in (256, 128, 64, 32, 16) if K % t == 0 and t * N * 4 <= CAST_BLOCK_BYTES)
    return pl.pallas_call(
        functools.partial(_pack_body, pieces=tuple(pieces)),
        grid=(K // tk,),
        in_specs=[pl.BlockSpec((None, tk, N), lambda i: (layer, i, 0))],
        out_specs=pl.BlockSpec((tk, n_out), lambda i: (i, 0)),
        out_shape=jax.ShapeDtypeStruct((K, n_out), BF16),
        compiler_params=_cparams(("parallel",)),
        name="weight_pack",
    )(w)


def _mm_res_norm_body(a_ref, b_ref, res_ref, gate_ref, g_ref, sh_ref, sc_ref, *out_refs, nk, keep_x):
    if nk > 1:
        *out_refs, acc_ref = out_refs
    h_ref = out_refs[-1]
    k = pl.program_id(1)
    part = jnp.dot(a_ref[...], b_ref[...], preferred_element_type=F32)

    def finish(acc):
        x = res_ref[...] + gate_ref[0] * acc
        if keep_x:
            out_refs[0][...] = x
        ms = jnp.mean(x * x, axis=-1, keepdims=True)
        xn = x * lax.rsqrt(ms + EPS) * g_ref[...]
        h_ref[...] = (xn * (1.0 + sc_ref[0]) + sh_ref[0]).astype(h_ref.dtype)

    if nk == 1:
        finish(part)
        return

    @pl.when(k == 0)
    def _():
        acc_ref[...] = part

    @pl.when(k > 0)
    def _():
        acc_ref[...] += part

    @pl.when(k == nk - 1)
    def _():
        finish(acc_ref[...])


def _mm_res_norm(a, b, res, gate, norm_g, shift, scale, group_idx, *, res_row0=0, keep_x=True, h_dtype=BF16,
                 name="mm_res_norm"):
    M, K = a.shape
    N = b.shape[1]
    h_bytes = jnp.dtype(h_dtype).itemsize

    def vmem_bytes(tm, tk):
        blocks = tm * tk * 2 + tk * N * 2 + tm * N * 4 + (tm * N * 4 if keep_x else 0) + tm * N * h_bytes
        return 2 * blocks + (tm * N * 4 if tk < K else 0)

    tm, tk = next(((tm, tk) for tk in (K, 1024, 512, 256, 128) if K % tk == 0
                   for tm in (512, 256, 128, 64, 32, 16, 8) if M % tm == 0
                   and vmem_bytes(tm, tk) <= VMEM_LIMIT - VMEM_INTERNAL), (8, 128))
    nk = K // tk
    if res_row0 % tm:
        res, res_row0 = res[res_row0:res_row0 + M], 0
    r0 = res_row0 // tm
    grp = lambda i, k: (group_idx(i * tm), 0, 0)
    row = pl.BlockSpec((tm, N), lambda i, k: (i, 0))
    out_specs = ([row] if keep_x else []) + [row]
    out_shape = ([jax.ShapeDtypeStruct((M, N), F32)] if keep_x else []) + [jax.ShapeDtypeStruct((M, N), h_dtype)]
    out = pl.pallas_call(
        functools.partial(_mm_res_norm_body, nk=nk, keep_x=keep_x),
        grid=(M // tm, nk),
        in_specs=[pl.BlockSpec((tm, tk), lambda i, k: (i, k)),
                  pl.BlockSpec((tk, N), lambda i, k: (k, 0)),
                  pl.BlockSpec((tm, N), lambda i, k: (r0 + i, 0)),
                  pl.BlockSpec((1, 1, N), grp),
                  pl.BlockSpec((1, N), lambda i, k: (0, 0)),
                  pl.BlockSpec((1, 1, N), grp), pl.BlockSpec((1, 1, N), grp)],
        out_specs=out_specs,
        out_shape=out_shape,
        scratch_shapes=[pltpu.VMEM((tm, N), F32)] if nk > 1 else [],
        compiler_params=_cparams(("parallel", "arbitrary")),
        name=name,
    )(a, b, res, gate, norm_g.reshape(1, N), shift, scale)
    return out if keep_x else out[0]


def _norm_body(x_ref, g_ref, sh_ref, sc_ref, o_ref):
    x = x_ref[...]
    ms = jnp.mean(x * x, axis=-1, keepdims=True)
    xn = x * lax.rsqrt(ms + EPS) * g_ref[...]
    o_ref[...] = (xn * (1.0 + sc_ref[0]) + sh_ref[0]).astype(o_ref.dtype)


def _norm_mod(x, g, shift, scale, group_idx, out_dtype=BF16):
    M, D = x.shape
    tm = _pick(M, STAGE_ROWS)
    return pl.pallas_call(
        _norm_body,
        grid=(M // tm,),
        in_specs=[pl.BlockSpec((tm, D), lambda i: (i, 0)),
                  pl.BlockSpec((1, D), lambda i: (0, 0)),
                  pl.BlockSpec((1, 1, D), lambda i: (group_idx(i * tm), 0, 0)),
                  pl.BlockSpec((1, 1, D), lambda i: (group_idx(i * tm), 0, 0))],
        out_specs=pl.BlockSpec((tm, D), lambda i: (i, 0)),
        out_shape=jax.ShapeDtypeStruct((M, D), out_dtype),
        compiler_params=_cparams(("parallel",)),
        name="norm_mod",
    )(x, g.reshape(1, D), shift, scale)


def _norm_first_body(c_ref, x_ref, g_ref, sh_ref, sc_ref, h_ref, xs_ref, *, n_ctx_tiles):
    x = jnp.where(pl.program_id(0) < n_ctx_tiles, c_ref[...], x_ref[...])
    xs_ref[...] = x
    ms = jnp.mean(x * x, axis=-1, keepdims=True)
    xn = x * lax.rsqrt(ms + EPS) * g_ref[...]
    h_ref[...] = (xn * (1.0 + sc_ref[0]) + sh_ref[0]).astype(h_ref.dtype)


def _norm_mod_first(ctx_rows, x_rows, g, shift, scale, group_idx):
    RC, D = ctx_rows.shape
    RX = x_rows.shape[0]
    tm = _pick(math.gcd(RC, RX), STAGE_ROWS)
    nc = RC // tm
    grp = lambda i: (group_idx(i * tm), 0, 0)
    row = pl.BlockSpec((tm, D), lambda i: (i, 0))
    return pl.pallas_call(
        functools.partial(_norm_first_body, n_ctx_tiles=nc),
        grid=((RC + RX) // tm,),
        in_specs=[pl.BlockSpec((tm, D), lambda i: (jnp.minimum(i, nc - 1), 0)),
                  pl.BlockSpec((tm, D), lambda i: (jnp.maximum(i - nc, 0), 0)),
                  pl.BlockSpec((1, D), lambda i: (0, 0)),
                  pl.BlockSpec((1, 1, D), grp), pl.BlockSpec((1, 1, D), grp)],
        out_specs=[row, row],
        out_shape=[jax.ShapeDtypeStruct((RC + RX, D), BF16), jax.ShapeDtypeStruct((RC + RX, D), F32)],
        compiler_params=_cparams(("parallel",)),
        name="norm_first",
    )(ctx_rows, x_rows, g.reshape(1, D), shift, scale)


def _seq_block(b, s, *, n_ctx_blk, n_x_blk, n_batch, reverse):
    if reverse:
        c = b * n_ctx_blk + (n_ctx_blk - 1 - s)
        x = n_batch * n_ctx_blk + b * n_x_blk + (n_x_blk - 1 - (s - n_ctx_blk))
    else:
        c = b * n_ctx_blk + s
        x = n_batch * n_ctx_blk + b * n_x_blk + (s - n_ctx_blk)
    return jnp.where(s < n_ctx_blk, c, x)


def _split2(x):
    hi = x.astype(BF16)
    lo = (x - hi.astype(F32)).astype(BF16)
    return hi, lo


def _rwkv_chunk(lw, r, v, kk, kd, be, st, run_sum, strict, incl, same_head, head0, eye):
    C = RWKV_CHUNK

    def stack(z):
        return jnp.concatenate([jnp.where(head0, z, 0.0), jnp.where(head0, 0.0, z)], axis=0)

    def unstack(z):
        return z[0:C] + z[C:2 * C]

    bdot = lambda a, b: jnp.dot(a, b, preferred_element_type=F32)
    bdot_nt = lambda a, b: lax.dot_general(a, b, (((1,), (1,)), ((), ())), preferred_element_type=F32)
    bdot_tn = lambda a, b: lax.dot_general(a, b, (((0,), (0,)), ((), ())), preferred_element_type=F32)
    cat = lambda *zs: jnp.concatenate(zs, axis=0)
    P = range(len(lw))
    cum = [run_sum(lw[p]) for p in P]
    total = [jnp.sum(lw[p], axis=0, keepdims=True) for p in P]
    a_t = [kk[p] * jnp.exp(cum[p] - lw[p]) for p in P]
    r_t = [r[p] * jnp.exp(cum[p]) for p in P]
    a_sp = [_split2(a_t[p]) for p in P]
    r_hi = [r_t[p].astype(BF16) for p in P]
    sa_sp = [_split2(stack(a_t[p])) for p in P]
    sr_hi = [stack(r_t[p]).astype(BF16) for p in P]
    pinv = [jnp.exp(-cum[p]) for p in P]
    bk_sp = [_split2(cat(stack(be[p] * pinv[p]), stack(kd[p] * pinv[p]))) for p in P]
    g1 = [bdot_nt(cat(sa_sp[p][0], sa_sp[p][1], sr_hi[p]), bk_sp[p][0]) for p in P]
    g2 = [bdot_nt(sa_sp[p][0], bk_sp[p][1]) for p in P]
    n_all = [g1[p][0:2 * C] + g1[p][2 * C:4 * C] + g2[p] for p in P]
    x_neg = [jnp.where(strict, -n_all[p][:, 0:2 * C], 0.0) for p in P]
    n_ak = [jnp.where(strict, n_all[p][:, 2 * C:4 * C], 0.0) for p in P]
    m_r = [jnp.concatenate([jnp.where(incl, g1[p][4 * C:6 * C, 0:2 * C], 0.0),
                            jnp.where(incl, g1[p][4 * C:6 * C, 2 * C:4 * C], 0.0)], axis=1).astype(BF16)
           for p in P]
    lcat = lambda *zs: jnp.concatenate(zs, axis=1)
    W2 = 2 * C
    t_inv = [eye + x_neg[p] for p in P]
    xb = [x_neg[p].astype(BF16) for p in P]
    xb = [bdot(xb[p], xb[p]).astype(BF16) for p in P]
    n_pow = int(math.log2(C))
    for it in range(1, n_pow):
        if it + 1 < n_pow:
            both = [bdot(xb[p], lcat(xb[p], t_inv[p].astype(BF16))) for p in P]
            xb = [both[p][:, 0:W2].astype(BF16) for p in P]
            t_inv = [t_inv[p] + both[p][:, W2:2 * W2] for p in P]
        else:
            t_inv = [t_inv[p] + bdot(xb[p], t_inv[p].astype(BF16)) for p in P]
    t_inv = [t_inv[p].astype(BF16) for p in P]
    st_sp = [_split2(st[p]) for p in P]
    as1 = [bdot_nt(a_sp[p][0], cat(st_sp[p][0], st_sp[p][1])) for p in P]
    as2 = [bdot_nt(cat(a_sp[p][1], r_hi[p]), st_sp[p][0]) for p in P]
    v_sp = [_split2(stack(v[p])) for p in P]
    nk_sp = [_split2(n_ak[p]) for p in P]
    nv1 = [bdot(lcat(nk_sp[p][0], nk_sp[p][1]), cat(v_sp[p][0], v_sp[p][0])) for p in P]
    nv2 = [bdot(nk_sp[p][0], v_sp[p][1]) for p in P]
    rhs = [-(stack(as1[p][:, 0:W2] + as1[p][:, W2:2 * W2] + as2[p][0:C]) + nv1[p] + nv2[p]) for p in P]
    u0 = [bdot(t_inv[p], rhs[p].astype(BF16)) for p in P]
    u0_sp = [_split2(u0[p]) for p in P]
    x_sp = [_split2(x_neg[p]) for p in P]
    xu1 = [bdot(lcat(x_sp[p][0], x_sp[p][1]), cat(u0_sp[p][0], u0_sp[p][0])) for p in P]
    xu2 = [bdot(x_sp[p][0], u0_sp[p][1]) for p in P]
    res = [(rhs[p] - u0[p] + xu1[p] + xu2[p]).astype(BF16) for p in P]
    u = [u0[p] + bdot(t_inv[p], res[p]) for p in P]
    y = [stack(as2[p][C:2 * C]) + bdot(m_r[p], cat(u[p].astype(BF16), v_sp[p][0])) for p in P]
    uv_sp = [_split2(cat(unstack(u[p]), v[p])) for p in P]
    pend = [jnp.exp(total[p] - cum[p]) for p in P]
    e_sp = [_split2(cat(be[p] * pend[p], kd[p] * pend[p])) for p in P]
    upd = [bdot_tn(cat(uv_sp[p][0], uv_sp[p][1], uv_sp[p][0]), cat(e_sp[p][0], e_sp[p][0], e_sp[p][1]))
           for p in P]
    st_new = [st[p] * jnp.exp(total[p]) + jnp.where(same_head, upd[p], 0.0) for p in P]
    return [unstack(y[p]) for p in P], st_new


def _rwkv_body(r_ref, v_ref, kk_ref, lw_ref, kd_ref, be_ref, y_ref, st_ref, *, reverse, npair):
    C = RWKV_CHUNK
    nchunk = SEQ_BLOCK // C
    s = pl.program_id(2)

    @pl.when(s == 0)
    def _():
        st_ref[...] = jnp.zeros_like(st_ref)

    sgn = -1 if reverse else 1
    row = lax.broadcasted_iota(jnp.int32, (2 * C, 2 * C), 0)
    col = lax.broadcasted_iota(jnp.int32, (2 * C, 2 * C), 1)
    diff = (row - col) * sgn
    strict = diff > 0
    incl = diff >= 0
    same_head = (row // C) == (col // C)
    eye = jnp.where(row == col, 1.0, 0.0)
    head0 = lax.broadcasted_iota(jnp.int32, (C, 2 * C), 1) < C
    t_idx = lax.broadcasted_iota(jnp.int32, (C, 1), 0)

    def run_sum(z):
        k = 1
        while k < C:
            if reverse:
                z = z + jnp.where(t_idx < C - k, pltpu.roll(z, C - k, axis=0), 0.0)
            else:
                z = z + jnp.where(t_idx >= k, pltpu.roll(z, k, axis=0), 0.0)
            k *= 2
        return z

    def chunk(ci, carry):
        ce = (nchunk - 1 - ci) if reverse else ci
        sl = pl.ds(pl.multiple_of(ce * C, C), C)
        lanes = [slice(p * LANES, (p + 1) * LANES) for p in range(npair)]
        get = lambda ref: [ref[sl, ln] for ln in lanes]
        ys, sts = _rwkv_chunk(get(lw_ref), get(r_ref), get(v_ref), get(kk_ref), get(kd_ref), get(be_ref),
                              [st_ref[p] for p in range(npair)], run_sum, strict, incl, same_head, head0, eye)
        for p in range(npair):
            y_ref[sl, lanes[p]] = ys[p]
            st_ref[p] = sts[p]
        return carry

    lax.fori_loop(0, nchunk, chunk, 0)


def _rwkv_scan(r, v, kk, lw, kd, be, *, n_batch, n_ctx_blk, n_x_blk, reverse, npair=8):
    R, W = r.shape
    ngrp = W // (LANES * npair)
    ns = n_ctx_blk + n_x_blk
    d = 1 if reverse else 0
    blk = functools.partial(_seq_block, n_ctx_blk=n_ctx_blk, n_x_blk=n_x_blk, n_batch=n_batch, reverse=reverse)
    spec = pl.BlockSpec((SEQ_BLOCK, LANES * npair), lambda b, p, s: (blk(b, s), p))
    dspec = pl.BlockSpec((None, SEQ_BLOCK, LANES * npair), lambda b, p, s: (d, blk(b, s), p))
    return pl.pallas_call(
        functools.partial(_rwkv_body, reverse=reverse, npair=npair),
        grid=(n_batch, ngrp, ns),
        in_specs=[spec] * 3 + [dspec] * 3,
        out_specs=spec,
        out_shape=jax.ShapeDtypeStruct((R, W), F32),
        scratch_shapes=[pltpu.VMEM((npair, LANES, LANES), F32)],
        compiler_params=_cparams(("parallel", "parallel", "arbitrary")),
        name="rwkv_scan_bwd" if reverse else "rwkv_scan_fwd",
    )(r, v, kk, lw, kd, be)


S5_SEG = SEQ_BLOCK // SUBLANES
S5_TAB_SLABS = 2 * S5_SEG + 2 + 8
S5_NGB = 4


def _s5_body(u_ref, bw_ref, cw_ref, tab_ref, y_ref, hb_ref, carry_ref, perm_ref, *, reverse):
    W = S5_GB * S5_N
    T = SEQ_BLOCK
    G = range(S5_NGB)
    s = pl.program_id(2)

    @pl.when(s == 0)
    def _():
        carry_ref[...] = jnp.zeros_like(carry_ref)
        prow = lax.broadcasted_iota(jnp.int32, (T, T), 0)
        pcol = lax.broadcasted_iota(jnp.int32, (T, T), 1)
        t_of_row = S5_SEG * (prow % SUBLANES) + prow // SUBLANES
        if reverse:
            t_of_row = T - 1 - t_of_row
        perm_ref[...] = jnp.where(pcol == t_of_row, 1.0, 0.0).astype(BF16)

    perm = perm_ref[...]
    u_p = jnp.dot(perm, u_ref[...].astype(BF16), preferred_element_type=F32).astype(BF16)
    for g in G:
        hb_ref[g] = jnp.dot(u_p[:, g * LANES:(g + 1) * LANES], bw_ref[g], preferred_element_type=F32)

    slab = lambda j: pl.ds(j * SUBLANES, SUBLANES)
    tab = lambda g, n: tab_ref[g, slab(n), :]
    o_seg = 2 * S5_SEG
    o_pow = o_seg + 2
    re, im = slice(0, W), slice(W, 2 * W)
    a = [(tab(g, o_pow), tab(g, o_pow + 1)) for g in G]
    h = [(hb_ref[g, slab(0), re], hb_ref[g, slab(0), im]) for g in G]
    for j in range(1, S5_SEG):
        h = [(hb_ref[g, slab(j), re] + a[g][0] * h[g][0] - a[g][1] * h[g][1],
              hb_ref[g, slab(j), im] + a[g][0] * h[g][1] + a[g][1] * h[g][0]) for g in G]
        for g in G:
            hb_ref[g, slab(j), re] = h[g][0]
            hb_ref[g, slab(j), im] = h[g][1]
    rows8 = lax.broadcasted_iota(jnp.int32, (SUBLANES, W), 0)
    ent = []
    for g in G:
        gr, gi = h[g]
        for n, k in enumerate((1, 2, 4)):
            pr, pi = tab(g, o_pow + 2 + 2 * n), tab(g, o_pow + 3 + 2 * n)
            sr = jnp.where(rows8 >= k, pltpu.roll(gr, k, axis=0), 0.0)
            si = jnp.where(rows8 >= k, pltpu.roll(gi, k, axis=0), 0.0)
            gr, gi = gr + pr * sr - pi * si, gi + pr * si + pi * sr
        cr, ci = carry_ref[g, :, re], carry_ref[g, :, im]
        qr, qi = tab(g, o_seg), tab(g, o_seg + 1)
        gr, gi = gr + qr * cr - qi * ci, gi + qr * ci + qi * cr
        ent.append((jnp.where(rows8 >= 1, pltpu.roll(gr, 1, axis=0), cr),
                    jnp.where(rows8 >= 1, pltpu.roll(gi, 1, axis=0), ci)))
        carry_ref[g, :, re] = jnp.broadcast_to(gr[SUBLANES - 1:SUBLANES, :], (SUBLANES, W))
        carry_ref[g, :, im] = jnp.broadcast_to(gi[SUBLANES - 1:SUBLANES, :], (SUBLANES, W))
    for g in G:
        in_r, in_i = ent[g]
        for j in range(S5_SEG):
            pr, pi = tab(g, j), tab(g, S5_SEG + j)
            hb_ref[g, slab(j), re] = hb_ref[g, slab(j), re] + pr * in_r - pi * in_i
            hb_ref[g, slab(j), im] = hb_ref[g, slab(j), im] + pr * in_i + pi * in_r
    y_p = jnp.concatenate([jnp.dot(hb_ref[g].astype(BF16), cw_ref[g], preferred_element_type=F32) for g in G],
                          axis=1)
    y_ref[...] = _dot_x_ones_tn(perm, y_p)


def _dot_x_ones_tn(ones, x):
    hi, mid, lo = _split3(x)
    f = lambda t: lax.dot_general(ones, t, (((0,), (0,)), ((), ())), preferred_element_type=F32)
    return f(hi) + f(mid) + f(lo)


def _s5_scan(p, col0_blk, bw, cw, tab, *, n_batch, n_ctx_blk, n_x_blk, reverse):
    R = p.shape[0]
    n = S5_NGB
    ngb = S5_G // S5_GB // n
    ns = n_ctx_blk + n_x_blk
    blk = functools.partial(_seq_block, n_ctx_blk=n_ctx_blk, n_x_blk=n_x_blk, n_batch=n_batch, reverse=reverse)
    W2 = 2 * S5_GB * S5_N
    assert col0_blk % n == 0
    return pl.pallas_call(
        functools.partial(_s5_body, reverse=reverse),
        grid=(n_batch, ngb, ns),
        in_specs=[pl.BlockSpec((SEQ_BLOCK, n * LANES), lambda b, g, s: (blk(b, s), col0_blk // n + g)),
                  pl.BlockSpec((n, LANES, W2), lambda b, g, s: (g, 0, 0)),
                  pl.BlockSpec((n, W2, LANES), lambda b, g, s: (g, 0, 0)),
                  pl.BlockSpec((n, S5_TAB_SLABS * SUBLANES, W2 // 2), lambda b, g, s: (g, 0, 0))],
        out_specs=pl.BlockSpec((SEQ_BLOCK, n * LANES), lambda b, g, s: (blk(b, s), g)),
        out_shape=jax.ShapeDtypeStruct((R, S5_W), F32),
        scratch_shapes=[pltpu.VMEM((n, SEQ_BLOCK, W2), F32), pltpu.VMEM((n, SUBLANES, W2), F32),
                        pltpu.VMEM((SEQ_BLOCK, SEQ_BLOCK), BF16)],
        compiler_params=_cparams(("parallel", "parallel", "arbitrary")),
        name="s5_scan_bwd" if reverse else "s5_scan_fwd",
    )(p, bw, cw, tab)


def _s5_weights(A_re, A_im, log_dt, B_re, B_im, C_re, C_im):
    lam_re = jnp.minimum(A_re, -1e-4)
    dt = jnp.exp(log_dt)[:, None]
    mag = jnp.exp(lam_re * dt)
    ab_re, ab_im = mag * jnp.cos(A_im * dt), mag * jnp.sin(A_im * dt)
    den = lam_re * lam_re + A_im * A_im
    f_re = ((ab_re - 1) * lam_re + ab_im * A_im) / den
    f_im = (ab_im * lam_re - (ab_re - 1) * A_im) / den
    bb_re = f_re[..., None] * B_re - f_im[..., None] * B_im
    bb_im = f_re[..., None] * B_im + f_im[..., None] * B_re
    ngb = S5_G // S5_GB
    eye = jnp.eye(S5_GB, dtype=F32)

    def in_block(bb):
        t = bb.reshape(ngb, S5_GB, S5_N, S5_P)
        return jnp.einsum("agnp,gh->agphn", t, eye).reshape(ngb, S5_GB * S5_P, S5_GB * S5_N)

    def out_block(cc):
        t = cc.reshape(ngb, S5_GB, S5_P, S5_N)
        return jnp.einsum("agpn,gh->agnhp", t, eye).reshape(ngb, S5_GB * S5_N, S5_GB * S5_P)

    bw = jnp.concatenate([in_block(bb_re), in_block(bb_im)], axis=2).astype(BF16)
    cw = jnp.concatenate([out_block(C_re), -out_block(C_im)], axis=1).astype(BF16)

    ones8 = [1.0] * SUBLANES
    slab_e = [[float(j + 1)] * SUBLANES for j in range(S5_SEG)]
    seg_e = [float(S5_SEG * (i + 1)) for i in range(SUBLANES)]
    expo, is_im = [], []
    for group in (slab_e, [seg_e]):
        expo += group + group
        is_im += [False] * len(group) + [True] * len(group)
    for k in (1.0, float(S5_SEG), 2.0 * S5_SEG, 4.0 * S5_SEG):
        expo += [[k * o for o in ones8]] * 2
        is_im += [False, True]
    e = jnp.asarray(expo, F32).reshape(S5_TAB_SLABS * SUBLANES, 1, 1)
    im = jnp.repeat(jnp.asarray(is_im), SUBLANES).reshape(S5_TAB_SLABS * SUBLANES, 1, 1)
    m = jnp.exp(e * (lam_re * dt))
    ang = e * (A_im * dt)
    tab = m * jnp.where(im, jnp.sin(ang), jnp.cos(ang))
    tab = tab.reshape(S5_TAB_SLABS * SUBLANES, ngb, S5_GB * S5_N).transpose(1, 0, 2)
    return bw, cw, tab


def _ssd_body(x_ref, b_ref, c_ref, dt_ref, dta_ref, sel_ref, y_ref, st_ref, *, reverse):
    Q = SSD_CHUNK
    s = pl.program_id(1)

    @pl.when(s == 0)
    def _():
        st_ref[...] = jnp.zeros_like(st_ref)

    row = lax.broadcasted_iota(jnp.int32, (Q, Q), 0)
    col = lax.broadcasted_iota(jnp.int32, (Q, Q), 1)
    mask = (col >= row) if reverse else (col <= row)
    tri = jnp.where(mask, 1.0, 0.0).astype(BF16)
    head0 = lax.broadcasted_iota(jnp.int32, (Q, LANES), 1) < SSD_HD
    head0_rows = lax.broadcasted_iota(jnp.int32, (LANES, SSD_N), 0) < SSD_HD

    h_lo = SSD_HEADS if reverse else 0
    dt = dt_ref[:, h_lo:h_lo + SSD_HEADS]
    dta = dta_ref[:, h_lo:h_lo + SSD_HEADS]
    dtt = dt_ref[...].T[h_lo:h_lo + SSD_HEADS, :]
    dtat = dta_ref[...].T[h_lo:h_lo + SSD_HEADS, :]
    acum = _dot_ones_x(tri, dta)
    acum_t = _dot_x_ones_nt(dtat, tri)
    total = jnp.sum(dta, axis=0, keepdims=True)
    total_t = jnp.sum(dtat, axis=1, keepdims=True)
    e_tot_t = jnp.exp(total_t)
    dt_wide = jnp.dot(dt.astype(BF16), sel_ref[...], preferred_element_type=F32)
    heads_per_group = SSD_HEADS // SSD_GROUPS
    G = range(SSD_GROUPS)
    pairs = [(g, g * heads_per_group + 2 * pr) for g in G for pr in range(heads_per_group // 2)]
    lanes2 = lambda h0: slice(h0 * SSD_HD, (h0 + 2) * SSD_HD)
    bg = [b_ref[:, g * SSD_N:(g + 1) * SSD_N].astype(BF16) for g in G]
    cg = [c_ref[:, g * SSD_N:(g + 1) * SSD_N].astype(BF16) for g in G]
    cb = [lax.dot_general(cg[g], bg[g], (((1,), (1,)), ((), ())), preferred_element_type=F32) for g in G]
    xp = [x_ref[:, lanes2(h0)] for _, h0 in pairs]
    xb = [t.astype(BF16) for t in xp]
    colb = [jnp.broadcast_to(acum[:, h:h + 1], (Q, LANES)) for h in range(SSD_HEADS)]

    def weights(g, h):
        seg = colb[h] - acum_t[h:h + 1, :]
        return (cb[g] * jnp.exp(jnp.where(mask, seg, -jnp.inf)) * dtt[h:h + 1, :]).astype(BF16)

    yd = [(jnp.dot(weights(g, h0), xb[i], preferred_element_type=F32),
           jnp.dot(weights(g, h0 + 1), xb[i], preferred_element_type=F32)) for i, (g, h0) in enumerate(pairs)]
    st = [st_ref[lanes2(h0), :] for _, h0 in pairs]
    yo = [lax.dot_general(cg[g], st[i].astype(BF16), (((1,), (1,)), ((), ())), preferred_element_type=F32)
          for i, (g, _) in enumerate(pairs)]
    for i, (g, h0) in enumerate(pairs):
        scale = jnp.exp(jnp.where(head0, colb[h0], colb[h0 + 1]))
        y_ref[:, lanes2(h0)] = jnp.where(head0, yd[i][0], yd[i][1]) + yo[i] * scale
    xt = [(xp[i] * dt_wide[:, lanes2(h0)]
           * jnp.exp(jnp.where(head0, total[:, h0:h0 + 1] - colb[h0], total[:, h0 + 1:h0 + 2] - colb[h0 + 1]))
           ).astype(BF16) for i, (_, h0) in enumerate(pairs)]
    upd = [lax.dot_general(xt[i], bg[g], (((0,), (0,)), ((), ())), preferred_element_type=F32)
           for i, (g, _) in enumerate(pairs)]
    for i, (_, h0) in enumerate(pairs):
        dec = jnp.where(head0_rows, e_tot_t[h0:h0 + 1, :], e_tot_t[h0 + 1:h0 + 2, :])
        st_ref[lanes2(h0), :] = st[i] * dec + upd[i]


def _ssd_scan(xbc, dt, dta, *, n_batch, n_ctx_blk, n_x_blk, reverse):
    R = xbc.shape[0]
    Q = SSD_CHUNK
    blk = functools.partial(_seq_block, n_ctx_blk=n_ctx_blk, n_x_blk=n_x_blk, n_batch=n_batch, reverse=reverse)
    ns = n_ctx_blk + n_x_blk
    gw = SSD_GROUPS * SSD_N
    row_spec = lambda w, c: pl.BlockSpec((Q, w), lambda b, s: (blk(b, s), c))
    head_of_lane = jnp.arange(SSD_W, dtype=jnp.int32) // SSD_HD
    sel = (jnp.arange(SSD_HEADS, dtype=jnp.int32)[:, None] == head_of_lane[None, :]).astype(BF16)
    return pl.pallas_call(
        functools.partial(_ssd_body, reverse=reverse),
        grid=(n_batch, ns),
        in_specs=[row_spec(SSD_W, 0), row_spec(gw, SSD_W // gw), row_spec(gw, SSD_W // gw + 1),
                  row_spec(LANES, 0), row_spec(LANES, 0), pl.BlockSpec((SSD_HEADS, SSD_W), lambda b, s: (0, 0))],
        out_specs=row_spec(SSD_W, 0),
        out_shape=jax.ShapeDtypeStruct((R, SSD_W), F32),
        scratch_shapes=[pltpu.VMEM((SSD_W, SSD_N), F32)],
        compiler_params=_cparams(("parallel", "arbitrary")),
        name="ssd_scan_bwd" if reverse else "ssd_scan_fwd",
    )(xbc, xbc, xbc, dt, dta, sel)


def _attn_body(q_ref, kp_ref, kc_ref, kn_ref, vp_ref, vc_ref, vn_ref, kx_ref, vx_ref, sink_ref, mix_ref, o_ref,
               band_ref, *, seq_len):
    del mix_ref
    n = pl.program_id(1)
    blk = ATT_BLOCK
    n_ctx = kx_ref.shape[0]
    rep = ATT_HEADS // KV_HEADS
    nk = n_ctx + 3 * blk

    @pl.when(n == 0)
    def _():
        r_q = lax.broadcasted_iota(jnp.int32, (rep * blk, nk), 0) % blk
        j = lax.broadcasted_iota(jnp.int32, (rep * blk, nk), 1)
        ok = (j < n_ctx) | (jnp.abs(r_q - (j - n_ctx - blk)) <= WINDOW)
        band_ref[...] = jnp.where(ok, 0.0, -jnp.inf)

    k_all = jnp.concatenate([kx_ref[...], kp_ref[...], kc_ref[...], kn_ref[...]], axis=0).astype(BF16)
    v_all = jnp.concatenate([vx_ref[...], vp_ref[...], vc_ref[...], vn_ref[...]], axis=0).astype(BF16)
    jb = lax.broadcasted_iota(jnp.int32, (1, nk), 1) - n_ctx
    k_pos = (n - 1) * blk + jb
    outside = (jb >= 0) & ((k_pos < 0) | (k_pos >= seq_len))
    mask = jnp.where(outside, -jnp.inf, band_ref[...])
    scale = ATT_HD ** -0.5
    head_of_row = lax.broadcasted_iota(jnp.int32, (rep * blk, 1), 0) // blk
    G = range(KV_HEADS)
    hd = lambda t, h: t[:, h * ATT_HD:(h + 1) * ATT_HD]
    q = (q_ref[...] * scale).astype(BF16)
    qs = [jnp.concatenate([hd(q, g * rep + i) for i in range(rep)], axis=0) for g in G]
    sinks = []
    for g in G:
        sk = jnp.zeros((rep * blk, 1), F32)
        for i in range(rep):
            sk = jnp.where(head_of_row == i, sink_ref[0:1, g * rep + i:g * rep + i + 1], sk)
        sinks.append(sk)
    sc = [lax.dot_general(qs[g], hd(k_all, g), (((1,), (1,)), ((), ())), preferred_element_type=F32) + mask
          for g in G]
    m = [jnp.maximum(jnp.max(sc[g], axis=-1, keepdims=True), sinks[g]) for g in G]
    e = [jnp.exp(sc[g] - m[g]) for g in G]
    den = [jnp.sum(e[g], axis=-1, keepdims=True) + jnp.exp(sinks[g] - m[g]) for g in G]
    pv = [jnp.dot(e[g].astype(BF16), hd(v_all, g), preferred_element_type=F32) / den[g] for g in G]
    outs = [pv[g][i * blk:(i + 1) * blk] for g in G for i in range(rep)]
    o_ref[...] = jnp.concatenate(outs, axis=1).astype(o_ref.dtype)


def _window_attn(q, k, p, v_col, sink, mix, mix_col, *, n_batch, n_ctx, seq_len):
    nb = seq_len // ATT_BLOCK
    x0 = n_batch * n_ctx // ATT_BLOCK
    vc = v_col // ATT_KV_W
    cur = lambda b, n: x0 + b * nb + n
    prev = lambda b, n: x0 + b * nb + jnp.maximum(n - 1, 0)
    nxt = lambda b, n: x0 + b * nb + jnp.minimum(n + 1, nb - 1)
    k_spec = lambda im: pl.BlockSpec((ATT_BLOCK, ATT_KV_W), lambda b, n: (im(b, n), 0))
    v_spec = lambda im: pl.BlockSpec((ATT_BLOCK, ATT_KV_W), lambda b, n: (im(b, n), vc))
    return pl.pallas_call(
        functools.partial(_attn_body, seq_len=seq_len),
        grid=(n_batch, nb),
        in_specs=[pl.BlockSpec((ATT_BLOCK, ATT_Q_W), lambda b, n: (cur(b, n), 0)),
                  k_spec(prev), k_spec(cur), k_spec(nxt), v_spec(prev), v_spec(cur), v_spec(nxt),
                  pl.BlockSpec((n_ctx, ATT_KV_W), lambda b, n: (b, 0)),
                  pl.BlockSpec((n_ctx, ATT_KV_W), lambda b, n: (b, vc)),
                  pl.BlockSpec((1, ATT_HEADS), lambda b, n: (0, 0)),
                  pl.BlockSpec(memory_space=pl.ANY)],
        out_specs=pl.BlockSpec((ATT_BLOCK, ATT_Q_W), lambda b, n: (b * nb + n, mix_col // ATT_Q_W)),
        out_shape=jax.ShapeDtypeStruct(mix.shape, mix.dtype),
        input_output_aliases={10: 0},
        scratch_shapes=[pltpu.VMEM((ATT_HEADS // KV_HEADS * ATT_BLOCK, n_ctx + 3 * ATT_BLOCK), F32)],
        compiler_params=_cparams(("parallel", "arbitrary")),
        name="window_attn",
    )(q, k, k, k, p, p, p, k, p, sink, mix)


def _tile_edges(row0, tm, n_ctx_rows, ctx_len, seq_len):
    in_ctx = row0 < n_ctx_rows
    pos0 = jnp.where(in_ctx, row0 % ctx_len, (row0 - n_ctx_rows) % seq_len)
    seg = jnp.where(in_ctx, ctx_len, seq_len)
    return pos0 == 0, pos0 + tm == seg


def _shifted_rows(cur, k, prev_blk, next_blk, first, last, rows):
    tm = cur.shape[0]
    if k < 0:
        edge = jnp.where(first, 0.0, pltpu.roll(prev_blk, -k, axis=0))
        body = pltpu.roll(cur, -k, axis=0)
        head = jnp.where(rows < -k, edge, body[0:SUBLANES])
        return jnp.concatenate([head, body[SUBLANES:]], axis=0)
    edge = jnp.where(last, 0.0, pltpu.roll(next_blk, SUBLANES - k, axis=0))
    body = pltpu.roll(cur, tm - k, axis=0)
    tail = jnp.where(rows >= SUBLANES - k, edge, body[tm - SUBLANES:])
    return jnp.concatenate([body[:tm - SUBLANES], tail], axis=0)


def _head_sum_mxu(t, blk):
    n = t.shape[1] // LANES
    hi, lo = _split2(t)
    both = jnp.concatenate([hi, lo], axis=0)
    rows = t.shape[0]
    out = []
    for j in range(n):
        s = jnp.dot(both[:, j * LANES:(j + 1) * LANES], blk, preferred_element_type=F32)
        out.append(s[0:rows] + s[rows:2 * rows])
    return jnp.concatenate(out, axis=1)


def _same_head_matrix():
    r = lax.broadcasted_iota(jnp.int32, (LANES, LANES), 0) // RWKV_HD
    c = lax.broadcasted_iota(jnp.int32, (LANES, LANES), 1) // RWKV_HD
    return jnp.where(r == c, 1.0, 0.0).astype(BF16)


def _softplus(x):
    return jnp.maximum(x, 0.0) + jnp.log(1.0 + jnp.exp(-jnp.abs(x)))


def _rwkv_pre_body(p_ref, hp_ref, hn_ref, mu_ref, lora_ref, gw_ref, w0_ref, a0_ref, kk_ref, ka_ref, rk_ref,
                   r_out, v_out, kk_out, g_out, bonus_out, lw_out, kd_out, be_out, *, tm, geo):
    W = RWKV_W
    first, last = _tile_edges(pl.program_id(0) * tm, tm, *geo)
    rows = lax.broadcasted_iota(jnp.int32, (SUBLANES, 1), 0)

    def tshift(c0, c1):
        cur = p_ref[:, c0:c1]
        prev = _shifted_rows(cur, -1, hp_ref[:, c0:c1], hn_ref[:, c0:c1], first, last, rows)
        nxt = _shifted_rows(cur, 1, hp_ref[:, c0:c1], hn_ref[:, c0:c1], first, last, rows)
        return cur + mu_ref[0:1, c0:c1] * (prev - cur) + mu_ref[1:2, c0:c1] * (nxt - cur)

    r = tshift(0, W)
    k = tshift(W, 2 * W)
    v = tshift(2 * W, 3 * W)
    xl = tshift(3 * W, 3 * W + 4 * LANES)
    lane = lax.broadcasted_iota(jnp.int32, (tm, 2 * LANES), 1)
    lin = jnp.where(lane < DECAY_LORA, jnp.tanh(xl[:, :2 * LANES]), xl[:, :2 * LANES])
    lora = _dot(lin, lora_ref[...])
    g_out[...] = _dot(jax.nn.sigmoid(xl), gw_ref[...])
    blk = _same_head_matrix()
    kk = k * kk_ref[...]
    kk = kk / jnp.maximum(jnp.sqrt(_head_sum_mxu(kk * kk, blk)), 1e-12)
    r_out[...] = r
    v_out[...] = v
    kk_out[...] = kk
    kd_sum = jnp.zeros_like(k)
    for d in range(2):
        w_log = -_softplus(-(w0_ref[d:d + 1, :] + lora[:, d * W:(d + 1) * W])) - 0.5
        a = jax.nn.sigmoid(a0_ref[d:d + 1, :] + lora[:, (2 + d) * W:(3 + d) * W])
        kd = k * (1.0 + (a - 1.0) * ka_ref[...])
        lw_out[d] = -jnp.exp(w_log)
        kd_out[d] = kd
        be_out[d] = a * kk
        kd_sum = kd_sum + kd
    bonus_out[...] = _head_sum_mxu(r * kd_sum * rk_ref[...], blk) * v


def _rwkv_pre(p, mu, lora_w, gate_w, w0, a0, k_k, k_a, r_k, *, geo):
    R = p.shape[0]
    W = RWKV_W
    tm = SEQ_BLOCK
    pw = 3 * W + 4 * LANES
    nb8 = R // SUBLANES
    row = lambda i: (i, 0)
    vec = lambda n: pl.BlockSpec((n, W), lambda i: (0, 0))
    out2 = pl.BlockSpec((2, tm, W), lambda i: (0, i, 0))
    f = jax.ShapeDtypeStruct((R, W), F32)
    f2 = jax.ShapeDtypeStruct((2, R, W), F32)
    return pl.pallas_call(
        functools.partial(_rwkv_pre_body, tm=tm, geo=geo),
        grid=(R // tm,),
        in_specs=[pl.BlockSpec((tm, pw), row),
                  pl.BlockSpec((SUBLANES, pw), lambda i: (jnp.maximum(i * (tm // SUBLANES) - 1, 0), 0)),
                  pl.BlockSpec((SUBLANES, pw), lambda i: (jnp.minimum((i + 1) * (tm // SUBLANES), nb8 - 1), 0)),
                  pl.BlockSpec((2, pw), lambda i: (0, 0)),
                  pl.BlockSpec(lora_w.shape, lambda i: (0, 0)),
                  pl.BlockSpec(gate_w.shape, lambda i: (0, 0)),
                  vec(2), vec(2), vec(1), vec(1), vec(1)],
        out_specs=[pl.BlockSpec((tm, W), row)] * 5 + [out2] * 3,
        out_shape=[f] * 5 + [f2] * 3,
        compiler_params=_cparams(("parallel",)),
        name="rwkv_pre",
    )(p, p, p, mu, lora_w, gate_w, w0, a0, k_k, k_a, r_k)


def _rwkv_post_body(yf_ref, yb_ref, bonus_ref, g_ref, lng_ref, lnb_ref, o_ref):
    blk = _same_head_matrix()
    y = yf_ref[...] + yb_ref[...]
    mean = _head_sum_mxu(y, blk) * (1.0 / RWKV_HD)
    d = y - mean
    var = _head_sum_mxu(d * d, blk) * (1.0 / RWKV_HD)
    yn = d * lax.rsqrt(var + GN_EPS)
    o_ref[...] = ((yn * lng_ref[...] + lnb_ref[...] + bonus_ref[...]) * g_ref[...]).astype(o_ref.dtype)


def _rwkv_post(yf, yb, bonus, g, ln_g, ln_b, out_cols):
    R, W = yf.shape
    tm = _pick(R, STAGE_ROWS)
    row = pl.BlockSpec((tm, W), lambda i: (i, 0))
    vec = pl.BlockSpec((1, W), lambda i: (0, 0))
    return pl.pallas_call(
        _rwkv_post_body,
        grid=(R // tm,),
        in_specs=[row, row, row, row, vec, vec],
        out_specs=row,
        out_shape=jax.ShapeDtypeStruct((R, out_cols), BF16),
        compiler_params=_cparams(("parallel",)),
        name="rwkv_post",
    )(yf, yb, bonus, g, ln_g, ln_b)


def _gelu_tanh(x):
    return 0.5 * x * (1.0 + jnp.tanh(math.sqrt(2.0 / math.pi) * (x + 0.044715 * (x * x * x))))


def _s5_post_body(u0_ref, u1_ref, yf_ref, yb_ref, d_ref, w_ref, b_ref, mix_ref, o_ref):
    del mix_ref
    u = jnp.concatenate([u0_ref[...], u1_ref[...]], axis=1)
    y = _gelu_tanh(d_ref[...] * u + yf_ref[...] + yb_ref[...])
    o_ref[...] = (y * jax.nn.sigmoid(_dot(y, w_ref[...]) + b_ref[...])).astype(o_ref.dtype)


def _s5_post(p, col0, yf, yb, d_skip, glu_w, glu_b, mix):
    R, W = yf.shape
    tm = _pick(R, STAGE_ROWS)
    half = W // 2
    row = pl.BlockSpec((tm, W), lambda i: (i, 0))
    vec = pl.BlockSpec((1, W), lambda i: (0, 0))
    return pl.pallas_call(
        _s5_post_body,
        grid=(R // tm,),
        in_specs=[pl.BlockSpec((tm, half), lambda i: (i, col0 // half)),
                  pl.BlockSpec((tm, half), lambda i: (i, col0 // half + 1)), row, row, vec,
                  pl.BlockSpec((W, W), lambda i: (0, 0)), vec, pl.BlockSpec(memory_space=pl.ANY)],
        out_specs=pl.BlockSpec((tm, W), lambda i: (i, 1)),
        out_shape=jax.ShapeDtypeStruct(mix.shape, mix.dtype),
        input_output_aliases={7: 0},
        compiler_params=_cparams(("parallel",)),
        name="s5_post",
    )(p, p, yf, yb, d_skip, glu_w, glu_b, mix)


def _cd_conv_body(x_ref, hp_ref, hn_ref, w_ref, b_ref, o_ref, *, tm, geo, taps):
    first, last = _tile_edges(pl.program_id(0) * tm, tm, *geo)
    rows = lax.broadcasted_iota(jnp.int32, (SUBLANES, 1), 0)
    cur = x_ref[...]
    half = taps // 2
    acc = b_ref[...] + w_ref[half:half + 1, :] * cur
    for j in range(taps):
        if j != half:
            acc = acc + w_ref[j:j + 1, :] * _shifted_rows(cur, j - half, hp_ref[...], hn_ref[...], first, last, rows)
    o_ref[...] = acc * jax.nn.sigmoid(acc)


def _cd_conv(p, col0, width, conv_w, conv_b, *, geo):
    R = p.shape[0]
    tm = SEQ_BLOCK
    tw = 1024
    taps = conv_w.shape[0]
    nb8 = R // SUBLANES
    c0 = col0 // tw
    w8 = jnp.pad(conv_w, ((0, SUBLANES - taps), (0, 0)))
    return pl.pallas_call(
        functools.partial(_cd_conv_body, tm=tm, geo=geo, taps=taps),
        grid=(R // tm, width // tw),
        in_specs=[pl.BlockSpec((tm, tw), lambda i, j: (i, c0 + j)),
                  pl.BlockSpec((SUBLANES, tw), lambda i, j: (jnp.maximum(i * (tm // SUBLANES) - 1, 0), c0 + j)),
                  pl.BlockSpec((SUBLANES, tw), lambda i, j: (jnp.minimum((i + 1) * (tm // SUBLANES), nb8 - 1), c0 + j)),
                  pl.BlockSpec((SUBLANES, tw), lambda i, j: (0, j)),
                  pl.BlockSpec((1, tw), lambda i, j: (0, j))],
        out_specs=pl.BlockSpec((tm, tw), lambda i, j: (i, j)),
        out_shape=jax.ShapeDtypeStruct((R, width), F32),
        compiler_params=_cparams(("parallel", "parallel")),
        name="cd_conv",
    )(p, p, p, w8, conv_b.reshape(1, width))


def _rope_tile(x, cos, sin):
    lane = lax.broadcasted_iota(jnp.int32, x.shape, 1)
    q = ATT_HD // 4
    partner = jnp.where(lane % (2 * q) < q, pltpu.roll(x, LANES - q, axis=1), pltpu.roll(x, q, axis=1))
    return x * cos + partner * sin


def _cd_rope_dt_body(q_ref, k_ref, dtr_ref, cos_ref, sin_ref, bias_ref, a_ref, q_out, k_out, dt_out, dta_out,
                     *, tm, n_ctx_rows):
    in_ctx = pl.program_id(0) * tm < n_ctx_rows
    cos = jnp.where(in_ctx, 1.0, cos_ref[...])
    sin = jnp.where(in_ctx, 0.0, sin_ref[...])
    for j in range(ATT_Q_W // LANES):
        q_out[:, j * LANES:(j + 1) * LANES] = _rope_tile(q_ref[:, j * LANES:(j + 1) * LANES], cos, sin)
    for j in range(ATT_KV_W // LANES):
        k_out[:, j * LANES:(j + 1) * LANES] = _rope_tile(k_ref[:, j * LANES:(j + 1) * LANES], cos, sin)
    dt = _softplus(dtr_ref[...] + bias_ref[...])
    dt_out[...] = dt
    dta_out[...] = dt * a_ref[...]


def _cd_rope_dt(p, q_col, k_col, dt_col, cos, sin, dt_bias, a_neg, *, n_ctx_rows, seq_len):
    R = p.shape[0]
    tm = _pick(math.gcd(n_ctx_rows, seq_len), STAGE_ROWS)
    tab = pl.BlockSpec((tm, LANES), lambda i: (jnp.maximum(i * tm - n_ctx_rows, 0) % seq_len // tm, 0))
    vec = pl.BlockSpec((1, LANES), lambda i: (0, 0))
    return pl.pallas_call(
        functools.partial(_cd_rope_dt_body, tm=tm, n_ctx_rows=n_ctx_rows),
        grid=(R // tm,),
        in_specs=[pl.BlockSpec((tm, ATT_Q_W), lambda i: (i, q_col // ATT_Q_W)),
                  pl.BlockSpec((tm, ATT_KV_W), lambda i: (i, k_col // ATT_KV_W)),
                  pl.BlockSpec((tm, LANES), lambda i: (i, dt_col // LANES)),
                  tab, tab, vec, vec],
        out_specs=[pl.BlockSpec((tm, ATT_Q_W), lambda i: (i, 0)), pl.BlockSpec((tm, ATT_KV_W), lambda i: (i, 0)),
                   pl.BlockSpec((tm, LANES), lambda i: (i, 0)), pl.BlockSpec((tm, LANES), lambda i: (i, 0))],
        out_shape=[jax.ShapeDtypeStruct((R, ATT_Q_W), F32), jax.ShapeDtypeStruct((R, ATT_KV_W), F32),
                   jax.ShapeDtypeStruct((R, LANES), F32), jax.ShapeDtypeStruct((R, LANES), F32)],
        compiler_params=_cparams(("parallel",)),
        name="cd_rope_dt",
    )(p, p, p, cos, sin, dt_bias, a_neg)


def _ssd_post_body(yf_ref, yb_ref, x_ref, z_ref, d_ref, g_ref, o_ref):
    z = z_ref[...]
    y = (yf_ref[...] + yb_ref[...] + d_ref[...] * x_ref[...]) * (z * jax.nn.sigmoid(z))
    gw = SSD_W // SSD_GROUPS
    for g in range(SSD_GROUPS):
        yg = y[:, g * gw:(g + 1) * gw]
        ms = jnp.mean(yg * yg, axis=-1, keepdims=True)
        o_ref[:, g * gw:(g + 1) * gw] = (yg * lax.rsqrt(ms + EPS) * g_ref[:, g * gw:(g + 1) * gw]).astype(o_ref.dtype)


def _ssd_post(yf, yb, xbc, p, d_skip, norm_g, out_cols, *, n_ctx_rows):
    R = yf.shape[0]
    tm = _pick(math.gcd(n_ctx_rows, R - n_ctx_rows), STAGE_ROWS)
    r0 = n_ctx_rows // tm
    row = pl.BlockSpec((tm, SSD_W), lambda i: (r0 + i, 0))
    vec = pl.BlockSpec((1, SSD_W), lambda i: (0, 0))
    return pl.pallas_call(
        _ssd_post_body,
        grid=((R - n_ctx_rows) // tm,),
        in_specs=[row, row, row, row, vec, vec],
        out_specs=pl.BlockSpec((tm, SSD_W), lambda i: (i, 0)),
        out_shape=jax.ShapeDtypeStruct((R - n_ctx_rows, out_cols), BF16),
        compiler_params=_cparams(("parallel",)),
        name="ssd_post",
    )(yf, yb, xbc, p, d_skip, norm_g)


def _rope_tables(L):
    nf = ATT_HD // 4
    inv_freq = ROPE_BASE ** (-jnp.arange(nf, dtype=F32) / nf)
    t = jnp.arange(L, dtype=jnp.int32)
    row_id = (t // GRID_W).astype(F32)
    col_id = (t % GRID_W).astype(F32)
    ar = row_id[:, None] * inv_freq
    ac = col_id[:, None] * inv_freq
    cos = jnp.concatenate([jnp.cos(ar), jnp.cos(ar), jnp.cos(ac), jnp.cos(ac)], axis=1)
    sin = jnp.concatenate([-jnp.sin(ar), jnp.sin(ar), -jnp.sin(ac), jnp.sin(ac)], axis=1)
    return cos, sin


def kernel(x, c, ctx, c_ctx, ada_w, ada_b, norm1_g, norm2_g, mlp_w1, mlp_w2, final_g, ab_w_in, ab_w_out, rwkv_mu, rwkv_w0, rwkv_w_up, rwkv_a0, rwkv_a_up, rwkv_g_up, rwkv_k_k, rwkv_k_a, rwkv_r_k, rwkv_ln_g, rwkv_ln_b, s5_A_re, s5_A_im, s5_log_dt, s5_B_re, s5_B_im, s5_C_re, s5_C_im, s5_D, s5_glu_w, s5_glu_b, cd_w_in, cd_w_out, ssd_conv_w, ssd_conv_b, ssd_A_log, ssd_dt_bias, ssd_D, ssd_norm_g, attn_sink):
    B, L, D = x.shape
    C = ctx.shape[1]
    RC, RX = B * C, B * L
    R = RC + RX
    assert C % SEQ_BLOCK == 0 and L % SEQ_BLOCK == 0
    geo = dict(n_batch=B, n_ctx_blk=C // SEQ_BLOCK, n_x_blk=L // SEQ_BLOCK)
    geo_ssd = dict(n_batch=B, n_ctx_blk=C // SSD_CHUNK, n_x_blk=L // SSD_CHUNK)
    group_all = lambda r0: jnp.where(r0 < RC, 0, 1 + (r0 - RC) // L)
    group_x = lambda r0: 1 + r0 // L

    cond = jax.nn.silu(jnp.concatenate([c_ctx[None, :], c], axis=0))
    cond = jnp.pad(cond, ((0, SUBLANES - (B + 1) % SUBLANES), (0, 0))) if (B + 1) % SUBLANES else cond
    n_groups = cond.shape[0]

    def modulation(i):
        mod = _mm(cond, ada_w, b_layer=i, name="ada") + ada_b[i]
        return [m.reshape(n_groups, 1, D) for m in jnp.split(mod, 6, axis=-1)]

    sh1, sc1, g1, sh2, sc2, g2 = modulation(0)
    sh1n, sc1n, g1n, sh2n, sc2n, g2n = modulation(1)
    h, xs = _norm_mod_first(ctx.reshape(RC, D), x.reshape(RX, D), norm1_g[0], sh1, sc1, group_all)
    pad_cols = (-RWKV_COLS) % LANES
    w_in = _layer_packed_bf16(ab_w_in, 0, [(0, RWKV_COLS), (None, pad_cols), (RWKV_COLS, S5_W)])
    s5_col0 = RWKV_COLS + pad_cols
    p = _mm(h, w_in, name="ab_in")

    W = RWKV_W
    lora_rows = 2 * LANES
    zw = jnp.zeros((DECAY_LORA, W), F32)
    lora_w = jnp.concatenate([
        jnp.concatenate([rwkv_w_up[0, 0], rwkv_w_up[0, 1], zw, zw], axis=1),
        jnp.concatenate([zw, zw, rwkv_a_up[0, 0], rwkv_a_up[0, 1]], axis=1),
        jnp.zeros((lora_rows - DECAY_LORA - AAA_LORA, 4 * W), F32)], axis=0).astype(BF16)
    gate_w = jnp.pad(rwkv_g_up[0], ((DECAY_LORA + AAA_LORA, pad_cols), (0, 0))).astype(BF16)
    mu = jnp.pad(rwkv_mu[0], ((0, 0), (0, pad_cols)))
    vec = lambda t: t.reshape(1, W)
    r, v, kk, g_gate, bonus, lw, kd, be = _rwkv_pre(
        p, mu, lora_w, gate_w, rwkv_w0[0], rwkv_a0[0], vec(rwkv_k_k[0]), vec(rwkv_k_a[0]), vec(rwkv_r_k[0]),
        geo=(RC, C, L))
    y_dirs = [_rwkv_scan(r, v, kk, lw, kd, be, reverse=(d == 1), **geo) for d in range(2)]
    mix = _rwkv_post(y_dirs[0], y_dirs[1], bonus, g_gate, vec(rwkv_ln_g[0]), vec(rwkv_ln_b[0]), W + S5_W)

    ys_dirs = []
    for d in range(2):
        bw, cw, tab = _s5_weights(s5_A_re[0, d], s5_A_im[0, d], s5_log_dt[0, d], s5_B_re[0, d], s5_B_im[0, d],
                                  s5_C_re[0, d], s5_C_im[0, d])
        ys_dirs.append(_s5_scan(p, s5_col0 // LANES, bw, cw, tab, reverse=(d == 1), **geo))
    mix = _s5_post(p, s5_col0, ys_dirs[0], ys_dirs[1], vec(s5_D[0]), _layer_bf16(s5_glu_w, 0),
                   vec(s5_glu_b[0]), mix)
    xs, h2 = _mm_res_norm(mix, _layer_bf16(ab_w_out, 0), xs, g1, norm2_g[0], sh2, sc2, group_all, name="ab_out")
    m1 = _mm(h2, _layer_bf16(mlp_w1, 0), out_dtype=BF16, act="relu2", name="mlp_up")
    xs = _mm(m1, _layer_bf16(mlp_w2, 0), res=xs, gate=g2, gate_idx=group_all, name="mlp_down")
    h = _norm_mod(xs, norm1_g[1], sh1n, sc1n, group_all)

    sh1, sc1, g1, sh2, sc2, g2 = sh1n, sc1n, g1n, sh2n, sc2n, g2n
    o_dt = SSD_W + XBC_W
    o_q = o_dt + 2 * SSD_HEADS
    n_qkv = ATT_Q_W + 2 * ATT_KV_W
    cd_cols = o_q + n_qkv
    cd_in = _layer_packed_bf16(cd_w_in, 0, [(0, o_dt), (o_q, n_qkv), (o_dt, o_q - o_dt), (None, (-cd_cols) % 512)])
    q_col = o_dt
    k_col = q_col + ATT_Q_W
    v_col = k_col + ATT_KV_W
    dt_col = v_col + ATT_KV_W
    p = _mm(h, cd_in, name="cd_in")

    xbc = _cd_conv(p, SSD_W, XBC_W, ssd_conv_w[0], ssd_conv_b[0], geo=(RC, C, L))
    cos, sin = _rope_tables(L)
    lane_pad = lambda t: jnp.pad(t.reshape(1, 2 * SSD_HEADS), ((0, 0), (0, LANES - 2 * SSD_HEADS)))
    q_rot, k_rot, dt, dta = _cd_rope_dt(
        p, q_col, k_col, dt_col, jnp.tile(cos, (1, LANES // ATT_HD)), jnp.tile(sin, (1, LANES // ATT_HD)),
        lane_pad(ssd_dt_bias[0]), lane_pad(-jnp.exp(ssd_A_log[0])), n_ctx_rows=RC, seq_len=L)
    y_dirs = [_ssd_scan(xbc, dt, dta, reverse=(d == 1), **geo_ssd) for d in range(2)]
    mix = _ssd_post(y_dirs[0], y_dirs[1], xbc, p, jnp.repeat(ssd_D[0], SSD_HD).reshape(1, SSD_W),
                    ssd_norm_g[0].reshape(1, SSD_W), SSD_W + ATT_Q_W, n_ctx_rows=RC)
    mix = _window_attn(q_rot, k_rot, p, v_col, attn_sink[0].reshape(1, ATT_HEADS), mix, SSD_W,
                       n_batch=B, n_ctx=C, seq_len=L)
    xo, h2 = _mm_res_norm(mix, _layer_bf16(cd_w_out, 0), xs, g1, norm2_g[1], sh2, sc2, group_x, res_row0=RC,
                          name="cd_out")
    m1 = _mm(h2, _layer_bf16(mlp_w1, 1), out_dtype=BF16, act="relu2", name="mlp_up")
    xo = _mm(m1, _layer_bf16(mlp_w2, 1), res=xo, gate=g2, gate_idx=group_x, name="mlp_down")
    zero = jnp.zeros((1, 1, D), F32)
    out = _norm_mod(xo, final_g, zero, zero, lambda r0: 0, out_dtype=F32)
    return out.reshape(B, L, D)
```
